```python
import math
import jax
import jax.numpy as jnp
from jax import lax
import numpy as np

D_MODEL = 1024
BATCH = 2
SEQ = 8192
DEPTH = 2

GRID_W = 64
CTX_LEN = 256
RMS_EPS = 1e-6
N_BRANCH = 3
BRANCH_W = 512
ATT_HEADS = 8
ATT_KV_HEADS = 2
ATT_HEAD_DIM = 64
ATT_GROUP = ATT_HEADS // ATT_KV_HEADS
Q_BLOCK = 128
ROPE_THETA = 10000.0
SSD_HEADS = 8
SSD_HEAD_DIM = 64
SSD_INNER = SSD_HEADS * SSD_HEAD_DIM
SSD_GROUPS = 2
SSD_STATE = 64
SSD_XBC = SSD_INNER + 2 * SSD_GROUPS * SSD_STATE
SSD_CONV = 5
SSD_CHUNK = 128
ML_HEADS = 4
ML_HEAD_DIM = 128
ML_INNER = ML_HEADS * ML_HEAD_DIM
ML_CHUNK = 128
PEER_HEADS = 8
PEER_KEYS = 128
PEER_EXPERTS = PEER_KEYS * PEER_KEYS
PEER_TOPK = 16
PEER_QDIM = 256
PEER_BLOCK = 128
IN_SPLITS = (ATT_HEADS * ATT_HEAD_DIM, ATT_KV_HEADS * ATT_HEAD_DIM, ATT_KV_HEADS * ATT_HEAD_DIM,
             SSD_INNER, SSD_XBC, 2 * SSD_HEADS,
             ML_INNER, ML_INNER, ML_INNER, ML_INNER, 4 * ML_HEADS,
             N_BRANCH * D_MODEL)
IN_COLS = sum(IN_SPLITS)

kernel_name = 'hybrid_gqa_ssd_mlstm_peer_dit'


def rmsnorm(x, g):
    xf = x.astype(jnp.float32)
    y = xf * lax.rsqrt(jnp.mean(xf * xf, axis=-1, keepdims=True) + RMS_EPS)
    return (y * g.astype(jnp.float32)).astype(x.dtype)


def split_cols(p):
    bounds = np.cumsum(IN_SPLITS)[:-1].tolist()
    return jnp.split(p, bounds, axis=-1)


def heads(t, n):
    return t.reshape(t.shape[0], t.shape[1], n, -1)


def axial_rope(rows):
    n_freq = ATT_HEAD_DIM // 4
    inv = ROPE_THETA ** (-jnp.arange(n_freq, dtype=jnp.float32) / n_freq)
    row = jnp.broadcast_to(jnp.arange(rows, dtype=jnp.float32)[:, None], (rows, GRID_W)).reshape(-1)
    col = jnp.broadcast_to(jnp.arange(GRID_W, dtype=jnp.float32)[None, :], (rows, GRID_W)).reshape(-1)
    ang = jnp.concatenate([row[:, None] * inv, col[:, None] * inv], axis=-1)
    return jnp.cos(ang), jnp.sin(ang)


def apply_rope(t, cos, sin):
    half = t.shape[-1] // 2
    tf = t.astype(jnp.float32)
    t1, t2 = tf[..., :half], tf[..., half:]
    c, s = cos[:, None, :], sin[:, None, :]
    return jnp.concatenate([t1 * c - t2 * s, t1 * s + t2 * c], axis=-1).astype(t.dtype)


def latent_attention(q, k, v, kc, vc):
    Bsz, S = q.shape[0], q.shape[1]
    kk = jnp.concatenate([k, kc], axis=1)
    vv = jnp.concatenate([v, vc], axis=1)
    nb = S // Q_BLOCK
    qb = q.reshape(Bsz, nb, Q_BLOCK, ATT_KV_HEADS, ATT_GROUP, ATT_HEAD_DIM).transpose(1, 0, 2, 3, 4, 5)
    scale = ATT_HEAD_DIM ** -0.5

    def block(qblk):
        s = jnp.einsum('bqkgd,bmkd->bkgqm', qblk, kk).astype(jnp.float32) * scale
        p = jax.nn.softmax(s, axis=-1).astype(vv.dtype)
        return jnp.einsum('bkgqm,bmkd->bqkgd', p, vv)

    o = lax.map(block, qb)
    return o.transpose(1, 0, 2, 3, 4, 5).reshape(Bsz, S, ATT_HEADS * ATT_HEAD_DIM)


def context_attention(qc, kc, vc):
    Bsz, L = qc.shape[0], qc.shape[1]
    qg = qc.reshape(Bsz, L, ATT_KV_HEADS, ATT_GROUP, ATT_HEAD_DIM)
    s = jnp.einsum('blkgd,bmkd->bkglm', qg, kc).astype(jnp.float32) * ATT_HEAD_DIM ** -0.5
    p = jax.nn.softmax(s, axis=-1).astype(vc.dtype)
    o = jnp.einsum('bkglm,bmkd->blkgd', p, vc)
    return o.reshape(Bsz, L, ATT_HEADS * ATT_HEAD_DIM)


def dwconv(x, w, b):
    out = lax.conv_general_dilated(x, w[:, None, :], window_strides=(1,), padding='SAME',
                                   dimension_numbers=('NWC', 'WIO', 'NWC'),
                                   feature_group_count=x.shape[-1])
    return out + b


def segsum(a):
    T = a.shape[-1]
    xa = jnp.broadcast_to(a[..., None], a.shape + (T,))
    xa = jnp.where(jnp.tril(jnp.ones((T, T), bool), -1), xa, 0.0)
    cs = jnp.cumsum(xa, axis=-2)
    return jnp.where(jnp.tril(jnp.ones((T, T), bool)), cs, -jnp.inf)


def ssd_scan(x, dt, A, Bm, Cm, init, with_output):
    Bsz, T, H, P = x.shape
    G, N = Bm.shape[2], Bm.shape[3]
    R = H // G
    Q = SSD_CHUNK
    nc = T // Q
    a = dt * A
    xc = (x * dt[..., None]).reshape(Bsz, nc, Q, G, R, P)
    ac = a.reshape(Bsz, nc, Q, G, R).transpose(0, 3, 4, 1, 2)
    Bc = Bm.reshape(Bsz, nc, Q, G, N)
    Cc = Cm.reshape(Bsz, nc, Q, G, N)
    a_cum = jnp.cumsum(ac, axis=-1)
    decay_states = jnp.exp(a_cum[..., -1:] - a_cum)
    states = jnp.einsum('bcqgn,bgrcq,bcqgrp->bcgrpn', Bc, decay_states, xc)
    init_c = init.reshape(Bsz, G, R, P, N)[:, None]
    states = jnp.concatenate([init_c, states.astype(init_c.dtype)], axis=1)
    chunk_decay = jnp.exp(segsum(jnp.pad(a_cum[..., -1], [(0, 0)] * 3 + [(1, 0)])))
    new_states = jnp.einsum('bgrzc,bcgrpn->bzgrpn', chunk_decay, states)
    start_states = new_states[:, :-1]
    final = new_states[:, -1].reshape(Bsz, H, P, N)
    if not with_output:
        return None, final
    Lmat = jnp.exp(segsum(ac))
    y_diag = jnp.einsum('bcqgn,bcsgn,bgrcqs,bcsgrp->bcqgrp', Cc, Bc, Lmat, xc)
    y_off = jnp.einsum('bcqgn,bcgrpn,bgrcq->bcqgrp', Cc, start_states, jnp.exp(a_cum))
    return (y_diag + y_off).reshape(Bsz, T, H, P), final


def ssd_zero_state(Bsz):
    z = jnp.zeros((Bsz, SSD_HEADS, SSD_HEAD_DIM, SSD_STATE), jnp.float32)
    return (z, z)


def ssd_mixer(z, xbc, dt_raw, conv_w, conv_b, dt_bias, a_log, d_skip, norm_g, inits, with_output):
    Bsz, T, _ = xbc.shape
    xbc = jax.nn.silu(dwconv(xbc, conv_w, conv_b))
    xs, Bm, Cm = jnp.split(xbc, [SSD_INNER, SSD_INNER + SSD_GROUPS * SSD_STATE], axis=-1)
    xs = xs.reshape(Bsz, T, SSD_HEADS, SSD_HEAD_DIM)
    Bm = Bm.reshape(Bsz, T, SSD_GROUPS, SSD_STATE)
    Cm = Cm.reshape(Bsz, T, SSD_GROUPS, SSD_STATE)
    dt = jax.nn.softplus(dt_raw.reshape(Bsz, T, 2, SSD_HEADS).astype(jnp.float32) + dt_bias.astype(jnp.float32))
    A = -jnp.exp(a_log.astype(jnp.float32))
    fl = lambda t: jnp.flip(t, axis=1)
    y_f, fin_f = ssd_scan(xs, dt[:, :, 0], A[0], Bm, Cm, inits[0], with_output)
    y_b, fin_b = ssd_scan(fl(xs), fl(dt[:, :, 1]), A[1], fl(Bm), fl(Cm), inits[1], with_output)
    if not with_output:
        return None, (fin_f, fin_b)
    y = y_f + fl(y_b) + xs.astype(jnp.float32) * d_skip.astype(jnp.float32)[:, None]
    y = y.reshape(Bsz, T, SSD_INNER).astype(z.dtype) * jax.nn.silu(z)
    return rmsnorm(y, norm_g), (fin_f, fin_b)


def mlstm_scan(q, k, v, i_pre, logf, init, with_output):
    Bsz, T, H, Dh = q.shape
    Q = ML_CHUNK
    nc = T // Q
    qc = q.reshape(Bsz, nc, Q, H, Dh)
    kc = k.reshape(Bsz, nc, Q, H, Dh)
    vc = v.reshape(Bsz, nc, Q, H, Dh)
    ic = i_pre.reshape(Bsz, nc, Q, H).transpose(0, 3, 1, 2)
    b = jnp.cumsum(logf.reshape(Bsz, nc, Q, H).transpose(0, 3, 1, 2), axis=-1)
    b_last = b[..., -1]
    g = b_last[..., None] - b + ic
    m_loc = jnp.max(g, axis=-1)
    w = jnp.exp(g - m_loc[..., None])
    C_loc = jnp.einsum('bhcs,bcshd,bcshe->bchde', w, vc, kc)
    n_loc = jnp.einsum('bhcs,bcshe->bche', w, kc)

    def step(carry, inp):
        C, n, m = carry
        Cl, nl, ml, bl = inp
        m_new = jnp.maximum(bl + m, ml)
        a_prev = jnp.exp(bl + m - m_new)
        a_loc = jnp.exp(ml - m_new)
        C_new = a_prev[..., None, None] * C + a_loc[..., None, None] * Cl
        n_new = a_prev[..., None] * n + a_loc[..., None] * nl
        return (C_new, n_new, m_new), (C, n, m)

    xs = (C_loc.transpose(1, 0, 2, 3, 4), n_loc.transpose(1, 0, 2, 3),
          m_loc.transpose(2, 0, 1), b_last.transpose(2, 0, 1))
    final, (C0, n0, m0) = lax.scan(step, init, xs)
    if not with_output:
        return None, final
    m0 = m0.transpose(1, 2, 0)
    Dm = b[..., :, None] - b[..., None, :] + ic[..., None, :]
    Dm = jnp.where(jnp.tril(jnp.ones((Q, Q), bool)), Dm, -jnp.inf)
    inter = b + m0[..., None]
    m_t = jnp.maximum(inter, jnp.max(Dm, axis=-1))
    Wts = jnp.exp(Dm - m_t[..., None])
    w_inter = jnp.exp(inter - m_t)
    s = jnp.einsum('bcqhd,bcshd->bhcqs', qc, kc) * Wts
    num = (jnp.einsum('bhcqs,bcshd->bcqhd', s, vc)
           + jnp.einsum('bcqhe,cbhde->bcqhd', qc, C0) * w_inter.transpose(0, 2, 3, 1)[..., None])
    den = jnp.sum(s, axis=-1) + w_inter * jnp.einsum('bcqhe,cbhe->bhcq', qc, n0)
    denom = jnp.maximum(jnp.abs(den), jnp.exp(-m_t)).transpose(0, 2, 3, 1)[..., None]
    return (num / denom).reshape(Bsz, T, H, Dh), final


def mlstm_zero_state(Bsz):
    st = (jnp.zeros((Bsz, ML_HEADS, ML_HEAD_DIM, ML_HEAD_DIM), jnp.float32),
          jnp.zeros((Bsz, ML_HEADS, ML_HEAD_DIM), jnp.float32),
          jnp.zeros((Bsz, ML_HEADS), jnp.float32))
    return (st, st)


def mlstm_mixer(q, k, v, o_pre, if_pre, gate_bias, norm_g, inits, with_output):
    Bsz, T, _ = q.shape
    out_dtype = q.dtype
    qh = heads(q, ML_HEADS).astype(jnp.float32)
    kh = heads(k, ML_HEADS).astype(jnp.float32) * ML_HEAD_DIM ** -0.5
    vh = heads(v, ML_HEADS).astype(jnp.float32)
    gates = if_pre.reshape(Bsz, T, 2, 2, ML_HEADS).astype(jnp.float32) + gate_bias.astype(jnp.float32)
    i_pre = gates[:, :, :, 0]
    logf = jax.nn.log_sigmoid(gates[:, :, :, 1])
    fl = lambda t: jnp.flip(t, axis=1)
    h_f, fin_f = mlstm_scan(qh, kh, vh, i_pre[:, :, 0], logf[:, :, 0], inits[0], with_output)
    h_b, fin_b = mlstm_scan(fl(qh), fl(kh), fl(vh), fl(i_pre[:, :, 1]), fl(logf[:, :, 1]), inits[1], with_output)
    if not with_output:
        return None, (fin_f, fin_b)
    h = rmsnorm(h_f + fl(h_b), norm_g).reshape(Bsz, T, ML_INNER)
    h = h * jax.nn.sigmoid(o_pre.astype(jnp.float32))
    return h.astype(out_dtype), (fin_f, fin_b)


def merge_branches(branches, gate_pre, w_branch, w_out):
    Bsz, T, _ = gate_pre.shape
    stacked = jnp.stack(branches, axis=2)
    proj = jnp.einsum('btrw,rwd->btrd', stacked, w_branch)
    gates = jax.nn.sigmoid(gate_pre.reshape(Bsz, T, N_BRANCH, D_MODEL))
    return jnp.sum(gates * proj, axis=2) @ w_out


def token_mixer(h, hc, w_in, qn, kn, conv_w, conv_b, dt_bias, a_log, d_skip, ssd_norm_g,
                ml_bias, ml_norm_g, w_branch, w_out, cos, sin, ctx_out):
    Bsz = h.shape[0]
    aq, ak, av, sz, sxbc, sdt, mq, mk, mv, mo, mif, gpre = split_cols(h @ w_in)
    caq, cak, cav, csz, csxbc, csdt, cmq, cmk, cmv, cmo, cmif, cgpre = split_cols(hc @ w_in)
    q = apply_rope(rmsnorm(heads(aq, ATT_HEADS), qn), cos, sin)
    k = apply_rope(rmsnorm(heads(ak, ATT_KV_HEADS), kn), cos, sin)
    kc = rmsnorm(heads(cak, ATT_KV_HEADS), kn)
    vc = heads(cav, ATT_KV_HEADS)
    att = latent_attention(q, k, heads(av, ATT_KV_HEADS), kc, vc)
    ssd_c, ssd_fin = ssd_mixer(csz, csxbc, csdt, conv_w, conv_b, dt_bias, a_log, d_skip, ssd_norm_g,
                               ssd_zero_state(Bsz), ctx_out)
    ssd_l, _ = ssd_mixer(sz, sxbc, sdt, conv_w, conv_b, dt_bias, a_log, d_skip, ssd_norm_g, ssd_fin, True)
    ml_c, ml_fin = mlstm_mixer(cmq, cmk, cmv, cmo, cmif, ml_bias, ml_norm_g, mlstm_zero_state(Bsz), ctx_out)
    ml_l, _ = mlstm_mixer(mq, mk, mv, mo, mif, ml_bias, ml_norm_g, ml_fin, True)
    y = merge_branches((att, ssd_l, ml_l), gpre, w_branch, w_out)
    if not ctx_out:
        return y, None
    att_c = context_attention(rmsnorm(heads(caq, ATT_HEADS), qn), kc, vc)
    yc = merge_branches((att_c, ssd_c, ml_c), cgpre, w_branch, w_out)
    return y, yc


def peer(h, w_q, subkeys, u, v):
    n_tok = h.shape[0]
    hb = h.reshape(n_tok // PEER_BLOCK, PEER_BLOCK, D_MODEL)

    def block(xb):
        q = (xb @ w_q).reshape(PEER_BLOCK, PEER_HEADS, 2, PEER_QDIM // 2)
        s = jnp.einsum('thpd,hpkd->thpk', q, subkeys).astype(jnp.float32)
        s1, i1 = lax.top_k(s[:, :, 0], PEER_TOPK)
        s2, i2 = lax.top_k(s[:, :, 1], PEER_TOPK)
        cand = (s1[..., :, None] + s2[..., None, :]).reshape(PEER_BLOCK, PEER_HEADS, PEER_TOPK * PEER_TOPK)
        cidx = (i1[..., :, None] * PEER_KEYS + i2[..., None, :]).reshape(PEER_BLOCK, PEER_HEADS, PEER_TOPK * PEER_TOPK)
        sc, pos = lax.top_k(cand, PEER_TOPK)
        idx = jnp.take_along_axis(cidx, pos, axis=-1)
        g = jax.nn.softmax(sc, axis=-1).astype(xb.dtype)
        act = jax.nn.gelu(jnp.einsum('thkd,td->thk', u[idx], xb))
        return jnp.einsum('thk,thkd->td', g * act, v[idx])

    return lax.map(block, hb).reshape(n_tok, D_MODEL)


def setup_inputs(seed: int = 0) -> dict:
    key = jax.random.key(seed)
    ks = jax.random.split(key, 28)
    f32 = jnp.float32
    L = DEPTH

    def nrm(k, shape, s):
        return jax.random.normal(k, shape, f32) * s

    dt0 = jnp.exp(jax.random.uniform(ks[12], (L, 2, SSD_HEADS), f32, math.log(1e-3), math.log(1e-1)))
    fgate = jnp.broadcast_to(jnp.linspace(3.0, 6.0, ML_HEADS, dtype=f32), (L, 2, ML_HEADS))
    return {
        'x': nrm(ks[0], (BATCH, SEQ, D_MODEL), 1.0),
        'c': nrm(ks[1], (BATCH, D_MODEL), 1.0),
        'ctx': nrm(ks[2], (BATCH, CTX_LEN, D_MODEL), 1.0),
        'c_ctx': nrm(ks[3], (D_MODEL,), 1.0),
        'norm1_g': 1.0 + nrm(ks[4], (L, D_MODEL), 0.02),
        'norm2_g': 1.0 + nrm(ks[5], (L, D_MODEL), 0.02),
        'w_mod': nrm(ks[6], (L, D_MODEL, 6 * D_MODEL), 0.02),
        'b_mod': nrm(ks[7], (L, 6 * D_MODEL), 0.01),
        'w_in': nrm(ks[8], (L, D_MODEL, IN_COLS), D_MODEL ** -0.5),
        'att_qnorm': 1.0 + nrm(ks[9], (L, ATT_HEAD_DIM), 0.02),
        'att_knorm': 1.0 + nrm(ks[10], (L, ATT_HEAD_DIM), 0.02),
        'ssd_conv_w': nrm(ks[11], (L, SSD_CONV, SSD_XBC), SSD_CONV ** -0.5),
        'ssd_conv_b': nrm(ks[13], (L, SSD_XBC), 0.01),
        'ssd_dt_bias': dt0 + jnp.log(-jnp.expm1(-dt0)),
        'ssd_a_log': jnp.log(jax.random.uniform(ks[14], (L, 2, SSD_HEADS), f32, 1.0, 16.0)),
        'ssd_d': 1.0 + nrm(ks[15], (L, SSD_HEADS), 0.1),
        'ssd_norm': 1.0 + nrm(ks[16], (L, SSD_INNER), 0.02),
        'ml_gate_bias': jnp.stack([nrm(ks[17], (L, 2, ML_HEADS), 0.1),
                                   fgate + nrm(ks[18], (L, 2, ML_HEADS), 0.1)], axis=2),
        'ml_norm': 1.0 + nrm(ks[19], (L, ML_HEADS, ML_HEAD_DIM), 0.02),
        'w_branch': nrm(ks[20], (L, N_BRANCH, BRANCH_W, D_MODEL), BRANCH_W ** -0.5),
        'w_out': nrm(ks[21], (L, D_MODEL, D_MODEL), D_MODEL ** -0.5),
        'peer_wq': nrm(ks[22], (L, D_MODEL, PEER_HEADS * PEER_QDIM), D_MODEL ** -0.5),
        'peer_subkeys': nrm(ks[23], (L, PEER_HEADS, 2, PEER_KEYS, PEER_QDIM // 2), (PEER_QDIM // 2) ** -0.5),
        'peer_u': nrm(ks[24], (L, PEER_EXPERTS, D_MODEL), D_MODEL ** -0.5),
        'peer_v': nrm(ks[25], (L, PEER_EXPERTS, D_MODEL), PEER_HEADS ** -0.5),
    }


def reference(x, c, ctx, c_ctx, norm1_g, norm2_g, w_mod, b_mod, w_in, att_qnorm, att_knorm,
              ssd_conv_w, ssd_conv_b, ssd_dt_bias, ssd_a_log, ssd_d, ssd_norm, ml_gate_bias, ml_norm,
              w_branch, w_out, peer_wq, peer_subkeys, peer_u, peer_v):
    Bsz, S, _ = x.shape
    ROWS = S // GRID_W
    cos, sin = axial_rope(ROWS)
    silu_c = jax.nn.silu(c)
    silu_cc = jax.nn.silu(c_ctx)
    xc = ctx
    for l in range(DEPTH):
        last = l == DEPTH - 1
        sh1, sc1, g1, sh2, sc2, g2 = jnp.split((silu_c @ w_mod[l] + b_mod[l])[:, None, :], 6, axis=-1)
        csh1, csc1, cg1, csh2, csc2, cg2 = jnp.split(silu_cc @ w_mod[l] + b_mod[l], 6, axis=-1)
        h = rmsnorm(x, norm1_g[l]) * (1.0 + sc1) + sh1
        hc = rmsnorm(xc, norm1_g[l]) * (1.0 + csc1) + csh1
        y, yc = token_mixer(h, hc, w_in[l], att_qnorm[l], att_knorm[l], ssd_conv_w[l], ssd_conv_b[l],
                            ssd_dt_bias[l], ssd_a_log[l], ssd_d[l], ssd_norm[l], ml_gate_bias[l], ml_norm[l],
                            w_branch[l], w_out[l], cos, sin, not last)
        x = x + g1 * y
        h2 = rmsnorm(x, norm2_g[l]) * (1.0 + sc2) + sh2
        x = x + g2 * peer(h2.reshape(-1, D_MODEL), peer_wq[l], peer_subkeys[l], peer_u[l], peer_v[l]).reshape(x.shape)
        if not last:
            xc = xc + cg1 * yc
            hc2 = rmsnorm(xc, norm2_g[l]) * (1.0 + csc2) + csh2
            xc = xc + cg2 * peer(hc2.reshape(-1, D_MODEL), peer_wq[l], peer_subkeys[l], peer_u[l], peer_v[l]).reshape(xc.shape)
    return x
```

```python
import functools
import math

import jax
import jax.numpy as jnp
import numpy as np
from jax import lax
from jax.experimental import pallas as pl
from jax.experimental.pallas import tpu as pltpu

D_MODEL = 1024
GRID_W = 64
RMS_EPS = 1e-6
N_BRANCH = 3
BRANCH_W = 512
ATT_HEADS = 8
ATT_KV_HEADS = 2
ATT_HEAD_DIM = 64
ATT_GROUP = ATT_HEADS // ATT_KV_HEADS
Q_BLOCK = 128
ROPE_THETA = 10000.0
SSD_HEADS = 8
SSD_HEAD_DIM = 64
SSD_INNER = SSD_HEADS * SSD_HEAD_DIM
SSD_GROUPS = 2
SSD_STATE = 64
SSD_XBC = SSD_INNER + 2 * SSD_GROUPS * SSD_STATE
SSD_CONV = 5
SSD_CHUNK = 128
ML_HEADS = 4
ML_HEAD_DIM = 128
ML_INNER = ML_HEADS * ML_HEAD_DIM
ML_CHUNK = 128
PEER_HEADS = 8
PEER_KEYS = 128
PEER_EXPERTS = PEER_KEYS * PEER_KEYS
PEER_TOPK = 16
PEER_QDIM = 256
PEER_HALF = PEER_QDIM // 2
IN_SPLITS = (ATT_HEADS * ATT_HEAD_DIM, ATT_KV_HEADS * ATT_HEAD_DIM, ATT_KV_HEADS * ATT_HEAD_DIM,
             SSD_INNER, SSD_XBC, 2 * SSD_HEADS,
             ML_INNER, ML_INNER, ML_INNER, ML_INNER, 4 * ML_HEADS,
             N_BRANCH * D_MODEL)

LANES = 128
SUBLANES = 8
VMEM_LIMIT_BYTES = 56 * 1024 * 1024

ROUTER_TOKENS = 256
ROUTER_CHUNK = LANES
PEER_TOKENS = 512
PEER_ROWS = 8
PEER_EBLK = PEER_ROWS * PEER_KEYS

NEG_INF = float("-inf")


def _top_values(work, n):
    rows = work.shape[0]
    iota = lax.broadcasted_iota(jnp.int32, work.shape, 0)
    vals = []
    for _ in range(n):
        m = jnp.max(work, axis=0, keepdims=True)
        vals.append(m)
        first = jnp.min(jnp.where(work == m, iota, rows), axis=0, keepdims=True)
        work = jnp.where(iota == first, NEG_INF, work)
    return vals


def _router_kernel(h_ref, wq_ref, sk_ref, s1_ref, e1_ref, s2_ref, e2_ref, tau_ref, q_scr, cand_scr):
    q_scr[...] = jnp.dot(h_ref[...], wq_ref[...], preferred_element_type=jnp.float32,
                         precision=lax.Precision.HIGHEST)
    nt = (((1,), (1,)), ((), ()))
    for h in range(PEER_HEADS):
        q1 = q_scr[:, (2 * h) * PEER_HALF:(2 * h + 1) * PEER_HALF]
        q2 = q_scr[:, (2 * h + 1) * PEER_HALF:(2 * h + 2) * PEER_HALF]
        s1 = lax.dot_general(sk_ref[2 * h], q1, nt, preferred_element_type=jnp.float32,
                             precision=lax.Precision.HIGHEST)
        s2 = lax.dot_general(sk_ref[2 * h + 1], q2, nt, preferred_element_type=jnp.float32,
                             precision=lax.Precision.HIGHEST)
        s1_ref[h] = s1
        s2_ref[h] = s2
        for c in range(ROUTER_TOKENS // ROUTER_CHUNK):
            cs = slice(c * ROUTER_CHUNK, (c + 1) * ROUTER_CHUNK)
            s1c = s1[:, cs]
            s2c = s2[:, cs]
            top1 = _top_values(s1c, PEER_TOPK)
            top2 = _top_values(s2c, PEER_TOPK)
            b_mat = jnp.concatenate(top2, axis=0)
            for i in range(PEER_TOPK):
                cand_scr[i * PEER_TOPK:(i + 1) * PEER_TOPK, :] = top1[i] + b_mat
            cand = cand_scr[...]
            tau = _top_values(cand, PEER_TOPK)[-1]
            m1, m2 = top1[0], top2[0]
            z = jnp.sum(jnp.where(cand >= tau, jnp.exp(cand - (m1 + m2)), 0.0), axis=0, keepdims=True)
            e1_ref[h, :, cs] = jnp.exp(s1c - m1) / z
            e2_ref[h, :, cs] = jnp.exp(s2c - m2)
            tau_ref[h, :, cs] = tau


def _peer_router(h2, wq, subkeys):
    n_tok = h2.shape[0]
    sk = subkeys.reshape(PEER_HEADS * 2, PEER_KEYS, PEER_HALF)
    big = jax.ShapeDtypeStruct((PEER_HEADS, PEER_KEYS, n_tok), jnp.float32)
    big_spec = pl.BlockSpec((PEER_HEADS, PEER_KEYS, ROUTER_TOKENS), lambda i: (0, 0, i))
    return pl.pallas_call(
        _router_kernel,
        grid=(n_tok // ROUTER_TOKENS,),
        in_specs=[
            pl.BlockSpec((ROUTER_TOKENS, D_MODEL), lambda i: (i, 0)),
            pl.BlockSpec((D_MODEL, PEER_HEADS * PEER_QDIM), lambda i: (0, 0)),
            pl.BlockSpec((PEER_HEADS * 2, PEER_KEYS, PEER_HALF), lambda i: (0, 0, 0)),
        ],
        out_specs=[big_spec, big_spec, big_spec, big_spec,
                   pl.BlockSpec((PEER_HEADS, 1, ROUTER_TOKENS), lambda i: (0, 0, i))],
        out_shape=[big, big, big, big,
                   jax.ShapeDtypeStruct((PEER_HEADS, 1, n_tok), jnp.float32)],
        scratch_shapes=[pltpu.VMEM((ROUTER_TOKENS, PEER_HEADS * PEER_QDIM), jnp.float32),
                        pltpu.VMEM((PEER_TOPK * PEER_TOPK, ROUTER_CHUNK), jnp.float32)],
        compiler_params=pltpu.CompilerParams(dimension_semantics=("arbitrary",),
                                             vmem_limit_bytes=VMEM_LIMIT_BYTES),
        name="peer_router",
    )(h2, wq, sk)


def _gelu_tanh(x):
    c = math.sqrt(2.0 / math.pi)
    return x * (0.5 * (1.0 + jnp.tanh(c * (x + 0.044715 * (x * x * x)))))


def _peer_dense_kernel(h_ref, u_ref, vt_ref, s1_ref, e1_ref, s2_ref, e2_ref, tau_ref, o_ref,
                       acc_scr, p_scr):
    j = pl.program_id(1)

    @pl.when(j == 0)
    def _():
        acc_scr[...] = jnp.zeros_like(acc_scr)

    nt = (((1,), (1,)), ((), ()))
    pre = lax.dot_general(u_ref[...], h_ref[...], nt, preferred_element_type=jnp.float32)
    for a in range(PEER_ROWS):
        rs = slice(a * PEER_KEYS, (a + 1) * PEER_KEYS)
        for c in range(PEER_TOKENS // LANES):
            cs = slice(c * LANES, (c + 1) * LANES)
            w = jnp.zeros((PEER_KEYS, LANES), jnp.float32)
            for h in range(PEER_HEADS):
                tot = s1_ref[h, a:a + 1, cs] + s2_ref[h, :, cs]
                gate = e1_ref[h, a:a + 1, cs] * e2_ref[h, :, cs]
                w = w + jnp.where(tot >= tau_ref[h, :, cs], gate, 0.0)
            p_scr[rs, cs] = (w * _gelu_tanh(pre[rs, cs])).astype(jnp.bfloat16)
    acc_scr[...] += jnp.dot(vt_ref[...], p_scr[...], preferred_element_type=jnp.float32)

    @pl.when(j == pl.num_programs(1) - 1)
    def _():
        o_ref[...] = acc_scr[...].T


def _peer_dense(h2_bf16, u_bf16, vt_bf16, s1, e1, s2, e2, tau):
    n_tok = h2_bf16.shape[0]
    row_spec = pl.BlockSpec((PEER_HEADS, PEER_ROWS, PEER_TOKENS), lambda i, j: (0, j, i))
    key_spec = pl.BlockSpec((PEER_HEADS, PEER_KEYS, PEER_TOKENS), lambda i, j: (0, 0, i))
    return pl.pallas_call(
        _peer_dense_kernel,
        grid=(n_tok // PEER_TOKENS, PEER_EXPERTS // PEER_EBLK),
        in_specs=[
            pl.BlockSpec((PEER_TOKENS, D_MODEL), lambda i, j: (i, 0)),
            pl.BlockSpec((PEER_EBLK, D_MODEL), lambda i, j: (j, 0)),
            pl.BlockSpec((D_MODEL, PEER_EBLK), lambda i, j: (0, j)),
            row_spec, row_spec, key_spec, key_spec,
            pl.BlockSpec((PEER_HEADS, 1, PEER_TOKENS), lambda i, j: (0, 0, i)),
        ],
        out_specs=pl.BlockSpec((PEER_TOKENS, D_MODEL), lambda i, j: (i, 0)),
        out_shape=jax.ShapeDtypeStruct((n_tok, D_MODEL), jnp.float32),
        scratch_shapes=[pltpu.VMEM((D_MODEL, PEER_TOKENS), jnp.float32),
                        pltpu.VMEM((PEER_EBLK, PEER_TOKENS), jnp.bfloat16)],
        compiler_params=pltpu.CompilerParams(dimension_semantics=("arbitrary", "arbitrary"),
                                             vmem_limit_bytes=VMEM_LIMIT_BYTES),
        name="peer_dense",
    )(h2_bf16, u_bf16, vt_bf16, s1, e1, s2, e2, tau)


def _peer(h2, wq, subkeys, u_bf16, vt_bf16):
    s1, e1, s2, e2, tau = _peer_router(h2, wq, subkeys)
    return _peer_dense(h2.astype(jnp.bfloat16), u_bf16, vt_bf16, s1, e1, s2, e2, tau)


def _rmsnorm(x, g):
    xf = x.astype(jnp.float32)
    y = xf * lax.rsqrt(jnp.mean(xf * xf, axis=-1, keepdims=True) + RMS_EPS)
    return (y * g.astype(jnp.float32)).astype(x.dtype)


def _split_cols(p):
    bounds = np.cumsum(IN_SPLITS)[:-1].tolist()
    return jnp.split(p, bounds, axis=-1)


def _heads(t, n):
    return t.reshape(t.shape[0], t.shape[1], n, -1)


def _axial_rope(rows):
    n_freq = ATT_HEAD_DIM // 4
    inv = ROPE_THETA ** (-jnp.arange(n_freq, dtype=jnp.float32) / n_freq)
    row = jnp.broadcast_to(jnp.arange(rows, dtype=jnp.float32)[:, None], (rows, GRID_W)).reshape(-1)
    col = jnp.broadcast_to(jnp.arange(GRID_W, dtype=jnp.float32)[None, :], (rows, GRID_W)).reshape(-1)
    ang = jnp.concatenate([row[:, None] * inv, col[:, None] * inv], axis=-1)
    return jnp.cos(ang), jnp.sin(ang)


def _apply_rope(t, cos, sin):
    half = t.shape[-1] // 2
    tf = t.astype(jnp.float32)
    t1, t2 = tf[..., :half], tf[..., half:]
    c, s = cos[:, None, :], sin[:, None, :]
    return jnp.concatenate([t1 * c - t2 * s, t1 * s + t2 * c], axis=-1).astype(t.dtype)


def _latent_attention(q, k, v, kc, vc):
    Bsz, S = q.shape[0], q.shape[1]
    kk = jnp.concatenate([k, kc], axis=1)
    vv = jnp.concatenate([v, vc], axis=1)
    nb = S // Q_BLOCK
    qb = q.reshape(Bsz, nb, Q_BLOCK, ATT_KV_HEADS, ATT_GROUP, ATT_HEAD_DIM).transpose(1, 0, 2, 3, 4, 5)
    scale = ATT_HEAD_DIM ** -0.5

    def block(qblk):
        s = jnp.einsum('bqkgd,bmkd->bkgqm', qblk, kk).astype(jnp.float32) * scale
        p = jax.nn.softmax(s, axis=-1).astype(vv.dtype)
        return jnp.einsum('bkgqm,bmkd->bqkgd', p, vv)

    o = lax.map(block, qb)
    return o.transpose(1, 0, 2, 3, 4, 5).reshape(Bsz, S, ATT_HEADS * ATT_HEAD_DIM)


def _context_attention(qc, kc, vc):
    Bsz, L = qc.shape[0], qc.shape[1]
    qg = qc.reshape(Bsz, L, ATT_KV_HEADS, ATT_GROUP, ATT_HEAD_DIM)
    s = jnp.einsum('blkgd,bmkd->bkglm', qg, kc).astype(jnp.float32) * ATT_HEAD_DIM ** -0.5
    p = jax.nn.softmax(s, axis=-1).astype(vc.dtype)
    o = jnp.einsum('bkglm,bmkd->blkgd', p, vc)
    return o.reshape(Bsz, L, ATT_HEADS * ATT_HEAD_DIM)


def _dwconv(x, w, b):
    out = lax.conv_general_dilated(x, w[:, None, :], window_strides=(1,), padding='SAME',
                                   dimension_numbers=('NWC', 'WIO', 'NWC'),
                                   feature_group_count=x.shape[-1])
    return out + b


def _segsum(a):
    T = a.shape[-1]
    xa = jnp.broadcast_to(a[..., None], a.shape + (T,))
    xa = jnp.where(jnp.tril(jnp.ones((T, T), bool), -1), xa, 0.0)
    cs = jnp.cumsum(xa, axis=-2)
    return jnp.where(jnp.tril(jnp.ones((T, T), bool)), cs, -jnp.inf)


def _ssd_scan(x, dt, A, Bm, Cm, init, with_output):
    Bsz, T, H, P = x.shape
    G, N = Bm.shape[2], Bm.shape[3]
    R = H // G
    Q = SSD_CHUNK
    nc = T // Q
    a = dt * A
    xc = (x * dt[..., None]).reshape(Bsz, nc, Q, G, R, P)
    ac = a.reshape(Bsz, nc, Q, G, R).transpose(0, 3, 4, 1, 2)
    Bc = Bm.reshape(Bsz, nc, Q, G, N)
    Cc = Cm.reshape(Bsz, nc, Q, G, N)
    a_cum = jnp.cumsum(ac, axis=-1)
    decay_states = jnp.exp(a_cum[..., -1:] - a_cum)
    states = jnp.einsum('bcqgn,bgrcq,bcqgrp->bcgrpn', Bc, decay_states, xc)
    init_c = init.reshape(Bsz, G, R, P, N)[:, None]
    states = jnp.concatenate([init_c, states.astype(init_c.dtype)], axis=1)
    chunk_decay = jnp.exp(_segsum(jnp.pad(a_cum[..., -1], [(0, 0)] * 3 + [(1, 0)])))
    new_states = jnp.einsum('bgrzc,bcgrpn->bzgrpn', chunk_decay, states)
    start_states = new_states[:, :-1]
    final = new_states[:, -1].reshape(Bsz, H, P, N)
    if not with_output:
        return None, final
    Lmat = jnp.exp(_segsum(ac))
    y_diag = jnp.einsum('bcqgn,bcsgn,bgrcqs,bcsgrp->bcqgrp', Cc, Bc, Lmat, xc)
    y_off = jnp.einsum('bcqgn,bcgrpn,bgrcq->bcqgrp', Cc, start_states, jnp.exp(a_cum))
    return (y_diag + y_off).reshape(Bsz, T, H, P), final


def _ssd_zero_state(Bsz):
    z = jnp.zeros((Bsz, SSD_HEADS, SSD_HEAD_DIM, SSD_STATE), jnp.float32)
    return (z, z)


def _ssd_mixer(z, xbc, dt_raw, conv_w, conv_b, dt_bias, a_log, d_skip, norm_g, inits, with_output):
    Bsz, T, _ = xbc.shape
    xbc = jax.nn.silu(_dwconv(xbc, conv_w, conv_b))
    xs, Bm, Cm = jnp.split(xbc, [SSD_INNER, SSD_INNER + SSD_GROUPS * SSD_STATE], axis=-1)
    xs = xs.reshape(Bsz, T, SSD_HEADS, SSD_HEAD_DIM)
    Bm = Bm.reshape(Bsz, T, SSD_GROUPS, SSD_STATE)
    Cm = Cm.reshape(Bsz, T, SSD_GROUPS, SSD_STATE)
    dt = jax.nn.softplus(dt_raw.reshape(Bsz, T, 2, SSD_HEADS).astype(jnp.float32) + dt_bias.astype(jnp.float32))
    A = -jnp.exp(a_log.astype(jnp.float32))
    fl = lambda t: jnp.flip(t, axis=1)
    y_f, fin_f = _ssd_scan(xs, dt[:, :, 0], A[0], Bm, Cm, inits[0], with_output)
    y_b, fin_b = _ssd_scan(fl(xs), fl(dt[:, :, 1]), A[1], fl(Bm), fl(Cm), inits[1], with_output)
    if not with_output:
        return None, (fin_f, fin_b)
    y = y_f + fl(y_b) + xs.astype(jnp.float32) * d_skip.astype(jnp.float32)[:, None]
    y = y.reshape(Bsz, T, SSD_INNER).astype(z.dtype) * jax.nn.silu(z)
    return _rmsnorm(y, norm_g), (fin_f, fin_b)


def _mlstm_scan(q, k, v, i_pre, logf, init, with_output):
    Bsz, T, H, Dh = q.shape
    Q = ML_CHUNK
    nc = T // Q
    qc = q.reshape(Bsz, nc, Q, H, Dh)
    kc = k.reshape(Bsz, nc, Q, H, Dh)
    vc = v.reshape(Bsz, nc, Q, H, Dh)
    ic = i_pre.reshape(Bsz, nc, Q, H).transpose(0, 3, 1, 2)
    b = jnp.cumsum(logf.reshape(Bsz, nc, Q, H).transpose(0, 3, 1, 2), axis=-1)
    b_last = b[..., -1]
    g = b_last[..., None] - b + ic
    m_loc = jnp.max(g, axis=-1)
    w = jnp.exp(g - m_loc[..., None])
    C_loc = jnp.einsum('bhcs,bcshd,bcshe->bchde', w, vc, kc)
    n_loc = jnp.einsum('bhcs,bcshe->bche', w, kc)

    def step(carry, inp):
        C, n, m = carry
        Cl, nl, ml, bl = inp
        m_new = jnp.maximum(bl + m, ml)
        a_prev = jnp.exp(bl + m - m_new)
        a_loc = jnp.exp(ml - m_new)
        C_new = a_prev[..., None, None] * C + a_loc[..., None, None] * Cl
        n_new = a_prev[..., None] * n + a_loc[..., None] * nl
        return (C_new, n_new, m_new), (C, n, m)

    xs = (C_loc.transpose(1, 0, 2, 3, 4), n_loc.transpose(1, 0, 2, 3),
          m_loc.transpose(2, 0, 1), b_last.transpose(2, 0, 1))
    final, (C0, n0, m0) = lax.scan(step, init, xs)
    if not with_output:
        return None, final
    m0 = m0.transpose(1, 2, 0)
    Dm = b[..., :, None] - b[..., None, :] + ic[..., None, :]
    Dm = jnp.where(jnp.tril(jnp.ones((Q, Q), bool)), Dm, -jnp.inf)
    inter = b + m0[..., None]
    m_t = jnp.maximum(inter, jnp.max(Dm, axis=-1))
    Wts = jnp.exp(Dm - m_t[..., None])
    w_inter = jnp.exp(inter - m_t)
    s = jnp.einsum('bcqhd,bcshd->bhcqs', qc, kc) * Wts
    num = (jnp.einsum('bhcqs,bcshd->bcqhd', s, vc)
           + jnp.einsum('bcqhe,cbhde->bcqhd', qc, C0) * w_inter.transpose(0, 2, 3, 1)[..., None])
    den = jnp.sum(s, axis=-1) + w_inter * jnp.einsum('bcqhe,cbhe->bhcq', qc, n0)
    denom = jnp.maximum(jnp.abs(den), jnp.exp(-m_t)).transpose(0, 2, 3, 1)[..., None]
    return (num / denom).reshape(Bsz, T, H, Dh), final


def _mlstm_zero_state(Bsz):
    st = (jnp.zeros((Bsz, ML_HEADS, ML_HEAD_DIM, ML_HEAD_DIM), jnp.float32),
          jnp.zeros((Bsz, ML_HEADS, ML_HEAD_DIM), jnp.float32),
          jnp.zeros((Bsz, ML_HEADS), jnp.float32))
    return (st, st)


def _mlstm_mixer(q, k, v, o_pre, if_pre, gate_bias, norm_g, inits, with_output):
    Bsz, T, _ = q.shape
    out_dtype = q.dtype
    qh = _heads(q, ML_HEADS).astype(jnp.float32)
    kh = _heads(k, ML_HEADS).astype(jnp.float32) * ML_HEAD_DIM ** -0.5
    vh = _heads(v, ML_HEADS).astype(jnp.float32)
    gates = if_pre.reshape(Bsz, T, 2, 2, ML_HEADS).astype(jnp.float32) + gate_bias.astype(jnp.float32)
    i_pre = gates[:, :, :, 0]
    logf = jax.nn.log_sigmoid(gates[:, :, :, 1])
    fl = lambda t: jnp.flip(t, axis=1)
    h_f, fin_f = _mlstm_scan(qh, kh, vh, i_pre[:, :, 0], logf[:, :, 0], inits[0], with_output)
    h_b, fin_b = _mlstm_scan(fl(qh), fl(kh), fl(vh), fl(i_pre[:, :, 1]), fl(logf[:, :, 1]), inits[1], with_output)
    if not with_output:
        return None, (fin_f, fin_b)
    h = _rmsnorm(h_f + fl(h_b), norm_g).reshape(Bsz, T, ML_INNER)
    h = h * jax.nn.sigmoid(o_pre.astype(jnp.float32))
    return h.astype(out_dtype), (fin_f, fin_b)


def _merge_branches(branches, gate_pre, w_branch, w_out):
    Bsz, T, _ = gate_pre.shape
    stacked = jnp.stack(branches, axis=2)
    proj = jnp.einsum('btrw,rwd->btrd', stacked, w_branch)
    gates = jax.nn.sigmoid(gate_pre.reshape(Bsz, T, N_BRANCH, D_MODEL))
    return jnp.sum(gates * proj, axis=2) @ w_out


def _token_mixer(h, hc, w_in, qn, kn, conv_w, conv_b, dt_bias, a_log, d_skip, ssd_norm_g,
                 ml_bias, ml_norm_g, w_branch, w_out, cos, sin, ctx_out):
    Bsz = h.shape[0]
    aq, ak, av, sz, sxbc, sdt, mq, mk, mv, mo, mif, gpre = _split_cols(h @ w_in)
    caq, cak, cav, csz, csxbc, csdt, cmq, cmk, cmv, cmo, cmif, cgpre = _split_cols(hc @ w_in)
    q = _apply_rope(_rmsnorm(_heads(aq, ATT_HEADS), qn), cos, sin)
    k = _apply_rope(_rmsnorm(_heads(ak, ATT_KV_HEADS), kn), cos, sin)
    kc = _rmsnorm(_heads(cak, ATT_KV_HEADS), kn)
    vc = _heads(cav, ATT_KV_HEADS)
    att = _latent_attention(q, k, _heads(av, ATT_KV_HEADS), kc, vc)
    ssd_c, ssd_fin = _ssd_mixer(csz, csxbc, csdt, conv_w, conv_b, dt_bias, a_log, d_skip, ssd_norm_g,
                                _ssd_zero_state(Bsz), ctx_out)
    ssd_l, _ = _ssd_mixer(sz, sxbc, sdt, conv_w, conv_b, dt_bias, a_log, d_skip, ssd_norm_g, ssd_fin, True)
    ml_c, ml_fin = _mlstm_mixer(cmq, cmk, cmv, cmo, cmif, ml_bias, ml_norm_g, _mlstm_zero_state(Bsz), ctx_out)
    ml_l, _ = _mlstm_mixer(mq, mk, mv, mo, mif, ml_bias, ml_norm_g, ml_fin, True)
    y = _merge_branches((att, ssd_l, ml_l), gpre, w_branch, w_out)
    if not ctx_out:
        return y, None
    att_c = _context_attention(_rmsnorm(_heads(caq, ATT_HEADS), qn), kc, vc)
    yc = _merge_branches((att_c, ssd_c, ml_c), cgpre, w_branch, w_out)
    return y, yc


def kernel(x, c, ctx, c_ctx, norm1_g, norm2_g, w_mod, b_mod, w_in, att_qnorm, att_knorm, ssd_conv_w, ssd_conv_b, ssd_dt_bias, ssd_a_log, ssd_d, ssd_norm, ml_gate_bias, ml_norm, w_branch, w_out, peer_wq, peer_subkeys, peer_u, peer_v):
    Bsz, S, _ = x.shape
    depth = w_in.shape[0]
    cos, sin = _axial_rope(S // GRID_W)
    silu_c = jax.nn.silu(c)
    silu_cc = jax.nn.silu(c_ctx)
    xc = ctx
    for l in range(depth):
        last = l == depth - 1
        sh1, sc1, g1, sh2, sc2, g2 = jnp.split((silu_c @ w_mod[l] + b_mod[l])[:, None, :], 6, axis=-1)
        csh1, csc1, cg1, csh2, csc2, cg2 = jnp.split(silu_cc @ w_mod[l] + b_mod[l], 6, axis=-1)
        h = _rmsnorm(x, norm1_g[l]) * (1.0 + sc1) + sh1
        hc = _rmsnorm(xc, norm1_g[l]) * (1.0 + csc1) + csh1
        y, yc = _token_mixer(h, hc, w_in[l], att_qnorm[l], att_knorm[l], ssd_conv_w[l], ssd_conv_b[l],
                             ssd_dt_bias[l], ssd_a_log[l], ssd_d[l], ssd_norm[l], ml_gate_bias[l], ml_norm[l],
                             w_branch[l], w_out[l], cos, sin, not last)
        x = x + g1 * y
        h2 = _rmsnorm(x, norm2_g[l]) * (1.0 + sc2) + sh2
        u_bf16 = peer_u[l].astype(jnp.bfloat16)
        vt_bf16 = peer_v[l].T.astype(jnp.bfloat16)
        tok = h2.reshape(-1, D_MODEL)
        if not last:
            xc = xc + cg1 * yc
            hc2 = _rmsnorm(xc, norm2_g[l]) * (1.0 + csc2) + csh2
            tok = jnp.concatenate([tok, hc2.reshape(-1, D_MODEL)], axis=0)
        po = _peer(tok, peer_wq[l], peer_subkeys[l], u_bf16, vt_bf16)
        x = x + g2 * po[:Bsz * S].reshape(x.shape)
        if not last:
            xc = xc + cg2 * po[Bsz * S:].reshape(xc.shape)
    return x
```

```python
import functools
import math

import jax
import jax.numpy as jnp
import numpy as np
from jax import lax
from jax.experimental import pallas as pl
from jax.experimental.pallas import tpu as pltpu

D_MODEL = 1024
GRID_W = 64
RMS_EPS = 1e-6
N_BRANCH = 3
BRANCH_W = 512
ATT_HEADS = 8
ATT_KV_HEADS = 2
ATT_HEAD_DIM = 64
ATT_GROUP = ATT_HEADS // ATT_KV_HEADS
Q_BLOCK = 128
ROPE_THETA = 10000.0
SSD_HEADS = 8
SSD_HEAD_DIM = 64
SSD_INNER = SSD_HEADS * SSD_HEAD_DIM
SSD_GROUPS = 2
SSD_STATE = 64
SSD_XBC = SSD_INNER + 2 * SSD_GROUPS * SSD_STATE
SSD_CONV = 5
SSD_CHUNK = 128
ML_HEADS = 4
ML_HEAD_DIM = 128
ML_INNER = ML_HEADS * ML_HEAD_DIM
ML_CHUNK = 128
PEER_HEADS = 8
PEER_KEYS = 128
PEER_EXPERTS = PEER_KEYS * PEER_KEYS
PEER_TOPK = 16
PEER_QDIM = 256
PEER_HALF = PEER_QDIM // 2
IN_SPLITS = (ATT_HEADS * ATT_HEAD_DIM, ATT_KV_HEADS * ATT_HEAD_DIM, ATT_KV_HEADS * ATT_HEAD_DIM,
             SSD_INNER, SSD_XBC, 2 * SSD_HEADS,
             ML_INNER, ML_INNER, ML_INNER, ML_INNER, 4 * ML_HEADS,
             N_BRANCH * D_MODEL)

LANES = 128
SUBLANES = 8
VMEM_LIMIT_BYTES = 56 * 1024 * 1024

ROUTER_TOKENS = 256
ROUTER_CHUNK = LANES
PEER_TOKENS = 512
PEER_ROWS = 8
PEER_EBLK = PEER_ROWS * PEER_KEYS
PEER_GROUPS = 2

NEG_INF = float("-inf")


def _sort_network(n):
    def merge(lo, hi, r):
        step = r * 2
        if step < hi - lo:
            yield from merge(lo, hi, step)
            yield from merge(lo + r, hi, step)
            yield from [(i, i + r) for i in range(lo + r, hi - r, step)]
        else:
            yield (lo, lo + r)

    def sort(lo, hi):
        if hi - lo >= 1:
            mid = lo + (hi - lo) // 2
            yield from sort(lo, mid)
            yield from sort(mid + 1, hi)
            yield from merge(lo, hi, 1)

    return tuple(sort(0, n - 1))


_SORT16 = _sort_network(PEER_TOPK)


def _merge_top(lists, n):
    lists = list(lists)
    depth = len(lists)
    sub = lax.broadcasted_iota(jnp.int32, lists[0].shape, 0)
    tops = []
    for i in range(n):
        head = lists[0]
        m = jnp.max(head, axis=0, keepdims=True)
        tops.append(m)
        live = min(depth, n - 1 - i)
        if live == 0:
            break
        first = jnp.min(jnp.where(head == m, sub, SUBLANES), axis=0, keepdims=True)
        pop = sub == first
        for r in range(live):
            nxt = lists[r + 1] if r + 1 < depth else NEG_INF
            lists[r] = jnp.where(pop, nxt, lists[r])
    return tops


def _sorted_top(s, n):
    xs = [s[r * SUBLANES:(r + 1) * SUBLANES, :] for r in range(s.shape[0] // SUBLANES)]
    for i, j in _SORT16:
        xs[i], xs[j] = jnp.maximum(xs[i], xs[j]), jnp.minimum(xs[i], xs[j])
    return _merge_top(xs, n)


def _rows_to_sublanes(rows, first_sublane, shape):
    sub = lax.broadcasted_iota(jnp.int32, shape, 0)
    out = jnp.zeros(shape, jnp.float32)
    for k, row in enumerate(rows):
        out = jnp.where(sub == first_sublane + k, row, out)
    return out


def _dup_bf16(x):
    bits = pltpu.bitcast(x.astype(jnp.bfloat16).astype(jnp.float32), jnp.uint32)
    return bits | (bits >> 16)


_CAND_LEN = (16, 8, 5, 4, 12, 4, 1, 0)


def _router_kernel(h_ref, wq_ref, sk_ref, rank_ref, e2_ref, cnt_ref, e1_ref, q_scr):
    q_scr[...] = jnp.dot(h_ref[...], wq_ref[...], preferred_element_type=jnp.float32).astype(jnp.bfloat16)
    nt = (((1,), (1,)), ((), ()))
    shape8 = (SUBLANES, ROUTER_CHUNK)
    sub = lax.broadcasted_iota(jnp.int32, shape8, 0)
    cand_len = jnp.zeros(shape8, jnp.int32)
    for g, n in enumerate(_CAND_LEN):
        cand_len = jnp.where(sub == g, n, cand_len)
    for h in range(PEER_HEADS):
        q1 = q_scr[:, (2 * h) * PEER_HALF:(2 * h + 1) * PEER_HALF]
        q2 = q_scr[:, (2 * h + 1) * PEER_HALF:(2 * h + 2) * PEER_HALF]
        s1 = lax.dot_general(sk_ref[2 * h], q1, nt, preferred_element_type=jnp.float32)
        s2 = lax.dot_general(sk_ref[2 * h + 1], q2, nt, preferred_element_type=jnp.float32)
        for c in range(ROUTER_TOKENS // ROUTER_CHUNK):
            cs = slice(c * ROUTER_CHUNK, (c + 1) * ROUTER_CHUNK)
            s1c = s1[:, cs]
            s2c = s2[:, cs]
            top1 = _sorted_top(s1c, PEER_TOPK)
            top2 = _sorted_top(s2c, PEER_TOPK)
            a_lo = _rows_to_sublanes(top1[:4], 0, shape8)
            b_lo = _rows_to_sublanes(top2[:3], 4, shape8)
            cands = []
            for r in range(PEER_TOPK):
                by_a = a_lo + top2[r]
                by_b = (top1[4 + r] + b_lo) if 4 + r < PEER_TOPK else by_a
                cands.append(jnp.where(r < cand_len, jnp.where(sub < 4, by_a, by_b), NEG_INF))
            tau = _merge_top(cands, PEER_TOPK)[-1]
            m1, m2 = top1[0], top2[0]
            z = jnp.zeros(shape8, jnp.float32)
            for cand in cands:
                z = z + jnp.where(cand >= tau, jnp.exp(cand - (m1 + m2)), 0.0)
            z = jnp.sum(z, axis=0, keepdims=True)
            rank = jnp.zeros(s2c.shape, jnp.float32)
            for t in top2:
                rank = rank + jnp.where(t > s2c, 1.0, 0.0)
            cnt = jnp.zeros(s1c.shape, jnp.float32)
            for t in top2:
                cnt = cnt + jnp.where(s1c + t >= tau, 1.0, 0.0)
            rank_ref[h, :, cs] = rank.astype(jnp.bfloat16)
            e2_ref[h, :, cs] = jnp.exp(s2c - m2).astype(jnp.bfloat16)
            cnt_ref[h, :, cs] = _dup_bf16(cnt)
            e1_ref[h, :, cs] = _dup_bf16(jnp.exp(s1c - m1) * (0.5 / z))


def _peer_router(h2_bf16, wq, subkeys):
    n_tok = h2_bf16.shape[0]
    sk = subkeys.reshape(PEER_HEADS * 2, PEER_KEYS, PEER_HALF)
    shape = (PEER_HEADS, PEER_KEYS, n_tok)
    spec = pl.BlockSpec((PEER_HEADS, PEER_KEYS, ROUTER_TOKENS), lambda i: (0, 0, i))
    return pl.pallas_call(
        _router_kernel,
        grid=(n_tok // ROUTER_TOKENS,),
        in_specs=[
            pl.BlockSpec((ROUTER_TOKENS, D_MODEL), lambda i: (i, 0)),
            pl.BlockSpec((D_MODEL, PEER_HEADS * PEER_QDIM), lambda i: (0, 0)),
            pl.BlockSpec((PEER_HEADS * 2, PEER_KEYS, PEER_HALF), lambda i: (0, 0, 0)),
        ],
        out_specs=[spec, spec, spec, spec],
        out_shape=[jax.ShapeDtypeStruct(shape, jnp.bfloat16), jax.ShapeDtypeStruct(shape, jnp.bfloat16),
                   jax.ShapeDtypeStruct(shape, jnp.uint32), jax.ShapeDtypeStruct(shape, jnp.uint32)],
        scratch_shapes=[pltpu.VMEM((ROUTER_TOKENS, PEER_HEADS * PEER_QDIM), jnp.bfloat16)],
        compiler_params=pltpu.CompilerParams(dimension_semantics=("arbitrary",),
                                             vmem_limit_bytes=VMEM_LIMIT_BYTES),
        name="peer_router",
    )(h2_bf16, wq.astype(jnp.bfloat16), sk.astype(jnp.bfloat16))


BF16_ROWS = 2 * SUBLANES
GELU_C0 = math.sqrt(2.0 / math.pi)
GELU_C1 = 0.044715 * GELU_C0


def _peer_dense_kernel(h_ref, u_ref, vt_ref, rank_ref, e2_ref, cnt_ref, e1_ref, o_ref,
                       acc_scr, pre_scr, p_scr, rank_scr, e2_scr):
    j = pl.program_id(1)

    @pl.when(j == 0)
    def _():
        acc_scr[...] = jnp.zeros_like(acc_scr)
        rank_scr[...] = rank_ref[...]
        e2_scr[...] = e2_ref[...]

    nt = (((1,), (1,)), ((), ()))
    zero = jnp.zeros((BF16_ROWS, LANES), jnp.bfloat16)
    group = PEER_EBLK // PEER_GROUPS
    for a in range(PEER_ROWS):
        g, first = divmod(a * PEER_KEYS, group)
        gs = slice(g * group, (g + 1) * group)
        if first == 0:
            pre_scr[gs, :] = lax.dot_general(u_ref[gs, :], h_ref[...], nt,
                                             preferred_element_type=jnp.float32)
            if g > 0:
                ps = slice((g - 1) * group, g * group)
                acc_scr[...] += jnp.dot(vt_ref[:, ps], p_scr[ps, :], preferred_element_type=jnp.float32)
        for c in range(PEER_TOKENS // LANES):
            cs = slice(c * LANES, (c + 1) * LANES)
            cnt = [pltpu.bitcast(jnp.broadcast_to(cnt_ref[h, a:a + 1, cs], (SUBLANES, LANES)), jnp.bfloat16)
                   for h in range(PEER_HEADS)]
            e1 = [pltpu.bitcast(jnp.broadcast_to(e1_ref[h, a:a + 1, cs], (SUBLANES, LANES)), jnp.bfloat16)
                  for h in range(PEER_HEADS)]
            for b in range(PEER_KEYS // BF16_ROWS):
                bs = slice(b * BF16_ROWS, (b + 1) * BF16_ROWS)
                rs = slice(a * PEER_KEYS + b * BF16_ROWS, a * PEER_KEYS + (b + 1) * BF16_ROWS)
                w = zero
                for h in range(PEER_HEADS):
                    w = w + jnp.where(rank_scr[h, bs, cs] < cnt[h], e2_scr[h, bs, cs], zero) * e1[h]
                x = pre_scr[rs, cs]
                act = x * (1.0 + jnp.tanh(x * (GELU_C0 + GELU_C1 * (x * x))))
                p_scr[rs, cs] = w * act.astype(jnp.bfloat16)
    ps = slice(PEER_EBLK - group, PEER_EBLK)
    acc_scr[...] += jnp.dot(vt_ref[:, ps], p_scr[ps, :], preferred_element_type=jnp.float32)

    @pl.when(j == pl.num_programs(1) - 1)
    def _():
        o_ref[...] = acc_scr[...].T


def _peer_dense(h2_bf16, u_bf16, vt_bf16, rank, e2, cnt, e1):
    n_tok = h2_bf16.shape[0]
    row_spec = pl.BlockSpec((PEER_HEADS, PEER_ROWS, PEER_TOKENS), lambda i, j: (0, j, i))
    key_spec = pl.BlockSpec((PEER_HEADS, PEER_KEYS, PEER_TOKENS), lambda i, j: (0, 0, i))
    return pl.pallas_call(
        _peer_dense_kernel,
        grid=(n_tok // PEER_TOKENS, PEER_EXPERTS // PEER_EBLK),
        in_specs=[
            pl.BlockSpec((PEER_TOKENS, D_MODEL), lambda i, j: (i, 0)),
            pl.BlockSpec((PEER_EBLK, D_MODEL), lambda i, j: (j, 0)),
            pl.BlockSpec((D_MODEL, PEER_EBLK), lambda i, j: (0, j)),
            key_spec, key_spec, row_spec, row_spec,
        ],
        out_specs=pl.BlockSpec((PEER_TOKENS, D_MODEL), lambda i, j: (i, 0)),
        out_shape=jax.ShapeDtypeStruct((n_tok, D_MODEL), jnp.float32),
        scratch_shapes=[pltpu.VMEM((D_MODEL, PEER_TOKENS), jnp.float32),
                        pltpu.VMEM((PEER_EBLK, PEER_TOKENS), jnp.float32),
                        pltpu.VMEM((PEER_EBLK, PEER_TOKENS), jnp.bfloat16),
                        pltpu.VMEM((PEER_HEADS, PEER_KEYS, PEER_TOKENS), jnp.bfloat16),
                        pltpu.VMEM((PEER_HEADS, PEER_KEYS, PEER_TOKENS), jnp.bfloat16)],
        compiler_params=pltpu.CompilerParams(dimension_semantics=("arbitrary", "arbitrary"),
                                             vmem_limit_bytes=VMEM_LIMIT_BYTES),
        name="peer_dense",
    )(h2_bf16, u_bf16, vt_bf16, rank, e2, cnt, e1)


def _peer(h2, wq, subkeys, u_bf16, vt_bf16):
    h2_bf16 = h2.astype(jnp.bfloat16)
    rank, e2, cnt, e1 = _peer_router(h2_bf16, wq, subkeys)
    return _peer_dense(h2_bf16, u_bf16, vt_bf16, rank, e2, cnt, e1)


ATT_Q_TOKENS = 256
ATT_V_COLS = 2 * ATT_HEAD_DIM


def _attention_kernel(q_ref, kt_ref, v_ref, o_ref, *, kv_chunk):
    tq = q_ref.shape[2]
    rows = ATT_GROUP * tq
    q = q_ref[0].reshape(rows, ATT_HEAD_DIM)
    n_chunks = kt_ref.shape[3] // kv_chunk

    def body(c, carry):
        m, acc = carry
        off = pl.multiple_of(c * kv_chunk, kv_chunk)
        s = jnp.dot(q, kt_ref[0, 0, :, pl.ds(off, kv_chunk)], preferred_element_type=jnp.float32)
        m_new = jnp.maximum(m, jnp.max(s, axis=-1, keepdims=True))
        p = jnp.exp(s - m_new).astype(jnp.bfloat16)
        acc = jnp.exp(m - m_new) * acc + jnp.dot(p, v_ref[0, 0, pl.ds(off, kv_chunk), :],
                                                 preferred_element_type=jnp.float32)
        return m_new, acc

    m0 = jnp.full((rows, 1), NEG_INF, jnp.float32)
    acc0 = jnp.zeros((rows, ATT_V_COLS), jnp.float32)
    _, acc = lax.fori_loop(0, n_chunks, body, (m0, acc0), unroll=True)
    out = acc[:, :ATT_HEAD_DIM] / acc[:, ATT_HEAD_DIM:ATT_HEAD_DIM + 1]
    o_ref[0] = out.reshape(ATT_GROUP, tq, ATT_HEAD_DIM)


def _attention(q, k, v, kv_chunk):
    Bsz, S = q.shape[0], q.shape[1]
    SK = k.shape[1]
    tq = min(ATT_Q_TOKENS, S)
    qh = (q * ATT_HEAD_DIM ** -0.5).astype(jnp.bfloat16).transpose(0, 2, 1, 3)
    kt = k.astype(jnp.bfloat16).transpose(0, 2, 3, 1)
    pad = jnp.concatenate([jnp.ones(v.shape[:-1] + (1,), v.dtype),
                           jnp.zeros(v.shape[:-1] + (ATT_V_COLS - ATT_HEAD_DIM - 1,), v.dtype)], axis=-1)
    vx = jnp.concatenate([v, pad], axis=-1).astype(jnp.bfloat16).transpose(0, 2, 1, 3)
    o = pl.pallas_call(
        functools.partial(_attention_kernel, kv_chunk=kv_chunk),
        grid=(Bsz, ATT_KV_HEADS, S // tq),
        in_specs=[
            pl.BlockSpec((1, ATT_GROUP, tq, ATT_HEAD_DIM), lambda b, g, i: (b, g, i, 0)),
            pl.BlockSpec((1, 1, ATT_HEAD_DIM, SK), lambda b, g, i: (b, g, 0, 0)),
            pl.BlockSpec((1, 1, SK, ATT_V_COLS), lambda b, g, i: (b, g, 0, 0)),
        ],
        out_specs=pl.BlockSpec((1, ATT_GROUP, tq, ATT_HEAD_DIM), lambda b, g, i: (b, g, i, 0)),
        out_shape=jax.ShapeDtypeStruct((Bsz, ATT_HEADS, S, ATT_HEAD_DIM), jnp.float32),
        compiler_params=pltpu.CompilerParams(dimension_semantics=("arbitrary", "arbitrary", "arbitrary"),
                                             vmem_limit_bytes=VMEM_LIMIT_BYTES),
        name="attention",
    )(qh, kt, vx)
    return o.transpose(0, 2, 1, 3).reshape(Bsz, S, ATT_HEADS * ATT_HEAD_DIM)


def _rmsnorm(x, g):
    xf = x.astype(jnp.float32)
    y = xf * lax.rsqrt(jnp.mean(xf * xf, axis=-1, keepdims=True) + RMS_EPS)
    return (y * g.astype(jnp.float32)).astype(x.dtype)


def _split_cols(p):
    bounds = np.cumsum(IN_SPLITS)[:-1].tolist()
    return jnp.split(p, bounds, axis=-1)


def _heads(t, n):
    return t.reshape(t.shape[0], t.shape[1], n, -1)


def _axial_rope(rows):
    n_freq = ATT_HEAD_DIM // 4
    inv = ROPE_THETA ** (-jnp.arange(n_freq, dtype=jnp.float32) / n_freq)
    row = jnp.broadcast_to(jnp.arange(rows, dtype=jnp.float32)[:, None], (rows, GRID_W)).reshape(-1)
    col = jnp.broadcast_to(jnp.arange(GRID_W, dtype=jnp.float32)[None, :], (rows, GRID_W)).reshape(-1)
    ang = jnp.concatenate([row[:, None] * inv, col[:, None] * inv], axis=-1)
    return jnp.cos(ang), jnp.sin(ang)


def _apply_rope(t, cos, sin):
    half = t.shape[-1] // 2
    tf = t.astype(jnp.float32)
    t1, t2 = tf[..., :half], tf[..., half:]
    c, s = cos[:, None, :], sin[:, None, :]
    return jnp.concatenate([t1 * c - t2 * s, t1 * s + t2 * c], axis=-1).astype(t.dtype)


ATT_KV_CHUNK = 768


def _latent_attention(q, k, v, kc, vc):
    kk = jnp.concatenate([k, kc], axis=1)
    vv = jnp.concatenate([v, vc], axis=1)
    assert kk.shape[1] % ATT_KV_CHUNK == 0
    return _attention(q, kk, vv, ATT_KV_CHUNK)


def _context_attention(qc, kc, vc):
    return _attention(qc, kc, vc, kc.shape[1])


def _dwconv(x, w, b):
    out = lax.conv_general_dilated(x, w[:, None, :], window_strides=(1,), padding='SAME',
                                   dimension_numbers=('NWC', 'WIO', 'NWC'),
                                   feature_group_count=x.shape[-1])
    return out + b


def _segsum(a):
    T = a.shape[-1]
    xa = jnp.broadcast_to(a[..., None], a.shape + (T,))
    xa = jnp.where(jnp.tril(jnp.ones((T, T), bool), -1), xa, 0.0)
    cs = jnp.cumsum(xa, axis=-2)
    return jnp.where(jnp.tril(jnp.ones((T, T), bool)), cs, -jnp.inf)


def _ssd_scan(x, dt, A, Bm, Cm, init, with_output):
    Bsz, T, H, P = x.shape
    G, N = Bm.shape[2], Bm.shape[3]
    R = H // G
    Q = SSD_CHUNK
    nc = T // Q
    a = dt * A
    xc = (x * dt[..., None]).reshape(Bsz, nc, Q, G, R, P)
    ac = a.reshape(Bsz, nc, Q, G, R).transpose(0, 3, 4, 1, 2)
    Bc = Bm.reshape(Bsz, nc, Q, G, N)
    Cc = Cm.reshape(Bsz, nc, Q, G, N)
    a_cum = jnp.cumsum(ac, axis=-1)
    decay_states = jnp.exp(a_cum[..., -1:] - a_cum)
    states = jnp.einsum('bcqgn,bgrcq,bcqgrp->bcgrpn', Bc, decay_states, xc)
    init_c = init.reshape(Bsz, G, R, P, N)[:, None]
    states = jnp.concatenate([init_c, states.astype(init_c.dtype)], axis=1)
    chunk_decay = jnp.exp(_segsum(jnp.pad(a_cum[..., -1], [(0, 0)] * 3 + [(1, 0)])))
    new_states = jnp.einsum('bgrzc,bcgrpn->bzgrpn', chunk_decay, states)
    start_states = new_states[:, :-1]
    final = new_states[:, -1].reshape(Bsz, H, P, N)
    if not with_output:
        return None, final
    Lmat = jnp.exp(_segsum(ac))
    y_diag = jnp.einsum('bcqgn,bcsgn,bgrcqs,bcsgrp->bcqgrp', Cc, Bc, Lmat, xc)
    y_off = jnp.einsum('bcqgn,bcgrpn,bgrcq->bcqgrp', Cc, start_states, jnp.exp(a_cum))
    return (y_diag + y_off).reshape(Bsz, T, H, P), final


def _ssd_zero_state(Bsz):
    z = jnp.zeros((Bsz, SSD_HEADS, SSD_HEAD_DIM, SSD_STATE), jnp.float32)
    return (z, z)


def _ssd_mixer(z, xbc, dt_raw, conv_w, conv_b, dt_bias, a_log, d_skip, norm_g, inits, with_output):
    Bsz, T, _ = xbc.shape
    xbc = jax.nn.silu(_dwconv(xbc, conv_w, conv_b))
    xs, Bm, Cm = jnp.split(xbc, [SSD_INNER, SSD_INNER + SSD_GROUPS * SSD_STATE], axis=-1)
    xs = xs.reshape(Bsz, T, SSD_HEADS, SSD_HEAD_DIM)
    Bm = Bm.reshape(Bsz, T, SSD_GROUPS, SSD_STATE)
    Cm = Cm.reshape(Bsz, T, SSD_GROUPS, SSD_STATE)
    dt = jax.nn.softplus(dt_raw.reshape(Bsz, T, 2, SSD_HEADS).astype(jnp.float32) + dt_bias.astype(jnp.float32))
    A = -jnp.exp(a_log.astype(jnp.float32))
    fl = lambda t: jnp.flip(t, axis=1)
    y_f, fin_f = _ssd_scan(xs, dt[:, :, 0], A[0], Bm, Cm, inits[0], with_output)
    y_b, fin_b = _ssd_scan(fl(xs), fl(dt[:, :, 1]), A[1], fl(Bm), fl(Cm), inits[1], with_output)
    if not with_output:
        return None, (fin_f, fin_b)
    y = y_f + fl(y_b) + xs.astype(jnp.float32) * d_skip.astype(jnp.float32)[:, None]
    y = y.reshape(Bsz, T, SSD_INNER).astype(z.dtype) * jax.nn.silu(z)
    return _rmsnorm(y, norm_g), (fin_f, fin_b)


def _mlstm_scan(q, k, v, i_pre, logf, init, with_output):
    Bsz, T, H, Dh = q.shape
    Q = ML_CHUNK
    nc = T // Q
    qc = q.reshape(Bsz, nc, Q, H, Dh)
    kc = k.reshape(Bsz, nc, Q, H, Dh)
    vc = v.reshape(Bsz, nc, Q, H, Dh)
    ic = i_pre.reshape(Bsz, nc, Q, H).transpose(0, 3, 1, 2)
    b = jnp.cumsum(logf.reshape(Bsz, nc, Q, H).transpose(0, 3, 1, 2), axis=-1)
    b_last = b[..., -1]
    g = b_last[..., None] - b + ic
    m_loc = jnp.max(g, axis=-1)
    w = jnp.exp(g - m_loc[..., None])
    C_loc = jnp.einsum('bhcs,bcshd,bcshe->bchde', w, vc, kc)
    n_loc = jnp.einsum('bhcs,bcshe->bche', w, kc)

    def step(carry, inp):
        C, n, m = carry
        Cl, nl, ml, bl = inp
        m_new = jnp.maximum(bl + m, ml)
        a_prev = jnp.exp(bl + m - m_new)
        a_loc = jnp.exp(ml - m_new)
        C_new = a_prev[..., None, None] * C + a_loc[..., None, None] * Cl
        n_new = a_prev[..., None] * n + a_loc[..., None] * nl
        return (C_new, n_new, m_new), (C, n, m)

    xs = (C_loc.transpose(1, 0, 2, 3, 4), n_loc.transpose(1, 0, 2, 3),
          m_loc.transpose(2, 0, 1), b_last.transpose(2, 0, 1))
    final, (C0, n0, m0) = lax.scan(step, init, xs)
    if not with_output:
        return None, final
    m0 = m0.transpose(1, 2, 0)
    Dm = b[..., :, None] - b[..., None, :] + ic[..., None, :]
    Dm = jnp.where(jnp.tril(jnp.ones((Q, Q), bool)), Dm, -jnp.inf)
    inter = b + m0[..., None]
    m_t = jnp.maximum(inter, jnp.max(Dm, axis=-1))
    Wts = jnp.exp(Dm - m_t[..., None])
    w_inter = jnp.exp(inter - m_t)
    s = jnp.einsum('bcqhd,bcshd->bhcqs', qc, kc) * Wts
    num = (jnp.einsum('bhcqs,bcshd->bcqhd', s, vc)
           + jnp.einsum('bcqhe,cbhde->bcqhd', qc, C0) * w_inter.transpose(0, 2, 3, 1)[..., None])
    den = jnp.sum(s, axis=-1) + w_inter * jnp.einsum('bcqhe,cbhe->bhcq', qc, n0)
    denom = jnp.maximum(jnp.abs(den), jnp.exp(-m_t)).transpose(0, 2, 3, 1)[..., None]
    return (num / denom).reshape(Bsz, T, H, Dh), final


def _mlstm_zero_state(Bsz):
    st = (jnp.zeros((Bsz, ML_HEADS, ML_HEAD_DIM, ML_HEAD_DIM), jnp.float32),
          jnp.zeros((Bsz, ML_HEADS, ML_HEAD_DIM), jnp.float32),
          jnp.zeros((Bsz, ML_HEADS), jnp.float32))
    return (st, st)


def _mlstm_mixer(q, k, v, o_pre, if_pre, gate_bias, norm_g, inits, with_output):
    Bsz, T, _ = q.shape
    out_dtype = q.dtype
    qh = _heads(q, ML_HEADS).astype(jnp.float32)
    kh = _heads(k, ML_HEADS).astype(jnp.float32) * ML_HEAD_DIM ** -0.5
    vh = _heads(v, ML_HEADS).astype(jnp.float32)
    gates = if_pre.reshape(Bsz, T, 2, 2, ML_HEADS).astype(jnp.float32) + gate_bias.astype(jnp.float32)
    i_pre = gates[:, :, :, 0]
    logf = jax.nn.log_sigmoid(gates[:, :, :, 1])
    fl = lambda t: jnp.flip(t, axis=1)
    h_f, fin_f = _mlstm_scan(qh, kh, vh, i_pre[:, :, 0], logf[:, :, 0], inits[0], with_output)
    h_b, fin_b = _mlstm_scan(fl(qh), fl(kh), fl(vh), fl(i_pre[:, :, 1]), fl(logf[:, :, 1]), inits[1], with_output)
    if not with_output:
        return None, (fin_f, fin_b)
    h = _rmsnorm(h_f + fl(h_b), norm_g).reshape(Bsz, T, ML_INNER)
    h = h * jax.nn.sigmoid(o_pre.astype(jnp.float32))
    return h.astype(out_dtype), (fin_f, fin_b)


def _merge_branches(branches, gate_pre, w_branch, w_out):
    Bsz, T, _ = gate_pre.shape
    stacked = jnp.stack(branches, axis=2)
    proj = jnp.einsum('btrw,rwd->btrd', stacked, w_branch)
    gates = jax.nn.sigmoid(gate_pre.reshape(Bsz, T, N_BRANCH, D_MODEL))
    return jnp.sum(gates * proj, axis=2) @ w_out


def _token_mixer(h, hc, w_in, qn, kn, conv_w, conv_b, dt_bias, a_log, d_skip, ssd_norm_g,
                 ml_bias, ml_norm_g, w_branch, w_out, cos, sin, ctx_out):
    Bsz = h.shape[0]
    aq, ak, av, sz, sxbc, sdt, mq, mk, mv, mo, mif, gpre = _split_cols(h @ w_in)
    caq, cak, cav, csz, csxbc, csdt, cmq, cmk, cmv, cmo, cmif, cgpre = _split_cols(hc @ w_in)
    q = _apply_rope(_rmsnorm(_heads(aq, ATT_HEADS), qn), cos, sin)
    k = _apply_rope(_rmsnorm(_heads(ak, ATT_KV_HEADS), kn), cos, sin)
    kc = _rmsnorm(_heads(cak, ATT_KV_HEADS), kn)
    vc = _heads(cav, ATT_KV_HEADS)
    att = _latent_attention(q, k, _heads(av, ATT_KV_HEADS), kc, vc)
    ssd_c, ssd_fin = _ssd_mixer(csz, csxbc, csdt, conv_w, conv_b, dt_bias, a_log, d_skip, ssd_norm_g,
                                _ssd_zero_state(Bsz), ctx_out)
    ssd_l, _ = _ssd_mixer(sz, sxbc, sdt, conv_w, conv_b, dt_bias, a_log, d_skip, ssd_norm_g, ssd_fin, True)
    ml_c, ml_fin = _mlstm_mixer(cmq, cmk, cmv, cmo, cmif, ml_bias, ml_norm_g, _mlstm_zero_state(Bsz), ctx_out)
    ml_l, _ = _mlstm_mixer(mq, mk, mv, mo, mif, ml_bias, ml_norm_g, ml_fin, True)
    y = _merge_branches((att, ssd_l, ml_l), gpre, w_branch, w_out)
    if not ctx_out:
        return y, None
    att_c = _context_attention(_rmsnorm(_heads(caq, ATT_HEADS), qn), kc, vc)
    yc = _merge_branches((att_c, ssd_c, ml_c), cgpre, w_branch, w_out)
    return y, yc


def kernel(x, c, ctx, c_ctx, norm1_g, norm2_g, w_mod, b_mod, w_in, att_qnorm, att_knorm, ssd_conv_w, ssd_conv_b, ssd_dt_bias, ssd_a_log, ssd_d, ssd_norm, ml_gate_bias, ml_norm, w_branch, w_out, peer_wq, peer_subkeys, peer_u, peer_v):
    Bsz, S, _ = x.shape
    depth = w_in.shape[0]
    cos, sin = _axial_rope(S // GRID_W)
    silu_c = jax.nn.silu(c)
    silu_cc = jax.nn.silu(c_ctx)
    xc = ctx
    for l in range(depth):
        last = l == depth - 1
        sh1, sc1, g1, sh2, sc2, g2 = jnp.split((silu_c @ w_mod[l] + b_mod[l])[:, None, :], 6, axis=-1)
        csh1, csc1, cg1, csh2, csc2, cg2 = jnp.split(silu_cc @ w_mod[l] + b_mod[l], 6, axis=-1)
        h = _rmsnorm(x, norm1_g[l]) * (1.0 + sc1) + sh1
        hc = _rmsnorm(xc, norm1_g[l]) * (1.0 + csc1) + csh1
        y, yc = _token_mixer(h, hc, w_in[l], att_qnorm[l], att_knorm[l], ssd_conv_w[l], ssd_conv_b[l],
                             ssd_dt_bias[l], ssd_a_log[l], ssd_d[l], ssd_norm[l], ml_gate_bias[l], ml_norm[l],
                             w_branch[l], w_out[l], cos, sin, not last)
        x = x + g1 * y
        h2 = _rmsnorm(x, norm2_g[l]) * (1.0 + sc2) + sh2
        u_bf16 = peer_u[l].astype(jnp.bfloat16)
        vt_bf16 = peer_v[l].T.astype(jnp.bfloat16)
        tok = h2.reshape(-1, D_MODEL)
        if not last:
            xc = xc + cg1 * yc
            hc2 = _rmsnorm(xc, norm2_g[l]) * (1.0 + csc2) + csh2
            tok = jnp.concatenate([tok, hc2.reshape(-1, D_MODEL)], axis=0)
        po = _peer(tok, peer_wq[l], peer_subkeys[l], u_bf16, vt_bf16)
        x = x + g2 * po[:Bsz * S].reshape(x.shape)
        if not last:
            xc = xc + cg2 * po[Bsz * S:].reshape(xc.shape)
    return x
```

```python
import functools
import math

import jax
import jax.numpy as jnp
import numpy as np
from jax import lax
from jax.experimental import pallas as pl
from jax.experimental.pallas import tpu as pltpu

D_MODEL = 1024
GRID_W = 64
RMS_EPS = 1e-6
N_BRANCH = 3
BRANCH_W = 512
ATT_HEADS = 8
ATT_KV_HEADS = 2
ATT_HEAD_DIM = 64
ATT_GROUP = ATT_HEADS // ATT_KV_HEADS
Q_BLOCK = 128
ROPE_THETA = 10000.0
SSD_HEADS = 8
SSD_HEAD_DIM = 64
SSD_INNER = SSD_HEADS * SSD_HEAD_DIM
SSD_GROUPS = 2
SSD_STATE = 64
SSD_XBC = SSD_INNER + 2 * SSD_GROUPS * SSD_STATE
SSD_CONV = 5
SSD_CHUNK = 128
ML_HEADS = 4
ML_HEAD_DIM = 128
ML_INNER = ML_HEADS * ML_HEAD_DIM
ML_CHUNK = 128
PEER_HEADS = 8
PEER_KEYS = 128
PEER_EXPERTS = PEER_KEYS * PEER_KEYS
PEER_TOPK = 16
PEER_QDIM = 256
PEER_HALF = PEER_QDIM // 2
IN_SPLITS = (ATT_HEADS * ATT_HEAD_DIM, ATT_KV_HEADS * ATT_HEAD_DIM, ATT_KV_HEADS * ATT_HEAD_DIM,
             SSD_INNER, SSD_XBC, 2 * SSD_HEADS,
             ML_INNER, ML_INNER, ML_INNER, ML_INNER, 4 * ML_HEADS,
             N_BRANCH * D_MODEL)

LANES = 128
SUBLANES = 8
VMEM_LIMIT_BYTES = 56 * 1024 * 1024

ROUTER_TOKENS = 256
ROUTER_CHUNK = LANES
PEER_TOKENS = 1024
PEER_TOKENS_SMALL = 512
PEER_ROWS = 8
PEER_EBLK = PEER_ROWS * PEER_KEYS
PEER_GROUPS = 2

NEG_INF = float("-inf")


def _sort_network(n):
    def merge(lo, hi, r):
        step = r * 2
        if step < hi - lo:
            yield from merge(lo, hi, step)
            yield from merge(lo + r, hi, step)
            yield from [(i, i + r) for i in range(lo + r, hi - r, step)]
        else:
            yield (lo, lo + r)

    def sort(lo, hi):
        if hi - lo >= 1:
            mid = lo + (hi - lo) // 2
            yield from sort(lo, mid)
            yield from sort(mid + 1, hi)
            yield from merge(lo, hi, 1)

    return tuple(sort(0, n - 1))


_SORT16 = _sort_network(PEER_TOPK)


def _merge_top(lists, n):
    lists = list(lists)
    depth = len(lists)
    sub = lax.broadcasted_iota(jnp.int32, lists[0].shape, 0)
    tops = []
    for i in range(n):
        head = lists[0]
        m = jnp.max(head, axis=0, keepdims=True)
        tops.append(m)
        live = min(depth, n - 1 - i)
        if live == 0:
            break
        first = jnp.min(jnp.where(head == m, sub, SUBLANES), axis=0, keepdims=True)
        pop = sub == first
        for r in range(live):
            nxt = lists[r + 1] if r + 1 < depth else NEG_INF
            lists[r] = jnp.where(pop, nxt, lists[r])
    return tops


def _sorted_top(s, n):
    xs = [s[r * SUBLANES:(r + 1) * SUBLANES, :] for r in range(s.shape[0] // SUBLANES)]
    for i, j in _SORT16:
        xs[i], xs[j] = jnp.maximum(xs[i], xs[j]), jnp.minimum(xs[i], xs[j])
    return _merge_top(xs, n)


def _rows_to_sublanes(rows, first_sublane, shape):
    sub = lax.broadcasted_iota(jnp.int32, shape, 0)
    out = jnp.zeros(shape, jnp.float32)
    for k, row in enumerate(rows):
        out = jnp.where(sub == first_sublane + k, row, out)
    return out


def _dup_bf16(x):
    bits = pltpu.bitcast(x.astype(jnp.bfloat16).astype(jnp.float32), jnp.uint32)
    return bits | (bits >> 16)


_CAND_LEN = (16, 8, 5, 4, 12, 4, 1, 0)


def _router_kernel(h_ref, wq_ref, sk_ref, rank_ref, e2_ref, cnt_ref, e1_ref, q_scr):
    q_scr[...] = jnp.dot(h_ref[...], wq_ref[...], preferred_element_type=jnp.float32).astype(jnp.bfloat16)
    nt = (((1,), (1,)), ((), ()))
    shape8 = (SUBLANES, ROUTER_CHUNK)
    sub = lax.broadcasted_iota(jnp.int32, shape8, 0)
    cand_len = jnp.zeros(shape8, jnp.int32)
    for g, n in enumerate(_CAND_LEN):
        cand_len = jnp.where(sub == g, n, cand_len)
    for h in range(PEER_HEADS):
        q1 = q_scr[:, (2 * h) * PEER_HALF:(2 * h + 1) * PEER_HALF]
        q2 = q_scr[:, (2 * h + 1) * PEER_HALF:(2 * h + 2) * PEER_HALF]
        s1 = lax.dot_general(sk_ref[2 * h], q1, nt, preferred_element_type=jnp.float32)
        s2 = lax.dot_general(sk_ref[2 * h + 1], q2, nt, preferred_element_type=jnp.float32)
        for c in range(ROUTER_TOKENS // ROUTER_CHUNK):
            cs = slice(c * ROUTER_CHUNK, (c + 1) * ROUTER_CHUNK)
            s1c = s1[:, cs]
            s2c = s2[:, cs]
            top1 = _sorted_top(s1c, PEER_TOPK)
            top2 = _sorted_top(s2c, PEER_TOPK)
            a_lo = _rows_to_sublanes(top1[:4], 0, shape8)
            b_lo = _rows_to_sublanes(top2[:3], 4, shape8)
            cands = []
            for r in range(PEER_TOPK):
                by_a = a_lo + top2[r]
                by_b = (top1[4 + r] + b_lo) if 4 + r < PEER_TOPK else by_a
                cands.append(jnp.where(r < cand_len, jnp.where(sub < 4, by_a, by_b), NEG_INF))
            tau = _merge_top(cands, PEER_TOPK)[-1]
            m1, m2 = top1[0], top2[0]
            z = jnp.zeros(shape8, jnp.float32)
            for cand in cands:
                z = z + jnp.where(cand >= tau, jnp.exp(cand - (m1 + m2)), 0.0)
            z = jnp.sum(z, axis=0, keepdims=True)
            rank = jnp.zeros(s2c.shape, jnp.float32)
            for t in top2:
                rank = rank + jnp.where(t > s2c, 1.0, 0.0)
            cnt = jnp.zeros(s1c.shape, jnp.float32)
            for t in top2:
                cnt = cnt + jnp.where(s1c + t >= tau, 1.0, 0.0)
            rank_ref[h, :, cs] = rank.astype(jnp.bfloat16)
            e2_ref[h, :, cs] = jnp.exp(s2c - m2).astype(jnp.bfloat16)
            cnt_ref[h, :, cs] = _dup_bf16(cnt)
            e1_ref[h, :, cs] = _dup_bf16(jnp.exp(s1c - m1) * (0.5 / z))


def _peer_router(h2_bf16, wq, subkeys):
    n_tok = h2_bf16.shape[0]
    sk = subkeys.reshape(PEER_HEADS * 2, PEER_KEYS, PEER_HALF)
    shape = (PEER_HEADS, PEER_KEYS, n_tok)
    spec = pl.BlockSpec((PEER_HEADS, PEER_KEYS, ROUTER_TOKENS), lambda i: (0, 0, i))
    return pl.pallas_call(
        _router_kernel,
        grid=(n_tok // ROUTER_TOKENS,),
        in_specs=[
            pl.BlockSpec((ROUTER_TOKENS, D_MODEL), lambda i: (i, 0)),
            pl.BlockSpec((D_MODEL, PEER_HEADS * PEER_QDIM), lambda i: (0, 0)),
            pl.BlockSpec((PEER_HEADS * 2, PEER_KEYS, PEER_HALF), lambda i: (0, 0, 0)),
        ],
        out_specs=[spec, spec, spec, spec],
        out_shape=[jax.ShapeDtypeStruct(shape, jnp.bfloat16), jax.ShapeDtypeStruct(shape, jnp.bfloat16),
                   jax.ShapeDtypeStruct(shape, jnp.uint32), jax.ShapeDtypeStruct(shape, jnp.uint32)],
        scratch_shapes=[pltpu.VMEM((ROUTER_TOKENS, PEER_HEADS * PEER_QDIM), jnp.bfloat16)],
        compiler_params=pltpu.CompilerParams(dimension_semantics=("arbitrary",),
                                             vmem_limit_bytes=VMEM_LIMIT_BYTES),
        name="peer_router",
    )(h2_bf16, wq.astype(jnp.bfloat16), sk.astype(jnp.bfloat16))


BF16_ROWS = 2 * SUBLANES
GELU_C0 = math.sqrt(2.0 / math.pi)
GELU_C1 = 0.044715 * GELU_C0


def _peer_dense_kernel(h_ref, u_ref, vt_ref, rank_ref, e2_ref, cnt_ref, e1_ref, o_ref,
                       acc_scr, pre_scr, p_scr, rank_scr, e2_scr):
    j = pl.program_id(1)
    n_tok = h_ref.shape[0]

    @pl.when(j == 0)
    def _():
        acc_scr[...] = jnp.zeros_like(acc_scr)
        rank_scr[...] = rank_ref[...]
        e2_scr[...] = e2_ref[...]

    nt = (((1,), (1,)), ((), ()))
    zero = jnp.zeros((BF16_ROWS, LANES), jnp.bfloat16)
    group = PEER_EBLK // PEER_GROUPS
    for a in range(PEER_ROWS):
        g, first = divmod(a * PEER_KEYS, group)
        gs = slice(g * group, (g + 1) * group)
        if first == 0:
            pre_scr[gs, :] = lax.dot_general(u_ref[gs, :], h_ref[...], nt,
                                             preferred_element_type=jnp.float32)
            if g > 0:
                ps = slice((g - 1) * group, g * group)
                acc_scr[...] += jnp.dot(vt_ref[0, :, ps], p_scr[ps, :], preferred_element_type=jnp.float32)
        for c in range(n_tok // LANES):
            cs = slice(c * LANES, (c + 1) * LANES)
            cnt = [pltpu.bitcast(jnp.broadcast_to(cnt_ref[h, a:a + 1, cs], (SUBLANES, LANES)), jnp.bfloat16)
                   for h in range(PEER_HEADS)]
            e1 = [pltpu.bitcast(jnp.broadcast_to(e1_ref[h, a:a + 1, cs], (SUBLANES, LANES)), jnp.bfloat16)
                  for h in range(PEER_HEADS)]
            for b in range(PEER_KEYS // BF16_ROWS):
                bs = slice(b * BF16_ROWS, (b + 1) * BF16_ROWS)
                rs = slice(a * PEER_KEYS + b * BF16_ROWS, a * PEER_KEYS + (b + 1) * BF16_ROWS)
                w = zero
                for h in range(PEER_HEADS):
                    w = w + jnp.where(rank_scr[h, bs, cs] < cnt[h], e2_scr[h, bs, cs], zero) * e1[h]
                x = pre_scr[rs, cs]
                act = x * (1.0 + jnp.tanh(x * (GELU_C0 + GELU_C1 * (x * x))))
                p_scr[rs, cs] = w * act.astype(jnp.bfloat16)
    ps = slice(PEER_EBLK - group, PEER_EBLK)
    acc_scr[...] += jnp.dot(vt_ref[0, :, ps], p_scr[ps, :], preferred_element_type=jnp.float32)

    @pl.when(j == pl.num_programs(1) - 1)
    def _():
        o_ref[...] = acc_scr[...].T


def _peer_dense(h2_bf16, u_bf16, vt_bf16, rank, e2, cnt, e1, tb):
    n_tok = h2_bf16.shape[0]
    row_spec = pl.BlockSpec((PEER_HEADS, PEER_ROWS, tb), lambda i, j: (0, j, i))
    key_spec = pl.BlockSpec((PEER_HEADS, PEER_KEYS, tb), lambda i, j: (0, 0, i))
    return pl.pallas_call(
        _peer_dense_kernel,
        grid=(n_tok // tb, PEER_EXPERTS // PEER_EBLK),
        in_specs=[
            pl.BlockSpec((tb, D_MODEL), lambda i, j: (i, 0)),
            pl.BlockSpec((PEER_EBLK, D_MODEL), lambda i, j: (j, 0)),
            pl.BlockSpec((1, D_MODEL, PEER_EBLK), lambda i, j: (j, 0, 0)),
            key_spec, key_spec, row_spec, row_spec,
        ],
        out_specs=pl.BlockSpec((tb, D_MODEL), lambda i, j: (i, 0)),
        out_shape=jax.ShapeDtypeStruct((n_tok, D_MODEL), jnp.float32),
        scratch_shapes=[pltpu.VMEM((D_MODEL, tb), jnp.float32),
                        pltpu.VMEM((PEER_EBLK, tb), jnp.float32),
                        pltpu.VMEM((PEER_EBLK, tb), jnp.bfloat16),
                        pltpu.VMEM((PEER_HEADS, PEER_KEYS, tb), jnp.bfloat16),
                        pltpu.VMEM((PEER_HEADS, PEER_KEYS, tb), jnp.bfloat16)],
        compiler_params=pltpu.CompilerParams(dimension_semantics=("arbitrary", "arbitrary"),
                                             vmem_limit_bytes=VMEM_LIMIT_BYTES),
        name="peer_dense",
    )(h2_bf16, u_bf16, vt_bf16, rank, e2, cnt, e1)


def _peer(h2, wq, subkeys, u_bf16, vt_bf16):
    n_tok = h2.shape[0]
    tb = PEER_TOKENS if n_tok % PEER_TOKENS == 0 else PEER_TOKENS_SMALL
    h2_bf16 = h2.astype(jnp.bfloat16)
    rank, e2, cnt, e1 = _peer_router(h2_bf16, wq, subkeys)
    return _peer_dense(h2_bf16, u_bf16, vt_bf16, rank, e2, cnt, e1, tb)


ATT_Q_TOKENS = 256
ATT_V_COLS = 2 * ATT_HEAD_DIM


def _attention_kernel(q_ref, kt_ref, v_ref, o_ref, *, kv_chunk):
    tq = q_ref.shape[2]
    rows = ATT_GROUP * tq
    q = q_ref[0].reshape(rows, ATT_HEAD_DIM)
    n_chunks = kt_ref.shape[3] // kv_chunk

    def body(c, carry):
        m, acc = carry
        off = pl.multiple_of(c * kv_chunk, kv_chunk)
        s = jnp.dot(q, kt_ref[0, 0, :, pl.ds(off, kv_chunk)], preferred_element_type=jnp.float32)
        m_new = jnp.maximum(m, jnp.max(s, axis=-1, keepdims=True))
        p = jnp.exp(s - m_new).astype(jnp.bfloat16)
        acc = jnp.exp(m - m_new) * acc + jnp.dot(p, v_ref[0, 0, pl.ds(off, kv_chunk), :],
                                                 preferred_element_type=jnp.float32)
        return m_new, acc

    m0 = jnp.full((rows, 1), NEG_INF, jnp.float32)
    acc0 = jnp.zeros((rows, ATT_V_COLS), jnp.float32)
    _, acc = lax.fori_loop(0, n_chunks, body, (m0, acc0), unroll=True)
    out = acc[:, :ATT_HEAD_DIM] / acc[:, ATT_HEAD_DIM:ATT_HEAD_DIM + 1]
    o_ref[0] = out.reshape(ATT_GROUP, tq, ATT_HEAD_DIM)


def _attention(q, k, v, kv_chunk):
    Bsz, S = q.shape[0], q.shape[1]
    SK = k.shape[1]
    tq = min(ATT_Q_TOKENS, S)
    qh = (q * ATT_HEAD_DIM ** -0.5).astype(jnp.bfloat16).transpose(0, 2, 1, 3)
    kt = k.astype(jnp.bfloat16).transpose(0, 2, 3, 1)
    pad = jnp.concatenate([jnp.ones(v.shape[:-1] + (1,), v.dtype),
                           jnp.zeros(v.shape[:-1] + (ATT_V_COLS - ATT_HEAD_DIM - 1,), v.dtype)], axis=-1)
    vx = jnp.concatenate([v, pad], axis=-1).astype(jnp.bfloat16).transpose(0, 2, 1, 3)
    o = pl.pallas_call(
        functools.partial(_attention_kernel, kv_chunk=kv_chunk),
        grid=(Bsz, ATT_KV_HEADS, S // tq),
        in_specs=[
            pl.BlockSpec((1, ATT_GROUP, tq, ATT_HEAD_DIM), lambda b, g, i: (b, g, i, 0)),
            pl.BlockSpec((1, 1, ATT_HEAD_DIM, SK), lambda b, g, i: (b, g, 0, 0)),
            pl.BlockSpec((1, 1, SK, ATT_V_COLS), lambda b, g, i: (b, g, 0, 0)),
        ],
        out_specs=pl.BlockSpec((1, ATT_GROUP, tq, ATT_HEAD_DIM), lambda b, g, i: (b, g, i, 0)),
        out_shape=jax.ShapeDtypeStruct((Bsz, ATT_HEADS, S, ATT_HEAD_DIM), jnp.float32),
        compiler_params=pltpu.CompilerParams(dimension_semantics=("arbitrary", "arbitrary", "arbitrary"),
                                             vmem_limit_bytes=VMEM_LIMIT_BYTES),
        name="attention",
    )(qh, kt, vx)
    return o.transpose(0, 2, 1, 3).reshape(Bsz, S, ATT_HEADS * ATT_HEAD_DIM)


def _rmsnorm(x, g):
    xf = x.astype(jnp.float32)
    y = xf * lax.rsqrt(jnp.mean(xf * xf, axis=-1, keepdims=True) + RMS_EPS)
    return (y * g.astype(jnp.float32)).astype(x.dtype)


def _split_cols(p):
    bounds = np.cumsum(IN_SPLITS)[:-1].tolist()
    return jnp.split(p, bounds, axis=-1)


def _heads(t, n):
    return t.reshape(t.shape[0], t.shape[1], n, -1)


def _axial_rope(rows):
    n_freq = ATT_HEAD_DIM // 4
    inv = ROPE_THETA ** (-jnp.arange(n_freq, dtype=jnp.float32) / n_freq)
    row = jnp.broadcast_to(jnp.arange(rows, dtype=jnp.float32)[:, None], (rows, GRID_W)).reshape(-1)
    col = jnp.broadcast_to(jnp.arange(GRID_W, dtype=jnp.float32)[None, :], (rows, GRID_W)).reshape(-1)
    ang = jnp.concatenate([row[:, None] * inv, col[:, None] * inv], axis=-1)
    return jnp.cos(ang), jnp.sin(ang)


def _apply_rope(t, cos, sin):
    half = t.shape[-1] // 2
    tf = t.astype(jnp.float32)
    t1, t2 = tf[..., :half], tf[..., half:]
    c, s = cos[:, None, :], sin[:, None, :]
    return jnp.concatenate([t1 * c - t2 * s, t1 * s + t2 * c], axis=-1).astype(t.dtype)


ATT_KV_CHUNK = 768


def _latent_attention(q, k, v, kc, vc):
    kk = jnp.concatenate([k, kc], axis=1)
    vv = jnp.concatenate([v, vc], axis=1)
    assert kk.shape[1] % ATT_KV_CHUNK == 0
    return _attention(q, kk, vv, ATT_KV_CHUNK)


def _context_attention(qc, kc, vc):
    return _attention(qc, kc, vc, kc.shape[1])


def _dwconv(x, w, b):
    out = lax.conv_general_dilated(x, w[:, None, :], window_strides=(1,), padding='SAME',
                                   dimension_numbers=('NWC', 'WIO', 'NWC'),
                                   feature_group_count=x.shape[-1])
    return out + b


def _segsum(a):
    T = a.shape[-1]
    xa = jnp.broadcast_to(a[..., None], a.shape + (T,))
    xa = jnp.where(jnp.tril(jnp.ones((T, T), bool), -1), xa, 0.0)
    cs = jnp.cumsum(xa, axis=-2)
    return jnp.where(jnp.tril(jnp.ones((T, T), bool)), cs, -jnp.inf)


def _ssd_scan(x, dt, A, Bm, Cm, init, with_output):
    Bsz, T, H, P = x.shape
    G, N = Bm.shape[2], Bm.shape[3]
    R = H // G
    Q = SSD_CHUNK
    nc = T // Q
    a = dt * A
    xc = (x * dt[..., None]).reshape(Bsz, nc, Q, G, R, P)
    ac = a.reshape(Bsz, nc, Q, G, R).transpose(0, 3, 4, 1, 2)
    Bc = Bm.reshape(Bsz, nc, Q, G, N)
    Cc = Cm.reshape(Bsz, nc, Q, G, N)
    a_cum = jnp.cumsum(ac, axis=-1)
    decay_states = jnp.exp(a_cum[..., -1:] - a_cum)
    states = jnp.einsum('bcqgn,bgrcq,bcqgrp->bcgrpn', Bc, decay_states, xc)
    init_c = init.reshape(Bsz, G, R, P, N)[:, None]
    states = jnp.concatenate([init_c, states.astype(init_c.dtype)], axis=1)
    chunk_decay = jnp.exp(_segsum(jnp.pad(a_cum[..., -1], [(0, 0)] * 3 + [(1, 0)])))
    new_states = jnp.einsum('bgrzc,bcgrpn->bzgrpn', chunk_decay, states)
    start_states = new_states[:, :-1]
    final = new_states[:, -1].reshape(Bsz, H, P, N)
    if not with_output:
        return None, final
    Lmat = jnp.exp(_segsum(ac))
    y_diag = jnp.einsum('bcqgn,bcsgn,bgrcqs,bcsgrp->bcqgrp', Cc, Bc, Lmat, xc)
    y_off = jnp.einsum('bcqgn,bcgrpn,bgrcq->bcqgrp', Cc, start_states, jnp.exp(a_cum))
    return (y_diag + y_off).reshape(Bsz, T, H, P), final


def _ssd_zero_state(Bsz):
    z = jnp.zeros((Bsz, SSD_HEADS, SSD_HEAD_DIM, SSD_STATE), jnp.float32)
    return (z, z)


def _ssd_mixer(z, xbc, dt_raw, conv_w, conv_b, dt_bias, a_log, d_skip, norm_g, inits, with_output):
    Bsz, T, _ = xbc.shape
    xbc = jax.nn.silu(_dwconv(xbc, conv_w, conv_b))
    xs, Bm, Cm = jnp.split(xbc, [SSD_INNER, SSD_INNER + SSD_GROUPS * SSD_STATE], axis=-1)
    xs = xs.reshape(Bsz, T, SSD_HEADS, SSD_HEAD_DIM)
    Bm = Bm.reshape(Bsz, T, SSD_GROUPS, SSD_STATE)
    Cm = Cm.reshape(Bsz, T, SSD_GROUPS, SSD_STATE)
    dt = jax.nn.softplus(dt_raw.reshape(Bsz, T, 2, SSD_HEADS).astype(jnp.float32) + dt_bias.astype(jnp.float32))
    A = -jnp.exp(a_log.astype(jnp.float32))
    fl = lambda t: jnp.flip(t, axis=1)
    y_f, fin_f = _ssd_scan(xs, dt[:, :, 0], A[0], Bm, Cm, inits[0], with_output)
    y_b, fin_b = _ssd_scan(fl(xs), fl(dt[:, :, 1]), A[1], fl(Bm), fl(Cm), inits[1], with_output)
    if not with_output:
        return None, (fin_f, fin_b)
    y = y_f + fl(y_b) + xs.astype(jnp.float32) * d_skip.astype(jnp.float32)[:, None]
    y = y.reshape(Bsz, T, SSD_INNER).astype(z.dtype) * jax.nn.silu(z)
    return _rmsnorm(y, norm_g), (fin_f, fin_b)


def _mlstm_scan(q, k, v, i_pre, logf, init, with_output):
    Bsz, T, H, Dh = q.shape
    Q = ML_CHUNK
    nc = T // Q
    qc = q.reshape(Bsz, nc, Q, H, Dh)
    kc = k.reshape(Bsz, nc, Q, H, Dh)
    vc = v.reshape(Bsz, nc, Q, H, Dh)
    ic = i_pre.reshape(Bsz, nc, Q, H).transpose(0, 3, 1, 2)
    b = jnp.cumsum(logf.reshape(Bsz, nc, Q, H).transpose(0, 3, 1, 2), axis=-1)
    b_last = b[..., -1]
    g = b_last[..., None] - b + ic
    m_loc = jnp.max(g, axis=-1)
    w = jnp.exp(g - m_loc[..., None])
    C_loc = jnp.einsum('bhcs,bcshd,bcshe->bchde', w, vc, kc)
    n_loc = jnp.einsum('bhcs,bcshe->bche', w, kc)

    def step(carry, inp):
        C, n, m = carry
        Cl, nl, ml, bl = inp
        m_new = jnp.maximum(bl + m, ml)
        a_prev = jnp.exp(bl + m - m_new)
        a_loc = jnp.exp(ml - m_new)
        C_new = a_prev[..., None, None] * C + a_loc[..., None, None] * Cl
        n_new = a_prev[..., None] * n + a_loc[..., None] * nl
        return (C_new, n_new, m_new), (C, n, m)

    xs = (C_loc.transpose(1, 0, 2, 3, 4), n_loc.transpose(1, 0, 2, 3),
          m_loc.transpose(2, 0, 1), b_last.transpose(2, 0, 1))
    final, (C0, n0, m0) = lax.scan(step, init, xs)
    if not with_output:
        return None, final
    m0 = m0.transpose(1, 2, 0)
    Dm = b[..., :, None] - b[..., None, :] + ic[..., None, :]
    Dm = jnp.where(jnp.tril(jnp.ones((Q, Q), bool)), Dm, -jnp.inf)
    inter = b + m0[..., None]
    m_t = jnp.maximum(inter, jnp.max(Dm, axis=-1))
    Wts = jnp.exp(Dm - m_t[..., None])
    w_inter = jnp.exp(inter - m_t)
    s = jnp.einsum('bcqhd,bcshd->bhcqs', qc, kc) * Wts
    num = (jnp.einsum('bhcqs,bcshd->bcqhd', s, vc)
           + jnp.einsum('bcqhe,cbhde->bcqhd', qc, C0) * w_inter.transpose(0, 2, 3, 1)[..., None])
    den = jnp.sum(s, axis=-1) + w_inter * jnp.einsum('bcqhe,cbhe->bhcq', qc, n0)
    denom = jnp.maximum(jnp.abs(den), jnp.exp(-m_t)).transpose(0, 2, 3, 1)[..., None]
    return (num / denom).reshape(Bsz, T, H, Dh), final


def _mlstm_zero_state(Bsz):
    st = (jnp.zeros((Bsz, ML_HEADS, ML_HEAD_DIM, ML_HEAD_DIM), jnp.float32),
          jnp.zeros((Bsz, ML_HEADS, ML_HEAD_DIM), jnp.float32),
          jnp.zeros((Bsz, ML_HEADS), jnp.float32))
    return (st, st)


def _mlstm_mixer(q, k, v, o_pre, if_pre, gate_bias, norm_g, inits, with_output):
    Bsz, T, _ = q.shape
    out_dtype = q.dtype
    qh = _heads(q, ML_HEADS).astype(jnp.float32)
    kh = _heads(k, ML_HEADS).astype(jnp.float32) * ML_HEAD_DIM ** -0.5
    vh = _heads(v, ML_HEADS).astype(jnp.float32)
    gates = if_pre.reshape(Bsz, T, 2, 2, ML_HEADS).astype(jnp.float32) + gate_bias.astype(jnp.float32)
    i_pre = gates[:, :, :, 0]
    logf = jax.nn.log_sigmoid(gates[:, :, :, 1])
    fl = lambda t: jnp.flip(t, axis=1)
    h_f, fin_f = _mlstm_scan(qh, kh, vh, i_pre[:, :, 0], logf[:, :, 0], inits[0], with_output)
    h_b, fin_b = _mlstm_scan(fl(qh), fl(kh), fl(vh), fl(i_pre[:, :, 1]), fl(logf[:, :, 1]), inits[1], with_output)
    if not with_output:
        return None, (fin_f, fin_b)
    h = _rmsnorm(h_f + fl(h_b), norm_g).reshape(Bsz, T, ML_INNER)
    h = h * jax.nn.sigmoid(o_pre.astype(jnp.float32))
    return h.astype(out_dtype), (fin_f, fin_b)


def _merge_branches(branches, gate_pre, w_branch, w_out):
    Bsz, T, _ = gate_pre.shape
    stacked = jnp.stack(branches, axis=2)
    proj = jnp.einsum('btrw,rwd->btrd', stacked, w_branch)
    gates = jax.nn.sigmoid(gate_pre.reshape(Bsz, T, N_BRANCH, D_MODEL))
    return jnp.sum(gates * proj, axis=2) @ w_out


def _token_mixer(h, hc, w_in, qn, kn, conv_w, conv_b, dt_bias, a_log, d_skip, ssd_norm_g,
                 ml_bias, ml_norm_g, w_branch, w_out, cos, sin, ctx_out):
    Bsz = h.shape[0]
    aq, ak, av, sz, sxbc, sdt, mq, mk, mv, mo, mif, gpre = _split_cols(h @ w_in)
    caq, cak, cav, csz, csxbc, csdt, cmq, cmk, cmv, cmo, cmif, cgpre = _split_cols(hc @ w_in)
    q = _apply_rope(_rmsnorm(_heads(aq, ATT_HEADS), qn), cos, sin)
    k = _apply_rope(_rmsnorm(_heads(ak, ATT_KV_HEADS), kn), cos, sin)
    kc = _rmsnorm(_heads(cak, ATT_KV_HEADS), kn)
    vc = _heads(cav, ATT_KV_HEADS)
    att = _latent_attention(q, k, _heads(av, ATT_KV_HEADS), kc, vc)
    ssd_c, ssd_fin = _ssd_mixer(csz, csxbc, csdt, conv_w, conv_b, dt_bias, a_log, d_skip, ssd_norm_g,
                                _ssd_zero_state(Bsz), ctx_out)
    ssd_l, _ = _ssd_mixer(sz, sxbc, sdt, conv_w, conv_b, dt_bias, a_log, d_skip, ssd_norm_g, ssd_fin, True)
    ml_c, ml_fin = _mlstm_mixer(cmq, cmk, cmv, cmo, cmif, ml_bias, ml_norm_g, _mlstm_zero_state(Bsz), ctx_out)
    ml_l, _ = _mlstm_mixer(mq, mk, mv, mo, mif, ml_bias, ml_norm_g, ml_fin, True)
    y = _merge_branches((att, ssd_l, ml_l), gpre, w_branch, w_out)
    if not ctx_out:
        return y, None
    att_c = _context_attention(_rmsnorm(_heads(caq, ATT_HEADS), qn), kc, vc)
    yc = _merge_branches((att_c, ssd_c, ml_c), cgpre, w_branch, w_out)
    return y, yc


def kernel(x, c, ctx, c_ctx, norm1_g, norm2_g, w_mod, b_mod, w_in, att_qnorm, att_knorm, ssd_conv_w, ssd_conv_b, ssd_dt_bias, ssd_a_log, ssd_d, ssd_norm, ml_gate_bias, ml_norm, w_branch, w_out, peer_wq, peer_subkeys, peer_u, peer_v):
    Bsz, S, _ = x.shape
    depth = w_in.shape[0]
    cos, sin = _axial_rope(S // GRID_W)
    silu_c = jax.nn.silu(c)
    silu_cc = jax.nn.silu(c_ctx)
    xc = ctx
    for l in range(depth):
        last = l == depth - 1
        sh1, sc1, g1, sh2, sc2, g2 = jnp.split((silu_c @ w_mod[l] + b_mod[l])[:, None, :], 6, axis=-1)
        csh1, csc1, cg1, csh2, csc2, cg2 = jnp.split(silu_cc @ w_mod[l] + b_mod[l], 6, axis=-1)
        h = _rmsnorm(x, norm1_g[l]) * (1.0 + sc1) + sh1
        hc = _rmsnorm(xc, norm1_g[l]) * (1.0 + csc1) + csh1
        y, yc = _token_mixer(h, hc, w_in[l], att_qnorm[l], att_knorm[l], ssd_conv_w[l], ssd_conv_b[l],
                             ssd_dt_bias[l], ssd_a_log[l], ssd_d[l], ssd_norm[l], ml_gate_bias[l], ml_norm[l],
                             w_branch[l], w_out[l], cos, sin, not last)
        x = x + g1 * y
        h2 = _rmsnorm(x, norm2_g[l]) * (1.0 + sc2) + sh2
        u_bf16 = peer_u[l].astype(jnp.bfloat16)
        vt_bf16 = peer_v[l].astype(jnp.bfloat16).reshape(PEER_EXPERTS // PEER_EBLK, PEER_EBLK, D_MODEL)
        vt_bf16 = vt_bf16.transpose(0, 2, 1)
        po = _peer(h2.reshape(-1, D_MODEL), peer_wq[l], peer_subkeys[l], u_bf16, vt_bf16)
        x = x + g2 * po.reshape(x.shape)
        if not last:
            xc = xc + cg1 * yc
            hc2 = _rmsnorm(xc, norm2_g[l]) * (1.0 + csc2) + csh2
            pc = _peer(hc2.reshape(-1, D_MODEL), peer_wq[l], peer_subkeys[l], u_bf16, vt_bf16)
            xc = xc + cg2 * pc.reshape(xc.shape)
    return x
```

```python
import functools
import math

import jax
import jax.numpy as jnp
import numpy as np
from jax import lax
from jax.experimental import pallas as pl
from jax.experimental.pallas import tpu as pltpu

D_MODEL = 1024
GRID_W = 64
RMS_EPS = 1e-6
N_BRANCH = 3
BRANCH_W = 512
ATT_HEADS = 8
ATT_KV_HEADS = 2
ATT_HEAD_DIM = 64
ATT_GROUP = ATT_HEADS // ATT_KV_HEADS
Q_BLOCK = 128
ROPE_THETA = 10000.0
SSD_HEADS = 8
SSD_HEAD_DIM = 64
SSD_INNER = SSD_HEADS * SSD_HEAD_DIM
SSD_GROUPS = 2
SSD_STATE = 64
SSD_XBC = SSD_INNER + 2 * SSD_GROUPS * SSD_STATE
SSD_CONV = 5
SSD_CHUNK = 128
ML_HEADS = 4
ML_HEAD_DIM = 128
ML_INNER = ML_HEADS * ML_HEAD_DIM
ML_CHUNK = 128
PEER_HEADS = 8
PEER_KEYS = 128
PEER_EXPERTS = PEER_KEYS * PEER_KEYS
PEER_TOPK = 16
PEER_QDIM = 256
PEER_HALF = PEER_QDIM // 2
IN_SPLITS = (ATT_HEADS * ATT_HEAD_DIM, ATT_KV_HEADS * ATT_HEAD_DIM, ATT_KV_HEADS * ATT_HEAD_DIM,
             SSD_INNER, SSD_XBC, 2 * SSD_HEADS,
             ML_INNER, ML_INNER, ML_INNER, ML_INNER, 4 * ML_HEADS,
             N_BRANCH * D_MODEL)

LANES = 128
SUBLANES = 8
VMEM_LIMIT_BYTES = 56 * 1024 * 1024

ROUTER_TOKENS = 256
ROUTER_CHUNK = LANES
PEER_TOKENS = 1024
PEER_TOKENS_SMALL = 512
PEER_ROWS = 8
PEER_EBLK = PEER_ROWS * PEER_KEYS
PEER_GROUPS = 2

NEG_INF = float("-inf")


def _sort_network(n):
    def merge(lo, hi, r):
        step = r * 2
        if step < hi - lo:
            yield from merge(lo, hi, step)
            yield from merge(lo + r, hi, step)
            yield from [(i, i + r) for i in range(lo + r, hi - r, step)]
        else:
            yield (lo, lo + r)

    def sort(lo, hi):
        if hi - lo >= 1:
            mid = lo + (hi - lo) // 2
            yield from sort(lo, mid)
            yield from sort(mid + 1, hi)
            yield from merge(lo, hi, 1)

    return tuple(sort(0, n - 1))


_SORT16 = _sort_network(PEER_TOPK)


def _merge_top(lists, n):
    lists = list(lists)
    depth = len(lists)
    sub = lax.broadcasted_iota(jnp.int32, lists[0].shape, 0)
    tops = []
    for i in range(n):
        head = lists[0]
        m = jnp.max(head, axis=0, keepdims=True)
        tops.append(m)
        live = min(depth, n - 1 - i)
        if live == 0:
            break
        first = jnp.min(jnp.where(head == m, sub, SUBLANES), axis=0, keepdims=True)
        pop = sub == first
        for r in range(live):
            nxt = lists[r + 1] if r + 1 < depth else NEG_INF
            lists[r] = jnp.where(pop, nxt, lists[r])
    return tops


def _sorted_top(s, n):
    xs = [s[r * SUBLANES:(r + 1) * SUBLANES, :] for r in range(s.shape[0] // SUBLANES)]
    for i, j in _SORT16:
        xs[i], xs[j] = jnp.maximum(xs[i], xs[j]), jnp.minimum(xs[i], xs[j])
    return _merge_top(xs, n)


def _rows_to_sublanes(rows, first_sublane, shape):
    sub = lax.broadcasted_iota(jnp.int32, shape, 0)
    out = jnp.zeros(shape, jnp.float32)
    for k, row in enumerate(rows):
        out = jnp.where(sub == first_sublane + k, row, out)
    return out


def _dup_bf16(x):
    bits = pltpu.bitcast(x.astype(jnp.bfloat16).astype(jnp.float32), jnp.uint32)
    return bits | (bits >> 16)


_CAND_LEN = (16, 8, 5, 4, 12, 4, 1, 0)


def _router_kernel(h_ref, wq_ref, sk_ref, rank_ref, e2_ref, cnt_ref, e1_ref, q_scr):
    q_scr[...] = jnp.dot(h_ref[...], wq_ref[...], preferred_element_type=jnp.float32).astype(jnp.bfloat16)
    nt = (((1,), (1,)), ((), ()))
    shape8 = (SUBLANES, ROUTER_CHUNK)
    sub = lax.broadcasted_iota(jnp.int32, shape8, 0)
    cand_len = jnp.zeros(shape8, jnp.int32)
    for g, n in enumerate(_CAND_LEN):
        cand_len = jnp.where(sub == g, n, cand_len)
    for h in range(PEER_HEADS):
        q1 = q_scr[:, (2 * h) * PEER_HALF:(2 * h + 1) * PEER_HALF]
        q2 = q_scr[:, (2 * h + 1) * PEER_HALF:(2 * h + 2) * PEER_HALF]
        s1 = lax.dot_general(sk_ref[2 * h], q1, nt, preferred_element_type=jnp.float32)
        s2 = lax.dot_general(sk_ref[2 * h + 1], q2, nt, preferred_element_type=jnp.float32)
        for c in range(ROUTER_TOKENS // ROUTER_CHUNK):
            cs = slice(c * ROUTER_CHUNK, (c + 1) * ROUTER_CHUNK)
            s1c = s1[:, cs]
            s2c = s2[:, cs]
            top1 = _sorted_top(s1c, PEER_TOPK)
            top2 = _sorted_top(s2c, PEER_TOPK)
            a_lo = _rows_to_sublanes(top1[:4], 0, shape8)
            b_lo = _rows_to_sublanes(top2[:3], 4, shape8)
            cands = []
            for r in range(PEER_TOPK):
                by_a = a_lo + top2[r]
                by_b = (top1[4 + r] + b_lo) if 4 + r < PEER_TOPK else by_a
                cands.append(jnp.where(r < cand_len, jnp.where(sub < 4, by_a, by_b), NEG_INF))
            tau = _merge_top(cands, PEER_TOPK)[-1]
            m1, m2 = top1[0], top2[0]
            z = jnp.zeros(shape8, jnp.float32)
            for cand in cands:
                z = z + jnp.where(cand >= tau, jnp.exp(cand - (m1 + m2)), 0.0)
            z = jnp.sum(z, axis=0, keepdims=True)
            rank = jnp.zeros(s2c.shape, jnp.float32)
            for t in top2:
                rank = rank + jnp.where(t > s2c, 1.0, 0.0)
            cnt = jnp.zeros(s1c.shape, jnp.float32)
            for t in top2:
                cnt = cnt + jnp.where(s1c + t >= tau, 1.0, 0.0)
            rank_ref[h, :, cs] = rank.astype(jnp.bfloat16)
            e2_ref[h, :, cs] = jnp.exp(s2c - m2).astype(jnp.bfloat16)
            cnt_ref[h, :, cs] = _dup_bf16(cnt)
            e1_ref[h, :, cs] = _dup_bf16(jnp.exp(s1c - m1) * (0.5 / z))


def _peer_router(h2_bf16, wq, subkeys):
    n_tok = h2_bf16.shape[0]
    sk = subkeys.reshape(PEER_HEADS * 2, PEER_KEYS, PEER_HALF)
    shape = (PEER_HEADS, PEER_KEYS, n_tok)
    spec = pl.BlockSpec((PEER_HEADS, PEER_KEYS, ROUTER_TOKENS), lambda i: (0, 0, i))
    return pl.pallas_call(
        _router_kernel,
        grid=(n_tok // ROUTER_TOKENS,),
        in_specs=[
            pl.BlockSpec((ROUTER_TOKENS, D_MODEL), lambda i: (i, 0)),
            pl.BlockSpec((D_MODEL, PEER_HEADS * PEER_QDIM), lambda i: (0, 0)),
            pl.BlockSpec((PEER_HEADS * 2, PEER_KEYS, PEER_HALF), lambda i: (0, 0, 0)),
        ],
        out_specs=[spec, spec, spec, spec],
        out_shape=[jax.ShapeDtypeStruct(shape, jnp.bfloat16), jax.ShapeDtypeStruct(shape, jnp.bfloat16),
                   jax.ShapeDtypeStruct(shape, jnp.uint32), jax.ShapeDtypeStruct(shape, jnp.uint32)],
        scratch_shapes=[pltpu.VMEM((ROUTER_TOKENS, PEER_HEADS * PEER_QDIM), jnp.bfloat16)],
        compiler_params=pltpu.CompilerParams(dimension_semantics=("arbitrary",),
                                             vmem_limit_bytes=VMEM_LIMIT_BYTES),
        name="peer_router",
    )(h2_bf16, wq.astype(jnp.bfloat16), sk.astype(jnp.bfloat16))


BF16_ROWS = 2 * SUBLANES
GELU_C0 = math.sqrt(2.0 / math.pi)
GELU_C1 = 0.044715 * GELU_C0


def _peer_dense_kernel(h_ref, u_ref, vt_ref, rank_ref, e2_ref, cnt_ref, e1_ref, o_ref,
                       acc_scr, pre_scr, p_scr, rank_scr, e2_scr):
    j = pl.program_id(1)
    n_tok = h_ref.shape[0]

    @pl.when(j == 0)
    def _():
        acc_scr[...] = jnp.zeros_like(acc_scr)
        rank_scr[...] = rank_ref[...]
        e2_scr[...] = e2_ref[...]

    nt = (((1,), (1,)), ((), ()))
    zero = jnp.zeros((BF16_ROWS, LANES), jnp.bfloat16)
    group = PEER_EBLK // PEER_GROUPS
    for a in range(PEER_ROWS):
        g, first = divmod(a * PEER_KEYS, group)
        gs = slice(g * group, (g + 1) * group)
        if first == 0:
            pre_scr[gs, :] = lax.dot_general(u_ref[gs, :], h_ref[...], nt,
                                             preferred_element_type=jnp.float32)
            if g > 0:
                ps = slice((g - 1) * group, g * group)
                acc_scr[...] += jnp.dot(vt_ref[0, :, ps], p_scr[ps, :], preferred_element_type=jnp.float32)
        for c in range(n_tok // LANES):
            cs = slice(c * LANES, (c + 1) * LANES)
            cnt = [pltpu.bitcast(jnp.broadcast_to(cnt_ref[h, a:a + 1, cs], (SUBLANES, LANES)), jnp.bfloat16)
                   for h in range(PEER_HEADS)]
            e1 = [pltpu.bitcast(jnp.broadcast_to(e1_ref[h, a:a + 1, cs], (SUBLANES, LANES)), jnp.bfloat16)
                  for h in range(PEER_HEADS)]
            for b in range(PEER_KEYS // BF16_ROWS):
                bs = slice(b * BF16_ROWS, (b + 1) * BF16_ROWS)
                rs = slice(a * PEER_KEYS + b * BF16_ROWS, a * PEER_KEYS + (b + 1) * BF16_ROWS)
                w = zero
                for h in range(PEER_HEADS):
                    w = w + jnp.where(rank_scr[h, bs, cs] < cnt[h], e2_scr[h, bs, cs], zero) * e1[h]
                x = pre_scr[rs, cs]
                act = x * (1.0 + jnp.tanh(x * (GELU_C0 + GELU_C1 * (x * x))))
                p_scr[rs, cs] = w * act.astype(jnp.bfloat16)
    ps = slice(PEER_EBLK - group, PEER_EBLK)
    acc_scr[...] += jnp.dot(vt_ref[0, :, ps], p_scr[ps, :], preferred_element_type=jnp.float32)

    @pl.when(j == pl.num_programs(1) - 1)
    def _():
        o_ref[...] = acc_scr[...].T


def _peer_dense(h2_bf16, u_bf16, vt_bf16, rank, e2, cnt, e1, tb):
    n_tok = h2_bf16.shape[0]
    row_spec = pl.BlockSpec((PEER_HEADS, PEER_ROWS, tb), lambda i, j: (0, j, i))
    key_spec = pl.BlockSpec((PEER_HEADS, PEER_KEYS, tb), lambda i, j: (0, 0, i))
    return pl.pallas_call(
        _peer_dense_kernel,
        grid=(n_tok // tb, PEER_EXPERTS // PEER_EBLK),
        in_specs=[
            pl.BlockSpec((tb, D_MODEL), lambda i, j: (i, 0)),
            pl.BlockSpec((PEER_EBLK, D_MODEL), lambda i, j: (j, 0)),
            pl.BlockSpec((1, D_MODEL, PEER_EBLK), lambda i, j: (j, 0, 0)),
            key_spec, key_spec, row_spec, row_spec,
        ],
        out_specs=pl.BlockSpec((tb, D_MODEL), lambda i, j: (i, 0)),
        out_shape=jax.ShapeDtypeStruct((n_tok, D_MODEL), jnp.float32),
        scratch_shapes=[pltpu.VMEM((D_MODEL, tb), jnp.float32),
                        pltpu.VMEM((PEER_EBLK, tb), jnp.float32),
                        pltpu.VMEM((PEER_EBLK, tb), jnp.bfloat16),
                        pltpu.VMEM((PEER_HEADS, PEER_KEYS, tb), jnp.bfloat16),
                        pltpu.VMEM((PEER_HEADS, PEER_KEYS, tb), jnp.bfloat16)],
        compiler_params=pltpu.CompilerParams(dimension_semantics=("arbitrary", "arbitrary"),
                                             vmem_limit_bytes=VMEM_LIMIT_BYTES),
        name="peer_dense",
    )(h2_bf16, u_bf16, vt_bf16, rank, e2, cnt, e1)


def _peer(h2, wq, subkeys, u_bf16, vt_bf16):
    n_tok = h2.shape[0]
    tb = PEER_TOKENS if n_tok % PEER_TOKENS == 0 else PEER_TOKENS_SMALL
    h2_bf16 = h2.astype(jnp.bfloat16)
    rank, e2, cnt, e1 = _peer_router(h2_bf16, wq, subkeys)
    return _peer_dense(h2_bf16, u_bf16, vt_bf16, rank, e2, cnt, e1, tb)


ATT_Q_TOKENS = 256
ATT_V_COLS = 2 * ATT_HEAD_DIM


def _attention_kernel(q_ref, kt_ref, v_ref, o_ref, *, kv_chunk):
    tq = q_ref.shape[2]
    rows = ATT_GROUP * tq
    q = q_ref[0].reshape(rows, ATT_HEAD_DIM)
    n_chunks = kt_ref.shape[3] // kv_chunk

    def body(c, carry):
        m, acc = carry
        off = pl.multiple_of(c * kv_chunk, kv_chunk)
        s = jnp.dot(q, kt_ref[0, 0, :, pl.ds(off, kv_chunk)], preferred_element_type=jnp.float32)
        m_new = jnp.maximum(m, jnp.max(s, axis=-1, keepdims=True))
        p = jnp.exp(s - m_new).astype(jnp.bfloat16)
        acc = jnp.exp(m - m_new) * acc + jnp.dot(p, v_ref[0, 0, pl.ds(off, kv_chunk), :],
                                                 preferred_element_type=jnp.float32)
        return m_new, acc

    m0 = jnp.full((rows, 1), NEG_INF, jnp.float32)
    acc0 = jnp.zeros((rows, ATT_V_COLS), jnp.float32)
    _, acc = lax.fori_loop(0, n_chunks, body, (m0, acc0), unroll=True)
    out = acc[:, :ATT_HEAD_DIM] / acc[:, ATT_HEAD_DIM:ATT_HEAD_DIM + 1]
    o_ref[0] = out.reshape(ATT_GROUP, tq, ATT_HEAD_DIM)


def _attention(q, k, v, kv_chunk):
    Bsz, S = q.shape[0], q.shape[1]
    SK = k.shape[1]
    tq = min(ATT_Q_TOKENS, S)
    qh = (q * ATT_HEAD_DIM ** -0.5).astype(jnp.bfloat16).transpose(0, 2, 1, 3)
    kt = k.astype(jnp.bfloat16).transpose(0, 2, 3, 1)
    pad = jnp.concatenate([jnp.ones(v.shape[:-1] + (1,), v.dtype),
                           jnp.zeros(v.shape[:-1] + (ATT_V_COLS - ATT_HEAD_DIM - 1,), v.dtype)], axis=-1)
    vx = jnp.concatenate([v, pad], axis=-1).astype(jnp.bfloat16).transpose(0, 2, 1, 3)
    o = pl.pallas_call(
        functools.partial(_attention_kernel, kv_chunk=kv_chunk),
        grid=(Bsz, ATT_KV_HEADS, S // tq),
        in_specs=[
            pl.BlockSpec((1, ATT_GROUP, tq, ATT_HEAD_DIM), lambda b, g, i: (b, g, i, 0)),
            pl.BlockSpec((1, 1, ATT_HEAD_DIM, SK), lambda b, g, i: (b, g, 0, 0)),
            pl.BlockSpec((1, 1, SK, ATT_V_COLS), lambda b, g, i: (b, g, 0, 0)),
        ],
        out_specs=pl.BlockSpec((1, ATT_GROUP, tq, ATT_HEAD_DIM), lambda b, g, i: (b, g, i, 0)),
        out_shape=jax.ShapeDtypeStruct((Bsz, ATT_HEADS, S, ATT_HEAD_DIM), jnp.float32),
        compiler_params=pltpu.CompilerParams(dimension_semantics=("arbitrary", "arbitrary", "arbitrary"),
                                             vmem_limit_bytes=VMEM_LIMIT_BYTES),
        name="attention",
    )(qh, kt, vx)
    return o.transpose(0, 2, 1, 3).reshape(Bsz, S, ATT_HEADS * ATT_HEAD_DIM)


SSD_GROUP_HEADS = SSD_HEADS // SSD_GROUPS
SSD_GROUP_W = SSD_GROUP_HEADS * SSD_HEAD_DIM
SSD_GATE_COLS = LANES


def _ssd_kernel(*refs, reverse, combine):
    if combine:
        x_ref, bm_ref, cm_ref, grow_ref, gcol_ref, other_ref, z_ref, skip_ref, gain_ref, out_ref, s_scr = refs
    else:
        x_ref, bm_ref, cm_ref, grow_ref, gcol_ref, out_ref, s_scr = refs
    Q, P, N, H = SSD_CHUNK, SSD_HEAD_DIM, SSD_STATE, SSD_HEADS

    @pl.when(pl.program_id(1) == 0)
    def _():
        s_scr[...] = jnp.zeros_like(s_scr)

    row = lax.broadcasted_iota(jnp.int32, (Q, Q), 0)
    col = lax.broadcasted_iota(jnp.int32, (Q, Q), 1)
    not_after = (col >= row) if reverse else (col <= row)
    tri = not_after.astype(jnp.float32)
    last = 0 if reverse else Q - 1
    nt = (((1,), (1,)), ((), ()))
    grow = grow_ref[0]
    gcol = gcol_ref[0]
    a_cols = jnp.dot(tri, gcol, preferred_element_type=jnp.float32, precision=lax.Precision.HIGHEST)
    a_rows = lax.dot_general(grow, tri, nt, preferred_element_type=jnp.float32,
                             precision=lax.Precision.HIGHEST)
    ys = []
    for g in range(SSD_GROUPS):
        ns = slice(g * N, (g + 1) * N)
        bm = bm_ref[0, :, ns].astype(jnp.bfloat16)
        cm = cm_ref[0, :, ns].astype(jnp.bfloat16)
        cb = lax.dot_general(cm, bm, nt, preferred_element_type=jnp.float32)
        state = s_scr[g]
        y_off = jnp.dot(cm, state.astype(jnp.bfloat16), preferred_element_type=jnp.float32)
        xw, decay = [], []
        for r in range(SSD_GROUP_HEADS):
            h = g * SSD_GROUP_HEADS + r
            hs = slice(h * P, (h + 1) * P)
            aq = a_cols[:, H + h:H + h + 1]
            as_ = a_rows[H + h:H + h + 1, :]
            a_tot = as_[:, last:last + 1]
            xdt = x_ref[0, :, hs] * gcol[:, h:h + 1]
            lmat = jnp.exp(jnp.where(not_after, aq - as_, NEG_INF))
            y = jnp.dot((cb * lmat).astype(jnp.bfloat16), xdt.astype(jnp.bfloat16),
                        preferred_element_type=jnp.float32)
            ys.append(y + y_off[:, r * P:(r + 1) * P] * jnp.exp(aq))
            xw.append(jnp.exp(a_tot - aq) * xdt)
            decay.append(jnp.broadcast_to(jnp.exp(a_tot), (1, P)))
        s_loc = lax.dot_general(bm, jnp.concatenate(xw, axis=1).astype(jnp.bfloat16),
                                (((0,), (0,)), ((), ())), preferred_element_type=jnp.float32)
        s_scr[g] = state * jnp.concatenate(decay, axis=1) + s_loc
    y = jnp.concatenate(ys, axis=1)
    if combine:
        x = x_ref[0]
        z = z_ref[0]
        tot = (y + other_ref[0] + x * skip_ref[...]) * (z * jax.nn.sigmoid(z))
        out_ref[0] = tot * lax.rsqrt(jnp.mean(tot * tot, axis=1, keepdims=True) + RMS_EPS) * gain_ref[...]
    else:
        out_ref[0] = y


def _ssd_direction(xbc, grow, gcol, n_ctx_chunks, reverse, extra=None):
    Bsz, TT, _ = xbc.shape
    n_chunks = TT // SSD_CHUNK
    if reverse:
        chunk = lambda c: jnp.where(c < n_ctx_chunks, n_ctx_chunks - 1 - c, n_chunks - 1 + n_ctx_chunks - c)
    else:
        chunk = lambda c: c
    gn = SSD_GROUPS * SSD_STATE
    seq_spec = pl.BlockSpec((1, SSD_CHUNK, SSD_INNER), lambda b, c: (b, chunk(c), 0))
    vec_spec = pl.BlockSpec((1, SSD_INNER), lambda b, c: (0, 0))
    in_specs = [seq_spec,
                pl.BlockSpec((1, SSD_CHUNK, gn), lambda b, c: (b, chunk(c), SSD_INNER // gn)),
                pl.BlockSpec((1, SSD_CHUNK, gn), lambda b, c: (b, chunk(c), SSD_INNER // gn + 1)),
                pl.BlockSpec((1, 2 * SSD_HEADS, SSD_CHUNK), lambda b, c: (b, 0, chunk(c))),
                pl.BlockSpec((1, SSD_CHUNK, SSD_GATE_COLS), lambda b, c: (b, chunk(c), 0))]
    args = [xbc, xbc, xbc, grow, gcol]
    if extra is not None:
        other, z, skip, gain = extra
        in_specs += [seq_spec, seq_spec, vec_spec, vec_spec]
        args += [other, z, skip, gain]
    return pl.pallas_call(
        functools.partial(_ssd_kernel, reverse=reverse, combine=extra is not None),
        grid=(Bsz, n_chunks),
        in_specs=in_specs,
        out_specs=seq_spec,
        out_shape=jax.ShapeDtypeStruct((Bsz, TT, SSD_INNER), jnp.float32),
        scratch_shapes=[pltpu.VMEM((SSD_GROUPS, SSD_STATE, SSD_GROUP_W), jnp.float32)],
        compiler_params=pltpu.CompilerParams(dimension_semantics=("arbitrary", "arbitrary"),
                                             vmem_limit_bytes=VMEM_LIMIT_BYTES),
        name="ssd_bwd" if reverse else "ssd_fwd",
    )(*args)


def _ssd_block(z, xbc_act, dt_raw, dt_bias, a_log, d_skip, norm_g, n_ctx):
    Bsz, TT, _ = z.shape
    dt = jax.nn.softplus(dt_raw.reshape(Bsz, TT, 2, SSD_HEADS) + dt_bias)
    a = dt * -jnp.exp(a_log)
    gate_cols = jnp.concatenate([dt, a], axis=-1)
    skip = jnp.repeat(d_skip, SSD_HEAD_DIM).reshape(1, SSD_INNER)
    outs = None
    for direction in (1, 0):
        gc = gate_cols[:, :, direction]
        grow = gc.transpose(0, 2, 1)
        gcol = jnp.pad(gc, ((0, 0), (0, 0), (0, SSD_GATE_COLS - 2 * SSD_HEADS)))
        extra = None if direction == 1 else (outs, z, skip, norm_g.reshape(1, SSD_INNER))
        outs = _ssd_direction(xbc_act, grow, gcol, n_ctx // SSD_CHUNK, direction == 1, extra)
    return outs


ML_STATE_COLS = 2 * ML_HEAD_DIM
ML_GATE_COLS = LANES


def _mlstm_kernel(*refs, reverse, combine):
    if combine:
        q_ref, k_ref, v_ref, grow_ref, gcol_ref, other_ref, o_ref, gain_ref, out_ref, s_scr, m_scr = refs
    else:
        q_ref, k_ref, v_ref, grow_ref, gcol_ref, out_ref, s_scr, m_scr = refs
    Q, D, H = ML_CHUNK, ML_HEAD_DIM, ML_HEADS

    @pl.when(pl.program_id(1) == 0)
    def _():
        s_scr[...] = jnp.zeros_like(s_scr)
        m_scr[...] = jnp.zeros_like(m_scr)

    row = lax.broadcasted_iota(jnp.int32, (Q, Q), 0)
    col = lax.broadcasted_iota(jnp.int32, (Q, Q), 1)
    not_after = (col >= row) if reverse else (col <= row)
    tri = not_after.astype(jnp.float32)
    last = 0 if reverse else Q - 1
    nt = (((1,), (1,)), ((), ()))
    grow = grow_ref[0]
    gcol = gcol_ref[0]
    b_cols = jnp.dot(tri, gcol, preferred_element_type=jnp.float32, precision=lax.Precision.HIGHEST)
    b_rows = lax.dot_general(grow, tri, nt, preferred_element_type=jnp.float32,
                             precision=lax.Precision.HIGHEST)
    one_col = (lax.broadcasted_iota(jnp.int32, (Q, D), 1) == 0).astype(jnp.float32)
    for h in range(H):
        hs = slice(h * D, (h + 1) * D)
        q = q_ref[0, :, hs].astype(jnp.bfloat16)
        k = (k_ref[0, :, hs] * D ** -0.5).astype(jnp.bfloat16)
        v_ext = jnp.concatenate([v_ref[0, :, hs], one_col], axis=1)
        bq = b_cols[:, H + h:H + h + 1]
        bs = b_rows[H + h:H + h + 1, :]
        i_q = gcol[:, h:h + 1]
        i_s = grow[h:h + 1, :]
        b_last = bs[:, last:last + 1]
        m0 = m_scr[h, 0:1, 0:1]
        state = s_scr[h]
        dm = jnp.where(not_after, bq - bs + i_s, NEG_INF)
        inter = bq + m0
        m_t = jnp.maximum(inter, jnp.max(dm, axis=1, keepdims=True))
        s_mat = lax.dot_general(q, k, nt, preferred_element_type=jnp.float32) * jnp.exp(dm - m_t)
        num = (jnp.dot(s_mat.astype(jnp.bfloat16), v_ext.astype(jnp.bfloat16), preferred_element_type=jnp.float32)
               + jnp.exp(inter - m_t) * jnp.dot(q, state.astype(jnp.bfloat16), preferred_element_type=jnp.float32))
        den = num[:, D:D + 1]
        hout = num[:, :D] / jnp.maximum(jnp.abs(den), jnp.exp(-m_t))
        g_q = b_last - bq + i_q
        m_loc = jnp.max(g_q, axis=0, keepdims=True)
        wv = (jnp.exp(g_q - m_loc) * v_ext).astype(jnp.bfloat16)
        s_loc = lax.dot_general(k, wv, (((0,), (0,)), ((), ())), preferred_element_type=jnp.float32)
        m_new = jnp.maximum(b_last + m0, m_loc)
        s_scr[h] = jnp.exp(b_last + m0 - m_new) * state + jnp.exp(m_loc - m_new) * s_loc
        m_scr[h] = jnp.broadcast_to(m_new, (SUBLANES, LANES))
        if combine:
            tot = hout + other_ref[0, :, hs]
            y = tot * lax.rsqrt(jnp.mean(tot * tot, axis=1, keepdims=True) + RMS_EPS) * gain_ref[:, hs]
            out_ref[0, :, hs] = y * jax.nn.sigmoid(o_ref[0, :, hs])
        else:
            out_ref[0, :, hs] = hout


def _mlstm_direction(q, k, v, grow, gcol, n_ctx_chunks, reverse, extra=None):
    Bsz, TT, _ = q.shape
    n_chunks = TT // ML_CHUNK
    if reverse:
        chunk = lambda c: jnp.where(c < n_ctx_chunks, n_ctx_chunks - 1 - c, n_chunks - 1 + n_ctx_chunks - c)
    else:
        chunk = lambda c: c
    seq_spec = pl.BlockSpec((1, ML_CHUNK, ML_INNER), lambda b, c: (b, chunk(c), 0))
    in_specs = [seq_spec, seq_spec, seq_spec,
                pl.BlockSpec((1, 2 * ML_HEADS, ML_CHUNK), lambda b, c: (b, 0, chunk(c))),
                pl.BlockSpec((1, ML_CHUNK, ML_GATE_COLS), lambda b, c: (b, chunk(c), 0))]
    args = [q, k, v, grow, gcol]
    if extra is not None:
        other, o_pre, gain = extra
        in_specs += [seq_spec, seq_spec, pl.BlockSpec((1, ML_INNER), lambda b, c: (0, 0))]
        args += [other, o_pre, gain]
    return pl.pallas_call(
        functools.partial(_mlstm_kernel, reverse=reverse, combine=extra is not None),
        grid=(Bsz, n_chunks),
        in_specs=in_specs,
        out_specs=seq_spec,
        out_shape=jax.ShapeDtypeStruct((Bsz, TT, ML_INNER), jnp.float32),
        scratch_shapes=[pltpu.VMEM((ML_HEADS, ML_HEAD_DIM, ML_STATE_COLS), jnp.float32),
                        pltpu.VMEM((ML_HEADS, SUBLANES, LANES), jnp.float32)],
        compiler_params=pltpu.CompilerParams(dimension_semantics=("arbitrary", "arbitrary"),
                                             vmem_limit_bytes=VMEM_LIMIT_BYTES),
        name="mlstm_bwd" if reverse else "mlstm_fwd",
    )(*args)


def _mlstm_block(q, k, v, o_pre, if_pre, gate_bias, norm_g, n_ctx):
    Bsz, TT, _ = q.shape
    gates = if_pre.reshape(Bsz, TT, 2, 2, ML_HEADS) + gate_bias
    gate_cols = jnp.concatenate([gates[:, :, :, 0], jax.nn.log_sigmoid(gates[:, :, :, 1])], axis=-1)
    outs = None
    for direction in (1, 0):
        gc = gate_cols[:, :, direction]
        grow = gc.transpose(0, 2, 1)
        gcol = jnp.pad(gc, ((0, 0), (0, 0), (0, ML_GATE_COLS - 2 * ML_HEADS)))
        extra = None if direction == 1 else (outs, o_pre, norm_g.reshape(1, ML_INNER))
        outs = _mlstm_direction(q, k, v, grow, gcol, n_ctx // ML_CHUNK, direction == 1, extra)
    return outs


def _rmsnorm(x, g):
    xf = x.astype(jnp.float32)
    y = xf * lax.rsqrt(jnp.mean(xf * xf, axis=-1, keepdims=True) + RMS_EPS)
    return (y * g.astype(jnp.float32)).astype(x.dtype)


def _split_cols(p):
    bounds = np.cumsum(IN_SPLITS)[:-1].tolist()
    return jnp.split(p, bounds, axis=-1)


def _heads(t, n):
    return t.reshape(t.shape[0], t.shape[1], n, -1)


def _axial_rope(rows):
    n_freq = ATT_HEAD_DIM // 4
    inv = ROPE_THETA ** (-jnp.arange(n_freq, dtype=jnp.float32) / n_freq)
    row = jnp.broadcast_to(jnp.arange(rows, dtype=jnp.float32)[:, None], (rows, GRID_W)).reshape(-1)
    col = jnp.broadcast_to(jnp.arange(GRID_W, dtype=jnp.float32)[None, :], (rows, GRID_W)).reshape(-1)
    ang = jnp.concatenate([row[:, None] * inv, col[:, None] * inv], axis=-1)
    return jnp.cos(ang), jnp.sin(ang)


def _apply_rope(t, cos, sin):
    half = t.shape[-1] // 2
    tf = t.astype(jnp.float32)
    t1, t2 = tf[..., :half], tf[..., half:]
    c, s = cos[:, None, :], sin[:, None, :]
    return jnp.concatenate([t1 * c - t2 * s, t1 * s + t2 * c], axis=-1).astype(t.dtype)


ATT_KV_CHUNK = 768


def _latent_attention(q, k, v, kc, vc):
    kk = jnp.concatenate([k, kc], axis=1)
    vv = jnp.concatenate([v, vc], axis=1)
    assert kk.shape[1] % ATT_KV_CHUNK == 0
    return _attention(q, kk, vv, ATT_KV_CHUNK)


def _context_attention(qc, kc, vc):
    return _attention(qc, kc, vc, kc.shape[1])


def _dwconv(x, w, b):
    out = lax.conv_general_dilated(x, w[:, None, :], window_strides=(1,), padding='SAME',
                                   dimension_numbers=('NWC', 'WIO', 'NWC'),
                                   feature_group_count=x.shape[-1])
    return out + b


def _merge_branches(branches, gate_pre, w_branch, w_out):
    Bsz, T, _ = gate_pre.shape
    stacked = jnp.stack(branches, axis=2)
    proj = jnp.einsum('btrw,rwd->btrd', stacked, w_branch)
    gates = jax.nn.sigmoid(gate_pre.reshape(Bsz, T, N_BRANCH, D_MODEL))
    return jnp.sum(gates * proj, axis=2) @ w_out


def _token_mixer(h, hc, w_in, qn, kn, conv_w, conv_b, dt_bias, a_log, d_skip, ssd_norm_g,
                 ml_bias, ml_norm_g, w_branch, w_out, cos, sin, ctx_out):
    Bsz = h.shape[0]
    aq, ak, av, sz, sxbc, sdt, mq, mk, mv, mo, mif, gpre = _split_cols(h @ w_in)
    caq, cak, cav, csz, csxbc, csdt, cmq, cmk, cmv, cmo, cmif, cgpre = _split_cols(hc @ w_in)
    q = _apply_rope(_rmsnorm(_heads(aq, ATT_HEADS), qn), cos, sin)
    k = _apply_rope(_rmsnorm(_heads(ak, ATT_KV_HEADS), kn), cos, sin)
    kc = _rmsnorm(_heads(cak, ATT_KV_HEADS), kn)
    vc = _heads(cav, ATT_KV_HEADS)
    att = _latent_attention(q, k, _heads(av, ATT_KV_HEADS), kc, vc)
    n_ctx = hc.shape[1]
    cat = lambda a, b: jnp.concatenate([a, b], axis=1)
    xbc_act = cat(jax.nn.silu(_dwconv(csxbc, conv_w, conv_b)), jax.nn.silu(_dwconv(sxbc, conv_w, conv_b)))
    ssd_all = _ssd_block(cat(csz, sz), xbc_act, cat(csdt, sdt), dt_bias, a_log, d_skip, ssd_norm_g, n_ctx)
    ssd_c, ssd_l = ssd_all[:, :n_ctx], ssd_all[:, n_ctx:]
    ml_all = _mlstm_block(cat(cmq, mq), cat(cmk, mk), cat(cmv, mv), cat(cmo, mo), cat(cmif, mif),
                          ml_bias, ml_norm_g, n_ctx)
    ml_c, ml_l = ml_all[:, :n_ctx], ml_all[:, n_ctx:]
    y = _merge_branches((att, ssd_l, ml_l), gpre, w_branch, w_out)
    if not ctx_out:
        return y, None
    att_c = _context_attention(_rmsnorm(_heads(caq, ATT_HEADS), qn), kc, vc)
    yc = _merge_branches((att_c, ssd_c, ml_c), cgpre, w_branch, w_out)
    return y, yc


def kernel(x, c, ctx, c_ctx, norm1_g, norm2_g, w_mod, b_mod, w_in, att_qnorm, att_knorm, ssd_conv_w, ssd_conv_b, ssd_dt_bias, ssd_a_log, ssd_d, ssd_norm, ml_gate_bias, ml_norm, w_branch, w_out, peer_wq, peer_subkeys, peer_u, peer_v):
    Bsz, S, _ = x.shape
    depth = w_in.shape[0]
    cos, sin = _axial_rope(S // GRID_W)
    silu_c = jax.nn.silu(c)
    silu_cc = jax.nn.silu(c_ctx)
    xc = ctx
    for l in range(depth):
        last = l == depth - 1
        sh1, sc1, g1, sh2, sc2, g2 = jnp.split((silu_c @ w_mod[l] + b_mod[l])[:, None, :], 6, axis=-1)
        csh1, csc1, cg1, csh2, csc2, cg2 = jnp.split(silu_cc @ w_mod[l] + b_mod[l], 6, axis=-1)
        h = _rmsnorm(x, norm1_g[l]) * (1.0 + sc1) + sh1
        hc = _rmsnorm(xc, norm1_g[l]) * (1.0 + csc1) + csh1
        y, yc = _token_mixer(h, hc, w_in[l], att_qnorm[l], att_knorm[l], ssd_conv_w[l], ssd_conv_b[l],
                             ssd_dt_bias[l], ssd_a_log[l], ssd_d[l], ssd_norm[l], ml_gate_bias[l], ml_norm[l],
                             w_branch[l], w_out[l], cos, sin, not last)
        x = x + g1 * y
        h2 = _rmsnorm(x, norm2_g[l]) * (1.0 + sc2) + sh2
        u_bf16 = peer_u[l].astype(jnp.bfloat16)
        vt_bf16 = peer_v[l].astype(jnp.bfloat16).reshape(PEER_EXPERTS // PEER_EBLK, PEER_EBLK, D_MODEL)
        vt_bf16 = vt_bf16.transpose(0, 2, 1)
        po = _peer(h2.reshape(-1, D_MODEL), peer_wq[l], peer_subkeys[l], u_bf16, vt_bf16)
        x = x + g2 * po.reshape(x.shape)
        if not last:
            xc = xc + cg1 * yc
            hc2 = _rmsnorm(xc, norm2_g[l]) * (1.0 + csc2) + csh2
            pc = _peer(hc2.reshape(-1, D_MODEL), peer_wq[l], peer_subkeys[l], u_bf16, vt_bf16)
            xc = xc + cg2 * pc.reshape(xc.shape)
    return x
```

```python
import functools
import math

import jax
import jax.numpy as jnp
import numpy as np
from jax import lax
from jax.experimental import pallas as pl
from jax.experimental.pallas import tpu as pltpu

D_MODEL = 1024
GRID_W = 64
RMS_EPS = 1e-6
N_BRANCH = 3
BRANCH_W = 512
ATT_HEADS = 8
ATT_KV_HEADS = 2
ATT_HEAD_DIM = 64
ATT_GROUP = ATT_HEADS // ATT_KV_HEADS
Q_BLOCK = 128
ROPE_THETA = 10000.0
SSD_HEADS = 8
SSD_HEAD_DIM = 64
SSD_INNER = SSD_HEADS * SSD_HEAD_DIM
SSD_GROUPS = 2
SSD_STATE = 64
SSD_XBC = SSD_INNER + 2 * SSD_GROUPS * SSD_STATE
SSD_CONV = 5
SSD_CHUNK = 128
ML_HEADS = 4
ML_HEAD_DIM = 128
ML_INNER = ML_HEADS * ML_HEAD_DIM
ML_CHUNK = 128
PEER_HEADS = 8
PEER_KEYS = 128
PEER_EXPERTS = PEER_KEYS * PEER_KEYS
PEER_TOPK = 16
PEER_QDIM = 256
PEER_HALF = PEER_QDIM // 2
IN_SPLITS = (ATT_HEADS * ATT_HEAD_DIM, ATT_KV_HEADS * ATT_HEAD_DIM, ATT_KV_HEADS * ATT_HEAD_DIM,
             SSD_INNER, SSD_XBC, 2 * SSD_HEADS,
             ML_INNER, ML_INNER, ML_INNER, ML_INNER, 4 * ML_HEADS,
             N_BRANCH * D_MODEL)

LANES = 128
SUBLANES = 8
VMEM_LIMIT_BYTES = 56 * 1024 * 1024

ROUTER_TOKENS = 256
ROUTER_CHUNK = LANES
PEER_TOKENS = 1024
PEER_TOKENS_SMALL = 512
PEER_ROWS = 8
PEER_EBLK = PEER_ROWS * PEER_KEYS
PEER_SLICE = 2 * PEER_KEYS
PEER_CHUNK = 256

NEG_INF = float("-inf")


def _sort_network(n):
    def merge(lo, hi, r):
        step = r * 2
        if step < hi - lo:
            yield from merge(lo, hi, step)
            yield from merge(lo + r, hi, step)
            yield from [(i, i + r) for i in range(lo + r, hi - r, step)]
        else:
            yield (lo, lo + r)

    def sort(lo, hi):
        if hi - lo >= 1:
            mid = lo + (hi - lo) // 2
            yield from sort(lo, mid)
            yield from sort(mid + 1, hi)
            yield from merge(lo, hi, 1)

    return tuple(sort(0, n - 1))


_SORT16 = _sort_network(PEER_TOPK)


def _merge_top(lists, n):
    lists = list(lists)
    depth = len(lists)
    sub = lax.broadcasted_iota(jnp.int32, lists[0].shape, 0)
    tops = []
    for i in range(n):
        head = lists[0]
        m = jnp.max(head, axis=0, keepdims=True)
        tops.append(m)
        live = min(depth, n - 1 - i)
        if live == 0:
            break
        first = jnp.min(jnp.where(head == m, sub, SUBLANES), axis=0, keepdims=True)
        pop = sub == first
        for r in range(live):
            nxt = lists[r + 1] if r + 1 < depth else NEG_INF
            lists[r] = jnp.where(pop, nxt, lists[r])
    return tops


def _sorted_top(s, n):
    xs = [s[r * SUBLANES:(r + 1) * SUBLANES, :] for r in range(s.shape[0] // SUBLANES)]
    for i, j in _SORT16:
        xs[i], xs[j] = jnp.maximum(xs[i], xs[j]), jnp.minimum(xs[i], xs[j])
    return _merge_top(xs, n)


def _rows_to_sublanes(rows, first_sublane, shape):
    sub = lax.broadcasted_iota(jnp.int32, shape, 0)
    out = jnp.zeros(shape, jnp.float32)
    for k, row in enumerate(rows):
        out = jnp.where(sub == first_sublane + k, row, out)
    return out


_CAND_LEN = (16, 8, 5, 4, 12, 4, 1, 0)


def _router_kernel(h_ref, wq_ref, sk_ref, rank_ref, e2_ref, cnt_ref, e1_ref, q_scr):
    q_scr[...] = jnp.dot(h_ref[...], wq_ref[...], preferred_element_type=jnp.float32).astype(jnp.bfloat16)
    nt = (((1,), (1,)), ((), ()))
    shape8 = (SUBLANES, ROUTER_CHUNK)
    sub = lax.broadcasted_iota(jnp.int32, shape8, 0)
    cand_len = jnp.zeros(shape8, jnp.int32)
    for g, n in enumerate(_CAND_LEN):
        cand_len = jnp.where(sub == g, n, cand_len)
    for h in range(PEER_HEADS):
        q1 = q_scr[:, (2 * h) * PEER_HALF:(2 * h + 1) * PEER_HALF]
        q2 = q_scr[:, (2 * h + 1) * PEER_HALF:(2 * h + 2) * PEER_HALF]
        s1 = lax.dot_general(sk_ref[2 * h], q1, nt, preferred_element_type=jnp.float32)
        s2 = lax.dot_general(sk_ref[2 * h + 1], q2, nt, preferred_element_type=jnp.float32)
        for c in range(ROUTER_TOKENS // ROUTER_CHUNK):
            cs = slice(c * ROUTER_CHUNK, (c + 1) * ROUTER_CHUNK)
            s1c = s1[:, cs]
            s2c = s2[:, cs]
            top1 = _sorted_top(s1c, PEER_TOPK)
            top2 = _sorted_top(s2c, PEER_TOPK)
            a_lo = _rows_to_sublanes(top1[:4], 0, shape8)
            b_lo = _rows_to_sublanes(top2[:3], 4, shape8)
            cands = []
            for r in range(PEER_TOPK):
                by_a = a_lo + top2[r]
                by_b = (top1[4 + r] + b_lo) if 4 + r < PEER_TOPK else by_a
                cands.append(jnp.where(r < cand_len, jnp.where(sub < 4, by_a, by_b), NEG_INF))
            tau = _merge_top(cands, PEER_TOPK)[-1]
            m1, m2 = top1[0], top2[0]
            z = jnp.zeros(shape8, jnp.float32)
            for cand in cands:
                z = z + jnp.where(cand >= tau, jnp.exp(cand - (m1 + m2)), 0.0)
            z = jnp.sum(z, axis=0, keepdims=True)
            rank = jnp.zeros(s2c.shape, jnp.float32)
            for t in top2:
                rank = rank + jnp.where(t > s2c, 1.0, 0.0)
            cnt = jnp.zeros(s1c.shape, jnp.float32)
            for t in top2:
                cnt = cnt + jnp.where(s1c + t >= tau, 1.0, 0.0)
            rank_ref[h, :, cs] = rank.astype(jnp.bfloat16)
            e2_ref[h, :, cs] = jnp.exp(s2c - m2).astype(jnp.bfloat16)
            cnt_ref[h, :, cs] = cnt
            e1_ref[h, :, cs] = jnp.exp(s1c - m1) * (0.5 / z)


def _peer_router(h2_bf16, wq, subkeys):
    n_tok = h2_bf16.shape[0]
    sk = subkeys.reshape(PEER_HEADS * 2, PEER_KEYS, PEER_HALF)
    shape = (PEER_HEADS, PEER_KEYS, n_tok)
    spec = pl.BlockSpec((PEER_HEADS, PEER_KEYS, ROUTER_TOKENS), lambda i: (0, 0, i))
    return pl.pallas_call(
        _router_kernel,
        grid=(n_tok // ROUTER_TOKENS,),
        in_specs=[
            pl.BlockSpec((ROUTER_TOKENS, D_MODEL), lambda i: (i, 0)),
            pl.BlockSpec((D_MODEL, PEER_HEADS * PEER_QDIM), lambda i: (0, 0)),
            pl.BlockSpec((PEER_HEADS * 2, PEER_KEYS, PEER_HALF), lambda i: (0, 0, 0)),
        ],
        out_specs=[spec, spec, spec, spec],
        out_shape=[jax.ShapeDtypeStruct(shape, jnp.bfloat16), jax.ShapeDtypeStruct(shape, jnp.bfloat16),
                   jax.ShapeDtypeStruct(shape, jnp.float32), jax.ShapeDtypeStruct(shape, jnp.float32)],
        scratch_shapes=[pltpu.VMEM((ROUTER_TOKENS, PEER_HEADS * PEER_QDIM), jnp.bfloat16)],
        compiler_params=pltpu.CompilerParams(dimension_semantics=("arbitrary",),
                                             vmem_limit_bytes=VMEM_LIMIT_BYTES),
        name="peer_router",
    )(h2_bf16, wq.astype(jnp.bfloat16), sk.astype(jnp.bfloat16))


GELU_C0 = math.sqrt(2.0 / math.pi)
GELU_C1 = 0.044715 * GELU_C0


def _peer_dense_kernel(h_ref, u_ref, vt_ref, rank_ref, e2_ref, cnt_ref, e1_ref, o_ref,
                       acc_scr, pre_scr, p_scr, rank_scr, e2_scr):
    j = pl.program_id(1)
    n_tok = h_ref.shape[0]
    n_slices = PEER_EBLK // PEER_SLICE
    n_chunks = n_tok // PEER_CHUNK
    nt = (((1,), (1,)), ((), ()))

    @pl.when(j == 0)
    def _():
        acc_scr[...] = jnp.zeros_like(acc_scr)

        def relayout(t, carry):
            lanes = pl.ds(pl.multiple_of(t * LANES, LANES), LANES)
            for h in range(PEER_HEADS):
                rank = rank_ref[h, :, lanes].astype(jnp.float32)
                e2 = e2_ref[h, :, lanes].astype(jnp.float32)
                for b in range(PEER_KEYS // SUBLANES):
                    rank_scr[t, b, h] = rank[b * SUBLANES:(b + 1) * SUBLANES]
                    e2_scr[t, b, h] = e2[b * SUBLANES:(b + 1) * SUBLANES]
            return carry

        lax.fori_loop(0, n_tok // LANES, relayout, 0)

    def pre_matmul(c, ap, slot):
        tok = pl.multiple_of(c * PEER_CHUNK, PEER_CHUNK)
        rows = slice(ap * PEER_SLICE, (ap + 1) * PEER_SLICE)
        pre_scr[slot] = lax.dot_general(u_ref[rows, :], h_ref[pl.ds(tok, PEER_CHUNK), :], nt,
                                        preferred_element_type=jnp.float32)

    def out_matmul(c, ap, slot):
        tok = pl.multiple_of(c * PEER_CHUNK, PEER_CHUNK)
        rows = slice(ap * PEER_SLICE, (ap + 1) * PEER_SLICE)
        acc_scr[:, pl.ds(tok, PEER_CHUNK)] += jnp.dot(vt_ref[0, :, rows], p_scr[slot],
                                                      preferred_element_type=jnp.float32)

    def gates(c, ap, slot):
        first_a = ap * (PEER_SLICE // PEER_KEYS)
        for t in range(PEER_CHUNK // LANES):
            lanes = pl.ds(pl.multiple_of(c * PEER_CHUNK + t * LANES, LANES), LANES)
            ts = slice(t * LANES, (t + 1) * LANES)
            tile = c * (PEER_CHUNK // LANES) + t
            for b in range(PEER_KEYS // SUBLANES):
                rank = [rank_scr[tile, b, h] for h in range(PEER_HEADS)]
                e2 = [e2_scr[tile, b, h] for h in range(PEER_HEADS)]
                for a in range(PEER_SLICE // PEER_KEYS):
                    rs = slice(a * PEER_KEYS + b * SUBLANES, a * PEER_KEYS + (b + 1) * SUBLANES)
                    terms = []
                    for h in range(PEER_HEADS):
                        cnt = cnt_ref[first_a + a, h:h + 1, lanes]
                        e1 = e1_ref[first_a + a, h:h + 1, lanes]
                        terms.append(jnp.where(rank[h] < cnt, e2[h], 0.0) * e1)
                    while len(terms) > 1:
                        terms = [x + y for x, y in zip(terms[::2], terms[1::2])]
                    x = pre_scr[slot, rs, ts]
                    act = x * (1.0 + jnp.tanh(x * (GELU_C0 + GELU_C1 * (x * x))))
                    pre_scr[slot, rs, ts] = terms[0] * act
        p_scr[slot] = pre_scr[slot].astype(jnp.bfloat16)

    p_scr[1] = jnp.zeros(p_scr.shape[1:], p_scr.dtype)
    pre_matmul(0, 0, 0)

    def body(c, carry):
        for ap in range(n_slices):
            slot = ap % 2
            if ap + 1 < n_slices:
                pre_matmul(c, ap + 1, 1 - slot)
            else:
                pre_matmul(jnp.minimum(c + 1, n_chunks - 1), 0, 1 - slot)
            gates(c, ap, slot)
            if ap > 0:
                out_matmul(c, ap - 1, 1 - slot)
            else:
                out_matmul(jnp.maximum(c - 1, 0), n_slices - 1, 1 - slot)
        return carry

    lax.fori_loop(0, n_chunks, body, 0)
    out_matmul(n_chunks - 1, n_slices - 1, 1)

    @pl.when(j == pl.num_programs(1) - 1)
    def _():
        o_ref[...] = acc_scr[...].T


def _peer_dense(h2_bf16, u_bf16, vt_bf16, rank, e2, cnt, e1, tb):
    n_tok = h2_bf16.shape[0]
    row_spec = pl.BlockSpec((PEER_ROWS, PEER_HEADS, tb), lambda i, j: (j, 0, i))
    tiled = (tb // LANES, PEER_KEYS // SUBLANES, PEER_HEADS, SUBLANES, LANES)
    key_spec = pl.BlockSpec((PEER_HEADS, PEER_KEYS, tb), lambda i, j: (0, 0, i))
    return pl.pallas_call(
        _peer_dense_kernel,
        grid=(n_tok // tb, PEER_EXPERTS // PEER_EBLK),
        in_specs=[
            pl.BlockSpec((tb, D_MODEL), lambda i, j: (i, 0)),
            pl.BlockSpec((PEER_EBLK, D_MODEL), lambda i, j: (j, 0)),
            pl.BlockSpec((1, D_MODEL, PEER_EBLK), lambda i, j: (j, 0, 0)),
            key_spec, key_spec, row_spec, row_spec,
        ],
        out_specs=pl.BlockSpec((tb, D_MODEL), lambda i, j: (i, 0)),
        out_shape=jax.ShapeDtypeStruct((n_tok, D_MODEL), jnp.float32),
        scratch_shapes=[pltpu.VMEM((D_MODEL, tb), jnp.float32),
                        pltpu.VMEM((2, PEER_SLICE, PEER_CHUNK), jnp.float32),
                        pltpu.VMEM((2, PEER_SLICE, PEER_CHUNK), jnp.bfloat16),
                        pltpu.VMEM(tiled, jnp.float32),
                        pltpu.VMEM(tiled, jnp.float32)],
        compiler_params=pltpu.CompilerParams(dimension_semantics=("arbitrary", "arbitrary"),
                                             vmem_limit_bytes=VMEM_LIMIT_BYTES),
        name="peer_dense",
    )(h2_bf16, u_bf16, vt_bf16, rank, e2, cnt, e1)


def _peer(h2, wq, subkeys, u_bf16, vt_bf16):
    n_tok = h2.shape[0]
    tb = PEER_TOKENS if n_tok % PEER_TOKENS == 0 else PEER_TOKENS_SMALL
    h2_bf16 = h2.astype(jnp.bfloat16)
    rank, e2, cnt, e1 = _peer_router(h2_bf16, wq, subkeys)
    cnt, e1 = cnt.transpose(1, 0, 2), e1.transpose(1, 0, 2)
    return _peer_dense(h2_bf16, u_bf16, vt_bf16, rank, e2, cnt, e1, tb)


ATT_Q_TOKENS = 256
ATT_V_COLS = 2 * ATT_HEAD_DIM


def _attention_kernel(q_ref, kt_ref, v_ref, o_ref, *, kv_chunk):
    tq = q_ref.shape[2]
    rows = ATT_GROUP * tq
    q = q_ref[0].reshape(rows, ATT_HEAD_DIM)
    n_chunks = kt_ref.shape[3] // kv_chunk

    def body(c, carry):
        m, acc = carry
        off = pl.multiple_of(c * kv_chunk, kv_chunk)
        s = jnp.dot(q, kt_ref[0, 0, :, pl.ds(off, kv_chunk)], preferred_element_type=jnp.float32)
        m_new = jnp.maximum(m, jnp.max(s, axis=-1, keepdims=True))
        p = jnp.exp(s - m_new).astype(jnp.bfloat16)
        acc = jnp.exp(m - m_new) * acc + jnp.dot(p, v_ref[0, 0, pl.ds(off, kv_chunk), :],
                                                 preferred_element_type=jnp.float32)
        return m_new, acc

    m0 = jnp.full((rows, 1), NEG_INF, jnp.float32)
    acc0 = jnp.zeros((rows, ATT_V_COLS), jnp.float32)
    _, acc = lax.fori_loop(0, n_chunks, body, (m0, acc0), unroll=True)
    out = acc[:, :ATT_HEAD_DIM] / acc[:, ATT_HEAD_DIM:ATT_HEAD_DIM + 1]
    o_ref[0] = out.reshape(ATT_GROUP, tq, ATT_HEAD_DIM)


def _attention(q, k, v, kv_chunk):
    Bsz, S = q.shape[0], q.shape[1]
    SK = k.shape[1]
    tq = min(ATT_Q_TOKENS, S)
    qh = (q * ATT_HEAD_DIM ** -0.5).astype(jnp.bfloat16).transpose(0, 2, 1, 3)
    kt = k.astype(jnp.bfloat16).transpose(0, 2, 3, 1)
    pad = jnp.concatenate([jnp.ones(v.shape[:-1] + (1,), v.dtype),
                           jnp.zeros(v.shape[:-1] + (ATT_V_COLS - ATT_HEAD_DIM - 1,), v.dtype)], axis=-1)
    vx = jnp.concatenate([v, pad], axis=-1).astype(jnp.bfloat16).transpose(0, 2, 1, 3)
    o = pl.pallas_call(
        functools.partial(_attention_kernel, kv_chunk=kv_chunk),
        grid=(Bsz, ATT_KV_HEADS, S // tq),
        in_specs=[
            pl.BlockSpec((1, ATT_GROUP, tq, ATT_HEAD_DIM), lambda b, g, i: (b, g, i, 0)),
            pl.BlockSpec((1, 1, ATT_HEAD_DIM, SK), lambda b, g, i: (b, g, 0, 0)),
            pl.BlockSpec((1, 1, SK, ATT_V_COLS), lambda b, g, i: (b, g, 0, 0)),
        ],
        out_specs=pl.BlockSpec((1, ATT_GROUP, tq, ATT_HEAD_DIM), lambda b, g, i: (b, g, i, 0)),
        out_shape=jax.ShapeDtypeStruct((Bsz, ATT_HEADS, S, ATT_HEAD_DIM), jnp.float32),
        compiler_params=pltpu.CompilerParams(dimension_semantics=("arbitrary", "arbitrary", "arbitrary"),
                                             vmem_limit_bytes=VMEM_LIMIT_BYTES),
        name="attention",
    )(qh, kt, vx)
    return o.transpose(0, 2, 1, 3).reshape(Bsz, S, ATT_HEADS * ATT_HEAD_DIM)


SSD_GROUP_HEADS = SSD_HEADS // SSD_GROUPS
SSD_GROUP_W = SSD_GROUP_HEADS * SSD_HEAD_DIM
SSD_GATE_COLS = LANES


def _ssd_kernel(*refs, reverse, combine):
    if combine:
        x_ref, bm_ref, cm_ref, grow_ref, gcol_ref, other_ref, z_ref, skip_ref, gain_ref, out_ref, s_scr = refs
    else:
        x_ref, bm_ref, cm_ref, grow_ref, gcol_ref, out_ref, s_scr = refs
    Q, P, N, H = SSD_CHUNK, SSD_HEAD_DIM, SSD_STATE, SSD_HEADS

    @pl.when(pl.program_id(1) == 0)
    def _():
        s_scr[...] = jnp.zeros_like(s_scr)

    row = lax.broadcasted_iota(jnp.int32, (Q, Q), 0)
    col = lax.broadcasted_iota(jnp.int32, (Q, Q), 1)
    not_after = (col >= row) if reverse else (col <= row)
    tri = not_after.astype(jnp.float32)
    last = 0 if reverse else Q - 1
    nt = (((1,), (1,)), ((), ()))
    grow = grow_ref[0]
    gcol = gcol_ref[0]
    a_cols = jnp.dot(tri, gcol, preferred_element_type=jnp.float32, precision=lax.Precision.HIGHEST)
    a_rows = lax.dot_general(grow, tri, nt, preferred_element_type=jnp.float32,
                             precision=lax.Precision.HIGHEST)
    ys = []
    for g in range(SSD_GROUPS):
        ns = slice(g * N, (g + 1) * N)
        bm = bm_ref[0, :, ns].astype(jnp.bfloat16)
        cm = cm_ref[0, :, ns].astype(jnp.bfloat16)
        cb = lax.dot_general(cm, bm, nt, preferred_element_type=jnp.float32)
        state = s_scr[g]
        y_off = jnp.dot(cm, state.astype(jnp.bfloat16), preferred_element_type=jnp.float32)
        xw, decay = [], []
        for r in range(SSD_GROUP_HEADS):
            h = g * SSD_GROUP_HEADS + r
            hs = slice(h * P, (h + 1) * P)
            aq = a_cols[:, H + h:H + h + 1]
            as_ = a_rows[H + h:H + h + 1, :]
            a_tot = as_[:, last:last + 1]
            xdt = x_ref[0, :, hs] * gcol[:, h:h + 1]
            lmat = jnp.exp(jnp.where(not_after, aq - as_, NEG_INF))
            y = jnp.dot((cb * lmat).astype(jnp.bfloat16), xdt.astype(jnp.bfloat16),
                        preferred_element_type=jnp.float32)
            ys.append(y + y_off[:, r * P:(r + 1) * P] * jnp.exp(aq))
            xw.append(jnp.exp(a_tot - aq) * xdt)
            decay.append(jnp.broadcast_to(jnp.exp(a_tot), (1, P)))
        s_loc = lax.dot_general(bm, jnp.concatenate(xw, axis=1).astype(jnp.bfloat16),
                                (((0,), (0,)), ((), ())), preferred_element_type=jnp.float32)
        s_scr[g] = state * jnp.concatenate(decay, axis=1) + s_loc
    y = jnp.concatenate(ys, axis=1)
    if combine:
        x = x_ref[0]
        z = z_ref[0]
        tot = (y + other_ref[0] + x * skip_ref[...]) * (z * jax.nn.sigmoid(z))
        out_ref[0] = tot * lax.rsqrt(jnp.mean(tot * tot, axis=1, keepdims=True) + RMS_EPS) * gain_ref[...]
    else:
        out_ref[0] = y


def _ssd_direction(xbc, grow, gcol, n_ctx_chunks, reverse, extra=None):
    Bsz, TT, _ = xbc.shape
    n_chunks = TT // SSD_CHUNK
    if reverse:
        chunk = lambda c: jnp.where(c < n_ctx_chunks, n_ctx_chunks - 1 - c, n_chunks - 1 + n_ctx_chunks - c)
    else:
        chunk = lambda c: c
    gn = SSD_GROUPS * SSD_STATE
    seq_spec = pl.BlockSpec((1, SSD_CHUNK, SSD_INNER), lambda b, c: (b, chunk(c), 0))
    vec_spec = pl.BlockSpec((1, SSD_INNER), lambda b, c: (0, 0))
    in_specs = [seq_spec,
                pl.BlockSpec((1, SSD_CHUNK, gn), lambda b, c: (b, chunk(c), SSD_INNER // gn)),
                pl.BlockSpec((1, SSD_CHUNK, gn), lambda b, c: (b, chunk(c), SSD_INNER // gn + 1)),
                pl.BlockSpec((1, 2 * SSD_HEADS, SSD_CHUNK), lambda b, c: (b, 0, chunk(c))),
                pl.BlockSpec((1, SSD_CHUNK, SSD_GATE_COLS), lambda b, c: (b, chunk(c), 0))]
    args = [xbc, xbc, xbc, grow, gcol]
    if extra is not None:
        other, z, skip, gain = extra
        in_specs += [seq_spec, seq_spec, vec_spec, vec_spec]
        args += [other, z, skip, gain]
    return pl.pallas_call(
        functools.partial(_ssd_kernel, reverse=reverse, combine=extra is not None),
        grid=(Bsz, n_chunks),
        in_specs=in_specs,
        out_specs=seq_spec,
        out_shape=jax.ShapeDtypeStruct((Bsz, TT, SSD_INNER), jnp.float32),
        scratch_shapes=[pltpu.VMEM((SSD_GROUPS, SSD_STATE, SSD_GROUP_W), jnp.float32)],
        compiler_params=pltpu.CompilerParams(dimension_semantics=("arbitrary", "arbitrary"),
                                             vmem_limit_bytes=VMEM_LIMIT_BYTES),
        name="ssd_bwd" if reverse else "ssd_fwd",
    )(*args)


def _ssd_block(z, xbc_act, dt_raw, dt_bias, a_log, d_skip, norm_g, n_ctx):
    Bsz, TT, _ = z.shape
    dt = jax.nn.softplus(dt_raw.reshape(Bsz, TT, 2, SSD_HEADS) + dt_bias)
    a = dt * -jnp.exp(a_log)
    gate_cols = jnp.concatenate([dt, a], axis=-1)
    skip = jnp.repeat(d_skip, SSD_HEAD_DIM).reshape(1, SSD_INNER)
    outs = None
    for direction in (1, 0):
        gc = gate_cols[:, :, direction]
        grow = gc.transpose(0, 2, 1)
        gcol = jnp.pad(gc, ((0, 0), (0, 0), (0, SSD_GATE_COLS - 2 * SSD_HEADS)))
        extra = None if direction == 1 else (outs, z, skip, norm_g.reshape(1, SSD_INNER))
        outs = _ssd_direction(xbc_act, grow, gcol, n_ctx // SSD_CHUNK, direction == 1, extra)
    return outs


ML_STATE_COLS = 2 * ML_HEAD_DIM
ML_GATE_COLS = LANES


def _mlstm_kernel(*refs, reverse, combine):
    if combine:
        q_ref, k_ref, v_ref, grow_ref, gcol_ref, other_ref, o_ref, gain_ref, out_ref, s_scr, m_scr = refs
    else:
        q_ref, k_ref, v_ref, grow_ref, gcol_ref, out_ref, s_scr, m_scr = refs
    Q, D, H = ML_CHUNK, ML_HEAD_DIM, ML_HEADS

    @pl.when(pl.program_id(1) == 0)
    def _():
        s_scr[...] = jnp.zeros_like(s_scr)
        m_scr[...] = jnp.zeros_like(m_scr)

    row = lax.broadcasted_iota(jnp.int32, (Q, Q), 0)
    col = lax.broadcasted_iota(jnp.int32, (Q, Q), 1)
    not_after = (col >= row) if reverse else (col <= row)
    tri = not_after.astype(jnp.float32)
    last = 0 if reverse else Q - 1
    nt = (((1,), (1,)), ((), ()))
    grow = grow_ref[0]
    gcol = gcol_ref[0]
    b_cols = jnp.dot(tri, gcol, preferred_element_type=jnp.float32, precision=lax.Precision.HIGHEST)
    b_rows = lax.dot_general(grow, tri, nt, preferred_element_type=jnp.float32,
                             precision=lax.Precision.HIGHEST)
    one_col = (lax.broadcasted_iota(jnp.int32, (Q, D), 1) == 0).astype(jnp.float32)
    for h in range(H):
        hs = slice(h * D, (h + 1) * D)
        q = q_ref[0, :, hs].astype(jnp.bfloat16)
        k = (k_ref[0, :, hs] * D ** -0.5).astype(jnp.bfloat16)
        v_ext = jnp.concatenate([v_ref[0, :, hs], one_col], axis=1)
        bq = b_cols[:, H + h:H + h + 1]
        bs = b_rows[H + h:H + h + 1, :]
        i_q = gcol[:, h:h + 1]
        i_s = grow[h:h + 1, :]
        b_last = bs[:, last:last + 1]
        m0 = m_scr[h, 0:1, 0:1]
        state = s_scr[h]
        dm = jnp.where(not_after, bq - bs + i_s, NEG_INF)
        inter = bq + m0
        m_t = jnp.maximum(inter, jnp.max(dm, axis=1, keepdims=True))
        s_mat = lax.dot_general(q, k, nt, preferred_element_type=jnp.float32) * jnp.exp(dm - m_t)
        num = (jnp.dot(s_mat.astype(jnp.bfloat16), v_ext.astype(jnp.bfloat16), preferred_element_type=jnp.float32)
               + jnp.exp(inter - m_t) * jnp.dot(q, state.astype(jnp.bfloat16), preferred_element_type=jnp.float32))
        den = num[:, D:D + 1]
        hout = num[:, :D] / jnp.maximum(jnp.abs(den), jnp.exp(-m_t))
        g_q = b_last - bq + i_q
        m_loc = jnp.max(g_q, axis=0, keepdims=True)
        wv = (jnp.exp(g_q - m_loc) * v_ext).astype(jnp.bfloat16)
        s_loc = lax.dot_general(k, wv, (((0,), (0,)), ((), ())), preferred_element_type=jnp.float32)
        m_new = jnp.maximum(b_last + m0, m_loc)
        s_scr[h] = jnp.exp(b_last + m0 - m_new) * state + jnp.exp(m_loc - m_new) * s_loc
        m_scr[h] = jnp.broadcast_to(m_new, (SUBLANES, LANES))
        if combine:
            tot = hout + other_ref[0, :, hs]
            y = tot * lax.rsqrt(jnp.mean(tot * tot, axis=1, keepdims=True) + RMS_EPS) * gain_ref[:, hs]
            out_ref[0, :, hs] = y * jax.nn.sigmoid(o_ref[0, :, hs])
        else:
            out_ref[0, :, hs] = hout


def _mlstm_direction(q, k, v, grow, gcol, n_ctx_chunks, reverse, extra=None):
    Bsz, TT, _ = q.shape
    n_chunks = TT // ML_CHUNK
    if reverse:
        chunk = lambda c: jnp.where(c < n_ctx_chunks, n_ctx_chunks - 1 - c, n_chunks - 1 + n_ctx_chunks - c)
    else:
        chunk = lambda c: c
    seq_spec = pl.BlockSpec((1, ML_CHUNK, ML_INNER), lambda b, c: (b, chunk(c), 0))
    in_specs = [seq_spec, seq_spec, seq_spec,
                pl.BlockSpec((1, 2 * ML_HEADS, ML_CHUNK), lambda b, c: (b, 0, chunk(c))),
                pl.BlockSpec((1, ML_CHUNK, ML_GATE_COLS), lambda b, c: (b, chunk(c), 0))]
    args = [q, k, v, grow, gcol]
    if extra is not None:
        other, o_pre, gain = extra
        in_specs += [seq_spec, seq_spec, pl.BlockSpec((1, ML_INNER), lambda b, c: (0, 0))]
        args += [other, o_pre, gain]
    return pl.pallas_call(
        functools.partial(_mlstm_kernel, reverse=reverse, combine=extra is not None),
        grid=(Bsz, n_chunks),
        in_specs=in_specs,
        out_specs=seq_spec,
        out_shape=jax.ShapeDtypeStruct((Bsz, TT, ML_INNER), jnp.float32),
        scratch_shapes=[pltpu.VMEM((ML_HEADS, ML_HEAD_DIM, ML_STATE_COLS), jnp.float32),
                        pltpu.VMEM((ML_HEADS, SUBLANES, LANES), jnp.float32)],
        compiler_params=pltpu.CompilerParams(dimension_semantics=("arbitrary", "arbitrary"),
                                             vmem_limit_bytes=VMEM_LIMIT_BYTES),
        name="mlstm_bwd" if reverse else "mlstm_fwd",
    )(*args)


def _mlstm_block(q, k, v, o_pre, if_pre, gate_bias, norm_g, n_ctx):
    Bsz, TT, _ = q.shape
    gates = if_pre.reshape(Bsz, TT, 2, 2, ML_HEADS) + gate_bias
    gate_cols = jnp.concatenate([gates[:, :, :, 0], jax.nn.log_sigmoid(gates[:, :, :, 1])], axis=-1)
    outs = None
    for direction in (1, 0):
        gc = gate_cols[:, :, direction]
        grow = gc.transpose(0, 2, 1)
        gcol = jnp.pad(gc, ((0, 0), (0, 0), (0, ML_GATE_COLS - 2 * ML_HEADS)))
        extra = None if direction == 1 else (outs, o_pre, norm_g.reshape(1, ML_INNER))
        outs = _mlstm_direction(q, k, v, grow, gcol, n_ctx // ML_CHUNK, direction == 1, extra)
    return outs


def _rmsnorm(x, g):
    xf = x.astype(jnp.float32)
    y = xf * lax.rsqrt(jnp.mean(xf * xf, axis=-1, keepdims=True) + RMS_EPS)
    return (y * g.astype(jnp.float32)).astype(x.dtype)


def _split_cols(p):
    bounds = np.cumsum(IN_SPLITS)[:-1].tolist()
    return jnp.split(p, bounds, axis=-1)


def _heads(t, n):
    return t.reshape(t.shape[0], t.shape[1], n, -1)


def _axial_rope(rows):
    n_freq = ATT_HEAD_DIM // 4
    inv = ROPE_THETA ** (-jnp.arange(n_freq, dtype=jnp.float32) / n_freq)
    row = jnp.broadcast_to(jnp.arange(rows, dtype=jnp.float32)[:, None], (rows, GRID_W)).reshape(-1)
    col = jnp.broadcast_to(jnp.arange(GRID_W, dtype=jnp.float32)[None, :], (rows, GRID_W)).reshape(-1)
    ang = jnp.concatenate([row[:, None] * inv, col[:, None] * inv], axis=-1)
    return jnp.cos(ang), jnp.sin(ang)


def _apply_rope(t, cos, sin):
    half = t.shape[-1] // 2
    tf = t.astype(jnp.float32)
    t1, t2 = tf[..., :half], tf[..., half:]
    c, s = cos[:, None, :], sin[:, None, :]
    return jnp.concatenate([t1 * c - t2 * s, t1 * s + t2 * c], axis=-1).astype(t.dtype)


ATT_KV_CHUNK = 768


def _latent_attention(q, k, v, kc, vc):
    kk = jnp.concatenate([k, kc], axis=1)
    vv = jnp.concatenate([v, vc], axis=1)
    assert kk.shape[1] % ATT_KV_CHUNK == 0
    return _attention(q, kk, vv, ATT_KV_CHUNK)


def _context_attention(qc, kc, vc):
    return _attention(qc, kc, vc, kc.shape[1])


def _dwconv(x, w, b):
    out = lax.conv_general_dilated(x, w[:, None, :], window_strides=(1,), padding='SAME',
                                   dimension_numbers=('NWC', 'WIO', 'NWC'),
                                   feature_group_count=x.shape[-1])
    return out + b


def _merge_branches(branches, gate_pre, w_branch, w_out):
    Bsz, T, _ = gate_pre.shape
    stacked = jnp.stack(branches, axis=2)
    proj = jnp.einsum('btrw,rwd->btrd', stacked, w_branch)
    gates = jax.nn.sigmoid(gate_pre.reshape(Bsz, T, N_BRANCH, D_MODEL))
    return jnp.sum(gates * proj, axis=2) @ w_out


def _token_mixer(h, hc, w_in, qn, kn, conv_w, conv_b, dt_bias, a_log, d_skip, ssd_norm_g,
                 ml_bias, ml_norm_g, w_branch, w_out, cos, sin, ctx_out):
    Bsz = h.shape[0]
    aq, ak, av, sz, sxbc, sdt, mq, mk, mv, mo, mif, gpre = _split_cols(h @ w_in)
    caq, cak, cav, csz, csxbc, csdt, cmq, cmk, cmv, cmo, cmif, cgpre = _split_cols(hc @ w_in)
    q = _apply_rope(_rmsnorm(_heads(aq, ATT_HEADS), qn), cos, sin)
    k = _apply_rope(_rmsnorm(_heads(ak, ATT_KV_HEADS), kn), cos, sin)
    kc = _rmsnorm(_heads(cak, ATT_KV_HEADS), kn)
    vc = _heads(cav, ATT_KV_HEADS)
    att = _latent_attention(q, k, _heads(av, ATT_KV_HEADS), kc, vc)
    n_ctx = hc.shape[1]
    cat = lambda a, b: jnp.concatenate([a, b], axis=1)
    xbc_act = cat(jax.nn.silu(_dwconv(csxbc, conv_w, conv_b)), jax.nn.silu(_dwconv(sxbc, conv_w, conv_b)))
    ssd_all = _ssd_block(cat(csz, sz), xbc_act, cat(csdt, sdt), dt_bias, a_log, d_skip, ssd_norm_g, n_ctx)
    ssd_c, ssd_l = ssd_all[:, :n_ctx], ssd_all[:, n_ctx:]
    ml_all = _mlstm_block(cat(cmq, mq), cat(cmk, mk), cat(cmv, mv), cat(cmo, mo), cat(cmif, mif),
                          ml_bias, ml_norm_g, n_ctx)
    ml_c, ml_l = ml_all[:, :n_ctx], ml_all[:, n_ctx:]
    y = _merge_branches((att, ssd_l, ml_l), gpre, w_branch, w_out)
    if not ctx_out:
        return y, None
    att_c = _context_attention(_rmsnorm(_heads(caq, ATT_HEADS), qn), kc, vc)
    yc = _merge_branches((att_c, ssd_c, ml_c), cgpre, w_branch, w_out)
    return y, yc


def kernel(x, c, ctx, c_ctx, norm1_g, norm2_g, w_mod, b_mod, w_in, att_qnorm, att_knorm, ssd_conv_w, ssd_conv_b, ssd_dt_bias, ssd_a_log, ssd_d, ssd_norm, ml_gate_bias, ml_norm, w_branch, w_out, peer_wq, peer_subkeys, peer_u, peer_v):
    Bsz, S, _ = x.shape
    depth = w_in.shape[0]
    cos, sin = _axial_rope(S // GRID_W)
    silu_c = jax.nn.silu(c)
    silu_cc = jax.nn.silu(c_ctx)
    xc = ctx
    for l in range(depth):
        last = l == depth - 1
        sh1, sc1, g1, sh2, sc2, g2 = jnp.split((silu_c @ w_mod[l] + b_mod[l])[:, None, :], 6, axis=-1)
        csh1, csc1, cg1, csh2, csc2, cg2 = jnp.split(silu_cc @ w_mod[l] + b_mod[l], 6, axis=-1)
        h = _rmsnorm(x, norm1_g[l]) * (1.0 + sc1) + sh1
        hc = _rmsnorm(xc, norm1_g[l]) * (1.0 + csc1) + csh1
        y, yc = _token_mixer(h, hc, w_in[l], att_qnorm[l], att_knorm[l], ssd_conv_w[l], ssd_conv_b[l],
                             ssd_dt_bias[l], ssd_a_log[l], ssd_d[l], ssd_norm[l], ml_gate_bias[l], ml_norm[l],
                             w_branch[l], w_out[l], cos, sin, not last)
        x = x + g1 * y
        h2 = _rmsnorm(x, norm2_g[l]) * (1.0 + sc2) + sh2
        u_bf16 = peer_u[l].astype(jnp.bfloat16)
        vt_bf16 = peer_v[l].astype(jnp.bfloat16).reshape(PEER_EXPERTS // PEER_EBLK, PEER_EBLK, D_MODEL)
        vt_bf16 = vt_bf16.transpose(0, 2, 1)
        po = _peer(h2.reshape(-1, D_MODEL), peer_wq[l], peer_subkeys[l], u_bf16, vt_bf16)
        x = x + g2 * po.reshape(x.shape)
        if not last:
            xc = xc + cg1 * yc
            hc2 = _rmsnorm(xc, norm2_g[l]) * (1.0 + csc2) + csh2
            pc = _peer(hc2.reshape(-1, D_MODEL), peer_wq[l], peer_subkeys[l], u_bf16, vt_bf16)
            xc = xc + cg2 * pc.reshape(xc.shape)
    return x
```

```python
import functools
import math

import jax
import jax.numpy as jnp
import numpy as np
from jax import lax
from jax.experimental import pallas as pl
from jax.experimental.pallas import tpu as pltpu

D_MODEL = 1024
GRID_W = 64
RMS_EPS = 1e-6
N_BRANCH = 3
BRANCH_W = 512
ATT_HEADS = 8
ATT_KV_HEADS = 2
ATT_HEAD_DIM = 64
ATT_GROUP = ATT_HEADS // ATT_KV_HEADS
Q_BLOCK = 128
ROPE_THETA = 10000.0
SSD_HEADS = 8
SSD_HEAD_DIM = 64
SSD_INNER = SSD_HEADS * SSD_HEAD_DIM
SSD_GROUPS = 2
SSD_STATE = 64
SSD_XBC = SSD_INNER + 2 * SSD_GROUPS * SSD_STATE
SSD_CONV = 5
SSD_CHUNK = 128
ML_HEADS = 4
ML_HEAD_DIM = 128
ML_INNER = ML_HEADS * ML_HEAD_DIM
ML_CHUNK = 128
PEER_HEADS = 8
PEER_KEYS = 128
PEER_EXPERTS = PEER_KEYS * PEER_KEYS
PEER_TOPK = 16
PEER_QDIM = 256
PEER_HALF = PEER_QDIM // 2
IN_SPLITS = (ATT_HEADS * ATT_HEAD_DIM, ATT_KV_HEADS * ATT_HEAD_DIM, ATT_KV_HEADS * ATT_HEAD_DIM,
             SSD_INNER, SSD_XBC, 2 * SSD_HEADS,
             ML_INNER, ML_INNER, ML_INNER, ML_INNER, 4 * ML_HEADS,
             N_BRANCH * D_MODEL)

LANES = 128
SUBLANES = 8
VMEM_LIMIT_BYTES = 56 * 1024 * 1024

ROUTER_TOKENS = 256
ROUTER_CHUNK = LANES
PEER_TOKENS = 1024
PEER_TOKENS_SMALL = 512
PEER_ROWS = 8
PEER_EBLK = PEER_ROWS * PEER_KEYS
PEER_SLICE = 2 * PEER_KEYS
PEER_CHUNK = 256

NEG_INF = float("-inf")


def _sort_network(n):
    def merge(lo, hi, r):
        step = r * 2
        if step < hi - lo:
            yield from merge(lo, hi, step)
            yield from merge(lo + r, hi, step)
            yield from [(i, i + r) for i in range(lo + r, hi - r, step)]
        else:
            yield (lo, lo + r)

    def sort(lo, hi):
        if hi - lo >= 1:
            mid = lo + (hi - lo) // 2
            yield from sort(lo, mid)
            yield from sort(mid + 1, hi)
            yield from merge(lo, hi, 1)

    return tuple(sort(0, n - 1))


_SORT16 = _sort_network(PEER_TOPK)


def _merge_top(lists, n):
    lists = list(lists)
    depth = len(lists)
    sub = lax.broadcasted_iota(jnp.int32, lists[0].shape, 0)
    tops = []
    for i in range(n):
        head = lists[0]
        m = jnp.max(head, axis=0, keepdims=True)
        tops.append(m)
        live = min(depth, n - 1 - i)
        if live == 0:
            break
        first = jnp.min(jnp.where(head == m, sub, SUBLANES), axis=0, keepdims=True)
        pop = sub == first
        for r in range(live):
            nxt = lists[r + 1] if r + 1 < depth else NEG_INF
            lists[r] = jnp.where(pop, nxt, lists[r])
    return tops


def _sorted_top(s, n):
    xs = [s[r * SUBLANES:(r + 1) * SUBLANES, :] for r in range(s.shape[0] // SUBLANES)]
    for i, j in _SORT16:
        xs[i], xs[j] = jnp.maximum(xs[i], xs[j]), jnp.minimum(xs[i], xs[j])
    return _merge_top(xs, n)


def _rows_to_sublanes(rows, first_sublane, shape):
    sub = lax.broadcasted_iota(jnp.int32, shape, 0)
    out = jnp.zeros(shape, jnp.float32)
    for k, row in enumerate(rows):
        out = jnp.where(sub == first_sublane + k, row, out)
    return out


_CAND_LEN = (16, 8, 5, 4, 12, 4, 1, 0)


def _router_kernel(h_ref, wq_ref, sk_ref, rank_ref, e2_ref, cnt_ref, e1_ref, q_scr):
    q_scr[...] = jnp.dot(h_ref[...], wq_ref[...], preferred_element_type=jnp.float32).astype(jnp.bfloat16)
    nt = (((1,), (1,)), ((), ()))
    shape8 = (SUBLANES, ROUTER_CHUNK)
    sub = lax.broadcasted_iota(jnp.int32, shape8, 0)
    cand_len = jnp.zeros(shape8, jnp.int32)
    for g, n in enumerate(_CAND_LEN):
        cand_len = jnp.where(sub == g, n, cand_len)
    for h in range(PEER_HEADS):
        q1 = q_scr[:, (2 * h) * PEER_HALF:(2 * h + 1) * PEER_HALF]
        q2 = q_scr[:, (2 * h + 1) * PEER_HALF:(2 * h + 2) * PEER_HALF]
        s1 = lax.dot_general(sk_ref[2 * h], q1, nt, preferred_element_type=jnp.float32)
        s2 = lax.dot_general(sk_ref[2 * h + 1], q2, nt, preferred_element_type=jnp.float32)
        for c in range(ROUTER_TOKENS // ROUTER_CHUNK):
            cs = slice(c * ROUTER_CHUNK, (c + 1) * ROUTER_CHUNK)
            s1c = s1[:, cs]
            s2c = s2[:, cs]
            top1 = _sorted_top(s1c, PEER_TOPK)
            top2 = _sorted_top(s2c, PEER_TOPK)
            a_lo = _rows_to_sublanes(top1[:4], 0, shape8)
            b_lo = _rows_to_sublanes(top2[:3], 4, shape8)
            cands = []
            for r in range(PEER_TOPK):
                by_a = a_lo + top2[r]
                by_b = (top1[4 + r] + b_lo) if 4 + r < PEER_TOPK else by_a
                cands.append(jnp.where(r < cand_len, jnp.where(sub < 4, by_a, by_b), NEG_INF))
            tau = _merge_top(cands, PEER_TOPK)[-1]
            m1, m2 = top1[0], top2[0]
            z = jnp.zeros(shape8, jnp.float32)
            for cand in cands:
                z = z + jnp.where(cand >= tau, jnp.exp(cand - (m1 + m2)), 0.0)
            z = jnp.sum(z, axis=0, keepdims=True)
            rank = jnp.zeros(s2c.shape, jnp.float32)
            for t in top2:
                rank = rank + jnp.where(t > s2c, 1.0, 0.0)
            cnt = jnp.zeros(s1c.shape, jnp.float32)
            for t in top2:
                cnt = cnt + jnp.where(s1c + t >= tau, 1.0, 0.0)
            rank_ref[h, :, cs] = rank.astype(jnp.bfloat16)
            e2_ref[h, :, cs] = jnp.exp(s2c - m2).astype(jnp.bfloat16)
            cnt_ref[h, :, cs] = cnt
            e1_ref[h, :, cs] = jnp.exp(s1c - m1) * (0.5 / z)


def _peer_router(h2_bf16, wq, subkeys):
    n_tok = h2_bf16.shape[0]
    sk = subkeys.reshape(PEER_HEADS * 2, PEER_KEYS, PEER_HALF)
    shape = (PEER_HEADS, PEER_KEYS, n_tok)
    spec = pl.BlockSpec((PEER_HEADS, PEER_KEYS, ROUTER_TOKENS), lambda i: (0, 0, i))
    return pl.pallas_call(
        _router_kernel,
        grid=(n_tok // ROUTER_TOKENS,),
        in_specs=[
            pl.BlockSpec((ROUTER_TOKENS, D_MODEL), lambda i: (i, 0)),
            pl.BlockSpec((D_MODEL, PEER_HEADS * PEER_QDIM), lambda i: (0, 0)),
            pl.BlockSpec((PEER_HEADS * 2, PEER_KEYS, PEER_HALF), lambda i: (0, 0, 0)),
        ],
        out_specs=[spec, spec, spec, spec],
        out_shape=[jax.ShapeDtypeStruct(shape, jnp.bfloat16), jax.ShapeDtypeStruct(shape, jnp.bfloat16),
                   jax.ShapeDtypeStruct(shape, jnp.float32), jax.ShapeDtypeStruct(shape, jnp.float32)],
        scratch_shapes=[pltpu.VMEM((ROUTER_TOKENS, PEER_HEADS * PEER_QDIM), jnp.bfloat16)],
        compiler_params=pltpu.CompilerParams(dimension_semantics=("arbitrary",),
                                             vmem_limit_bytes=VMEM_LIMIT_BYTES),
        name="peer_router",
    )(h2_bf16, wq.astype(jnp.bfloat16), sk.astype(jnp.bfloat16))


GELU_C0 = math.sqrt(2.0 / math.pi)
GELU_C1 = 0.044715 * GELU_C0


def _peer_dense_kernel(h_ref, u_ref, vt_ref, rank_ref, e2_ref, cnt_ref, e1_ref, o_ref,
                       acc_scr, pre_scr, p_scr, rank_scr, e2_scr):
    j = pl.program_id(1)
    n_tok = h_ref.shape[0]
    n_slices = PEER_EBLK // PEER_SLICE
    n_chunks = n_tok // PEER_CHUNK
    nt = (((1,), (1,)), ((), ()))

    @pl.when(j == 0)
    def _():
        acc_scr[...] = jnp.zeros_like(acc_scr)

        def relayout(t, carry):
            lanes = pl.ds(pl.multiple_of(t * LANES, LANES), LANES)
            for h in range(PEER_HEADS):
                rank = rank_ref[h, :, lanes].astype(jnp.float32)
                e2 = e2_ref[h, :, lanes].astype(jnp.float32)
                for b in range(PEER_KEYS // SUBLANES):
                    rank_scr[t, b, h] = rank[b * SUBLANES:(b + 1) * SUBLANES]
                    e2_scr[t, b, h] = e2[b * SUBLANES:(b + 1) * SUBLANES]
            return carry

        lax.fori_loop(0, n_tok // LANES, relayout, 0)

    def pre_matmul(c, ap, slot):
        tok = pl.multiple_of(c * PEER_CHUNK, PEER_CHUNK)
        rows = slice(ap * PEER_SLICE, (ap + 1) * PEER_SLICE)
        pre_scr[slot] = lax.dot_general(u_ref[rows, :], h_ref[pl.ds(tok, PEER_CHUNK), :], nt,
                                        preferred_element_type=jnp.float32)

    def out_matmul(c, ap, slot):
        tok = pl.multiple_of(c * PEER_CHUNK, PEER_CHUNK)
        rows = slice(ap * PEER_SLICE, (ap + 1) * PEER_SLICE)
        acc_scr[:, pl.ds(tok, PEER_CHUNK)] += jnp.dot(vt_ref[0, :, rows], p_scr[slot],
                                                      preferred_element_type=jnp.float32)

    def gates(c, ap, slot):
        first_a = ap * (PEER_SLICE // PEER_KEYS)
        for t in range(PEER_CHUNK // LANES):
            lanes = pl.ds(pl.multiple_of(c * PEER_CHUNK + t * LANES, LANES), LANES)
            ts = slice(t * LANES, (t + 1) * LANES)
            tile = c * (PEER_CHUNK // LANES) + t
            for b in range(PEER_KEYS // SUBLANES):
                rank = [rank_scr[tile, b, h] for h in range(PEER_HEADS)]
                e2 = [e2_scr[tile, b, h] for h in range(PEER_HEADS)]
                for a in range(PEER_SLICE // PEER_KEYS):
                    rs = slice(a * PEER_KEYS + b * SUBLANES, a * PEER_KEYS + (b + 1) * SUBLANES)
                    terms = []
                    for h in range(PEER_HEADS):
                        cnt = cnt_ref[first_a + a, h:h + 1, lanes]
                        e1 = e1_ref[first_a + a, h:h + 1, lanes]
                        terms.append(jnp.where(rank[h] < cnt, e2[h], 0.0) * e1)
                    while len(terms) > 1:
                        terms = [x + y for x, y in zip(terms[::2], terms[1::2])]
                    x = pre_scr[slot, rs, ts]
                    act = x * (1.0 + jnp.tanh(x * (GELU_C0 + GELU_C1 * (x * x))))
                    pre_scr[slot, rs, ts] = terms[0] * act
        p_scr[slot] = pre_scr[slot].astype(jnp.bfloat16)

    p_scr[1] = jnp.zeros(p_scr.shape[1:], p_scr.dtype)
    pre_matmul(0, 0, 0)

    def body(c, carry):
        for ap in range(n_slices):
            slot = ap % 2
            if ap + 1 < n_slices:
                pre_matmul(c, ap + 1, 1 - slot)
            else:
                pre_matmul(jnp.minimum(c + 1, n_chunks - 1), 0, 1 - slot)
            gates(c, ap, slot)
            if ap > 0:
                out_matmul(c, ap - 1, 1 - slot)
            else:
                out_matmul(jnp.maximum(c - 1, 0), n_slices - 1, 1 - slot)
        return carry

    lax.fori_loop(0, n_chunks, body, 0)
    out_matmul(n_chunks - 1, n_slices - 1, 1)

    @pl.when(j == pl.num_programs(1) - 1)
    def _():
        o_ref[...] = acc_scr[...].T


def _peer_dense(h2_bf16, u_bf16, vt_bf16, rank, e2, cnt, e1, tb):
    n_tok = h2_bf16.shape[0]
    row_spec = pl.BlockSpec((PEER_ROWS, PEER_HEADS, tb), lambda i, j: (j, 0, i))
    tiled = (tb // LANES, PEER_KEYS // SUBLANES, PEER_HEADS, SUBLANES, LANES)
    key_spec = pl.BlockSpec((PEER_HEADS, PEER_KEYS, tb), lambda i, j: (0, 0, i))
    return pl.pallas_call(
        _peer_dense_kernel,
        grid=(n_tok // tb, PEER_EXPERTS // PEER_EBLK),
        in_specs=[
            pl.BlockSpec((tb, D_MODEL), lambda i, j: (i, 0)),
            pl.BlockSpec((PEER_EBLK, D_MODEL), lambda i, j: (j, 0)),
            pl.BlockSpec((1, D_MODEL, PEER_EBLK), lambda i, j: (j, 0, 0)),
            key_spec, key_spec, row_spec, row_spec,
        ],
        out_specs=pl.BlockSpec((tb, D_MODEL), lambda i, j: (i, 0)),
        out_shape=jax.ShapeDtypeStruct((n_tok, D_MODEL), jnp.float32),
        scratch_shapes=[pltpu.VMEM((D_MODEL, tb), jnp.float32),
                        pltpu.VMEM((2, PEER_SLICE, PEER_CHUNK), jnp.float32),
                        pltpu.VMEM((2, PEER_SLICE, PEER_CHUNK), jnp.bfloat16),
                        pltpu.VMEM(tiled, jnp.float32),
                        pltpu.VMEM(tiled, jnp.float32)],
        compiler_params=pltpu.CompilerParams(dimension_semantics=("arbitrary", "arbitrary"),
                                             vmem_limit_bytes=VMEM_LIMIT_BYTES),
        name="peer_dense",
    )(h2_bf16, u_bf16, vt_bf16, rank, e2, cnt, e1)


def _peer(h2, wq, subkeys, u_bf16, vt_bf16):
    n_tok = h2.shape[0]
    tb = PEER_TOKENS if n_tok % PEER_TOKENS == 0 else PEER_TOKENS_SMALL
    h2_bf16 = h2.astype(jnp.bfloat16)
    rank, e2, cnt, e1 = _peer_router(h2_bf16, wq, subkeys)
    cnt, e1 = cnt.transpose(1, 0, 2), e1.transpose(1, 0, 2)
    return _peer_dense(h2_bf16, u_bf16, vt_bf16, rank, e2, cnt, e1, tb)


ATT_Q_TOKENS = 256
ATT_V_COLS = 2 * ATT_HEAD_DIM


def _attention_kernel(q_ref, kt_ref, v_ref, o_ref, *, kv_chunk):
    tq = q_ref.shape[2]
    rows = ATT_GROUP * tq
    q = q_ref[0].reshape(rows, ATT_HEAD_DIM)
    n_chunks = kt_ref.shape[3] // kv_chunk

    def body(c, carry):
        m, acc = carry
        off = pl.multiple_of(c * kv_chunk, kv_chunk)
        s = jnp.dot(q, kt_ref[0, 0, :, pl.ds(off, kv_chunk)], preferred_element_type=jnp.float32)
        m_new = jnp.maximum(m, jnp.max(s, axis=-1, keepdims=True))
        p = jnp.exp(s - m_new).astype(jnp.bfloat16)
        acc = jnp.exp(m - m_new) * acc + jnp.dot(p, v_ref[0, 0, pl.ds(off, kv_chunk), :],
                                                 preferred_element_type=jnp.float32)
        return m_new, acc

    m0 = jnp.full((rows, 1), NEG_INF, jnp.float32)
    acc0 = jnp.zeros((rows, ATT_V_COLS), jnp.float32)
    _, acc = lax.fori_loop(0, n_chunks, body, (m0, acc0), unroll=True)
    out = acc[:, :ATT_HEAD_DIM] / acc[:, ATT_HEAD_DIM:ATT_HEAD_DIM + 1]
    o_ref[0] = out.reshape(ATT_GROUP, tq, ATT_HEAD_DIM)


def _attention(q, k, v, kv_chunk):
    Bsz, S = q.shape[0], q.shape[1]
    SK = k.shape[1]
    tq = min(ATT_Q_TOKENS, S)
    qh = (q * ATT_HEAD_DIM ** -0.5).astype(jnp.bfloat16).transpose(0, 2, 1, 3)
    kt = k.astype(jnp.bfloat16).transpose(0, 2, 3, 1)
    pad = jnp.concatenate([jnp.ones(v.shape[:-1] + (1,), v.dtype),
                           jnp.zeros(v.shape[:-1] + (ATT_V_COLS - ATT_HEAD_DIM - 1,), v.dtype)], axis=-1)
    vx = jnp.concatenate([v, pad], axis=-1).astype(jnp.bfloat16).transpose(0, 2, 1, 3)
    o = pl.pallas_call(
        functools.partial(_attention_kernel, kv_chunk=kv_chunk),
        grid=(Bsz, ATT_KV_HEADS, S // tq),
        in_specs=[
            pl.BlockSpec((1, ATT_GROUP, tq, ATT_HEAD_DIM), lambda b, g, i: (b, g, i, 0)),
            pl.BlockSpec((1, 1, ATT_HEAD_DIM, SK), lambda b, g, i: (b, g, 0, 0)),
            pl.BlockSpec((1, 1, SK, ATT_V_COLS), lambda b, g, i: (b, g, 0, 0)),
        ],
        out_specs=pl.BlockSpec((1, ATT_GROUP, tq, ATT_HEAD_DIM), lambda b, g, i: (b, g, i, 0)),
        out_shape=jax.ShapeDtypeStruct((Bsz, ATT_HEADS, S, ATT_HEAD_DIM), jnp.float32),
        compiler_params=pltpu.CompilerParams(dimension_semantics=("arbitrary", "arbitrary", "arbitrary"),
                                             vmem_limit_bytes=VMEM_LIMIT_BYTES),
        name="attention",
    )(qh, kt, vx)
    return o.transpose(0, 2, 1, 3).reshape(Bsz, S, ATT_HEADS * ATT_HEAD_DIM)


SSD_GROUP_HEADS = SSD_HEADS // SSD_GROUPS
SSD_GROUP_W = SSD_GROUP_HEADS * SSD_HEAD_DIM
SSD_GATE_COLS = LANES


def _ssd_kernel(*refs, reverse, combine):
    if combine:
        x_ref, bm_ref, cm_ref, grow_ref, gcol_ref, other_ref, z_ref, skip_ref, gain_ref, out_ref, s_scr = refs
    else:
        x_ref, bm_ref, cm_ref, grow_ref, gcol_ref, out_ref, s_scr = refs
    Q, P, N, H = SSD_CHUNK, SSD_HEAD_DIM, SSD_STATE, SSD_HEADS

    @pl.when(pl.program_id(1) == 0)
    def _():
        s_scr[...] = jnp.zeros_like(s_scr)

    row = lax.broadcasted_iota(jnp.int32, (Q, Q), 0)
    col = lax.broadcasted_iota(jnp.int32, (Q, Q), 1)
    not_after = (col >= row) if reverse else (col <= row)
    tri = not_after.astype(jnp.float32)
    last = 0 if reverse else Q - 1
    nt = (((1,), (1,)), ((), ()))
    grow = grow_ref[0]
    gcol = gcol_ref[0]
    a_cols = jnp.dot(tri, gcol, preferred_element_type=jnp.float32, precision=lax.Precision.HIGHEST)
    a_rows = lax.dot_general(grow, tri, nt, preferred_element_type=jnp.float32,
                             precision=lax.Precision.HIGHEST)
    ys = []
    for g in range(SSD_GROUPS):
        ns = slice(g * N, (g + 1) * N)
        bm = bm_ref[0, :, ns].astype(jnp.bfloat16)
        cm = cm_ref[0, :, ns].astype(jnp.bfloat16)
        cb = lax.dot_general(cm, bm, nt, preferred_element_type=jnp.float32)
        state = s_scr[g]
        y_off = jnp.dot(cm, state.astype(jnp.bfloat16), preferred_element_type=jnp.float32)
        xw, decay = [], []
        for r in range(SSD_GROUP_HEADS):
            h = g * SSD_GROUP_HEADS + r
            hs = slice(h * P, (h + 1) * P)
            aq = a_cols[:, H + h:H + h + 1]
            as_ = a_rows[H + h:H + h + 1, :]
            a_tot = as_[:, last:last + 1]
            xdt = x_ref[0, :, hs] * gcol[:, h:h + 1]
            lmat = jnp.exp(jnp.where(not_after, aq - as_, NEG_INF))
            y = jnp.dot((cb * lmat).astype(jnp.bfloat16), xdt.astype(jnp.bfloat16),
                        preferred_element_type=jnp.float32)
            ys.append(y + y_off[:, r * P:(r + 1) * P] * jnp.exp(aq))
            xw.append(jnp.exp(a_tot - aq) * xdt)
            decay.append(jnp.broadcast_to(jnp.exp(a_tot), (1, P)))
        s_loc = lax.dot_general(bm, jnp.concatenate(xw, axis=1).astype(jnp.bfloat16),
                                (((0,), (0,)), ((), ())), preferred_element_type=jnp.float32)
        s_scr[g] = state * jnp.concatenate(decay, axis=1) + s_loc
    y = jnp.concatenate(ys, axis=1)
    if combine:
        x = x_ref[0]
        z = z_ref[0]
        tot = (y + other_ref[0] + x * skip_ref[...]) * (z * jax.nn.sigmoid(z))
        out_ref[0] = tot * lax.rsqrt(jnp.mean(tot * tot, axis=1, keepdims=True) + RMS_EPS) * gain_ref[...]
    else:
        out_ref[0] = y


def _ssd_direction(xbc, grow, gcol, n_ctx_chunks, reverse, extra=None):
    Bsz, TT, _ = xbc.shape
    n_chunks = TT // SSD_CHUNK
    if reverse:
        chunk = lambda c: jnp.where(c < n_ctx_chunks, n_ctx_chunks - 1 - c, n_chunks - 1 + n_ctx_chunks - c)
    else:
        chunk = lambda c: c
    gn = SSD_GROUPS * SSD_STATE
    seq_spec = pl.BlockSpec((1, SSD_CHUNK, SSD_INNER), lambda b, c: (b, chunk(c), 0))
    vec_spec = pl.BlockSpec((1, SSD_INNER), lambda b, c: (0, 0))
    in_specs = [seq_spec,
                pl.BlockSpec((1, SSD_CHUNK, gn), lambda b, c: (b, chunk(c), SSD_INNER // gn)),
                pl.BlockSpec((1, SSD_CHUNK, gn), lambda b, c: (b, chunk(c), SSD_INNER // gn + 1)),
                pl.BlockSpec((1, 2 * SSD_HEADS, SSD_CHUNK), lambda b, c: (b, 0, chunk(c))),
                pl.BlockSpec((1, SSD_CHUNK, SSD_GATE_COLS), lambda b, c: (b, chunk(c), 0))]
    args = [xbc, xbc, xbc, grow, gcol]
    if extra is not None:
        other, z, skip, gain = extra
        z_spec = pl.BlockSpec((1, SSD_CHUNK, SSD_INNER), lambda b, c: (b, chunk(c), z[1]))
        in_specs += [seq_spec, z_spec, vec_spec, vec_spec]
        args += [other, z[0], skip, gain]
    return pl.pallas_call(
        functools.partial(_ssd_kernel, reverse=reverse, combine=extra is not None),
        grid=(Bsz, n_chunks),
        in_specs=in_specs,
        out_specs=seq_spec,
        out_shape=jax.ShapeDtypeStruct((Bsz, TT, SSD_INNER), jnp.float32),
        scratch_shapes=[pltpu.VMEM((SSD_GROUPS, SSD_STATE, SSD_GROUP_W), jnp.float32)],
        compiler_params=pltpu.CompilerParams(dimension_semantics=("arbitrary", "arbitrary"),
                                             vmem_limit_bytes=VMEM_LIMIT_BYTES),
        name="ssd_bwd" if reverse else "ssd_fwd",
    )(*args)


def _ssd_block(z, xbc_act, dt_raw, dt_bias, a_log, d_skip, norm_g, n_ctx):
    Bsz, TT, _ = dt_raw.shape
    dt = jax.nn.softplus(dt_raw.reshape(Bsz, TT, 2, SSD_HEADS) + dt_bias)
    a = dt * -jnp.exp(a_log)
    gate_cols = jnp.concatenate([dt, a], axis=-1)
    skip = jnp.repeat(d_skip, SSD_HEAD_DIM).reshape(1, SSD_INNER)
    outs = None
    for direction in (1, 0):
        gc = gate_cols[:, :, direction]
        grow = gc.transpose(0, 2, 1)
        gcol = jnp.pad(gc, ((0, 0), (0, 0), (0, SSD_GATE_COLS - 2 * SSD_HEADS)))
        extra = None if direction == 1 else (outs, z, skip, norm_g.reshape(1, SSD_INNER))
        outs = _ssd_direction(xbc_act, grow, gcol, n_ctx // SSD_CHUNK, direction == 1, extra)
    return outs


ML_STATE_COLS = 2 * ML_HEAD_DIM
ML_GATE_COLS = LANES


def _mlstm_kernel(*refs, reverse, combine):
    if combine:
        q_ref, k_ref, v_ref, grow_ref, gcol_ref, other_ref, o_ref, gain_ref, out_ref, s_scr, m_scr = refs
    else:
        q_ref, k_ref, v_ref, grow_ref, gcol_ref, out_ref, s_scr, m_scr = refs
    Q, D, H = ML_CHUNK, ML_HEAD_DIM, ML_HEADS

    @pl.when(pl.program_id(1) == 0)
    def _():
        s_scr[...] = jnp.zeros_like(s_scr)
        m_scr[...] = jnp.zeros_like(m_scr)

    row = lax.broadcasted_iota(jnp.int32, (Q, Q), 0)
    col = lax.broadcasted_iota(jnp.int32, (Q, Q), 1)
    not_after = (col >= row) if reverse else (col <= row)
    tri = not_after.astype(jnp.float32)
    last = 0 if reverse else Q - 1
    nt = (((1,), (1,)), ((), ()))
    grow = grow_ref[0]
    gcol = gcol_ref[0]
    b_cols = jnp.dot(tri, gcol, preferred_element_type=jnp.float32, precision=lax.Precision.HIGHEST)
    b_rows = lax.dot_general(grow, tri, nt, preferred_element_type=jnp.float32,
                             precision=lax.Precision.HIGHEST)
    one_col = (lax.broadcasted_iota(jnp.int32, (Q, D), 1) == 0).astype(jnp.float32)
    for h in range(H):
        hs = slice(h * D, (h + 1) * D)
        q = q_ref[0, :, hs].astype(jnp.bfloat16)
        k = (k_ref[0, :, hs] * D ** -0.5).astype(jnp.bfloat16)
        v_ext = jnp.concatenate([v_ref[0, :, hs], one_col], axis=1)
        bq = b_cols[:, H + h:H + h + 1]
        bs = b_rows[H + h:H + h + 1, :]
        i_q = gcol[:, h:h + 1]
        i_s = grow[h:h + 1, :]
        b_last = bs[:, last:last + 1]
        m0 = m_scr[h, 0:1, 0:1]
        state = s_scr[h]
        dm = jnp.where(not_after, bq - bs + i_s, NEG_INF)
        inter = bq + m0
        m_t = jnp.maximum(inter, jnp.max(dm, axis=1, keepdims=True))
        s_mat = lax.dot_general(q, k, nt, preferred_element_type=jnp.float32) * jnp.exp(dm - m_t)
        num = (jnp.dot(s_mat.astype(jnp.bfloat16), v_ext.astype(jnp.bfloat16), preferred_element_type=jnp.float32)
               + jnp.exp(inter - m_t) * jnp.dot(q, state.astype(jnp.bfloat16), preferred_element_type=jnp.float32))
        den = num[:, D:D + 1]
        hout = num[:, :D] / jnp.maximum(jnp.abs(den), jnp.exp(-m_t))
        g_q = b_last - bq + i_q
        m_loc = jnp.max(g_q, axis=0, keepdims=True)
        wv = (jnp.exp(g_q - m_loc) * v_ext).astype(jnp.bfloat16)
        s_loc = lax.dot_general(k, wv, (((0,), (0,)), ((), ())), preferred_element_type=jnp.float32)
        m_new = jnp.maximum(b_last + m0, m_loc)
        s_scr[h] = jnp.exp(b_last + m0 - m_new) * state + jnp.exp(m_loc - m_new) * s_loc
        m_scr[h] = jnp.broadcast_to(m_new, (SUBLANES, LANES))
        if combine:
            tot = hout + other_ref[0, :, hs]
            y = tot * lax.rsqrt(jnp.mean(tot * tot, axis=1, keepdims=True) + RMS_EPS) * gain_ref[:, hs]
            out_ref[0, :, hs] = y * jax.nn.sigmoid(o_ref[0, :, hs])
        else:
            out_ref[0, :, hs] = hout


def _mlstm_direction(q, k, v, grow, gcol, n_ctx_chunks, reverse, extra=None):
    Bsz, TT, _ = q[0].shape
    n_chunks = TT // ML_CHUNK
    if reverse:
        chunk = lambda c: jnp.where(c < n_ctx_chunks, n_ctx_chunks - 1 - c, n_chunks - 1 + n_ctx_chunks - c)
    else:
        chunk = lambda c: c
    seq_spec = pl.BlockSpec((1, ML_CHUNK, ML_INNER), lambda b, c: (b, chunk(c), 0))
    col_spec = lambda j: pl.BlockSpec((1, ML_CHUNK, ML_INNER), lambda b, c: (b, chunk(c), j))
    in_specs = [col_spec(q[1]), col_spec(k[1]), col_spec(v[1]),
                pl.BlockSpec((1, 2 * ML_HEADS, ML_CHUNK), lambda b, c: (b, 0, chunk(c))),
                pl.BlockSpec((1, ML_CHUNK, ML_GATE_COLS), lambda b, c: (b, chunk(c), 0))]
    args = [q[0], k[0], v[0], grow, gcol]
    if extra is not None:
        other, o_pre, gain = extra
        in_specs += [seq_spec, col_spec(o_pre[1]), pl.BlockSpec((1, ML_INNER), lambda b, c: (0, 0))]
        args += [other, o_pre[0], gain]
    return pl.pallas_call(
        functools.partial(_mlstm_kernel, reverse=reverse, combine=extra is not None),
        grid=(Bsz, n_chunks),
        in_specs=in_specs,
        out_specs=seq_spec,
        out_shape=jax.ShapeDtypeStruct((Bsz, TT, ML_INNER), jnp.float32),
        scratch_shapes=[pltpu.VMEM((ML_HEADS, ML_HEAD_DIM, ML_STATE_COLS), jnp.float32),
                        pltpu.VMEM((ML_HEADS, SUBLANES, LANES), jnp.float32)],
        compiler_params=pltpu.CompilerParams(dimension_semantics=("arbitrary", "arbitrary"),
                                             vmem_limit_bytes=VMEM_LIMIT_BYTES),
        name="mlstm_bwd" if reverse else "mlstm_fwd",
    )(*args)


def _mlstm_block(q, k, v, o_pre, if_pre, gate_bias, norm_g, n_ctx):
    Bsz, TT, _ = if_pre.shape
    gates = if_pre.reshape(Bsz, TT, 2, 2, ML_HEADS) + gate_bias
    gate_cols = jnp.concatenate([gates[:, :, :, 0], jax.nn.log_sigmoid(gates[:, :, :, 1])], axis=-1)
    outs = None
    for direction in (1, 0):
        gc = gate_cols[:, :, direction]
        grow = gc.transpose(0, 2, 1)
        gcol = jnp.pad(gc, ((0, 0), (0, 0), (0, ML_GATE_COLS - 2 * ML_HEADS)))
        extra = None if direction == 1 else (outs, o_pre, norm_g.reshape(1, ML_INNER))
        outs = _mlstm_direction(q, k, v, grow, gcol, n_ctx // ML_CHUNK, direction == 1, extra)
    return outs


_PROJ_ORDER = ('aq', 'sz', 'mq', 'mk', 'mv', 'mo', 'gpre', 'sxbc', 'ak', 'av', 'sdt', 'mif')
_IN_NAMES = ('aq', 'ak', 'av', 'sz', 'sxbc', 'sdt', 'mq', 'mk', 'mv', 'mo', 'mif', 'gpre')
_IN_START = dict(zip(_IN_NAMES, np.cumsum((0,) + IN_SPLITS[:-1]).tolist()))
_IN_WIDTH = dict(zip(_IN_NAMES, IN_SPLITS))
_PROJ_PERM = np.concatenate([np.arange(_IN_START[n], _IN_START[n] + _IN_WIDTH[n]) for n in _PROJ_ORDER])
_PROJ_START = dict(zip(_PROJ_ORDER, np.cumsum([0] + [_IN_WIDTH[n] for n in _PROJ_ORDER[:-1]]).tolist()))
PROJ_COL_BLOCKS = 3
PROJ_COLS = -(-sum(IN_SPLITS) // (PROJ_COL_BLOCKS * LANES)) * (PROJ_COL_BLOCKS * LANES)
ROW_BLOCK = 256


def _proj_cols(proj, name):
    return proj[..., _PROJ_START[name]:_PROJ_START[name] + _IN_WIDTH[name]]


def _proj_block(name, width):
    assert _PROJ_START[name] % width == 0
    return _PROJ_START[name] // width


def _modulated_norm(x, gain, scale, shift):
    y = x * lax.rsqrt(jnp.mean(x * x, axis=-1, keepdims=True) + RMS_EPS)
    return (y * gain) * (1.0 + scale) + shift


def _inproj_kernel(x_ref, gain_ref, mod_ref, w_ref, o_ref):
    h = _modulated_norm(x_ref[0], gain_ref[...], mod_ref[0, 0, 0:1, :], mod_ref[0, 0, 1:2, :])
    o_ref[0] = jnp.dot(h.astype(jnp.bfloat16), w_ref[...], preferred_element_type=jnp.float32)


def _input_projection(x_all, gain, mod, w_perm, n_ctx):
    Bsz, TT, _ = x_all.shape
    cols = PROJ_COLS // PROJ_COL_BLOCKS
    ctx_blocks = n_ctx // ROW_BLOCK
    return pl.pallas_call(
        _inproj_kernel,
        grid=(PROJ_COL_BLOCKS, Bsz, TT // ROW_BLOCK),
        in_specs=[
            pl.BlockSpec((1, ROW_BLOCK, D_MODEL), lambda n, b, r: (b, r, 0)),
            pl.BlockSpec((1, D_MODEL), lambda n, b, r: (0, 0)),
            pl.BlockSpec((1, 1, 2, D_MODEL), lambda n, b, r: (b, jnp.where(r < ctx_blocks, 0, 1), 0, 0)),
            pl.BlockSpec((D_MODEL, cols), lambda n, b, r: (0, n)),
        ],
        out_specs=pl.BlockSpec((1, ROW_BLOCK, cols), lambda n, b, r: (b, r, n)),
        out_shape=jax.ShapeDtypeStruct((Bsz, TT, PROJ_COLS), jnp.float32),
        compiler_params=pltpu.CompilerParams(dimension_semantics=("arbitrary", "arbitrary", "arbitrary"),
                                             vmem_limit_bytes=VMEM_LIMIT_BYTES),
        name="input_projection",
    )(x_all, gain.reshape(1, D_MODEL), mod, w_perm)


def _merge_kernel(att_ref, ssd_ref, ml_ref, g0_ref, g1_ref, g2_ref, wb_ref, wo_ref, x_ref, gain_ref, mod_ref,
                  xo_ref, h2_ref):
    mixed = None
    for r, (branch, gate) in enumerate(((att_ref, g0_ref), (ssd_ref, g1_ref), (ml_ref, g2_ref))):
        p = jnp.dot(branch[0].astype(jnp.bfloat16), wb_ref[r], preferred_element_type=jnp.float32)
        term = jax.nn.sigmoid(gate[0]) * p
        mixed = term if mixed is None else mixed + term
    y = jnp.dot(mixed.astype(jnp.bfloat16), wo_ref[...], preferred_element_type=jnp.float32)
    x_new = x_ref[0] + mod_ref[0, 0, 0:1, :] * y
    xo_ref[0] = x_new
    h2 = _modulated_norm(x_new, gain_ref[...], mod_ref[0, 0, 1:2, :], mod_ref[0, 0, 2:3, :])
    h2_ref[0] = h2.astype(jnp.bfloat16)


def _merge(att, ssd, ml, proj, w_branch, w_out, x_all, gain, mod, n_ctx):
    Bsz, TT, _ = x_all.shape
    ctx_blocks = n_ctx // ROW_BLOCK
    row = lambda w, j=0: pl.BlockSpec((1, ROW_BLOCK, w), lambda b, r: (b, r, j))
    g_first = _proj_block('gpre', D_MODEL)
    return pl.pallas_call(
        _merge_kernel,
        grid=(Bsz, TT // ROW_BLOCK),
        in_specs=[
            row(BRANCH_W), row(BRANCH_W), row(BRANCH_W),
            row(D_MODEL, g_first), row(D_MODEL, g_first + 1), row(D_MODEL, g_first + 2),
            pl.BlockSpec((N_BRANCH, BRANCH_W, D_MODEL), lambda b, r: (0, 0, 0)),
            pl.BlockSpec((D_MODEL, D_MODEL), lambda b, r: (0, 0)),
            row(D_MODEL),
            pl.BlockSpec((1, D_MODEL), lambda b, r: (0, 0)),
            pl.BlockSpec((1, 1, 3, D_MODEL), lambda b, r: (b, jnp.where(r < ctx_blocks, 0, 1), 0, 0)),
        ],
        out_specs=[row(D_MODEL), row(D_MODEL)],
        out_shape=[jax.ShapeDtypeStruct((Bsz, TT, D_MODEL), jnp.float32),
                   jax.ShapeDtypeStruct((Bsz, TT, D_MODEL), jnp.bfloat16)],
        compiler_params=pltpu.CompilerParams(dimension_semantics=("arbitrary", "arbitrary"),
                                             vmem_limit_bytes=VMEM_LIMIT_BYTES),
        name="merge",
    )(att, ssd, ml, proj, proj, proj, w_branch.astype(jnp.bfloat16), w_out.astype(jnp.bfloat16),
      x_all, gain.reshape(1, D_MODEL), mod)


def _rmsnorm(x, g):
    xf = x.astype(jnp.float32)
    y = xf * lax.rsqrt(jnp.mean(xf * xf, axis=-1, keepdims=True) + RMS_EPS)
    return (y * g.astype(jnp.float32)).astype(x.dtype)


def _heads(t, n):
    return t.reshape(t.shape[0], t.shape[1], n, -1)


def _axial_rope(rows):
    n_freq = ATT_HEAD_DIM // 4
    inv = ROPE_THETA ** (-jnp.arange(n_freq, dtype=jnp.float32) / n_freq)
    row = jnp.broadcast_to(jnp.arange(rows, dtype=jnp.float32)[:, None], (rows, GRID_W)).reshape(-1)
    col = jnp.broadcast_to(jnp.arange(GRID_W, dtype=jnp.float32)[None, :], (rows, GRID_W)).reshape(-1)
    ang = jnp.concatenate([row[:, None] * inv, col[:, None] * inv], axis=-1)
    return jnp.cos(ang), jnp.sin(ang)


def _apply_rope(t, cos, sin):
    half = t.shape[-1] // 2
    tf = t.astype(jnp.float32)
    t1, t2 = tf[..., :half], tf[..., half:]
    c, s = cos[:, None, :], sin[:, None, :]
    return jnp.concatenate([t1 * c - t2 * s, t1 * s + t2 * c], axis=-1).astype(t.dtype)


ATT_KV_CHUNK = 768


def _latent_attention(q, k, v, kc, vc):
    kk = jnp.concatenate([k, kc], axis=1)
    vv = jnp.concatenate([v, vc], axis=1)
    assert kk.shape[1] % ATT_KV_CHUNK == 0
    return _attention(q, kk, vv, ATT_KV_CHUNK)


def _context_attention(qc, kc, vc):
    return _attention(qc, kc, vc, kc.shape[1])


def _dwconv(x, w, b):
    out = lax.conv_general_dilated(x, w[:, None, :], window_strides=(1,), padding='SAME',
                                   dimension_numbers=('NWC', 'WIO', 'NWC'),
                                   feature_group_count=x.shape[-1])
    return out + b


def _token_mixer(proj, n_ctx, qn, kn, conv_w, conv_b, dt_bias, a_log, d_skip, ssd_norm_g,
                 ml_bias, ml_norm_g, cos, sin, ctx_out):
    Bsz = proj.shape[0]
    cat = lambda a, b: jnp.concatenate([a, b], axis=1)
    aq, ak, av = _proj_cols(proj, 'aq'), _proj_cols(proj, 'ak'), _proj_cols(proj, 'av')
    q = _apply_rope(_rmsnorm(_heads(aq[:, n_ctx:], ATT_HEADS), qn), cos, sin)
    k = _apply_rope(_rmsnorm(_heads(ak[:, n_ctx:], ATT_KV_HEADS), kn), cos, sin)
    kc = _rmsnorm(_heads(ak[:, :n_ctx], ATT_KV_HEADS), kn)
    vc = _heads(av[:, :n_ctx], ATT_KV_HEADS)
    att = _latent_attention(q, k, _heads(av[:, n_ctx:], ATT_KV_HEADS), kc, vc)
    if ctx_out:
        att_c = _context_attention(_rmsnorm(_heads(aq[:, :n_ctx], ATT_HEADS), qn), kc, vc)
    else:
        att_c = jnp.zeros((Bsz, n_ctx, att.shape[-1]), att.dtype)
    sxbc = _proj_cols(proj, 'sxbc')
    xbc_act = cat(jax.nn.silu(_dwconv(sxbc[:, :n_ctx], conv_w, conv_b)),
                  jax.nn.silu(_dwconv(sxbc[:, n_ctx:], conv_w, conv_b)))
    ssd = _ssd_block((proj, _proj_block('sz', SSD_INNER)), xbc_act, _proj_cols(proj, 'sdt'),
                     dt_bias, a_log, d_skip, ssd_norm_g, n_ctx)
    blk = lambda name: (proj, _proj_block(name, ML_INNER))
    ml = _mlstm_block(blk('mq'), blk('mk'), blk('mv'), blk('mo'), _proj_cols(proj, 'mif'),
                      ml_bias, ml_norm_g, n_ctx)
    return cat(att_c, att), ssd, ml


def kernel(x, c, ctx, c_ctx, norm1_g, norm2_g, w_mod, b_mod, w_in, att_qnorm, att_knorm, ssd_conv_w, ssd_conv_b, ssd_dt_bias, ssd_a_log, ssd_d, ssd_norm, ml_gate_bias, ml_norm, w_branch, w_out, peer_wq, peer_subkeys, peer_u, peer_v):
    Bsz, S, _ = x.shape
    depth = w_in.shape[0]
    cos, sin = _axial_rope(S // GRID_W)
    silu_c = jax.nn.silu(c)
    silu_cc = jax.nn.silu(c_ctx)
    xc = ctx
    n_ctx = ctx.shape[1]
    for l in range(depth):
        last = l == depth - 1
        m = (silu_c @ w_mod[l] + b_mod[l]).reshape(Bsz, 6, D_MODEL)
        mc = jnp.broadcast_to((silu_cc @ w_mod[l] + b_mod[l]).reshape(1, 6, D_MODEL), (Bsz, 6, D_MODEL))
        both = jnp.stack([mc, m], axis=1)
        x_all = jnp.concatenate([xc, x], axis=1)
        w_perm = jnp.pad(w_in[l][:, _PROJ_PERM], ((0, 0), (0, PROJ_COLS - _PROJ_PERM.size))).astype(jnp.bfloat16)
        proj = _input_projection(x_all, norm1_g[l], both[:, :, np.array([1, 0])], w_perm, n_ctx)
        att, ssd, ml = _token_mixer(proj, n_ctx, att_qnorm[l], att_knorm[l], ssd_conv_w[l], ssd_conv_b[l],
                                    ssd_dt_bias[l], ssd_a_log[l], ssd_d[l], ssd_norm[l], ml_gate_bias[l],
                                    ml_norm[l], cos, sin, not last)
        x_all, h2 = _merge(att, ssd, ml, proj, w_branch[l], w_out[l], x_all, norm2_g[l],
                           both[:, :, np.array([2, 4, 3])], n_ctx)
        u_bf16 = peer_u[l].astype(jnp.bfloat16)
        vt_bf16 = peer_v[l].astype(jnp.bfloat16).reshape(PEER_EXPERTS // PEER_EBLK, PEER_EBLK, D_MODEL)
        vt_bf16 = vt_bf16.transpose(0, 2, 1)
        po = _peer(h2[:, n_ctx:].reshape(-1, D_MODEL), peer_wq[l], peer_subkeys[l], u_bf16, vt_bf16)
        x = x_all[:, n_ctx:] + m[:, None, 5] * po.reshape(x.shape)
        if not last:
            pc = _peer(h2[:, :n_ctx].reshape(-1, D_MODEL), peer_wq[l], peer_subkeys[l], u_bf16, vt_bf16)
            xc = x_all[:, :n_ctx] + mc[:, None, 5] * pc.reshape(xc.shape)
    return x
```

```python
import functools
import math

import jax
import jax.numpy as jnp
import numpy as np
from jax import lax
from jax.experimental import pallas as pl
from jax.experimental.pallas import tpu as pltpu

D_MODEL = 1024
GRID_W = 64
RMS_EPS = 1e-6
N_BRANCH = 3
BRANCH_W = 512
ATT_HEADS = 8
ATT_KV_HEADS = 2
ATT_HEAD_DIM = 64
ATT_GROUP = ATT_HEADS // ATT_KV_HEADS
Q_BLOCK = 128
ROPE_THETA = 10000.0
SSD_HEADS = 8
SSD_HEAD_DIM = 64
SSD_INNER = SSD_HEADS * SSD_HEAD_DIM
SSD_GROUPS = 2
SSD_STATE = 64
SSD_XBC = SSD_INNER + 2 * SSD_GROUPS * SSD_STATE
SSD_CONV = 5
SSD_CHUNK = 128
ML_HEADS = 4
ML_HEAD_DIM = 128
ML_INNER = ML_HEADS * ML_HEAD_DIM
ML_CHUNK = 128
PEER_HEADS = 8
PEER_KEYS = 128
PEER_EXPERTS = PEER_KEYS * PEER_KEYS
PEER_TOPK = 16
PEER_QDIM = 256
PEER_HALF = PEER_QDIM // 2
IN_SPLITS = (ATT_HEADS * ATT_HEAD_DIM, ATT_KV_HEADS * ATT_HEAD_DIM, ATT_KV_HEADS * ATT_HEAD_DIM,
             SSD_INNER, SSD_XBC, 2 * SSD_HEADS,
             ML_INNER, ML_INNER, ML_INNER, ML_INNER, 4 * ML_HEADS,
             N_BRANCH * D_MODEL)

LANES = 128
SUBLANES = 8
VMEM_LIMIT_BYTES = 56 * 1024 * 1024

ROUTER_TOKENS = 256
ROUTER_CHUNK = LANES
PEER_TOKENS = 1024
PEER_TOKENS_SMALL = 512
PEER_ROWS = 8
PEER_EBLK = PEER_ROWS * PEER_KEYS
PEER_SLICE = 2 * PEER_KEYS
PEER_CHUNK = 256

NEG_INF = float("-inf")


def _sort_network(n):
    def merge(lo, hi, r):
        step = r * 2
        if step < hi - lo:
            yield from merge(lo, hi, step)
            yield from merge(lo + r, hi, step)
            yield from [(i, i + r) for i in range(lo + r, hi - r, step)]
        else:
            yield (lo, lo + r)

    def sort(lo, hi):
        if hi - lo >= 1:
            mid = lo + (hi - lo) // 2
            yield from sort(lo, mid)
            yield from sort(mid + 1, hi)
            yield from merge(lo, hi, 1)

    return tuple(sort(0, n - 1))


_SORT16 = _sort_network(PEER_TOPK)


def _merge_top(lists, n):
    lists = list(lists)
    depth = len(lists)
    sub = lax.broadcasted_iota(jnp.int32, lists[0].shape, 0)
    tops = []
    for i in range(n):
        head = lists[0]
        m = jnp.max(head, axis=0, keepdims=True)
        tops.append(m)
        live = min(depth, n - 1 - i)
        if live == 0:
            break
        first = jnp.min(jnp.where(head == m, sub, SUBLANES), axis=0, keepdims=True)
        pop = sub == first
        for r in range(live):
            nxt = lists[r + 1] if r + 1 < depth else NEG_INF
            lists[r] = jnp.where(pop, nxt, lists[r])
    return tops


def _sorted_top(s, n):
    xs = [s[r * SUBLANES:(r + 1) * SUBLANES, :] for r in range(s.shape[0] // SUBLANES)]
    for i, j in _SORT16:
        xs[i], xs[j] = jnp.maximum(xs[i], xs[j]), jnp.minimum(xs[i], xs[j])
    return _merge_top(xs, n)


def _prefix_count(values, pred):
    n = len(values)
    steps = [n >> (i + 1) for i in range(n.bit_length() - 1)]
    taken = []

    def pivot(level, base, step):
        if level == len(taken):
            return values[base + step - 1]
        return jnp.where(taken[level], pivot(level + 1, base + steps[level], step), pivot(level + 1, base, step))

    count = None
    for step in steps:
        ok = pred(pivot(0, 0, step))
        taken.append(ok)
        term = jnp.where(ok, float(step), 0.0)
        count = term if count is None else count + term
    return jnp.where(pred(values[n - 1]), float(n), count)


def _rows_to_sublanes(rows, first_sublane, shape):
    sub = lax.broadcasted_iota(jnp.int32, shape, 0)
    out = jnp.zeros(shape, jnp.float32)
    for k, row in enumerate(rows):
        out = jnp.where(sub == first_sublane + k, row, out)
    return out


_CAND_LEN = (16, 8, 5, 4, 12, 4, 1, 0)


def _router_kernel(h_ref, wq_ref, sk_ref, rank_ref, e2_ref, cnt_ref, e1_ref, q_scr):
    q_scr[...] = jnp.dot(h_ref[...], wq_ref[...], preferred_element_type=jnp.float32).astype(jnp.bfloat16)
    nt = (((1,), (1,)), ((), ()))
    shape8 = (SUBLANES, ROUTER_CHUNK)
    sub = lax.broadcasted_iota(jnp.int32, shape8, 0)
    cand_len = jnp.zeros(shape8, jnp.int32)
    for g, n in enumerate(_CAND_LEN):
        cand_len = jnp.where(sub == g, n, cand_len)
    for h in range(PEER_HEADS):
        q1 = q_scr[:, (2 * h) * PEER_HALF:(2 * h + 1) * PEER_HALF]
        q2 = q_scr[:, (2 * h + 1) * PEER_HALF:(2 * h + 2) * PEER_HALF]
        s1 = lax.dot_general(sk_ref[2 * h], q1, nt, preferred_element_type=jnp.float32)
        s2 = lax.dot_general(sk_ref[2 * h + 1], q2, nt, preferred_element_type=jnp.float32)
        for c in range(ROUTER_TOKENS // ROUTER_CHUNK):
            cs = slice(c * ROUTER_CHUNK, (c + 1) * ROUTER_CHUNK)
            s1c = s1[:, cs]
            s2c = s2[:, cs]
            top1 = _sorted_top(s1c, PEER_TOPK)
            top2 = _sorted_top(s2c, PEER_TOPK)
            a_lo = _rows_to_sublanes(top1[:4], 0, shape8)
            b_lo = _rows_to_sublanes(top2[:3], 4, shape8)
            cands = []
            for r in range(PEER_TOPK):
                by_a = a_lo + top2[r]
                by_b = (top1[4 + r] + b_lo) if 4 + r < PEER_TOPK else by_a
                cands.append(jnp.where(r < cand_len, jnp.where(sub < 4, by_a, by_b), NEG_INF))
            tau = _merge_top(cands, PEER_TOPK)[-1]
            m1, m2 = top1[0], top2[0]
            z = jnp.zeros(shape8, jnp.float32)
            for cand in cands:
                z = z + jnp.where(cand >= tau, jnp.exp(cand - (m1 + m2)), 0.0)
            z = jnp.sum(z, axis=0, keepdims=True)
            rank = _prefix_count(top2, lambda t: t > s2c)
            cnt = _prefix_count(top2, lambda t: s1c + t >= tau)
            rank_ref[h, :, cs] = rank.astype(jnp.bfloat16)
            e2_ref[h, :, cs] = jnp.exp(s2c - m2).astype(jnp.bfloat16)
            cnt_ref[h, :, cs] = cnt
            e1_ref[h, :, cs] = jnp.exp(s1c - m1) * (0.5 / z)


def _peer_router(h2_bf16, wq, subkeys):
    n_tok = h2_bf16.shape[0]
    sk = subkeys.reshape(PEER_HEADS * 2, PEER_KEYS, PEER_HALF)
    shape = (PEER_HEADS, PEER_KEYS, n_tok)
    spec = pl.BlockSpec((PEER_HEADS, PEER_KEYS, ROUTER_TOKENS), lambda i: (0, 0, i))
    return pl.pallas_call(
        _router_kernel,
        grid=(n_tok // ROUTER_TOKENS,),
        in_specs=[
            pl.BlockSpec((ROUTER_TOKENS, D_MODEL), lambda i: (i, 0)),
            pl.BlockSpec((D_MODEL, PEER_HEADS * PEER_QDIM), lambda i: (0, 0)),
            pl.BlockSpec((PEER_HEADS * 2, PEER_KEYS, PEER_HALF), lambda i: (0, 0, 0)),
        ],
        out_specs=[spec, spec, spec, spec],
        out_shape=[jax.ShapeDtypeStruct(shape, jnp.bfloat16), jax.ShapeDtypeStruct(shape, jnp.bfloat16),
                   jax.ShapeDtypeStruct(shape, jnp.float32), jax.ShapeDtypeStruct(shape, jnp.float32)],
        scratch_shapes=[pltpu.VMEM((ROUTER_TOKENS, PEER_HEADS * PEER_QDIM), jnp.bfloat16)],
        compiler_params=pltpu.CompilerParams(dimension_semantics=("arbitrary",),
                                             vmem_limit_bytes=VMEM_LIMIT_BYTES),
        name="peer_router",
    )(h2_bf16, wq.astype(jnp.bfloat16), sk.astype(jnp.bfloat16))


GELU_C0 = math.sqrt(2.0 / math.pi)
GELU_C1 = 0.044715 * GELU_C0


def _peer_dense_kernel(h_ref, u_ref, vt_ref, rank_ref, e2_ref, cnt_ref, e1_ref, o_ref,
                       acc_scr, pre_scr, p_scr, rank_scr, e2_scr):
    j = pl.program_id(1)
    n_tok = h_ref.shape[0]
    n_slices = PEER_EBLK // PEER_SLICE
    n_chunks = n_tok // PEER_CHUNK
    nt = (((1,), (1,)), ((), ()))

    @pl.when(j == 0)
    def _():
        acc_scr[...] = jnp.zeros_like(acc_scr)

        def relayout(t, carry):
            lanes = pl.ds(pl.multiple_of(t * LANES, LANES), LANES)
            for h in range(PEER_HEADS):
                rank = rank_ref[h, :, lanes].astype(jnp.float32)
                e2 = e2_ref[h, :, lanes].astype(jnp.float32)
                for b in range(PEER_KEYS // SUBLANES):
                    rank_scr[t, b, h] = rank[b * SUBLANES:(b + 1) * SUBLANES]
                    e2_scr[t, b, h] = e2[b * SUBLANES:(b + 1) * SUBLANES]
            return carry

        lax.fori_loop(0, n_tok // LANES, relayout, 0)

    def pre_matmul(c, ap, slot):
        tok = pl.multiple_of(c * PEER_CHUNK, PEER_CHUNK)
        rows = slice(ap * PEER_SLICE, (ap + 1) * PEER_SLICE)
        pre_scr[slot] = lax.dot_general(u_ref[rows, :], h_ref[pl.ds(tok, PEER_CHUNK), :], nt,
                                        preferred_element_type=jnp.float32)

    def out_matmul(c, ap, slot):
        tok = pl.multiple_of(c * PEER_CHUNK, PEER_CHUNK)
        rows = slice(ap * PEER_SLICE, (ap + 1) * PEER_SLICE)
        acc_scr[:, pl.ds(tok, PEER_CHUNK)] += jnp.dot(vt_ref[0, :, rows], p_scr[slot],
                                                      preferred_element_type=jnp.float32)

    def gates(c, ap, slot):
        first_a = ap * (PEER_SLICE // PEER_KEYS)
        for t in range(PEER_CHUNK // LANES):
            lanes = pl.ds(pl.multiple_of(c * PEER_CHUNK + t * LANES, LANES), LANES)
            ts = slice(t * LANES, (t + 1) * LANES)
            tile = c * (PEER_CHUNK // LANES) + t
            for b in range(PEER_KEYS // SUBLANES):
                rank = [rank_scr[tile, b, h] for h in range(PEER_HEADS)]
                e2 = [e2_scr[tile, b, h] for h in range(PEER_HEADS)]
                for a in range(PEER_SLICE // PEER_KEYS):
                    rs = slice(a * PEER_KEYS + b * SUBLANES, a * PEER_KEYS + (b + 1) * SUBLANES)
                    terms = []
                    for h in range(PEER_HEADS):
                        cnt = cnt_ref[first_a + a, h:h + 1, lanes]
                        e1 = e1_ref[first_a + a, h:h + 1, lanes]
                        terms.append(jnp.where(rank[h] < cnt, e2[h], 0.0) * e1)
                    while len(terms) > 1:
                        terms = [x + y for x, y in zip(terms[::2], terms[1::2])]
                    x = pre_scr[slot, rs, ts]
                    act = x * (1.0 + jnp.tanh(x * (GELU_C0 + GELU_C1 * (x * x))))
                    pre_scr[slot, rs, ts] = terms[0] * act
        p_scr[slot] = pre_scr[slot].astype(jnp.bfloat16)

    p_scr[1] = jnp.zeros(p_scr.shape[1:], p_scr.dtype)
    pre_matmul(0, 0, 0)

    def body(c, carry):
        for ap in range(n_slices):
            slot = ap % 2
            if ap + 1 < n_slices:
                pre_matmul(c, ap + 1, 1 - slot)
            else:
                pre_matmul(jnp.minimum(c + 1, n_chunks - 1), 0, 1 - slot)
            gates(c, ap, slot)
            if ap > 0:
                out_matmul(c, ap - 1, 1 - slot)
            else:
                out_matmul(jnp.maximum(c - 1, 0), n_slices - 1, 1 - slot)
        return carry

    lax.fori_loop(0, n_chunks, body, 0)
    out_matmul(n_chunks - 1, n_slices - 1, 1)

    @pl.when(j == pl.num_programs(1) - 1)
    def _():
        o_ref[...] = acc_scr[...].T


def _peer_dense(h2_bf16, u_bf16, vt_bf16, rank, e2, cnt, e1, tb):
    n_tok = h2_bf16.shape[0]
    row_spec = pl.BlockSpec((PEER_ROWS, PEER_HEADS, tb), lambda i, j: (j, 0, i))
    tiled = (tb // LANES, PEER_KEYS // SUBLANES, PEER_HEADS, SUBLANES, LANES)
    key_spec = pl.BlockSpec((PEER_HEADS, PEER_KEYS, tb), lambda i, j: (0, 0, i))
    return pl.pallas_call(
        _peer_dense_kernel,
        grid=(n_tok // tb, PEER_EXPERTS // PEER_EBLK),
        in_specs=[
            pl.BlockSpec((tb, D_MODEL), lambda i, j: (i, 0)),
            pl.BlockSpec((PEER_EBLK, D_MODEL), lambda i, j: (j, 0)),
            pl.BlockSpec((1, D_MODEL, PEER_EBLK), lambda i, j: (j, 0, 0)),
            key_spec, key_spec, row_spec, row_spec,
        ],
        out_specs=pl.BlockSpec((tb, D_MODEL), lambda i, j: (i, 0)),
        out_shape=jax.ShapeDtypeStruct((n_tok, D_MODEL), jnp.float32),
        scratch_shapes=[pltpu.VMEM((D_MODEL, tb), jnp.float32),
                        pltpu.VMEM((2, PEER_SLICE, PEER_CHUNK), jnp.float32),
                        pltpu.VMEM((2, PEER_SLICE, PEER_CHUNK), jnp.bfloat16),
                        pltpu.VMEM(tiled, jnp.float32),
                        pltpu.VMEM(tiled, jnp.float32)],
        compiler_params=pltpu.CompilerParams(dimension_semantics=("arbitrary", "arbitrary"),
                                             vmem_limit_bytes=VMEM_LIMIT_BYTES),
        name="peer_dense",
    )(h2_bf16, u_bf16, vt_bf16, rank, e2, cnt, e1)


def _peer(h2, wq, subkeys, u_bf16, vt_bf16):
    n_tok = h2.shape[0]
    tb = PEER_TOKENS if n_tok % PEER_TOKENS == 0 else PEER_TOKENS_SMALL
    h2_bf16 = h2.astype(jnp.bfloat16)
    rank, e2, cnt, e1 = _peer_router(h2_bf16, wq, subkeys)
    cnt, e1 = cnt.transpose(1, 0, 2), e1.transpose(1, 0, 2)
    return _peer_dense(h2_bf16, u_bf16, vt_bf16, rank, e2, cnt, e1, tb)


ATT_Q_TOKENS = 256
ATT_V_COLS = 2 * ATT_HEAD_DIM


def _attention_kernel(q_ref, kt_ref, v_ref, o_ref, *, kv_chunk):
    tq = q_ref.shape[2]
    rows = ATT_GROUP * tq
    q = q_ref[0].reshape(rows, ATT_HEAD_DIM)
    n_chunks = kt_ref.shape[3] // kv_chunk

    def body(c, carry):
        m, acc = carry
        off = pl.multiple_of(c * kv_chunk, kv_chunk)
        s = jnp.dot(q, kt_ref[0, 0, :, pl.ds(off, kv_chunk)], preferred_element_type=jnp.float32)
        m_new = jnp.maximum(m, jnp.max(s, axis=-1, keepdims=True))
        p = jnp.exp(s - m_new).astype(jnp.bfloat16)
        acc = jnp.exp(m - m_new) * acc + jnp.dot(p, v_ref[0, 0, pl.ds(off, kv_chunk), :],
                                                 preferred_element_type=jnp.float32)
        return m_new, acc

    m0 = jnp.full((rows, 1), NEG_INF, jnp.float32)
    acc0 = jnp.zeros((rows, ATT_V_COLS), jnp.float32)
    _, acc = lax.fori_loop(0, n_chunks, body, (m0, acc0), unroll=True)
    out = acc[:, :ATT_HEAD_DIM] / acc[:, ATT_HEAD_DIM:ATT_HEAD_DIM + 1]
    o_ref[0] = out.reshape(ATT_GROUP, tq, ATT_HEAD_DIM)


def _attention(q, k, v, kv_chunk):
    Bsz, S = q.shape[0], q.shape[1]
    SK = k.shape[1]
    tq = min(ATT_Q_TOKENS, S)
    qh = (q * ATT_HEAD_DIM ** -0.5).astype(jnp.bfloat16).transpose(0, 2, 1, 3)
    kt = k.astype(jnp.bfloat16).transpose(0, 2, 3, 1)
    pad = jnp.concatenate([jnp.ones(v.shape[:-1] + (1,), v.dtype),
                           jnp.zeros(v.shape[:-1] + (ATT_V_COLS - ATT_HEAD_DIM - 1,), v.dtype)], axis=-1)
    vx = jnp.concatenate([v, pad], axis=-1).astype(jnp.bfloat16).transpose(0, 2, 1, 3)
    o = pl.pallas_call(
        functools.partial(_attention_kernel, kv_chunk=kv_chunk),
        grid=(Bsz, ATT_KV_HEADS, S // tq),
        in_specs=[
            pl.BlockSpec((1, ATT_GROUP, tq, ATT_HEAD_DIM), lambda b, g, i: (b, g, i, 0)),
            pl.BlockSpec((1, 1, ATT_HEAD_DIM, SK), lambda b, g, i: (b, g, 0, 0)),
            pl.BlockSpec((1, 1, SK, ATT_V_COLS), lambda b, g, i: (b, g, 0, 0)),
        ],
        out_specs=pl.BlockSpec((1, ATT_GROUP, tq, ATT_HEAD_DIM), lambda b, g, i: (b, g, i, 0)),
        out_shape=jax.ShapeDtypeStruct((Bsz, ATT_HEADS, S, ATT_HEAD_DIM), jnp.float32),
        compiler_params=pltpu.CompilerParams(dimension_semantics=("arbitrary", "arbitrary", "arbitrary"),
                                             vmem_limit_bytes=VMEM_LIMIT_BYTES),
        name="attention",
    )(qh, kt, vx)
    return o.transpose(0, 2, 1, 3).reshape(Bsz, S, ATT_HEADS * ATT_HEAD_DIM)


SCAN_STEP_CHUNKS = 1

SSD_GROUP_HEADS = SSD_HEADS // SSD_GROUPS
SSD_GROUP_W = SSD_GROUP_HEADS * SSD_HEAD_DIM
SSD_GATE_COLS = LANES


def _ssd_kernel(*refs, reverse, combine):
    if combine:
        x_ref, bm_ref, cm_ref, grow_ref, gcol_ref, other_ref, z_ref, skip_ref, gain_ref, out_ref, s_scr = refs
    else:
        x_ref, bm_ref, cm_ref, grow_ref, gcol_ref, out_ref, s_scr = refs
    Q, P, N, H = SSD_CHUNK, SSD_HEAD_DIM, SSD_STATE, SSD_HEADS

    @pl.when(pl.program_id(1) == 0)
    def _():
        s_scr[...] = jnp.zeros_like(s_scr)

    row = lax.broadcasted_iota(jnp.int32, (Q, Q), 0)
    col = lax.broadcasted_iota(jnp.int32, (Q, Q), 1)
    not_after = (col >= row) if reverse else (col <= row)
    tri = not_after.astype(jnp.float32)
    last = 0 if reverse else Q - 1
    nt = (((1,), (1,)), ((), ()))
    grow = grow_ref[0]
    gcol = gcol_ref[0]
    a_cols = jnp.dot(tri, gcol, preferred_element_type=jnp.float32, precision=lax.Precision.HIGHEST)
    a_rows = lax.dot_general(grow, tri, nt, preferred_element_type=jnp.float32,
                             precision=lax.Precision.HIGHEST)
    ys = []
    for g in range(SSD_GROUPS):
        ns = slice(g * N, (g + 1) * N)
        bm = bm_ref[0, :, ns].astype(jnp.bfloat16)
        cm = cm_ref[0, :, ns].astype(jnp.bfloat16)
        cb = lax.dot_general(cm, bm, nt, preferred_element_type=jnp.float32)
        state = s_scr[g]
        y_off = jnp.dot(cm, state.astype(jnp.bfloat16), preferred_element_type=jnp.float32)
        xw, decay = [], []
        for r in range(SSD_GROUP_HEADS):
            h = g * SSD_GROUP_HEADS + r
            hs = slice(h * P, (h + 1) * P)
            aq = a_cols[:, H + h:H + h + 1]
            as_ = a_rows[H + h:H + h + 1, :]
            a_tot = as_[:, last:last + 1]
            xdt = x_ref[0, :, hs] * gcol[:, h:h + 1]
            lmat = jnp.exp(jnp.where(not_after, aq - as_, NEG_INF))
            y = jnp.dot((cb * lmat).astype(jnp.bfloat16), xdt.astype(jnp.bfloat16),
                        preferred_element_type=jnp.float32)
            ys.append(y + y_off[:, r * P:(r + 1) * P] * jnp.exp(aq))
            xw.append(jnp.exp(a_tot - aq) * xdt)
            decay.append(jnp.broadcast_to(jnp.exp(a_tot), (1, P)))
        s_loc = lax.dot_general(bm, jnp.concatenate(xw, axis=1).astype(jnp.bfloat16),
                                (((0,), (0,)), ((), ())), preferred_element_type=jnp.float32)
        s_scr[g] = state * jnp.concatenate(decay, axis=1) + s_loc
    y = jnp.concatenate(ys, axis=1)
    if combine:
        x = x_ref[0]
        z = z_ref[0]
        tot = (y + other_ref[0] + x * skip_ref[...]) * (z * jax.nn.sigmoid(z))
        out_ref[0] = tot * lax.rsqrt(jnp.mean(tot * tot, axis=1, keepdims=True) + RMS_EPS) * gain_ref[...]
    else:
        out_ref[0] = y


def _ssd_direction(xbc, grow, gcol, n_ctx_chunks, reverse, extra=None):
    Bsz, TT, _ = xbc.shape
    n_chunks = TT // SSD_CHUNK
    if reverse:
        chunk = lambda c: jnp.where(c < n_ctx_chunks, n_ctx_chunks - 1 - c, n_chunks - 1 + n_ctx_chunks - c)
    else:
        chunk = lambda c: c
    gn = SSD_GROUPS * SSD_STATE
    seq_spec = pl.BlockSpec((1, SSD_CHUNK, SSD_INNER), lambda b, c: (b, chunk(c), 0))
    vec_spec = pl.BlockSpec((1, SSD_INNER), lambda b, c: (0, 0))
    in_specs = [seq_spec,
                pl.BlockSpec((1, SSD_CHUNK, gn), lambda b, c: (b, chunk(c), SSD_INNER // gn)),
                pl.BlockSpec((1, SSD_CHUNK, gn), lambda b, c: (b, chunk(c), SSD_INNER // gn + 1)),
                pl.BlockSpec((1, 2 * SSD_HEADS, SSD_CHUNK), lambda b, c: (b, 0, chunk(c))),
                pl.BlockSpec((1, SSD_CHUNK, SSD_GATE_COLS), lambda b, c: (b, chunk(c), 0))]
    args = [xbc, xbc, xbc, grow, gcol]
    if extra is not None:
        other, z, skip, gain = extra
        z_spec = pl.BlockSpec((1, SSD_CHUNK, SSD_INNER), lambda b, c: (b, chunk(c), z[1]))
        in_specs += [seq_spec, z_spec, vec_spec, vec_spec]
        args += [other, z[0], skip, gain]
    return pl.pallas_call(
        functools.partial(_ssd_kernel, reverse=reverse, combine=extra is not None),
        grid=(Bsz, n_chunks),
        in_specs=in_specs,
        out_specs=seq_spec,
        out_shape=jax.ShapeDtypeStruct((Bsz, TT, SSD_INNER), jnp.float32),
        scratch_shapes=[pltpu.VMEM((SSD_GROUPS, SSD_STATE, SSD_GROUP_W), jnp.float32)],
        compiler_params=pltpu.CompilerParams(dimension_semantics=("arbitrary", "arbitrary"),
                                             vmem_limit_bytes=VMEM_LIMIT_BYTES),
        name="ssd_bwd" if reverse else "ssd_fwd",
    )(*args)


def _ssd_block(z, xbc_act, dt_raw, dt_bias, a_log, d_skip, norm_g, n_ctx):
    Bsz, TT, _ = dt_raw.shape
    dt = jax.nn.softplus(dt_raw.reshape(Bsz, TT, 2, SSD_HEADS) + dt_bias)
    a = dt * -jnp.exp(a_log)
    gate_cols = jnp.concatenate([dt, a], axis=-1)
    skip = jnp.repeat(d_skip, SSD_HEAD_DIM).reshape(1, SSD_INNER)
    outs = None
    for direction in (1, 0):
        gc = gate_cols[:, :, direction]
        grow = gc.transpose(0, 2, 1)
        gcol = jnp.pad(gc, ((0, 0), (0, 0), (0, SSD_GATE_COLS - 2 * SSD_HEADS)))
        extra = None if direction == 1 else (outs, z, skip, norm_g.reshape(1, SSD_INNER))
        outs = _ssd_direction(xbc_act, grow, gcol, n_ctx // SSD_CHUNK, direction == 1, extra)
    return outs


ML_STATE_COLS = 2 * ML_HEAD_DIM
ML_GATE_COLS = LANES


def _mlstm_kernel(*refs, reverse, combine):
    if combine:
        q_ref, k_ref, v_ref, grow_ref, gcol_ref, other_ref, o_ref, gain_ref, out_ref, s_scr, m_scr = refs
    else:
        q_ref, k_ref, v_ref, grow_ref, gcol_ref, out_ref, s_scr, m_scr = refs
    Q, D, H = ML_CHUNK, ML_HEAD_DIM, ML_HEADS

    @pl.when(pl.program_id(1) == 0)
    def _():
        s_scr[...] = jnp.zeros_like(s_scr)
        m_scr[...] = jnp.zeros_like(m_scr)

    row = lax.broadcasted_iota(jnp.int32, (Q, Q), 0)
    col = lax.broadcasted_iota(jnp.int32, (Q, Q), 1)
    not_after = (col >= row) if reverse else (col <= row)
    tri = not_after.astype(jnp.float32)
    last = 0 if reverse else Q - 1
    nt = (((1,), (1,)), ((), ()))
    one_col = (lax.broadcasted_iota(jnp.int32, (Q, D), 1) == 0).astype(jnp.float32)
    states = [s_scr[h] for h in range(H)]
    ms = [m_scr[h, 0:1, 0:1] for h in range(H)]
    chunks = range(SCAN_STEP_CHUNKS)
    for sub in (reversed(chunks) if reverse else chunks):
        rows = slice(sub * Q, (sub + 1) * Q)
        grow = grow_ref[0, :, rows]
        gcol = gcol_ref[0, rows, :]
        b_cols = jnp.dot(tri, gcol, preferred_element_type=jnp.float32, precision=lax.Precision.HIGHEST)
        b_rows = lax.dot_general(grow, tri, nt, preferred_element_type=jnp.float32,
                                 precision=lax.Precision.HIGHEST)
        for h in range(H):
            hs = slice(h * D, (h + 1) * D)
            q = q_ref[0, rows, hs].astype(jnp.bfloat16)
            k = (k_ref[0, rows, hs] * D ** -0.5).astype(jnp.bfloat16)
            v_ext = jnp.concatenate([v_ref[0, rows, hs], one_col], axis=1)
            bq = b_cols[:, H + h:H + h + 1]
            bs = b_rows[H + h:H + h + 1, :]
            i_q = gcol[:, h:h + 1]
            i_s = grow[h:h + 1, :]
            b_last = bs[:, last:last + 1]
            m0, state = ms[h], states[h]
            dm = jnp.where(not_after, bq - bs + i_s, NEG_INF)
            inter = bq + m0
            m_t = jnp.maximum(inter, jnp.max(dm, axis=1, keepdims=True))
            s_mat = lax.dot_general(q, k, nt, preferred_element_type=jnp.float32) * jnp.exp(dm - m_t)
            num = (jnp.dot(s_mat.astype(jnp.bfloat16), v_ext.astype(jnp.bfloat16),
                           preferred_element_type=jnp.float32)
                   + jnp.exp(inter - m_t) * jnp.dot(q, state.astype(jnp.bfloat16),
                                                    preferred_element_type=jnp.float32))
            den = num[:, D:D + 1]
            hout = num[:, :D] / jnp.maximum(jnp.abs(den), jnp.exp(-m_t))
            g_q = b_last - bq + i_q
            m_loc = jnp.max(g_q, axis=0, keepdims=True)
            wv = (jnp.exp(g_q - m_loc) * v_ext).astype(jnp.bfloat16)
            s_loc = lax.dot_general(k, wv, (((0,), (0,)), ((), ())), preferred_element_type=jnp.float32)
            m_new = jnp.maximum(b_last + m0, m_loc)
            states[h] = jnp.exp(b_last + m0 - m_new) * state + jnp.exp(m_loc - m_new) * s_loc
            ms[h] = m_new
            if combine:
                tot = hout + other_ref[0, rows, hs]
                y = tot * lax.rsqrt(jnp.mean(tot * tot, axis=1, keepdims=True) + RMS_EPS) * gain_ref[:, hs]
                out_ref[0, rows, hs] = y * jax.nn.sigmoid(o_ref[0, rows, hs])
            else:
                out_ref[0, rows, hs] = hout
    for h in range(H):
        s_scr[h] = states[h]
        m_scr[h] = jnp.broadcast_to(ms[h], (SUBLANES, LANES))


def _mlstm_direction(q, k, v, grow, gcol, n_ctx_chunks, reverse, extra=None):
    Bsz, TT, _ = q[0].shape
    step = SCAN_STEP_CHUNKS * ML_CHUNK
    n_chunks = TT // step
    if reverse:
        chunk = lambda c: jnp.where(c < n_ctx_chunks, n_ctx_chunks - 1 - c, n_chunks - 1 + n_ctx_chunks - c)
    else:
        chunk = lambda c: c
    seq_spec = pl.BlockSpec((1, step, ML_INNER), lambda b, c: (b, chunk(c), 0))
    col_spec = lambda j: pl.BlockSpec((1, step, ML_INNER), lambda b, c: (b, chunk(c), j))
    in_specs = [col_spec(q[1]), col_spec(k[1]), col_spec(v[1]),
                pl.BlockSpec((1, 2 * ML_HEADS, step), lambda b, c: (b, 0, chunk(c))),
                pl.BlockSpec((1, step, ML_GATE_COLS), lambda b, c: (b, chunk(c), 0))]
    args = [q[0], k[0], v[0], grow, gcol]
    if extra is not None:
        other, o_pre, gain = extra
        in_specs += [seq_spec, col_spec(o_pre[1]), pl.BlockSpec((1, ML_INNER), lambda b, c: (0, 0))]
        args += [other, o_pre[0], gain]
    return pl.pallas_call(
        functools.partial(_mlstm_kernel, reverse=reverse, combine=extra is not None),
        grid=(Bsz, n_chunks),
        in_specs=in_specs,
        out_specs=seq_spec,
        out_shape=jax.ShapeDtypeStruct((Bsz, TT, ML_INNER), jnp.float32),
        scratch_shapes=[pltpu.VMEM((ML_HEADS, ML_HEAD_DIM, ML_STATE_COLS), jnp.float32),
                        pltpu.VMEM((ML_HEADS, SUBLANES, LANES), jnp.float32)],
        compiler_params=pltpu.CompilerParams(dimension_semantics=("arbitrary", "arbitrary"),
                                             vmem_limit_bytes=VMEM_LIMIT_BYTES),
        name="mlstm_bwd" if reverse else "mlstm_fwd",
    )(*args)


def _mlstm_block(q, k, v, o_pre, if_pre, gate_bias, norm_g, n_ctx):
    Bsz, TT, _ = if_pre.shape
    gates = if_pre.reshape(Bsz, TT, 2, 2, ML_HEADS) + gate_bias
    gate_cols = jnp.concatenate([gates[:, :, :, 0], jax.nn.log_sigmoid(gates[:, :, :, 1])], axis=-1)
    outs = None
    for direction in (1, 0):
        gc = gate_cols[:, :, direction]
        grow = gc.transpose(0, 2, 1)
        gcol = jnp.pad(gc, ((0, 0), (0, 0), (0, ML_GATE_COLS - 2 * ML_HEADS)))
        extra = None if direction == 1 else (outs, o_pre, norm_g.reshape(1, ML_INNER))
        outs = _mlstm_direction(q, k, v, grow, gcol, n_ctx // (SCAN_STEP_CHUNKS * ML_CHUNK), direction == 1, extra)
    return outs


_PROJ_ORDER = ('aq', 'sz', 'mq', 'mk', 'mv', 'mo', 'gpre', 'sxbc', 'ak', 'av', 'sdt', 'mif')
_IN_NAMES = ('aq', 'ak', 'av', 'sz', 'sxbc', 'sdt', 'mq', 'mk', 'mv', 'mo', 'mif', 'gpre')
_IN_START = dict(zip(_IN_NAMES, np.cumsum((0,) + IN_SPLITS[:-1]).tolist()))
_IN_WIDTH = dict(zip(_IN_NAMES, IN_SPLITS))
_PROJ_PERM = np.concatenate([np.arange(_IN_START[n], _IN_START[n] + _IN_WIDTH[n]) for n in _PROJ_ORDER])
_PROJ_START = dict(zip(_PROJ_ORDER, np.cumsum([0] + [_IN_WIDTH[n] for n in _PROJ_ORDER[:-1]]).tolist()))
PROJ_COL_BLOCKS = 3
PROJ_COLS = -(-sum(IN_SPLITS) // (PROJ_COL_BLOCKS * LANES)) * (PROJ_COL_BLOCKS * LANES)
ROW_BLOCK = 256


def _proj_cols(proj, name):
    return proj[..., _PROJ_START[name]:_PROJ_START[name] + _IN_WIDTH[name]]


def _proj_block(name, width):
    assert _PROJ_START[name] % width == 0
    return _PROJ_START[name] // width


def _modulated_norm(x, gain, scale, shift):
    y = x * lax.rsqrt(jnp.mean(x * x, axis=-1, keepdims=True) + RMS_EPS)
    return (y * gain) * (1.0 + scale) + shift


def _inproj_kernel(x_ref, gain_ref, mod_ref, w_ref, o_ref):
    h = _modulated_norm(x_ref[0], gain_ref[...], mod_ref[0, 0, 0:1, :], mod_ref[0, 0, 1:2, :])
    o_ref[0] = jnp.dot(h.astype(jnp.bfloat16), w_ref[...], preferred_element_type=jnp.float32)


def _input_projection(x_all, gain, mod, w_perm, n_ctx):
    Bsz, TT, _ = x_all.shape
    cols = PROJ_COLS // PROJ_COL_BLOCKS
    ctx_blocks = n_ctx // ROW_BLOCK
    return pl.pallas_call(
        _inproj_kernel,
        grid=(PROJ_COL_BLOCKS, Bsz, TT // ROW_BLOCK),
        in_specs=[
            pl.BlockSpec((1, ROW_BLOCK, D_MODEL), lambda n, b, r: (b, r, 0)),
            pl.BlockSpec((1, D_MODEL), lambda n, b, r: (0, 0)),
            pl.BlockSpec((1, 1, 2, D_MODEL), lambda n, b, r: (b, jnp.where(r < ctx_blocks, 0, 1), 0, 0)),
            pl.BlockSpec((D_MODEL, cols), lambda n, b, r: (0, n)),
        ],
        out_specs=pl.BlockSpec((1, ROW_BLOCK, cols), lambda n, b, r: (b, r, n)),
        out_shape=jax.ShapeDtypeStruct((Bsz, TT, PROJ_COLS), jnp.float32),
        compiler_params=pltpu.CompilerParams(dimension_semantics=("arbitrary", "arbitrary", "arbitrary"),
                                             vmem_limit_bytes=VMEM_LIMIT_BYTES),
        name="input_projection",
    )(x_all, gain.reshape(1, D_MODEL), mod, w_perm)


def _merge_kernel(att_ref, ssd_ref, ml_ref, g0_ref, g1_ref, g2_ref, wb_ref, wo_ref, x_ref, gain_ref, mod_ref,
                  xo_ref, h2_ref):
    mixed = None
    for r, (branch, gate) in enumerate(((att_ref, g0_ref), (ssd_ref, g1_ref), (ml_ref, g2_ref))):
        p = jnp.dot(branch[0].astype(jnp.bfloat16), wb_ref[r], preferred_element_type=jnp.float32)
        term = jax.nn.sigmoid(gate[0]) * p
        mixed = term if mixed is None else mixed + term
    y = jnp.dot(mixed.astype(jnp.bfloat16), wo_ref[...], preferred_element_type=jnp.float32)
    x_new = x_ref[0] + mod_ref[0, 0, 0:1, :] * y
    xo_ref[0] = x_new
    h2 = _modulated_norm(x_new, gain_ref[...], mod_ref[0, 0, 1:2, :], mod_ref[0, 0, 2:3, :])
    h2_ref[0] = h2.astype(jnp.bfloat16)


def _merge(att, ssd, ml, proj, w_branch, w_out, x_all, gain, mod, n_ctx):
    Bsz, TT, _ = x_all.shape
    ctx_blocks = n_ctx // ROW_BLOCK
    row = lambda w, j=0: pl.BlockSpec((1, ROW_BLOCK, w), lambda b, r: (b, r, j))
    g_first = _proj_block('gpre', D_MODEL)
    return pl.pallas_call(
        _merge_kernel,
        grid=(Bsz, TT // ROW_BLOCK),
        in_specs=[
            row(BRANCH_W), row(BRANCH_W), row(BRANCH_W),
            row(D_MODEL, g_first), row(D_MODEL, g_first + 1), row(D_MODEL, g_first + 2),
            pl.BlockSpec((N_BRANCH, BRANCH_W, D_MODEL), lambda b, r: (0, 0, 0)),
            pl.BlockSpec((D_MODEL, D_MODEL), lambda b, r: (0, 0)),
            row(D_MODEL),
            pl.BlockSpec((1, D_MODEL), lambda b, r: (0, 0)),
            pl.BlockSpec((1, 1, 3, D_MODEL), lambda b, r: (b, jnp.where(r < ctx_blocks, 0, 1), 0, 0)),
        ],
        out_specs=[row(D_MODEL), row(D_MODEL)],
        out_shape=[jax.ShapeDtypeStruct((Bsz, TT, D_MODEL), jnp.float32),
                   jax.ShapeDtypeStruct((Bsz, TT, D_MODEL), jnp.bfloat16)],
        compiler_params=pltpu.CompilerParams(dimension_semantics=("arbitrary", "arbitrary"),
                                             vmem_limit_bytes=VMEM_LIMIT_BYTES),
        name="merge",
    )(att, ssd, ml, proj, proj, proj, w_branch.astype(jnp.bfloat16), w_out.astype(jnp.bfloat16),
      x_all, gain.reshape(1, D_MODEL), mod)


def _rmsnorm(x, g):
    xf = x.astype(jnp.float32)
    y = xf * lax.rsqrt(jnp.mean(xf * xf, axis=-1, keepdims=True) + RMS_EPS)
    return (y * g.astype(jnp.float32)).astype(x.dtype)


def _heads(t, n):
    return t.reshape(t.shape[0], t.shape[1], n, -1)


def _axial_rope(rows):
    n_freq = ATT_HEAD_DIM // 4
    inv = ROPE_THETA ** (-jnp.arange(n_freq, dtype=jnp.float32) / n_freq)
    row = jnp.broadcast_to(jnp.arange(rows, dtype=jnp.float32)[:, None], (rows, GRID_W)).reshape(-1)
    col = jnp.broadcast_to(jnp.arange(GRID_W, dtype=jnp.float32)[None, :], (rows, GRID_W)).reshape(-1)
    ang = jnp.concatenate([row[:, None] * inv, col[:, None] * inv], axis=-1)
    return jnp.cos(ang), jnp.sin(ang)


def _apply_rope(t, cos, sin):
    half = t.shape[-1] // 2
    tf = t.astype(jnp.float32)
    t1, t2 = tf[..., :half], tf[..., half:]
    c, s = cos[:, None, :], sin[:, None, :]
    return jnp.concatenate([t1 * c - t2 * s, t1 * s + t2 * c], axis=-1).astype(t.dtype)


ATT_KV_CHUNK = 768


def _latent_attention(q, k, v, kc, vc):
    kk = jnp.concatenate([k, kc], axis=1)
    vv = jnp.concatenate([v, vc], axis=1)
    assert kk.shape[1] % ATT_KV_CHUNK == 0
    return _attention(q, kk, vv, ATT_KV_CHUNK)


def _context_attention(qc, kc, vc):
    return _attention(qc, kc, vc, kc.shape[1])


def _dwconv(x, w, b):
    out = lax.conv_general_dilated(x, w[:, None, :], window_strides=(1,), padding='SAME',
                                   dimension_numbers=('NWC', 'WIO', 'NWC'),
                                   feature_group_count=x.shape[-1])
    return out + b


def _token_mixer(proj, n_ctx, qn, kn, conv_w, conv_b, dt_bias, a_log, d_skip, ssd_norm_g,
                 ml_bias, ml_norm_g, cos, sin, ctx_out):
    Bsz = proj.shape[0]
    cat = lambda a, b: jnp.concatenate([a, b], axis=1)
    aq, ak, av = _proj_cols(proj, 'aq'), _proj_cols(proj, 'ak'), _proj_cols(proj, 'av')
    q = _apply_rope(_rmsnorm(_heads(aq[:, n_ctx:], ATT_HEADS), qn), cos, sin)
    k = _apply_rope(_rmsnorm(_heads(ak[:, n_ctx:], ATT_KV_HEADS), kn), cos, sin)
    kc = _rmsnorm(_heads(ak[:, :n_ctx], ATT_KV_HEADS), kn)
    vc = _heads(av[:, :n_ctx], ATT_KV_HEADS)
    att = _latent_attention(q, k, _heads(av[:, n_ctx:], ATT_KV_HEADS), kc, vc)
    if ctx_out:
        att_c = _context_attention(_rmsnorm(_heads(aq[:, :n_ctx], ATT_HEADS), qn), kc, vc)
    else:
        att_c = jnp.zeros((Bsz, n_ctx, att.shape[-1]), att.dtype)
    sxbc = _proj_cols(proj, 'sxbc')
    xbc_act = cat(jax.nn.silu(_dwconv(sxbc[:, :n_ctx], conv_w, conv_b)),
                  jax.nn.silu(_dwconv(sxbc[:, n_ctx:], conv_w, conv_b)))
    ssd = _ssd_block((proj, _proj_block('sz', SSD_INNER)), xbc_act, _proj_cols(proj, 'sdt'),
                     dt_bias, a_log, d_skip, ssd_norm_g, n_ctx)
    blk = lambda name: (proj, _proj_block(name, ML_INNER))
    ml = _mlstm_block(blk('mq'), blk('mk'), blk('mv'), blk('mo'), _proj_cols(proj, 'mif'),
                      ml_bias, ml_norm_g, n_ctx)
    return cat(att_c, att), ssd, ml


def kernel(x, c, ctx, c_ctx, norm1_g, norm2_g, w_mod, b_mod, w_in, att_qnorm, att_knorm, ssd_conv_w, ssd_conv_b, ssd_dt_bias, ssd_a_log, ssd_d, ssd_norm, ml_gate_bias, ml_norm, w_branch, w_out, peer_wq, peer_subkeys, peer_u, peer_v):
    Bsz, S, _ = x.shape
    depth = w_in.shape[0]
    cos, sin = _axial_rope(S // GRID_W)
    silu_c = jax.nn.silu(c)
    silu_cc = jax.nn.silu(c_ctx)
    xc = ctx
    n_ctx = ctx.shape[1]
    for l in range(depth):
        last = l == depth - 1
        m = (silu_c @ w_mod[l] + b_mod[l]).reshape(Bsz, 6, D_MODEL)
        mc = jnp.broadcast_to((silu_cc @ w_mod[l] + b_mod[l]).reshape(1, 6, D_MODEL), (Bsz, 6, D_MODEL))
        both = jnp.stack([mc, m], axis=1)
        x_all = jnp.concatenate([xc, x], axis=1)
        w_perm = jnp.concatenate(
            [w_in[l][:, _IN_START[n]:_IN_START[n] + _IN_WIDTH[n]] for n in _PROJ_ORDER]
            + [jnp.zeros((D_MODEL, PROJ_COLS - sum(IN_SPLITS)), w_in.dtype)], axis=1).astype(jnp.bfloat16)
        proj = _input_projection(x_all, norm1_g[l], both[:, :, np.array([1, 0])], w_perm, n_ctx)
        att, ssd, ml = _token_mixer(proj, n_ctx, att_qnorm[l], att_knorm[l], ssd_conv_w[l], ssd_conv_b[l],
                                    ssd_dt_bias[l], ssd_a_log[l], ssd_d[l], ssd_norm[l], ml_gate_bias[l],
                                    ml_norm[l], cos, sin, not last)
        x_all, h2 = _merge(att, ssd, ml, proj, w_branch[l], w_out[l], x_all, norm2_g[l],
                           both[:, :, np.array([2, 4, 3])], n_ctx)
        u_bf16 = peer_u[l].astype(jnp.bfloat16)
        vt_bf16 = peer_v[l].astype(jnp.bfloat16).reshape(PEER_EXPERTS // PEER_EBLK, PEER_EBLK, D_MODEL)
        vt_bf16 = vt_bf16.transpose(0, 2, 1)
        po = _peer(h2[:, n_ctx:].reshape(-1, D_MODEL), peer_wq[l], peer_subkeys[l], u_bf16, vt_bf16)
        x = x_all[:, n_ctx:] + m[:, None, 5] * po.reshape(x.shape)
        if not last:
            pc = _peer(h2[:, :n_ctx].reshape(-1, D_MODEL), peer_wq[l], peer_subkeys[l], u_bf16, vt_bf16)
            xc = x_all[:, :n_ctx] + mc[:, None, 5] * pc.reshape(xc.shape)
    return x
```

```python
import functools
import math

import jax
import jax.numpy as jnp
import numpy as np
from jax import lax
from jax.experimental import pallas as pl
from jax.experimental.pallas import tpu as pltpu

D_MODEL = 1024
GRID_W = 64
RMS_EPS = 1e-6
N_BRANCH = 3
BRANCH_W = 512
ATT_HEADS = 8
ATT_KV_HEADS = 2
ATT_HEAD_DIM = 64
ATT_GROUP = ATT_HEADS // ATT_KV_HEADS
Q_BLOCK = 128
ROPE_THETA = 10000.0
SSD_HEADS = 8
SSD_HEAD_DIM = 64
SSD_INNER = SSD_HEADS * SSD_HEAD_DIM
SSD_GROUPS = 2
SSD_STATE = 64
SSD_XBC = SSD_INNER + 2 * SSD_GROUPS * SSD_STATE
SSD_CONV = 5
SSD_CHUNK = 128
ML_HEADS = 4
ML_HEAD_DIM = 128
ML_INNER = ML_HEADS * ML_HEAD_DIM
ML_CHUNK = 128
PEER_HEADS = 8
PEER_KEYS = 128
PEER_EXPERTS = PEER_KEYS * PEER_KEYS
PEER_TOPK = 16
PEER_QDIM = 256
PEER_HALF = PEER_QDIM // 2
IN_SPLITS = (ATT_HEADS * ATT_HEAD_DIM, ATT_KV_HEADS * ATT_HEAD_DIM, ATT_KV_HEADS * ATT_HEAD_DIM,
             SSD_INNER, SSD_XBC, 2 * SSD_HEADS,
             ML_INNER, ML_INNER, ML_INNER, ML_INNER, 4 * ML_HEADS,
             N_BRANCH * D_MODEL)

LANES = 128
SUBLANES = 8
VMEM_LIMIT_BYTES = 56 * 1024 * 1024

ROUTER_TOKENS = 256
ROUTER_CHUNK = LANES
PEER_TOKENS = 1024
PEER_TOKENS_SMALL = 512
PEER_ROWS = 8
PEER_EBLK = PEER_ROWS * PEER_KEYS
PEER_SLICE = 2 * PEER_KEYS
PEER_CHUNK = 256

NEG_INF = float("-inf")


def _sort_network(n):
    def merge(lo, hi, r):
        step = r * 2
        if step < hi - lo:
            yield from merge(lo, hi, step)
            yield from merge(lo + r, hi, step)
            yield from [(i, i + r) for i in range(lo + r, hi - r, step)]
        else:
            yield (lo, lo + r)

    def sort(lo, hi):
        if hi - lo >= 1:
            mid = lo + (hi - lo) // 2
            yield from sort(lo, mid)
            yield from sort(mid + 1, hi)
            yield from merge(lo, hi, 1)

    return tuple(sort(0, n - 1))


_SORT16 = _sort_network(PEER_TOPK)


def _merge_top(lists, n):
    lists = list(lists)
    depth = len(lists)
    sub = lax.broadcasted_iota(jnp.int32, lists[0].shape, 0)
    tops = []
    for i in range(n):
        head = lists[0]
        m = jnp.max(head, axis=0, keepdims=True)
        tops.append(m)
        live = min(depth, n - 1 - i)
        if live == 0:
            break
        first = jnp.min(jnp.where(head == m, sub, SUBLANES), axis=0, keepdims=True)
        pop = sub == first
        for r in range(live):
            nxt = lists[r + 1] if r + 1 < depth else NEG_INF
            lists[r] = jnp.where(pop, nxt, lists[r])
    return tops


def _sorted_top(s, n):
    xs = [s[r * SUBLANES:(r + 1) * SUBLANES, :] for r in range(s.shape[0] // SUBLANES)]
    for i, j in _SORT16:
        xs[i], xs[j] = jnp.maximum(xs[i], xs[j]), jnp.minimum(xs[i], xs[j])
    return _merge_top(xs, n)


def _prefix_count(values, pred):
    n = len(values)
    steps = [n >> (i + 1) for i in range(n.bit_length() - 1)]
    taken = []

    def pivot(level, base, step):
        if level == len(taken):
            return values[base + step - 1]
        return jnp.where(taken[level], pivot(level + 1, base + steps[level], step), pivot(level + 1, base, step))

    count = None
    for step in steps:
        ok = pred(pivot(0, 0, step))
        taken.append(ok)
        term = jnp.where(ok, float(step), 0.0)
        count = term if count is None else count + term
    return jnp.where(pred(values[n - 1]), float(n), count)


def _rows_to_sublanes(rows, first_sublane, shape):
    sub = lax.broadcasted_iota(jnp.int32, shape, 0)
    out = jnp.zeros(shape, jnp.float32)
    for k, row in enumerate(rows):
        out = jnp.where(sub == first_sublane + k, row, out)
    return out


_CAND_LEN = (16, 8, 5, 4, 12, 4, 1, 0)


def _router_kernel(h_ref, wq_ref, sk_ref, rank_ref, e2_ref, cnt_ref, e1_ref, q_scr):
    q_scr[...] = jnp.dot(h_ref[...], wq_ref[...], preferred_element_type=jnp.float32).astype(jnp.bfloat16)
    nt = (((1,), (1,)), ((), ()))
    shape8 = (SUBLANES, ROUTER_CHUNK)
    sub = lax.broadcasted_iota(jnp.int32, shape8, 0)
    cand_len = jnp.zeros(shape8, jnp.int32)
    for g, n in enumerate(_CAND_LEN):
        cand_len = jnp.where(sub == g, n, cand_len)
    for h in range(PEER_HEADS):
        q1 = q_scr[:, (2 * h) * PEER_HALF:(2 * h + 1) * PEER_HALF]
        q2 = q_scr[:, (2 * h + 1) * PEER_HALF:(2 * h + 2) * PEER_HALF]
        s1 = lax.dot_general(sk_ref[2 * h], q1, nt, preferred_element_type=jnp.float32)
        s2 = lax.dot_general(sk_ref[2 * h + 1], q2, nt, preferred_element_type=jnp.float32)
        for c in range(ROUTER_TOKENS // ROUTER_CHUNK):
            cs = slice(c * ROUTER_CHUNK, (c + 1) * ROUTER_CHUNK)
            s1c = s1[:, cs]
            s2c = s2[:, cs]
            top1 = _sorted_top(s1c, PEER_TOPK)
            top2 = _sorted_top(s2c, PEER_TOPK)
            a_lo = _rows_to_sublanes(top1[:4], 0, shape8)
            b_lo = _rows_to_sublanes(top2[:3], 4, shape8)
            cands = []
            for r in range(PEER_TOPK):
                by_a = a_lo + top2[r]
                by_b = (top1[4 + r] + b_lo) if 4 + r < PEER_TOPK else by_a
                cands.append(jnp.where(r < cand_len, jnp.where(sub < 4, by_a, by_b), NEG_INF))
            tau = _merge_top(cands, PEER_TOPK)[-1]
            m1, m2 = top1[0], top2[0]
            z = jnp.zeros(shape8, jnp.float32)
            for cand in cands:
                z = z + jnp.where(cand >= tau, jnp.exp(cand - (m1 + m2)), 0.0)
            z = jnp.sum(z, axis=0, keepdims=True)
            rank = _prefix_count(top2, lambda t: t > s2c)
            cnt = _prefix_count(top2, lambda t: s1c + t >= tau)
            rank_ref[h, :, cs] = rank.astype(jnp.bfloat16)
            e2_ref[h, :, cs] = jnp.exp(s2c - m2).astype(jnp.bfloat16)
            cnt_ref[h, :, cs] = cnt
            e1_ref[h, :, cs] = jnp.exp(s1c - m1) * (0.5 / z)


def _peer_router(h2_bf16, wq, subkeys):
    n_tok = h2_bf16.shape[0]
    sk = subkeys.reshape(PEER_HEADS * 2, PEER_KEYS, PEER_HALF)
    shape = (PEER_HEADS, PEER_KEYS, n_tok)
    spec = pl.BlockSpec((PEER_HEADS, PEER_KEYS, ROUTER_TOKENS), lambda i: (0, 0, i))
    return pl.pallas_call(
        _router_kernel,
        grid=(n_tok // ROUTER_TOKENS,),
        in_specs=[
            pl.BlockSpec((ROUTER_TOKENS, D_MODEL), lambda i: (i, 0)),
            pl.BlockSpec((D_MODEL, PEER_HEADS * PEER_QDIM), lambda i: (0, 0)),
            pl.BlockSpec((PEER_HEADS * 2, PEER_KEYS, PEER_HALF), lambda i: (0, 0, 0)),
        ],
        out_specs=[spec, spec, spec, spec],
        out_shape=[jax.ShapeDtypeStruct(shape, jnp.bfloat16), jax.ShapeDtypeStruct(shape, jnp.bfloat16),
                   jax.ShapeDtypeStruct(shape, jnp.float32), jax.ShapeDtypeStruct(shape, jnp.float32)],
        scratch_shapes=[pltpu.VMEM((ROUTER_TOKENS, PEER_HEADS * PEER_QDIM), jnp.bfloat16)],
        compiler_params=pltpu.CompilerParams(dimension_semantics=("arbitrary",),
                                             vmem_limit_bytes=VMEM_LIMIT_BYTES),
        name="peer_router",
    )(h2_bf16, wq.astype(jnp.bfloat16), sk.astype(jnp.bfloat16))


GELU_C0 = math.sqrt(2.0 / math.pi)
GELU_C1 = 0.044715 * GELU_C0


def _peer_dense_kernel(h_ref, u_ref, vt_ref, rank_ref, e2_ref, cnt_ref, e1_ref, o_ref,
                       acc_scr, pre_scr, p_scr, rank_scr, e2_scr):
    j = pl.program_id(1)
    n_tok = h_ref.shape[0]
    n_slices = PEER_EBLK // PEER_SLICE
    n_chunks = n_tok // PEER_CHUNK
    nt = (((1,), (1,)), ((), ()))

    @pl.when(j == 0)
    def _():
        acc_scr[...] = jnp.zeros_like(acc_scr)

        def relayout(t, carry):
            lanes = pl.ds(pl.multiple_of(t * LANES, LANES), LANES)
            for h in range(PEER_HEADS):
                rank = rank_ref[h, :, lanes].astype(jnp.float32)
                e2 = e2_ref[h, :, lanes].astype(jnp.float32)
                for b in range(PEER_KEYS // SUBLANES):
                    rank_scr[t, b, h] = rank[b * SUBLANES:(b + 1) * SUBLANES]
                    e2_scr[t, b, h] = e2[b * SUBLANES:(b + 1) * SUBLANES]
            return carry

        lax.fori_loop(0, n_tok // LANES, relayout, 0)

    def pre_matmul(c, ap, slot):
        tok = pl.multiple_of(c * PEER_CHUNK, PEER_CHUNK)
        rows = slice(ap * PEER_SLICE, (ap + 1) * PEER_SLICE)
        pre_scr[slot] = lax.dot_general(u_ref[rows, :], h_ref[pl.ds(tok, PEER_CHUNK), :], nt,
                                        preferred_element_type=jnp.float32)

    def out_matmul(c, parity):
        tok = pl.multiple_of(c * PEER_CHUNK, PEER_CHUNK)
        acc_scr[:, pl.ds(tok, PEER_CHUNK)] += jnp.dot(vt_ref[0], p_scr[parity],
                                                      preferred_element_type=jnp.float32)

    def gates(c, ap, slot):
        first_a = ap * (PEER_SLICE // PEER_KEYS)
        for t in range(PEER_CHUNK // LANES):
            lanes = pl.ds(pl.multiple_of(c * PEER_CHUNK + t * LANES, LANES), LANES)
            ts = slice(t * LANES, (t + 1) * LANES)
            tile = c * (PEER_CHUNK // LANES) + t
            for b in range(PEER_KEYS // SUBLANES):
                rank = [rank_scr[tile, b, h] for h in range(PEER_HEADS)]
                e2 = [e2_scr[tile, b, h] for h in range(PEER_HEADS)]
                for a in range(PEER_SLICE // PEER_KEYS):
                    rs = slice(a * PEER_KEYS + b * SUBLANES, a * PEER_KEYS + (b + 1) * SUBLANES)
                    terms = []
                    for h in range(PEER_HEADS):
                        cnt = cnt_ref[first_a + a, h:h + 1, lanes]
                        e1 = e1_ref[first_a + a, h:h + 1, lanes]
                        terms.append(jnp.where(rank[h] < cnt, e2[h], 0.0) * e1)
                    while len(terms) > 1:
                        terms = [x + y for x, y in zip(terms[::2], terms[1::2])]
                    x = pre_scr[slot, rs, ts]
                    act = x * (1.0 + jnp.tanh(x * (GELU_C0 + GELU_C1 * (x * x))))
                    pre_scr[slot, rs, ts] = terms[0] * act
        p_scr[c % 2, ap * PEER_SLICE:(ap + 1) * PEER_SLICE, :] = pre_scr[slot].astype(jnp.bfloat16)

    p_scr[1] = jnp.zeros(p_scr.shape[1:], p_scr.dtype)
    pre_matmul(0, 0, 0)

    def body(c, carry):
        out_matmul(jnp.maximum(c - 1, 0), 1 - c % 2)
        for ap in range(n_slices):
            slot = ap % 2
            if ap + 1 < n_slices:
                pre_matmul(c, ap + 1, 1 - slot)
            else:
                pre_matmul(jnp.minimum(c + 1, n_chunks - 1), 0, 1 - slot)
            gates(c, ap, slot)
        return carry

    lax.fori_loop(0, n_chunks, body, 0)
    out_matmul(n_chunks - 1, (n_chunks - 1) % 2)

    @pl.when(j == pl.num_programs(1) - 1)
    def _():
        o_ref[...] = acc_scr[...].T


def _peer_dense(h2_bf16, u_bf16, vt_bf16, rank, e2, cnt, e1, tb):
    n_tok = h2_bf16.shape[0]
    row_spec = pl.BlockSpec((PEER_ROWS, PEER_HEADS, tb), lambda i, j: (j, 0, i))
    tiled = (tb // LANES, PEER_KEYS // SUBLANES, PEER_HEADS, SUBLANES, LANES)
    key_spec = pl.BlockSpec((PEER_HEADS, PEER_KEYS, tb), lambda i, j: (0, 0, i))
    return pl.pallas_call(
        _peer_dense_kernel,
        grid=(n_tok // tb, PEER_EXPERTS // PEER_EBLK),
        in_specs=[
            pl.BlockSpec((tb, D_MODEL), lambda i, j: (i, 0)),
            pl.BlockSpec((PEER_EBLK, D_MODEL), lambda i, j: (j, 0)),
            pl.BlockSpec((1, D_MODEL, PEER_EBLK), lambda i, j: (j, 0, 0)),
            key_spec, key_spec, row_spec, row_spec,
        ],
        out_specs=pl.BlockSpec((tb, D_MODEL), lambda i, j: (i, 0)),
        out_shape=jax.ShapeDtypeStruct((n_tok, D_MODEL), jnp.float32),
        scratch_shapes=[pltpu.VMEM((D_MODEL, tb), jnp.float32),
                        pltpu.VMEM((2, PEER_SLICE, PEER_CHUNK), jnp.float32),
                        pltpu.VMEM((2, PEER_EBLK, PEER_CHUNK), jnp.bfloat16),
                        pltpu.VMEM(tiled, jnp.float32),
                        pltpu.VMEM(tiled, jnp.float32)],
        compiler_params=pltpu.CompilerParams(dimension_semantics=("arbitrary", "arbitrary"),
                                             vmem_limit_bytes=VMEM_LIMIT_BYTES),
        name="peer_dense",
    )(h2_bf16, u_bf16, vt_bf16, rank, e2, cnt, e1)


def _peer(h2, wq, subkeys, u_bf16, vt_bf16):
    n_tok = h2.shape[0]
    tb = PEER_TOKENS if n_tok % PEER_TOKENS == 0 else PEER_TOKENS_SMALL
    h2_bf16 = h2.astype(jnp.bfloat16)
    rank, e2, cnt, e1 = _peer_router(h2_bf16, wq, subkeys)
    cnt, e1 = cnt.transpose(1, 0, 2), e1.transpose(1, 0, 2)
    return _peer_dense(h2_bf16, u_bf16, vt_bf16, rank, e2, cnt, e1, tb)


ATT_Q_TOKENS = 256
ATT_V_COLS = 2 * ATT_HEAD_DIM


def _attention_kernel(q_ref, kt_ref, v_ref, o_ref, *, kv_chunk):
    tq = q_ref.shape[2]
    rows = ATT_GROUP * tq
    q = q_ref[0].reshape(rows, ATT_HEAD_DIM)
    n_chunks = kt_ref.shape[3] // kv_chunk

    def body(c, carry):
        m, acc = carry
        off = pl.multiple_of(c * kv_chunk, kv_chunk)
        s = jnp.dot(q, kt_ref[0, 0, :, pl.ds(off, kv_chunk)], preferred_element_type=jnp.float32)
        m_new = jnp.maximum(m, jnp.max(s, axis=-1, keepdims=True))
        p = jnp.exp(s - m_new).astype(jnp.bfloat16)
        acc = jnp.exp(m - m_new) * acc + jnp.dot(p, v_ref[0, 0, pl.ds(off, kv_chunk), :],
                                                 preferred_element_type=jnp.float32)
        return m_new, acc

    m0 = jnp.full((rows, 1), NEG_INF, jnp.float32)
    acc0 = jnp.zeros((rows, ATT_V_COLS), jnp.float32)
    _, acc = lax.fori_loop(0, n_chunks, body, (m0, acc0), unroll=True)
    out = acc[:, :ATT_HEAD_DIM] / acc[:, ATT_HEAD_DIM:ATT_HEAD_DIM + 1]
    o_ref[0] = out.reshape(ATT_GROUP, tq, ATT_HEAD_DIM)


def _attention(q, k, v, kv_chunk):
    Bsz, S = q.shape[0], q.shape[1]
    SK = k.shape[1]
    tq = min(ATT_Q_TOKENS, S)
    qh = (q * ATT_HEAD_DIM ** -0.5).astype(jnp.bfloat16).transpose(0, 2, 1, 3)
    kt = k.astype(jnp.bfloat16).transpose(0, 2, 3, 1)
    pad = jnp.concatenate([jnp.ones(v.shape[:-1] + (1,), v.dtype),
                           jnp.zeros(v.shape[:-1] + (ATT_V_COLS - ATT_HEAD_DIM - 1,), v.dtype)], axis=-1)
    vx = jnp.concatenate([v, pad], axis=-1).astype(jnp.bfloat16).transpose(0, 2, 1, 3)
    o = pl.pallas_call(
        functools.partial(_attention_kernel, kv_chunk=kv_chunk),
        grid=(Bsz, ATT_KV_HEADS, S // tq),
        in_specs=[
            pl.BlockSpec((1, ATT_GROUP, tq, ATT_HEAD_DIM), lambda b, g, i: (b, g, i, 0)),
            pl.BlockSpec((1, 1, ATT_HEAD_DIM, SK), lambda b, g, i: (b, g, 0, 0)),
            pl.BlockSpec((1, 1, SK, ATT_V_COLS), lambda b, g, i: (b, g, 0, 0)),
        ],
        out_specs=pl.BlockSpec((1, ATT_GROUP, tq, ATT_HEAD_DIM), lambda b, g, i: (b, g, i, 0)),
        out_shape=jax.ShapeDtypeStruct((Bsz, ATT_HEADS, S, ATT_HEAD_DIM), jnp.float32),
        compiler_params=pltpu.CompilerParams(dimension_semantics=("arbitrary", "arbitrary", "arbitrary"),
                                             vmem_limit_bytes=VMEM_LIMIT_BYTES),
        name="attention",
    )(qh, kt, vx)
    return o.transpose(0, 2, 1, 3).reshape(Bsz, S, ATT_HEADS * ATT_HEAD_DIM)


SCAN_STEP_CHUNKS = 1

SSD_GROUP_HEADS = SSD_HEADS // SSD_GROUPS
SSD_GROUP_W = SSD_GROUP_HEADS * SSD_HEAD_DIM
SSD_GATE_COLS = LANES


def _ssd_kernel(*refs, reverse, combine):
    if combine:
        x_ref, bm_ref, cm_ref, grow_ref, gcol_ref, other_ref, z_ref, skip_ref, gain_ref, out_ref, s_scr = refs
    else:
        x_ref, bm_ref, cm_ref, grow_ref, gcol_ref, out_ref, s_scr = refs
    Q, P, N, H = SSD_CHUNK, SSD_HEAD_DIM, SSD_STATE, SSD_HEADS

    @pl.when(pl.program_id(1) == 0)
    def _():
        s_scr[...] = jnp.zeros_like(s_scr)

    row = lax.broadcasted_iota(jnp.int32, (Q, Q), 0)
    col = lax.broadcasted_iota(jnp.int32, (Q, Q), 1)
    not_after = (col >= row) if reverse else (col <= row)
    tri = not_after.astype(jnp.float32)
    last = 0 if reverse else Q - 1
    nt = (((1,), (1,)), ((), ()))
    grow = grow_ref[0]
    gcol = gcol_ref[0]
    a_cols = jnp.dot(tri, gcol, preferred_element_type=jnp.float32, precision=lax.Precision.HIGHEST)
    a_rows = lax.dot_general(grow, tri, nt, preferred_element_type=jnp.float32,
                             precision=lax.Precision.HIGHEST)
    ys = []
    for g in range(SSD_GROUPS):
        ns = slice(g * N, (g + 1) * N)
        bm = bm_ref[0, :, ns].astype(jnp.bfloat16)
        cm = cm_ref[0, :, ns].astype(jnp.bfloat16)
        cb = lax.dot_general(cm, bm, nt, preferred_element_type=jnp.float32)
        state = s_scr[g]
        y_off = jnp.dot(cm, state.astype(jnp.bfloat16), preferred_element_type=jnp.float32)
        xw, decay = [], []
        for r in range(SSD_GROUP_HEADS):
            h = g * SSD_GROUP_HEADS + r
            hs = slice(h * P, (h + 1) * P)
            aq = a_cols[:, H + h:H + h + 1]
            as_ = a_rows[H + h:H + h + 1, :]
            a_tot = as_[:, last:last + 1]
            xdt = x_ref[0, :, hs] * gcol[:, h:h + 1]
            lmat = jnp.exp(jnp.where(not_after, aq - as_, NEG_INF))
            y = jnp.dot((cb * lmat).astype(jnp.bfloat16), xdt.astype(jnp.bfloat16),
                        preferred_element_type=jnp.float32)
            ys.append(y + y_off[:, r * P:(r + 1) * P] * jnp.exp(aq))
            xw.append(jnp.exp(a_tot - aq) * xdt)
            decay.append(jnp.broadcast_to(jnp.exp(a_tot), (1, P)))
        s_loc = lax.dot_general(bm, jnp.concatenate(xw, axis=1).astype(jnp.bfloat16),
                                (((0,), (0,)), ((), ())), preferred_element_type=jnp.float32)
        s_scr[g] = state * jnp.concatenate(decay, axis=1) + s_loc
    y = jnp.concatenate(ys, axis=1)
    if combine:
        x = x_ref[0]
        z = z_ref[0]
        tot = (y + other_ref[0] + x * skip_ref[...]) * (z * jax.nn.sigmoid(z))
        out_ref[0] = tot * lax.rsqrt(jnp.mean(tot * tot, axis=1, keepdims=True) + RMS_EPS) * gain_ref[...]
    else:
        out_ref[0] = y


def _ssd_direction(xbc, grow, gcol, n_ctx_chunks, reverse, extra=None):
    Bsz, TT, _ = xbc.shape
    n_chunks = TT // SSD_CHUNK
    if reverse:
        chunk = lambda c: jnp.where(c < n_ctx_chunks, n_ctx_chunks - 1 - c, n_chunks - 1 + n_ctx_chunks - c)
    else:
        chunk = lambda c: c
    gn = SSD_GROUPS * SSD_STATE
    seq_spec = pl.BlockSpec((1, SSD_CHUNK, SSD_INNER), lambda b, c: (b, chunk(c), 0))
    vec_spec = pl.BlockSpec((1, SSD_INNER), lambda b, c: (0, 0))
    in_specs = [seq_spec,
                pl.BlockSpec((1, SSD_CHUNK, gn), lambda b, c: (b, chunk(c), SSD_INNER // gn)),
                pl.BlockSpec((1, SSD_CHUNK, gn), lambda b, c: (b, chunk(c), SSD_INNER // gn + 1)),
                pl.BlockSpec((1, 2 * SSD_HEADS, SSD_CHUNK), lambda b, c: (b, 0, chunk(c))),
                pl.BlockSpec((1, SSD_CHUNK, SSD_GATE_COLS), lambda b, c: (b, chunk(c), 0))]
    args = [xbc, xbc, xbc, grow, gcol]
    if extra is not None:
        other, z, skip, gain = extra
        z_spec = pl.BlockSpec((1, SSD_CHUNK, SSD_INNER), lambda b, c: (b, chunk(c), z[1]))
        in_specs += [seq_spec, z_spec, vec_spec, vec_spec]
        args += [other, z[0], skip, gain]
    return pl.pallas_call(
        functools.partial(_ssd_kernel, reverse=reverse, combine=extra is not None),
        grid=(Bsz, n_chunks),
        in_specs=in_specs,
        out_specs=seq_spec,
        out_shape=jax.ShapeDtypeStruct((Bsz, TT, SSD_INNER), jnp.float32),
        scratch_shapes=[pltpu.VMEM((SSD_GROUPS, SSD_STATE, SSD_GROUP_W), jnp.float32)],
        compiler_params=pltpu.CompilerParams(dimension_semantics=("arbitrary", "arbitrary"),
                                             vmem_limit_bytes=VMEM_LIMIT_BYTES),
        name="ssd_bwd" if reverse else "ssd_fwd",
    )(*args)


def _ssd_block(z, xbc_act, dt_raw, dt_bias, a_log, d_skip, norm_g, n_ctx):
    Bsz, TT, _ = dt_raw.shape
    dt = jax.nn.softplus(dt_raw.reshape(Bsz, TT, 2, SSD_HEADS) + dt_bias)
    a = dt * -jnp.exp(a_log)
    gate_cols = jnp.concatenate([dt, a], axis=-1)
    skip = jnp.repeat(d_skip, SSD_HEAD_DIM).reshape(1, SSD_INNER)
    outs = None
    for direction in (1, 0):
        gc = gate_cols[:, :, direction]
        grow = gc.transpose(0, 2, 1)
        gcol = jnp.pad(gc, ((0, 0), (0, 0), (0, SSD_GATE_COLS - 2 * SSD_HEADS)))
        extra = None if direction == 1 else (outs, z, skip, norm_g.reshape(1, SSD_INNER))
        outs = _ssd_direction(xbc_act, grow, gcol, n_ctx // SSD_CHUNK, direction == 1, extra)
    return outs


ML_STATE_COLS = 2 * ML_HEAD_DIM
ML_GATE_COLS = LANES


def _mlstm_kernel(*refs, reverse, combine):
    if combine:
        q_ref, k_ref, v_ref, grow_ref, gcol_ref, other_ref, o_ref, gain_ref, out_ref, s_scr, m_scr = refs
    else:
        q_ref, k_ref, v_ref, grow_ref, gcol_ref, out_ref, s_scr, m_scr = refs
    Q, D, H = ML_CHUNK, ML_HEAD_DIM, ML_HEADS

    @pl.when(pl.program_id(1) == 0)
    def _():
        s_scr[...] = jnp.zeros_like(s_scr)
        m_scr[...] = jnp.zeros_like(m_scr)

    row = lax.broadcasted_iota(jnp.int32, (Q, Q), 0)
    col = lax.broadcasted_iota(jnp.int32, (Q, Q), 1)
    not_after = (col >= row) if reverse else (col <= row)
    tri = not_after.astype(jnp.float32)
    last = 0 if reverse else Q - 1
    nt = (((1,), (1,)), ((), ()))
    one_col = (lax.broadcasted_iota(jnp.int32, (Q, D), 1) == 0).astype(jnp.float32)
    states = [s_scr[h] for h in range(H)]
    ms = [m_scr[h, 0:1, 0:1] for h in range(H)]
    chunks = range(SCAN_STEP_CHUNKS)
    for sub in (reversed(chunks) if reverse else chunks):
        rows = slice(sub * Q, (sub + 1) * Q)
        grow = grow_ref[0, :, rows]
        gcol = gcol_ref[0, rows, :]
        b_cols = jnp.dot(tri, gcol, preferred_element_type=jnp.float32, precision=lax.Precision.HIGHEST)
        b_rows = lax.dot_general(grow, tri, nt, preferred_element_type=jnp.float32,
                                 precision=lax.Precision.HIGHEST)
        for h in range(H):
            hs = slice(h * D, (h + 1) * D)
            q = q_ref[0, rows, hs].astype(jnp.bfloat16)
            k = (k_ref[0, rows, hs] * D ** -0.5).astype(jnp.bfloat16)
            v_ext = jnp.concatenate([v_ref[0, rows, hs], one_col], axis=1)
            bq = b_cols[:, H + h:H + h + 1]
            bs = b_rows[H + h:H + h + 1, :]
            i_q = gcol[:, h:h + 1]
            i_s = grow[h:h + 1, :]
            b_last = bs[:, last:last + 1]
            m0, state = ms[h], states[h]
            dm = jnp.where(not_after, bq - bs + i_s, NEG_INF)
            inter = bq + m0
            m_t = jnp.maximum(inter, jnp.max(dm, axis=1, keepdims=True))
            s_mat = lax.dot_general(q, k, nt, preferred_element_type=jnp.float32) * jnp.exp(dm - m_t)
            num = (jnp.dot(s_mat.astype(jnp.bfloat16), v_ext.astype(jnp.bfloat16),
                           preferred_element_type=jnp.float32)
                   + jnp.exp(inter - m_t) * jnp.dot(q, state.astype(jnp.bfloat16),
                                                    preferred_element_type=jnp.float32))
            den = num[:, D:D + 1]
            hout = num[:, :D] / jnp.maximum(jnp.abs(den), jnp.exp(-m_t))
            g_q = b_last - bq + i_q
            m_loc = jnp.max(g_q, axis=0, keepdims=True)
            wv = (jnp.exp(g_q - m_loc) * v_ext).astype(jnp.bfloat16)
            s_loc = lax.dot_general(k, wv, (((0,), (0,)), ((), ())), preferred_element_type=jnp.float32)
            m_new = jnp.maximum(b_last + m0, m_loc)
            states[h] = jnp.exp(b_last + m0 - m_new) * state + jnp.exp(m_loc - m_new) * s_loc
            ms[h] = m_new
            if combine:
                tot = hout + other_ref[0, rows, hs]
                y = tot * lax.rsqrt(jnp.mean(tot * tot, axis=1, keepdims=True) + RMS_EPS) * gain_ref[:, hs]
                out_ref[0, rows, hs] = y * jax.nn.sigmoid(o_ref[0, rows, hs])
            else:
                out_ref[0, rows, hs] = hout
    for h in range(H):
        s_scr[h] = states[h]
        m_scr[h] = jnp.broadcast_to(ms[h], (SUBLANES, LANES))


def _mlstm_direction(q, k, v, grow, gcol, n_ctx_chunks, reverse, extra=None):
    Bsz, TT, _ = q[0].shape
    step = SCAN_STEP_CHUNKS * ML_CHUNK
    n_chunks = TT // step
    if reverse:
        chunk = lambda c: jnp.where(c < n_ctx_chunks, n_ctx_chunks - 1 - c, n_chunks - 1 + n_ctx_chunks - c)
    else:
        chunk = lambda c: c
    seq_spec = pl.BlockSpec((1, step, ML_INNER), lambda b, c: (b, chunk(c), 0))
    col_spec = lambda j: pl.BlockSpec((1, step, ML_INNER), lambda b, c: (b, chunk(c), j))
    in_specs = [col_spec(q[1]), col_spec(k[1]), col_spec(v[1]),
                pl.BlockSpec((1, 2 * ML_HEADS, step), lambda b, c: (b, 0, chunk(c))),
                pl.BlockSpec((1, step, ML_GATE_COLS), lambda b, c: (b, chunk(c), 0))]
    args = [q[0], k[0], v[0], grow, gcol]
    if extra is not None:
        other, o_pre, gain = extra
        in_specs += [seq_spec, col_spec(o_pre[1]), pl.BlockSpec((1, ML_INNER), lambda b, c: (0, 0))]
        args += [other, o_pre[0], gain]
    return pl.pallas_call(
        functools.partial(_mlstm_kernel, reverse=reverse, combine=extra is not None),
        grid=(Bsz, n_chunks),
        in_specs=in_specs,
        out_specs=seq_spec,
        out_shape=jax.ShapeDtypeStruct((Bsz, TT, ML_INNER), jnp.float32),
        scratch_shapes=[pltpu.VMEM((ML_HEADS, ML_HEAD_DIM, ML_STATE_COLS), jnp.float32),
                        pltpu.VMEM((ML_HEADS, SUBLANES, LANES), jnp.float32)],
        compiler_params=pltpu.CompilerParams(dimension_semantics=("arbitrary", "arbitrary"),
                                             vmem_limit_bytes=VMEM_LIMIT_BYTES),
        name="mlstm_bwd" if reverse else "mlstm_fwd",
    )(*args)


def _mlstm_block(q, k, v, o_pre, if_pre, gate_bias, norm_g, n_ctx):
    Bsz, TT, _ = if_pre.shape
    gates = (if_pre.transpose(0, 2, 1) + gate_bias.reshape(1, 4 * ML_HEADS, 1)).reshape(Bsz, 2, 2, ML_HEADS, TT)
    gate_rows = jnp.concatenate([gates[:, :, 0], jax.nn.log_sigmoid(gates[:, :, 1])], axis=2)
    outs = None
    for direction in (1, 0):
        grow = gate_rows[:, direction]
        gcol = jnp.pad(grow.transpose(0, 2, 1), ((0, 0), (0, 0), (0, ML_GATE_COLS - 2 * ML_HEADS)))
        extra = None if direction == 1 else (outs, o_pre, norm_g.reshape(1, ML_INNER))
        outs = _mlstm_direction(q, k, v, grow, gcol, n_ctx // (SCAN_STEP_CHUNKS * ML_CHUNK), direction == 1, extra)
    return outs


_PROJ_ORDER = ('aq', 'sz', 'mq', 'mk', 'mv', 'mo', 'gpre', 'sxbc', 'ak', 'av', 'sdt', 'mif')
_IN_NAMES = ('aq', 'ak', 'av', 'sz', 'sxbc', 'sdt', 'mq', 'mk', 'mv', 'mo', 'mif', 'gpre')
_IN_START = dict(zip(_IN_NAMES, np.cumsum((0,) + IN_SPLITS[:-1]).tolist()))
_IN_WIDTH = dict(zip(_IN_NAMES, IN_SPLITS))
_PROJ_PERM = np.concatenate([np.arange(_IN_START[n], _IN_START[n] + _IN_WIDTH[n]) for n in _PROJ_ORDER])
_PROJ_START = dict(zip(_PROJ_ORDER, np.cumsum([0] + [_IN_WIDTH[n] for n in _PROJ_ORDER[:-1]]).tolist()))
PROJ_COL_BLOCKS = 3
PROJ_COLS = -(-sum(IN_SPLITS) // (PROJ_COL_BLOCKS * LANES)) * (PROJ_COL_BLOCKS * LANES)
ROW_BLOCK = 256


def _proj_cols(proj, name):
    return proj[..., _PROJ_START[name]:_PROJ_START[name] + _IN_WIDTH[name]]


def _proj_block(name, width):
    assert _PROJ_START[name] % width == 0
    return _PROJ_START[name] // width


def _modulated_norm(x, gain, scale, shift):
    y = x * lax.rsqrt(jnp.mean(x * x, axis=-1, keepdims=True) + RMS_EPS)
    return (y * gain) * (1.0 + scale) + shift


def _inproj_kernel(x_ref, gain_ref, mod_ref, w_ref, o_ref):
    h = _modulated_norm(x_ref[0], gain_ref[...], mod_ref[0, 0, 0:1, :], mod_ref[0, 0, 1:2, :])
    o_ref[0] = jnp.dot(h.astype(jnp.bfloat16), w_ref[...], preferred_element_type=jnp.float32)


def _input_projection(x_all, gain, mod, w_perm, n_ctx):
    Bsz, TT, _ = x_all.shape
    cols = PROJ_COLS // PROJ_COL_BLOCKS
    ctx_blocks = n_ctx // ROW_BLOCK
    return pl.pallas_call(
        _inproj_kernel,
        grid=(PROJ_COL_BLOCKS, Bsz, TT // ROW_BLOCK),
        in_specs=[
            pl.BlockSpec((1, ROW_BLOCK, D_MODEL), lambda n, b, r: (b, r, 0)),
            pl.BlockSpec((1, D_MODEL), lambda n, b, r: (0, 0)),
            pl.BlockSpec((1, 1, 2, D_MODEL), lambda n, b, r: (b, jnp.where(r < ctx_blocks, 0, 1), 0, 0)),
            pl.BlockSpec((D_MODEL, cols), lambda n, b, r: (0, n)),
        ],
        out_specs=pl.BlockSpec((1, ROW_BLOCK, cols), lambda n, b, r: (b, r, n)),
        out_shape=jax.ShapeDtypeStruct((Bsz, TT, PROJ_COLS), jnp.float32),
        compiler_params=pltpu.CompilerParams(dimension_semantics=("arbitrary", "arbitrary", "arbitrary"),
                                             vmem_limit_bytes=VMEM_LIMIT_BYTES),
        name="input_projection",
    )(x_all, gain.reshape(1, D_MODEL), mod, w_perm)


def _merge_kernel(att_ref, ssd_ref, ml_ref, g0_ref, g1_ref, g2_ref, wb_ref, wo_ref, x_ref, gain_ref, mod_ref,
                  xo_ref, h2_ref):
    mixed = None
    for r, (branch, gate) in enumerate(((att_ref, g0_ref), (ssd_ref, g1_ref), (ml_ref, g2_ref))):
        p = jnp.dot(branch[0].astype(jnp.bfloat16), wb_ref[r], preferred_element_type=jnp.float32)
        term = jax.nn.sigmoid(gate[0]) * p
        mixed = term if mixed is None else mixed + term
    y = jnp.dot(mixed.astype(jnp.bfloat16), wo_ref[...], preferred_element_type=jnp.float32)
    x_new = x_ref[0] + mod_ref[0, 0, 0:1, :] * y
    xo_ref[0] = x_new
    h2 = _modulated_norm(x_new, gain_ref[...], mod_ref[0, 0, 1:2, :], mod_ref[0, 0, 2:3, :])
    h2_ref[0] = h2.astype(jnp.bfloat16)


def _merge(att, ssd, ml, proj, w_branch, w_out, x_all, gain, mod, n_ctx):
    Bsz, TT, _ = x_all.shape
    ctx_blocks = n_ctx // ROW_BLOCK
    row = lambda w, j=0: pl.BlockSpec((1, ROW_BLOCK, w), lambda b, r: (b, r, j))
    g_first = _proj_block('gpre', D_MODEL)
    return pl.pallas_call(
        _merge_kernel,
        grid=(Bsz, TT // ROW_BLOCK),
        in_specs=[
            row(BRANCH_W), row(BRANCH_W), row(BRANCH_W),
            row(D_MODEL, g_first), row(D_MODEL, g_first + 1), row(D_MODEL, g_first + 2),
            pl.BlockSpec((N_BRANCH, BRANCH_W, D_MODEL), lambda b, r: (0, 0, 0)),
            pl.BlockSpec((D_MODEL, D_MODEL), lambda b, r: (0, 0)),
            row(D_MODEL),
            pl.BlockSpec((1, D_MODEL), lambda b, r: (0, 0)),
            pl.BlockSpec((1, 1, 3, D_MODEL), lambda b, r: (b, jnp.where(r < ctx_blocks, 0, 1), 0, 0)),
        ],
        out_specs=[row(D_MODEL), row(D_MODEL)],
        out_shape=[jax.ShapeDtypeStruct((Bsz, TT, D_MODEL), jnp.float32),
                   jax.ShapeDtypeStruct((Bsz, TT, D_MODEL), jnp.bfloat16)],
        compiler_params=pltpu.CompilerParams(dimension_semantics=("arbitrary", "arbitrary"),
                                             vmem_limit_bytes=VMEM_LIMIT_BYTES),
        name="merge",
    )(att, ssd, ml, proj, proj, proj, w_branch.astype(jnp.bfloat16), w_out.astype(jnp.bfloat16),
      x_all, gain.reshape(1, D_MODEL), mod)


def _rmsnorm(x, g):
    xf = x.astype(jnp.float32)
    y = xf * lax.rsqrt(jnp.mean(xf * xf, axis=-1, keepdims=True) + RMS_EPS)
    return (y * g.astype(jnp.float32)).astype(x.dtype)


def _heads(t, n):
    return t.reshape(t.shape[0], t.shape[1], n, -1)


def _axial_rope(rows):
    n_freq = ATT_HEAD_DIM // 4
    inv = ROPE_THETA ** (-jnp.arange(n_freq, dtype=jnp.float32) / n_freq)
    row = jnp.broadcast_to(jnp.arange(rows, dtype=jnp.float32)[:, None], (rows, GRID_W)).reshape(-1)
    col = jnp.broadcast_to(jnp.arange(GRID_W, dtype=jnp.float32)[None, :], (rows, GRID_W)).reshape(-1)
    ang = jnp.concatenate([row[:, None] * inv, col[:, None] * inv], axis=-1)
    return jnp.cos(ang), jnp.sin(ang)


def _apply_rope(t, cos, sin):
    half = t.shape[-1] // 2
    tf = t.astype(jnp.float32)
    t1, t2 = tf[..., :half], tf[..., half:]
    c, s = cos[:, None, :], sin[:, None, :]
    return jnp.concatenate([t1 * c - t2 * s, t1 * s + t2 * c], axis=-1).astype(t.dtype)


ATT_KV_CHUNK = 768


def _latent_attention(q, k, v, kc, vc):
    kk = jnp.concatenate([k, kc], axis=1)
    vv = jnp.concatenate([v, vc], axis=1)
    assert kk.shape[1] % ATT_KV_CHUNK == 0
    return _attention(q, kk, vv, ATT_KV_CHUNK)


def _context_attention(qc, kc, vc):
    return _attention(qc, kc, vc, kc.shape[1])


def _dwconv(x, w, b):
    out = lax.conv_general_dilated(x, w[:, None, :], window_strides=(1,), padding='SAME',
                                   dimension_numbers=('NWC', 'WIO', 'NWC'),
                                   feature_group_count=x.shape[-1])
    return out + b


def _token_mixer(proj, n_ctx, qn, kn, conv_w, conv_b, dt_bias, a_log, d_skip, ssd_norm_g,
                 ml_bias, ml_norm_g, cos, sin, ctx_out):
    Bsz = proj.shape[0]
    cat = lambda a, b: jnp.concatenate([a, b], axis=1)
    aq, ak = _proj_cols(proj, 'aq'), _proj_cols(proj, 'ak')
    av = lax.optimization_barrier(_proj_cols(proj, 'av'))
    q = _apply_rope(_rmsnorm(_heads(aq[:, n_ctx:], ATT_HEADS), qn), cos, sin)
    k = _apply_rope(_rmsnorm(_heads(ak[:, n_ctx:], ATT_KV_HEADS), kn), cos, sin)
    kc = _rmsnorm(_heads(ak[:, :n_ctx], ATT_KV_HEADS), kn)
    vc = _heads(av[:, :n_ctx], ATT_KV_HEADS)
    att = _latent_attention(q, k, _heads(av[:, n_ctx:], ATT_KV_HEADS), kc, vc)
    if ctx_out:
        att_c = _context_attention(_rmsnorm(_heads(aq[:, :n_ctx], ATT_HEADS), qn), kc, vc)
    else:
        att_c = jnp.zeros((Bsz, n_ctx, att.shape[-1]), att.dtype)
    sxbc = _proj_cols(proj, 'sxbc')
    xbc_act = cat(jax.nn.silu(_dwconv(sxbc[:, :n_ctx], conv_w, conv_b)),
                  jax.nn.silu(_dwconv(sxbc[:, n_ctx:], conv_w, conv_b)))
    ssd = _ssd_block((proj, _proj_block('sz', SSD_INNER)), xbc_act, _proj_cols(proj, 'sdt'),
                     dt_bias, a_log, d_skip, ssd_norm_g, n_ctx)
    blk = lambda name: (proj, _proj_block(name, ML_INNER))
    ml = _mlstm_block(blk('mq'), blk('mk'), blk('mv'), blk('mo'), _proj_cols(proj, 'mif'),
                      ml_bias, ml_norm_g, n_ctx)
    return cat(att_c, att), ssd, ml


def kernel(x, c, ctx, c_ctx, norm1_g, norm2_g, w_mod, b_mod, w_in, att_qnorm, att_knorm, ssd_conv_w, ssd_conv_b, ssd_dt_bias, ssd_a_log, ssd_d, ssd_norm, ml_gate_bias, ml_norm, w_branch, w_out, peer_wq, peer_subkeys, peer_u, peer_v):
    Bsz, S, _ = x.shape
    depth = w_in.shape[0]
    cos, sin = _axial_rope(S // GRID_W)
    silu_c = jax.nn.silu(c)
    silu_cc = jax.nn.silu(c_ctx)
    xc = ctx
    n_ctx = ctx.shape[1]
    for l in range(depth):
        last = l == depth - 1
        m = (silu_c @ w_mod[l] + b_mod[l]).reshape(Bsz, 6, D_MODEL)
        mc = jnp.broadcast_to((silu_cc @ w_mod[l] + b_mod[l]).reshape(1, 6, D_MODEL), (Bsz, 6, D_MODEL))
        both = jnp.stack([mc, m], axis=1)
        x_all = jnp.concatenate([xc, x], axis=1)
        w_perm = jnp.concatenate(
            [w_in[l][:, _IN_START[n]:_IN_START[n] + _IN_WIDTH[n]] for n in _PROJ_ORDER]
            + [jnp.zeros((D_MODEL, PROJ_COLS - sum(IN_SPLITS)), w_in.dtype)], axis=1).astype(jnp.bfloat16)
        proj = _input_projection(x_all, norm1_g[l], both[:, :, np.array([1, 0])], w_perm, n_ctx)
        att, ssd, ml = _token_mixer(proj, n_ctx, att_qnorm[l], att_knorm[l], ssd_conv_w[l], ssd_conv_b[l],
                                    ssd_dt_bias[l], ssd_a_log[l], ssd_d[l], ssd_norm[l], ml_gate_bias[l],
                                    ml_norm[l], cos, sin, not last)
        x_all, h2 = _merge(att, ssd, ml, proj, w_branch[l], w_out[l], x_all, norm2_g[l],
                           both[:, :, np.array([2, 4, 3])], n_ctx)
        u_bf16 = peer_u[l].astype(jnp.bfloat16)
        vt_bf16 = peer_v[l].astype(jnp.bfloat16).reshape(PEER_EXPERTS // PEER_EBLK, PEER_EBLK, D_MODEL)
        vt_bf16 = vt_bf16.transpose(0, 2, 1)
        po = _peer(h2[:, n_ctx:].reshape(-1, D_MODEL), peer_wq[l], peer_subkeys[l], u_bf16, vt_bf16)
        x = x_all[:, n_ctx:] + m[:, None, 5] * po.reshape(x.shape)
        if not last:
            pc = _peer(h2[:, :n_ctx].reshape(-1, D_MODEL), peer_wq[l], peer_subkeys[l], u_bf16, vt_bf16)
            xc = x_all[:, :n_ctx] + mc[:, None, 5] * pc.reshape(xc.shape)
    return x
```

```python
import functools
import math

import jax
import jax.numpy as jnp
import numpy as np
from jax import lax
from jax.experimental import pallas as pl
from jax.experimental.pallas import tpu as pltpu

D_MODEL = 1024
GRID_W = 64
RMS_EPS = 1e-6
N_BRANCH = 3
BRANCH_W = 512
ATT_HEADS = 8
ATT_KV_HEADS = 2
ATT_HEAD_DIM = 64
ATT_GROUP = ATT_HEADS // ATT_KV_HEADS
Q_BLOCK = 128
ROPE_THETA = 10000.0
SSD_HEADS = 8
SSD_HEAD_DIM = 64
SSD_INNER = SSD_HEADS * SSD_HEAD_DIM
SSD_GROUPS = 2
SSD_STATE = 64
SSD_XBC = SSD_INNER + 2 * SSD_GROUPS * SSD_STATE
SSD_CONV = 5
SSD_CHUNK = 128
ML_HEADS = 4
ML_HEAD_DIM = 128
ML_INNER = ML_HEADS * ML_HEAD_DIM
ML_CHUNK = 128
PEER_HEADS = 8
PEER_KEYS = 128
PEER_EXPERTS = PEER_KEYS * PEER_KEYS
PEER_TOPK = 16
PEER_QDIM = 256
PEER_HALF = PEER_QDIM // 2
IN_SPLITS = (ATT_HEADS * ATT_HEAD_DIM, ATT_KV_HEADS * ATT_HEAD_DIM, ATT_KV_HEADS * ATT_HEAD_DIM,
             SSD_INNER, SSD_XBC, 2 * SSD_HEADS,
             ML_INNER, ML_INNER, ML_INNER, ML_INNER, 4 * ML_HEADS,
             N_BRANCH * D_MODEL)

LANES = 128
SUBLANES = 8
VMEM_LIMIT_BYTES = 56 * 1024 * 1024

ROUTER_TOKENS = 256
ROUTER_CHUNK = LANES
PEER_TOKENS = 1024
PEER_TOKENS_SMALL = 512
PEER_ROWS = 8
PEER_EBLK = PEER_ROWS * PEER_KEYS
PEER_SLICE = 2 * PEER_KEYS
PEER_CHUNK = 256

NEG_INF = float("-inf")


def _sort_network(n):
    def merge(lo, hi, r):
        step = r * 2
        if step < hi - lo:
            yield from merge(lo, hi, step)
            yield from merge(lo + r, hi, step)
            yield from [(i, i + r) for i in range(lo + r, hi - r, step)]
        else:
            yield (lo, lo + r)

    def sort(lo, hi):
        if hi - lo >= 1:
            mid = lo + (hi - lo) // 2
            yield from sort(lo, mid)
            yield from sort(mid + 1, hi)
            yield from merge(lo, hi, 1)

    return tuple(sort(0, n - 1))


_SORT16 = _sort_network(PEER_TOPK)


def _merge_top(lists, n):
    lists = list(lists)
    depth = len(lists)
    sub = lax.broadcasted_iota(jnp.int32, lists[0].shape, 0)
    tops = []
    for i in range(n):
        head = lists[0]
        m = jnp.max(head, axis=0, keepdims=True)
        tops.append(m)
        live = min(depth, n - 1 - i)
        if live == 0:
            break
        first = jnp.min(jnp.where(head == m, sub, SUBLANES), axis=0, keepdims=True)
        pop = sub == first
        for r in range(live):
            nxt = lists[r + 1] if r + 1 < depth else NEG_INF
            lists[r] = jnp.where(pop, nxt, lists[r])
    return tops


def _sorted_top(s, n):
    xs = [s[r * SUBLANES:(r + 1) * SUBLANES, :] for r in range(s.shape[0] // SUBLANES)]
    for i, j in _SORT16:
        xs[i], xs[j] = jnp.maximum(xs[i], xs[j]), jnp.minimum(xs[i], xs[j])
    return _merge_top(xs, n)


def _prefix_count(values, pred):
    n = len(values)
    steps = [n >> (i + 1) for i in range(n.bit_length() - 1)]
    taken = []

    def pivot(level, base, step):
        if level == len(taken):
            return values[base + step - 1]
        return jnp.where(taken[level], pivot(level + 1, base + steps[level], step), pivot(level + 1, base, step))

    count = None
    for step in steps:
        ok = pred(pivot(0, 0, step))
        taken.append(ok)
        term = jnp.where(ok, float(step), 0.0)
        count = term if count is None else count + term
    return jnp.where(pred(values[n - 1]), float(n), count)


def _rows_to_sublanes(rows, first_sublane, shape):
    sub = lax.broadcasted_iota(jnp.int32, shape, 0)
    out = jnp.zeros(shape, jnp.float32)
    for k, row in enumerate(rows):
        out = jnp.where(sub == first_sublane + k, row, out)
    return out


_CAND_LEN = (16, 8, 5, 4, 12, 4, 1, 0)


def _router_kernel(h_ref, wq_ref, sk_ref, rank_ref, e2_ref, cnt_ref, e1_ref, q_scr):
    q_scr[...] = jnp.dot(h_ref[...], wq_ref[...], preferred_element_type=jnp.float32).astype(jnp.bfloat16)
    nt = (((1,), (1,)), ((), ()))
    shape8 = (SUBLANES, ROUTER_CHUNK)
    sub = lax.broadcasted_iota(jnp.int32, shape8, 0)
    cand_len = jnp.zeros(shape8, jnp.int32)
    for g, n in enumerate(_CAND_LEN):
        cand_len = jnp.where(sub == g, n, cand_len)
    for h in range(PEER_HEADS):
        q1 = q_scr[:, (2 * h) * PEER_HALF:(2 * h + 1) * PEER_HALF]
        q2 = q_scr[:, (2 * h + 1) * PEER_HALF:(2 * h + 2) * PEER_HALF]
        s1 = lax.dot_general(sk_ref[2 * h], q1, nt, preferred_element_type=jnp.float32)
        s2 = lax.dot_general(sk_ref[2 * h + 1], q2, nt, preferred_element_type=jnp.float32)
        for c in range(ROUTER_TOKENS // ROUTER_CHUNK):
            cs = slice(c * ROUTER_CHUNK, (c + 1) * ROUTER_CHUNK)
            s1c = s1[:, cs]
            s2c = s2[:, cs]
            top1 = _sorted_top(s1c, PEER_TOPK)
            top2 = _sorted_top(s2c, PEER_TOPK)
            a_lo = _rows_to_sublanes(top1[:4], 0, shape8)
            b_lo = _rows_to_sublanes(top2[:3], 4, shape8)
            cands = []
            for r in range(PEER_TOPK):
                by_a = a_lo + top2[r]
                by_b = (top1[4 + r] + b_lo) if 4 + r < PEER_TOPK else by_a
                cands.append(jnp.where(r < cand_len, jnp.where(sub < 4, by_a, by_b), NEG_INF))
            tau = _merge_top(cands, PEER_TOPK)[-1]
            m1, m2 = top1[0], top2[0]
            z = jnp.zeros(shape8, jnp.float32)
            for cand in cands:
                z = z + jnp.where(cand >= tau, jnp.exp(cand - (m1 + m2)), 0.0)
            z = jnp.sum(z, axis=0, keepdims=True)
            rank = _prefix_count(top2, lambda t: t > s2c)
            cnt = _prefix_count(top2, lambda t: s1c + t >= tau)
            rank_ref[h, :, cs] = rank.astype(jnp.bfloat16)
            e2_ref[h, :, cs] = jnp.exp(s2c - m2).astype(jnp.bfloat16)
            cnt_ref[h, :, cs] = cnt
            e1_ref[h, :, cs] = jnp.exp(s1c - m1) * (0.5 / z)


def _peer_router(h2_bf16, wq, subkeys):
    n_tok = h2_bf16.shape[0]
    sk = subkeys.reshape(PEER_HEADS * 2, PEER_KEYS, PEER_HALF)
    shape = (PEER_HEADS, PEER_KEYS, n_tok)
    spec = pl.BlockSpec((PEER_HEADS, PEER_KEYS, ROUTER_TOKENS), lambda i: (0, 0, i))
    return pl.pallas_call(
        _router_kernel,
        grid=(n_tok // ROUTER_TOKENS,),
        in_specs=[
            pl.BlockSpec((ROUTER_TOKENS, D_MODEL), lambda i: (i, 0)),
            pl.BlockSpec((D_MODEL, PEER_HEADS * PEER_QDIM), lambda i: (0, 0)),
            pl.BlockSpec((PEER_HEADS * 2, PEER_KEYS, PEER_HALF), lambda i: (0, 0, 0)),
        ],
        out_specs=[spec, spec, spec, spec],
        out_shape=[jax.ShapeDtypeStruct(shape, jnp.bfloat16), jax.ShapeDtypeStruct(shape, jnp.bfloat16),
                   jax.ShapeDtypeStruct(shape, jnp.float32), jax.ShapeDtypeStruct(shape, jnp.float32)],
        scratch_shapes=[pltpu.VMEM((ROUTER_TOKENS, PEER_HEADS * PEER_QDIM), jnp.bfloat16)],
        compiler_params=pltpu.CompilerParams(dimension_semantics=("arbitrary",),
                                             vmem_limit_bytes=VMEM_LIMIT_BYTES),
        name="peer_router",
    )(h2_bf16, wq.astype(jnp.bfloat16), sk.astype(jnp.bfloat16))


GELU_C0 = math.sqrt(2.0 / math.pi)
GELU_C1 = 0.044715 * GELU_C0


def _peer_dense_kernel(ht_ref, u_ref, vt_ref, rank_ref, e2_ref, cnt_ref, e1_ref, o_ref,
                       acc_scr, pre_scr, p_scr, rank_scr, e2_scr):
    j = pl.program_id(1)
    n_tok = ht_ref.shape[1]
    n_slices = PEER_EBLK // PEER_SLICE
    n_chunks = n_tok // PEER_CHUNK
    nt = (((1,), (1,)), ((), ()))

    @pl.when(j == 0)
    def _():
        acc_scr[...] = jnp.zeros_like(acc_scr)

        def relayout(t, carry):
            lanes = pl.ds(pl.multiple_of(t * LANES, LANES), LANES)
            for h in range(PEER_HEADS):
                rank = rank_ref[h, :, lanes].astype(jnp.float32)
                e2 = e2_ref[h, :, lanes].astype(jnp.float32)
                for b in range(PEER_KEYS // SUBLANES):
                    rank_scr[t, b, h] = rank[b * SUBLANES:(b + 1) * SUBLANES]
                    e2_scr[t, b, h] = e2[b * SUBLANES:(b + 1) * SUBLANES]
            return carry

        lax.fori_loop(0, n_tok // LANES, relayout, 0)

    def pre_matmul(c, ap, slot):
        tok = pl.multiple_of(c * PEER_CHUNK, PEER_CHUNK)
        rows = slice(ap * PEER_SLICE, (ap + 1) * PEER_SLICE)
        pre_scr[slot] = jnp.dot(u_ref[rows, :], ht_ref[:, pl.ds(tok, PEER_CHUNK)],
                                preferred_element_type=jnp.float32)

    def out_matmul(c, parity):
        tok = pl.multiple_of(c * PEER_CHUNK, PEER_CHUNK)
        acc_scr[:, pl.ds(tok, PEER_CHUNK)] += jnp.dot(vt_ref[0], p_scr[parity],
                                                      preferred_element_type=jnp.float32)

    def gates(c, ap, slot):
        first_a = ap * (PEER_SLICE // PEER_KEYS)
        for t in range(PEER_CHUNK // LANES):
            lanes = pl.ds(pl.multiple_of(c * PEER_CHUNK + t * LANES, LANES), LANES)
            ts = slice(t * LANES, (t + 1) * LANES)
            tile = c * (PEER_CHUNK // LANES) + t
            for b in range(PEER_KEYS // SUBLANES):
                rank = [rank_scr[tile, b, h] for h in range(PEER_HEADS)]
                e2 = [e2_scr[tile, b, h] for h in range(PEER_HEADS)]
                for a in range(PEER_SLICE // PEER_KEYS):
                    rs = slice(a * PEER_KEYS + b * SUBLANES, a * PEER_KEYS + (b + 1) * SUBLANES)
                    terms = []
                    for h in range(PEER_HEADS):
                        cnt = cnt_ref[first_a + a, h:h + 1, lanes]
                        e1 = e1_ref[first_a + a, h:h + 1, lanes]
                        terms.append(jnp.where(rank[h] < cnt, e2[h], 0.0) * e1)
                    while len(terms) > 1:
                        terms = [x + y for x, y in zip(terms[::2], terms[1::2])]
                    x = pre_scr[slot, rs, ts]
                    act = x * (1.0 + jnp.tanh(x * (GELU_C0 + GELU_C1 * (x * x))))
                    pre_scr[slot, rs, ts] = terms[0] * act
        p_scr[c % 2, ap * PEER_SLICE:(ap + 1) * PEER_SLICE, :] = pre_scr[slot].astype(jnp.bfloat16)

    p_scr[1] = jnp.zeros(p_scr.shape[1:], p_scr.dtype)
    pre_matmul(0, 0, 0)

    def body(c, carry):
        out_matmul(jnp.maximum(c - 1, 0), 1 - c % 2)
        for ap in range(n_slices):
            slot = ap % 2
            if ap + 1 < n_slices:
                pre_matmul(c, ap + 1, 1 - slot)
            else:
                pre_matmul(jnp.minimum(c + 1, n_chunks - 1), 0, 1 - slot)
            gates(c, ap, slot)
        return carry

    lax.fori_loop(0, n_chunks, body, 0)
    out_matmul(n_chunks - 1, (n_chunks - 1) % 2)

    @pl.when(j == pl.num_programs(1) - 1)
    def _():
        o_ref[...] = acc_scr[...].T


def _peer_dense(h2t_bf16, u_bf16, vt_bf16, rank, e2, cnt, e1, tb):
    n_tok = h2t_bf16.shape[1]
    row_spec = pl.BlockSpec((PEER_ROWS, PEER_HEADS, tb), lambda i, j: (j, 0, i))
    tiled = (tb // LANES, PEER_KEYS // SUBLANES, PEER_HEADS, SUBLANES, LANES)
    key_spec = pl.BlockSpec((PEER_HEADS, PEER_KEYS, tb), lambda i, j: (0, 0, i))
    return pl.pallas_call(
        _peer_dense_kernel,
        grid=(n_tok // tb, PEER_EXPERTS // PEER_EBLK),
        in_specs=[
            pl.BlockSpec((D_MODEL, tb), lambda i, j: (0, i)),
            pl.BlockSpec((PEER_EBLK, D_MODEL), lambda i, j: (j, 0)),
            pl.BlockSpec((1, D_MODEL, PEER_EBLK), lambda i, j: (j, 0, 0)),
            key_spec, key_spec, row_spec, row_spec,
        ],
        out_specs=pl.BlockSpec((tb, D_MODEL), lambda i, j: (i, 0)),
        out_shape=jax.ShapeDtypeStruct((n_tok, D_MODEL), jnp.float32),
        scratch_shapes=[pltpu.VMEM((D_MODEL, tb), jnp.float32),
                        pltpu.VMEM((2, PEER_SLICE, PEER_CHUNK), jnp.float32),
                        pltpu.VMEM((2, PEER_EBLK, PEER_CHUNK), jnp.bfloat16),
                        pltpu.VMEM(tiled, jnp.float32),
                        pltpu.VMEM(tiled, jnp.float32)],
        compiler_params=pltpu.CompilerParams(dimension_semantics=("arbitrary", "arbitrary"),
                                             vmem_limit_bytes=VMEM_LIMIT_BYTES),
        name="peer_dense",
    )(h2t_bf16, u_bf16, vt_bf16, rank, e2, cnt, e1)


def _peer(h2, wq, subkeys, u_bf16, vt_bf16):
    n_tok = h2.shape[0]
    tb = PEER_TOKENS if n_tok % PEER_TOKENS == 0 else PEER_TOKENS_SMALL
    h2_bf16 = h2.astype(jnp.bfloat16)
    rank, e2, cnt, e1 = _peer_router(h2_bf16, wq, subkeys)
    cnt, e1 = cnt.transpose(1, 0, 2), e1.transpose(1, 0, 2)
    return _peer_dense(h2_bf16.T, u_bf16, vt_bf16, rank, e2, cnt, e1, tb)


ATT_Q_TOKENS = 256
ATT_V_COLS = 2 * ATT_HEAD_DIM


def _attention_kernel(q_ref, kt_ref, v_ref, o_ref, *, kv_chunk):
    tq = q_ref.shape[2]
    rows = ATT_GROUP * tq
    q = q_ref[0].reshape(rows, ATT_HEAD_DIM)
    n_chunks = kt_ref.shape[3] // kv_chunk

    def body(c, carry):
        m, acc = carry
        off = pl.multiple_of(c * kv_chunk, kv_chunk)
        s = jnp.dot(q, kt_ref[0, 0, :, pl.ds(off, kv_chunk)], preferred_element_type=jnp.float32)
        m_new = jnp.maximum(m, jnp.max(s, axis=-1, keepdims=True))
        p = jnp.exp(s - m_new).astype(jnp.bfloat16)
        acc = jnp.exp(m - m_new) * acc + jnp.dot(p, v_ref[0, 0, pl.ds(off, kv_chunk), :],
                                                 preferred_element_type=jnp.float32)
        return m_new, acc

    m0 = jnp.full((rows, 1), NEG_INF, jnp.float32)
    acc0 = jnp.zeros((rows, ATT_V_COLS), jnp.float32)
    _, acc = lax.fori_loop(0, n_chunks, body, (m0, acc0), unroll=True)
    out = acc[:, :ATT_HEAD_DIM] / acc[:, ATT_HEAD_DIM:ATT_HEAD_DIM + 1]
    o_ref[0] = out.reshape(ATT_GROUP, tq, ATT_HEAD_DIM)


def _attention(q, k, v, kv_chunk):
    Bsz, S = q.shape[0], q.shape[1]
    SK = k.shape[1]
    tq = min(ATT_Q_TOKENS, S)
    qh = (q * ATT_HEAD_DIM ** -0.5).astype(jnp.bfloat16).transpose(0, 2, 1, 3)
    kt = k.astype(jnp.bfloat16).transpose(0, 2, 3, 1)
    pad = jnp.concatenate([jnp.ones(v.shape[:-1] + (1,), v.dtype),
                           jnp.zeros(v.shape[:-1] + (ATT_V_COLS - ATT_HEAD_DIM - 1,), v.dtype)], axis=-1)
    vx = jnp.concatenate([v, pad], axis=-1).astype(jnp.bfloat16).transpose(0, 2, 1, 3)
    o = pl.pallas_call(
        functools.partial(_attention_kernel, kv_chunk=kv_chunk),
        grid=(Bsz, ATT_KV_HEADS, S // tq),
        in_specs=[
            pl.BlockSpec((1, ATT_GROUP, tq, ATT_HEAD_DIM), lambda b, g, i: (b, g, i, 0)),
            pl.BlockSpec((1, 1, ATT_HEAD_DIM, SK), lambda b, g, i: (b, g, 0, 0)),
            pl.BlockSpec((1, 1, SK, ATT_V_COLS), lambda b, g, i: (b, g, 0, 0)),
        ],
        out_specs=pl.BlockSpec((1, ATT_GROUP, tq, ATT_HEAD_DIM), lambda b, g, i: (b, g, i, 0)),
        out_shape=jax.ShapeDtypeStruct((Bsz, ATT_HEADS, S, ATT_HEAD_DIM), jnp.float32),
        compiler_params=pltpu.CompilerParams(dimension_semantics=("arbitrary", "arbitrary", "arbitrary"),
                                             vmem_limit_bytes=VMEM_LIMIT_BYTES),
        name="attention",
    )(qh, kt, vx)
    return o.transpose(0, 2, 1, 3).reshape(Bsz, S, ATT_HEADS * ATT_HEAD_DIM)


SCAN_STEP_CHUNKS = 1

SSD_GROUP_HEADS = SSD_HEADS // SSD_GROUPS
SSD_GROUP_W = SSD_GROUP_HEADS * SSD_HEAD_DIM
SSD_GATE_COLS = LANES


def _ssd_kernel(*refs, reverse, combine):
    if combine:
        x_ref, bm_ref, cm_ref, grow_ref, gcol_ref, other_ref, z_ref, skip_ref, gain_ref, out_ref, s_scr = refs
    else:
        x_ref, bm_ref, cm_ref, grow_ref, gcol_ref, out_ref, s_scr = refs
    Q, P, N, H = SSD_CHUNK, SSD_HEAD_DIM, SSD_STATE, SSD_HEADS

    @pl.when(pl.program_id(1) == 0)
    def _():
        s_scr[...] = jnp.zeros_like(s_scr)

    row = lax.broadcasted_iota(jnp.int32, (Q, Q), 0)
    col = lax.broadcasted_iota(jnp.int32, (Q, Q), 1)
    not_after = (col >= row) if reverse else (col <= row)
    tri = not_after.astype(jnp.float32)
    last = 0 if reverse else Q - 1
    nt = (((1,), (1,)), ((), ()))
    grow = grow_ref[0]
    gcol = gcol_ref[0]
    a_cols = jnp.dot(tri, gcol, preferred_element_type=jnp.float32, precision=lax.Precision.HIGHEST)
    a_rows = lax.dot_general(grow, tri, nt, preferred_element_type=jnp.float32,
                             precision=lax.Precision.HIGHEST)
    ys = []
    for g in range(SSD_GROUPS):
        ns = slice(g * N, (g + 1) * N)
        bm = bm_ref[0, :, ns].astype(jnp.bfloat16)
        cm = cm_ref[0, :, ns].astype(jnp.bfloat16)
        cb = lax.dot_general(cm, bm, nt, preferred_element_type=jnp.float32)
        state = s_scr[g]
        y_off = jnp.dot(cm, state.astype(jnp.bfloat16), preferred_element_type=jnp.float32)
        xw, decay = [], []
        for r in range(SSD_GROUP_HEADS):
            h = g * SSD_GROUP_HEADS + r
            hs = slice(h * P, (h + 1) * P)
            aq = a_cols[:, H + h:H + h + 1]
            as_ = a_rows[H + h:H + h + 1, :]
            a_tot = as_[:, last:last + 1]
            xdt = x_ref[0, :, hs] * gcol[:, h:h + 1]
            lmat = jnp.exp(jnp.where(not_after, aq - as_, NEG_INF))
            y = jnp.dot((cb * lmat).astype(jnp.bfloat16), xdt.astype(jnp.bfloat16),
                        preferred_element_type=jnp.float32)
            ys.append(y + y_off[:, r * P:(r + 1) * P] * jnp.exp(aq))
            xw.append(jnp.exp(a_tot - aq) * xdt)
            decay.append(jnp.broadcast_to(jnp.exp(a_tot), (1, P)))
        s_loc = lax.dot_general(bm, jnp.concatenate(xw, axis=1).astype(jnp.bfloat16),
                                (((0,), (0,)), ((), ())), preferred_element_type=jnp.float32)
        s_scr[g] = state * jnp.concatenate(decay, axis=1) + s_loc
    y = jnp.concatenate(ys, axis=1)
    if combine:
        x = x_ref[0]
        z = z_ref[0]
        tot = (y + other_ref[0] + x * skip_ref[...]) * (z * jax.nn.sigmoid(z))
        out_ref[0] = tot * lax.rsqrt(jnp.mean(tot * tot, axis=1, keepdims=True) + RMS_EPS) * gain_ref[...]
    else:
        out_ref[0] = y


def _ssd_direction(xbc, grow, gcol, n_ctx_chunks, reverse, extra=None):
    Bsz, TT, _ = xbc.shape
    n_chunks = TT // SSD_CHUNK
    if reverse:
        chunk = lambda c: jnp.where(c < n_ctx_chunks, n_ctx_chunks - 1 - c, n_chunks - 1 + n_ctx_chunks - c)
    else:
        chunk = lambda c: c
    gn = SSD_GROUPS * SSD_STATE
    seq_spec = pl.BlockSpec((1, SSD_CHUNK, SSD_INNER), lambda b, c: (b, chunk(c), 0))
    vec_spec = pl.BlockSpec((1, SSD_INNER), lambda b, c: (0, 0))
    in_specs = [seq_spec,
                pl.BlockSpec((1, SSD_CHUNK, gn), lambda b, c: (b, chunk(c), SSD_INNER // gn)),
                pl.BlockSpec((1, SSD_CHUNK, gn), lambda b, c: (b, chunk(c), SSD_INNER // gn + 1)),
                pl.BlockSpec((1, 2 * SSD_HEADS, SSD_CHUNK), lambda b, c: (b, 0, chunk(c))),
                pl.BlockSpec((1, SSD_CHUNK, SSD_GATE_COLS), lambda b, c: (b, chunk(c), 0))]
    args = [xbc, xbc, xbc, grow, gcol]
    if extra is not None:
        other, z, skip, gain = extra
        z_spec = pl.BlockSpec((1, SSD_CHUNK, SSD_INNER), lambda b, c: (b, chunk(c), z[1]))
        in_specs += [seq_spec, z_spec, vec_spec, vec_spec]
        args += [other, z[0], skip, gain]
    return pl.pallas_call(
        functools.partial(_ssd_kernel, reverse=reverse, combine=extra is not None),
        grid=(Bsz, n_chunks),
        in_specs=in_specs,
        out_specs=seq_spec,
        out_shape=jax.ShapeDtypeStruct((Bsz, TT, SSD_INNER), jnp.float32),
        scratch_shapes=[pltpu.VMEM((SSD_GROUPS, SSD_STATE, SSD_GROUP_W), jnp.float32)],
        compiler_params=pltpu.CompilerParams(dimension_semantics=("arbitrary", "arbitrary"),
                                             vmem_limit_bytes=VMEM_LIMIT_BYTES),
        name="ssd_bwd" if reverse else "ssd_fwd",
    )(*args)


def _ssd_block(z, xbc_act, dt_raw, dt_bias, a_log, d_skip, norm_g, n_ctx):
    Bsz, TT, _ = dt_raw.shape
    dt = jax.nn.softplus(dt_raw.reshape(Bsz, TT, 2, SSD_HEADS) + dt_bias)
    a = dt * -jnp.exp(a_log)
    gate_cols = jnp.concatenate([dt, a], axis=-1)
    skip = jnp.repeat(d_skip, SSD_HEAD_DIM).reshape(1, SSD_INNER)
    outs = None
    for direction in (1, 0):
        gc = gate_cols[:, :, direction]
        grow = gc.transpose(0, 2, 1)
        gcol = jnp.pad(gc, ((0, 0), (0, 0), (0, SSD_GATE_COLS - 2 * SSD_HEADS)))
        extra = None if direction == 1 else (outs, z, skip, norm_g.reshape(1, SSD_INNER))
        outs = _ssd_direction(xbc_act, grow, gcol, n_ctx // SSD_CHUNK, direction == 1, extra)
    return outs


ML_STATE_COLS = 2 * ML_HEAD_DIM
ML_GATE_COLS = LANES


def _mlstm_kernel(*refs, reverse, combine):
    if combine:
        q_ref, k_ref, v_ref, grow_ref, gcol_ref, other_ref, o_ref, gain_ref, out_ref, s_scr, m_scr = refs
    else:
        q_ref, k_ref, v_ref, grow_ref, gcol_ref, out_ref, s_scr, m_scr = refs
    Q, D, H = ML_CHUNK, ML_HEAD_DIM, ML_HEADS

    @pl.when(pl.program_id(1) == 0)
    def _():
        s_scr[...] = jnp.zeros_like(s_scr)
        m_scr[...] = jnp.zeros_like(m_scr)

    row = lax.broadcasted_iota(jnp.int32, (Q, Q), 0)
    col = lax.broadcasted_iota(jnp.int32, (Q, Q), 1)
    not_after = (col >= row) if reverse else (col <= row)
    tri = not_after.astype(jnp.float32)
    last = 0 if reverse else Q - 1
    nt = (((1,), (1,)), ((), ()))
    one_col = (lax.broadcasted_iota(jnp.int32, (Q, D), 1) == 0).astype(jnp.float32)
    states = [s_scr[h] for h in range(H)]
    ms = [m_scr[h, 0:1, 0:1] for h in range(H)]
    chunks = range(SCAN_STEP_CHUNKS)
    for sub in (reversed(chunks) if reverse else chunks):
        rows = slice(sub * Q, (sub + 1) * Q)
        grow = grow_ref[0, :, rows]
        gcol = gcol_ref[0, rows, :]
        b_cols = jnp.dot(tri, gcol, preferred_element_type=jnp.float32, precision=lax.Precision.HIGHEST)
        b_rows = lax.dot_general(grow, tri, nt, preferred_element_type=jnp.float32,
                                 precision=lax.Precision.HIGHEST)
        for h in range(H):
            hs = slice(h * D, (h + 1) * D)
            q = q_ref[0, rows, hs].astype(jnp.bfloat16)
            k = (k_ref[0, rows, hs] * D ** -0.5).astype(jnp.bfloat16)
            v_ext = jnp.concatenate([v_ref[0, rows, hs], one_col], axis=1)
            bq = b_cols[:, H + h:H + h + 1]
            bs = b_rows[H + h:H + h + 1, :]
            i_q = gcol[:, h:h + 1]
            i_s = grow[h:h + 1, :]
            b_last = bs[:, last:last + 1]
            m0, state = ms[h], states[h]
            dm = jnp.where(not_after, bq - bs + i_s, NEG_INF)
            inter = bq + m0
            m_t = jnp.maximum(inter, jnp.max(dm, axis=1, keepdims=True))
            s_mat = lax.dot_general(q, k, nt, preferred_element_type=jnp.float32) * jnp.exp(dm - m_t)
            num = (jnp.dot(s_mat.astype(jnp.bfloat16), v_ext.astype(jnp.bfloat16),
                           preferred_element_type=jnp.float32)
                   + jnp.exp(inter - m_t) * jnp.dot(q, state.astype(jnp.bfloat16),
                                                    preferred_element_type=jnp.float32))
            den = num[:, D:D + 1]
            hout = num[:, :D] / jnp.maximum(jnp.abs(den), jnp.exp(-m_t))
            g_q = b_last - bq + i_q
            m_loc = jnp.max(g_q, axis=0, keepdims=True)
            wv = (jnp.exp(g_q - m_loc) * v_ext).astype(jnp.bfloat16)
            s_loc = lax.dot_general(k, wv, (((0,), (0,)), ((), ())), preferred_element_type=jnp.float32)
            m_new = jnp.maximum(b_last + m0, m_loc)
            states[h] = jnp.exp(b_last + m0 - m_new) * state + jnp.exp(m_loc - m_new) * s_loc
            ms[h] = m_new
            if combine:
                tot = hout + other_ref[0, rows, hs]
                y = tot * lax.rsqrt(jnp.mean(tot * tot, axis=1, keepdims=True) + RMS_EPS) * gain_ref[:, hs]
                out_ref[0, rows, hs] = y * jax.nn.sigmoid(o_ref[0, rows, hs])
            else:
                out_ref[0, rows, hs] = hout
    for h in range(H):
        s_scr[h] = states[h]
        m_scr[h] = jnp.broadcast_to(ms[h], (SUBLANES, LANES))


def _mlstm_direction(q, k, v, grow, gcol, n_ctx_chunks, reverse, extra=None):
    Bsz, TT, _ = q[0].shape
    step = SCAN_STEP_CHUNKS * ML_CHUNK
    n_chunks = TT // step
    if reverse:
        chunk = lambda c: jnp.where(c < n_ctx_chunks, n_ctx_chunks - 1 - c, n_chunks - 1 + n_ctx_chunks - c)
    else:
        chunk = lambda c: c
    seq_spec = pl.BlockSpec((1, step, ML_INNER), lambda b, c: (b, chunk(c), 0))
    col_spec = lambda j: pl.BlockSpec((1, step, ML_INNER), lambda b, c: (b, chunk(c), j))
    in_specs = [col_spec(q[1]), col_spec(k[1]), col_spec(v[1]),
                pl.BlockSpec((1, 2 * ML_HEADS, step), lambda b, c: (b, 0, chunk(c))),
                pl.BlockSpec((1, step, ML_GATE_COLS), lambda b, c: (b, chunk(c), 0))]
    args = [q[0], k[0], v[0], grow, gcol]
    if extra is not None:
        other, o_pre, gain = extra
        in_specs += [seq_spec, col_spec(o_pre[1]), pl.BlockSpec((1, ML_INNER), lambda b, c: (0, 0))]
        args += [other, o_pre[0], gain]
    return pl.pallas_call(
        functools.partial(_mlstm_kernel, reverse=reverse, combine=extra is not None),
        grid=(Bsz, n_chunks),
        in_specs=in_specs,
        out_specs=seq_spec,
        out_shape=jax.ShapeDtypeStruct((Bsz, TT, ML_INNER), jnp.float32),
        scratch_shapes=[pltpu.VMEM((ML_HEADS, ML_HEAD_DIM, ML_STATE_COLS), jnp.float32),
                        pltpu.VMEM((ML_HEADS, SUBLANES, LANES), jnp.float32)],
        compiler_params=pltpu.CompilerParams(dimension_semantics=("arbitrary", "arbitrary"),
                                             vmem_limit_bytes=VMEM_LIMIT_BYTES),
        name="mlstm_bwd" if reverse else "mlstm_fwd",
    )(*args)


def _mlstm_block(q, k, v, o_pre, if_pre, gate_bias, norm_g, n_ctx):
    Bsz, TT, _ = if_pre.shape
    gates = (if_pre.transpose(0, 2, 1) + gate_bias.reshape(1, 4 * ML_HEADS, 1)).reshape(Bsz, 2, 2, ML_HEADS, TT)
    gate_rows = jnp.concatenate([gates[:, :, 0], jax.nn.log_sigmoid(gates[:, :, 1])], axis=2)
    outs = None
    for direction in (1, 0):
        grow = gate_rows[:, direction]
        gcol = jnp.pad(grow.transpose(0, 2, 1), ((0, 0), (0, 0), (0, ML_GATE_COLS - 2 * ML_HEADS)))
        extra = None if direction == 1 else (outs, o_pre, norm_g.reshape(1, ML_INNER))
        outs = _mlstm_direction(q, k, v, grow, gcol, n_ctx // (SCAN_STEP_CHUNKS * ML_CHUNK), direction == 1, extra)
    return outs


_PROJ_ORDER = ('aq', 'sz', 'mq', 'mk', 'mv', 'mo', 'gpre', 'sxbc', 'ak', 'av', 'sdt', 'mif')
_IN_NAMES = ('aq', 'ak', 'av', 'sz', 'sxbc', 'sdt', 'mq', 'mk', 'mv', 'mo', 'mif', 'gpre')
_IN_START = dict(zip(_IN_NAMES, np.cumsum((0,) + IN_SPLITS[:-1]).tolist()))
_IN_WIDTH = dict(zip(_IN_NAMES, IN_SPLITS))
_PROJ_PERM = np.concatenate([np.arange(_IN_START[n], _IN_START[n] + _IN_WIDTH[n]) for n in _PROJ_ORDER])
_PROJ_START = dict(zip(_PROJ_ORDER, np.cumsum([0] + [_IN_WIDTH[n] for n in _PROJ_ORDER[:-1]]).tolist()))
PROJ_COL_BLOCKS = 3
PROJ_COLS = -(-sum(IN_SPLITS) // (PROJ_COL_BLOCKS * LANES)) * (PROJ_COL_BLOCKS * LANES)
ROW_BLOCK = 256


def _proj_cols(proj, name):
    return proj[..., _PROJ_START[name]:_PROJ_START[name] + _IN_WIDTH[name]]


def _proj_block(name, width):
    assert _PROJ_START[name] % width == 0
    return _PROJ_START[name] // width


def _modulated_norm(x, gain, scale, shift):
    y = x * lax.rsqrt(jnp.mean(x * x, axis=-1, keepdims=True) + RMS_EPS)
    return (y * gain) * (1.0 + scale) + shift


def _inproj_kernel(x_ref, gain_ref, mod_ref, w_ref, o_ref):
    h = _modulated_norm(x_ref[0], gain_ref[...], mod_ref[0, 0, 0:1, :], mod_ref[0, 0, 1:2, :])
    o_ref[0] = jnp.dot(h.astype(jnp.bfloat16), w_ref[...], preferred_element_type=jnp.float32)


def _input_projection(x_all, gain, mod, w_perm, n_ctx):
    Bsz, TT, _ = x_all.shape
    cols = PROJ_COLS // PROJ_COL_BLOCKS
    ctx_blocks = n_ctx // ROW_BLOCK
    return pl.pallas_call(
        _inproj_kernel,
        grid=(PROJ_COL_BLOCKS, Bsz, TT // ROW_BLOCK),
        in_specs=[
            pl.BlockSpec((1, ROW_BLOCK, D_MODEL), lambda n, b, r: (b, r, 0)),
            pl.BlockSpec((1, D_MODEL), lambda n, b, r: (0, 0)),
            pl.BlockSpec((1, 1, 2, D_MODEL), lambda n, b, r: (b, jnp.where(r < ctx_blocks, 0, 1), 0, 0)),
            pl.BlockSpec((D_MODEL, cols), lambda n, b, r: (0, n)),
        ],
        out_specs=pl.BlockSpec((1, ROW_BLOCK, cols), lambda n, b, r: (b, r, n)),
        out_shape=jax.ShapeDtypeStruct((Bsz, TT, PROJ_COLS), jnp.float32),
        compiler_params=pltpu.CompilerParams(dimension_semantics=("arbitrary", "arbitrary", "arbitrary"),
                                             vmem_limit_bytes=VMEM_LIMIT_BYTES),
        name="input_projection",
    )(x_all, gain.reshape(1, D_MODEL), mod, w_perm)


def _merge_kernel(att_ref, ssd_ref, ml_ref, g0_ref, g1_ref, g2_ref, wb_ref, wo_ref, x_ref, gain_ref, mod_ref,
                  xo_ref, h2_ref):
    mixed = None
    for r, (branch, gate) in enumerate(((att_ref, g0_ref), (ssd_ref, g1_ref), (ml_ref, g2_ref))):
        p = jnp.dot(branch[0].astype(jnp.bfloat16), wb_ref[r], preferred_element_type=jnp.float32)
        term = jax.nn.sigmoid(gate[0]) * p
        mixed = term if mixed is None else mixed + term
    y = jnp.dot(mixed.astype(jnp.bfloat16), wo_ref[...], preferred_element_type=jnp.float32)
    x_new = x_ref[0] + mod_ref[0, 0, 0:1, :] * y
    xo_ref[0] = x_new
    h2 = _modulated_norm(x_new, gain_ref[...], mod_ref[0, 0, 1:2, :], mod_ref[0, 0, 2:3, :])
    h2_ref[0] = h2.astype(jnp.bfloat16)


def _merge(att, ssd, ml, proj, w_branch, w_out, x_all, gain, mod, n_ctx):
    Bsz, TT, _ = x_all.shape
    ctx_blocks = n_ctx // ROW_BLOCK
    row = lambda w, j=0: pl.BlockSpec((1, ROW_BLOCK, w), lambda b, r: (b, r, j))
    g_first = _proj_block('gpre', D_MODEL)
    return pl.pallas_call(
        _merge_kernel,
        grid=(Bsz, TT // ROW_BLOCK),
        in_specs=[
            row(BRANCH_W), row(BRANCH_W), row(BRANCH_W),
            row(D_MODEL, g_first), row(D_MODEL, g_first + 1), row(D_MODEL, g_first + 2),
            pl.BlockSpec((N_BRANCH, BRANCH_W, D_MODEL), lambda b, r: (0, 0, 0)),
            pl.BlockSpec((D_MODEL, D_MODEL), lambda b, r: (0, 0)),
            row(D_MODEL),
            pl.BlockSpec((1, D_MODEL), lambda b, r: (0, 0)),
            pl.BlockSpec((1, 1, 3, D_MODEL), lambda b, r: (b, jnp.where(r < ctx_blocks, 0, 1), 0, 0)),
        ],
        out_specs=[row(D_MODEL), row(D_MODEL)],
        out_shape=[jax.ShapeDtypeStruct((Bsz, TT, D_MODEL), jnp.float32),
                   jax.ShapeDtypeStruct((Bsz, TT, D_MODEL), jnp.bfloat16)],
        compiler_params=pltpu.CompilerParams(dimension_semantics=("arbitrary", "arbitrary"),
                                             vmem_limit_bytes=VMEM_LIMIT_BYTES),
        name="merge",
    )(att, ssd, ml, proj, proj, proj, w_branch.astype(jnp.bfloat16), w_out.astype(jnp.bfloat16),
      x_all, gain.reshape(1, D_MODEL), mod)


def _rmsnorm(x, g):
    xf = x.astype(jnp.float32)
    y = xf * lax.rsqrt(jnp.mean(xf * xf, axis=-1, keepdims=True) + RMS_EPS)
    return (y * g.astype(jnp.float32)).astype(x.dtype)


def _heads(t, n):
    return t.reshape(t.shape[0], t.shape[1], n, -1)


def _axial_rope(rows):
    n_freq = ATT_HEAD_DIM // 4
    inv = ROPE_THETA ** (-jnp.arange(n_freq, dtype=jnp.float32) / n_freq)
    row = jnp.broadcast_to(jnp.arange(rows, dtype=jnp.float32)[:, None], (rows, GRID_W)).reshape(-1)
    col = jnp.broadcast_to(jnp.arange(GRID_W, dtype=jnp.float32)[None, :], (rows, GRID_W)).reshape(-1)
    ang = jnp.concatenate([row[:, None] * inv, col[:, None] * inv], axis=-1)
    return jnp.cos(ang), jnp.sin(ang)


def _apply_rope(t, cos, sin):
    half = t.shape[-1] // 2
    tf = t.astype(jnp.float32)
    t1, t2 = tf[..., :half], tf[..., half:]
    c, s = cos[:, None, :], sin[:, None, :]
    return jnp.concatenate([t1 * c - t2 * s, t1 * s + t2 * c], axis=-1).astype(t.dtype)


ATT_KV_CHUNK = 768


def _latent_attention(q, k, v, kc, vc):
    kk = jnp.concatenate([k, kc], axis=1)
    vv = jnp.concatenate([v, vc], axis=1)
    assert kk.shape[1] % ATT_KV_CHUNK == 0
    return _attention(q, kk, vv, ATT_KV_CHUNK)


def _context_attention(qc, kc, vc):
    return _attention(qc, kc, vc, kc.shape[1])


def _dwconv(x, w, b):
    out = lax.conv_general_dilated(x, w[:, None, :], window_strides=(1,), padding='SAME',
                                   dimension_numbers=('NWC', 'WIO', 'NWC'),
                                   feature_group_count=x.shape[-1])
    return out + b


def _token_mixer(proj, n_ctx, qn, kn, conv_w, conv_b, dt_bias, a_log, d_skip, ssd_norm_g,
                 ml_bias, ml_norm_g, cos, sin, ctx_out):
    Bsz = proj.shape[0]
    cat = lambda a, b: jnp.concatenate([a, b], axis=1)
    aq, ak = _proj_cols(proj, 'aq'), _proj_cols(proj, 'ak')
    av = lax.optimization_barrier(_proj_cols(proj, 'av'))
    q = _apply_rope(_rmsnorm(_heads(aq[:, n_ctx:], ATT_HEADS), qn), cos, sin)
    k = _apply_rope(_rmsnorm(_heads(ak[:, n_ctx:], ATT_KV_HEADS), kn), cos, sin)
    kc = _rmsnorm(_heads(ak[:, :n_ctx], ATT_KV_HEADS), kn)
    vc = _heads(av[:, :n_ctx], ATT_KV_HEADS)
    att = _latent_attention(q, k, _heads(av[:, n_ctx:], ATT_KV_HEADS), kc, vc)
    if ctx_out:
        att_c = _context_attention(_rmsnorm(_heads(aq[:, :n_ctx], ATT_HEADS), qn), kc, vc)
    else:
        att_c = jnp.zeros((Bsz, n_ctx, att.shape[-1]), att.dtype)
    sxbc = _proj_cols(proj, 'sxbc')
    xbc_act = cat(jax.nn.silu(_dwconv(sxbc[:, :n_ctx], conv_w, conv_b)),
                  jax.nn.silu(_dwconv(sxbc[:, n_ctx:], conv_w, conv_b)))
    ssd = _ssd_block((proj, _proj_block('sz', SSD_INNER)), xbc_act, _proj_cols(proj, 'sdt'),
                     dt_bias, a_log, d_skip, ssd_norm_g, n_ctx)
    blk = lambda name: (proj, _proj_block(name, ML_INNER))
    ml = _mlstm_block(blk('mq'), blk('mk'), blk('mv'), blk('mo'), _proj_cols(proj, 'mif'),
                      ml_bias, ml_norm_g, n_ctx)
    return cat(att_c, att), ssd, ml


def kernel(x, c, ctx, c_ctx, norm1_g, norm2_g, w_mod, b_mod, w_in, att_qnorm, att_knorm, ssd_conv_w, ssd_conv_b, ssd_dt_bias, ssd_a_log, ssd_d, ssd_norm, ml_gate_bias, ml_norm, w_branch, w_out, peer_wq, peer_subkeys, peer_u, peer_v):
    Bsz, S, _ = x.shape
    depth = w_in.shape[0]
    cos, sin = _axial_rope(S // GRID_W)
    silu_c = jax.nn.silu(c)
    silu_cc = jax.nn.silu(c_ctx)
    xc = ctx
    n_ctx = ctx.shape[1]
    for l in range(depth):
        last = l == depth - 1
        m = (silu_c @ w_mod[l] + b_mod[l]).reshape(Bsz, 6, D_MODEL)
        mc = jnp.broadcast_to((silu_cc @ w_mod[l] + b_mod[l]).reshape(1, 6, D_MODEL), (Bsz, 6, D_MODEL))
        both = jnp.stack([mc, m], axis=1)
        x_all = jnp.concatenate([xc, x], axis=1)
        w_perm = jnp.concatenate(
            [w_in[l][:, _IN_START[n]:_IN_START[n] + _IN_WIDTH[n]] for n in _PROJ_ORDER]
            + [jnp.zeros((D_MODEL, PROJ_COLS - sum(IN_SPLITS)), w_in.dtype)], axis=1).astype(jnp.bfloat16)
        proj = _input_projection(x_all, norm1_g[l], both[:, :, np.array([1, 0])], w_perm, n_ctx)
        att, ssd, ml = _token_mixer(proj, n_ctx, att_qnorm[l], att_knorm[l], ssd_conv_w[l], ssd_conv_b[l],
                                    ssd_dt_bias[l], ssd_a_log[l], ssd_d[l], ssd_norm[l], ml_gate_bias[l],
                                    ml_norm[l], cos, sin, not last)
        x_all, h2 = _merge(att, ssd, ml, proj, w_branch[l], w_out[l], x_all, norm2_g[l],
                           both[:, :, np.array([2, 4, 3])], n_ctx)
        u_bf16 = peer_u[l].astype(jnp.bfloat16)
        vt_bf16 = peer_v[l].astype(jnp.bfloat16).reshape(PEER_EXPERTS // PEER_EBLK, PEER_EBLK, D_MODEL)
        vt_bf16 = vt_bf16.transpose(0, 2, 1)
        po = _peer(h2[:, n_ctx:].reshape(-1, D_MODEL), peer_wq[l], peer_subkeys[l], u_bf16, vt_bf16)
        x = x_all[:, n_ctx:] + m[:, None, 5] * po.reshape(x.shape)
        if not last:
            pc = _peer(h2[:, :n_ctx].reshape(-1, D_MODEL), peer_wq[l], peer_subkeys[l], u_bf16, vt_bf16)
            xc = x_all[:, :n_ctx] + mc[:, None, 5] * pc.reshape(xc.shape)
    return x
```

```python
import functools
import math

import jax
import jax.numpy as jnp
import numpy as np
from jax import lax
from jax.experimental import pallas as pl
from jax.experimental.pallas import tpu as pltpu

D_MODEL = 1024
GRID_W = 64
RMS_EPS = 1e-6
N_BRANCH = 3
BRANCH_W = 512
ATT_HEADS = 8
ATT_KV_HEADS = 2
ATT_HEAD_DIM = 64
ATT_GROUP = ATT_HEADS // ATT_KV_HEADS
Q_BLOCK = 128
ROPE_THETA = 10000.0
SSD_HEADS = 8
SSD_HEAD_DIM = 64
SSD_INNER = SSD_HEADS * SSD_HEAD_DIM
SSD_GROUPS = 2
SSD_STATE = 64
SSD_XBC = SSD_INNER + 2 * SSD_GROUPS * SSD_STATE
SSD_CONV = 5
SSD_CHUNK = 128
ML_HEADS = 4
ML_HEAD_DIM = 128
ML_INNER = ML_HEADS * ML_HEAD_DIM
ML_CHUNK = 128
PEER_HEADS = 8
PEER_KEYS = 128
PEER_EXPERTS = PEER_KEYS * PEER_KEYS
PEER_TOPK = 16
PEER_QDIM = 256
PEER_HALF = PEER_QDIM // 2
IN_SPLITS = (ATT_HEADS * ATT_HEAD_DIM, ATT_KV_HEADS * ATT_HEAD_DIM, ATT_KV_HEADS * ATT_HEAD_DIM,
             SSD_INNER, SSD_XBC, 2 * SSD_HEADS,
             ML_INNER, ML_INNER, ML_INNER, ML_INNER, 4 * ML_HEADS,
             N_BRANCH * D_MODEL)

LANES = 128
SUBLANES = 8
VMEM_LIMIT_BYTES = 56 * 1024 * 1024

ROUTER_TOKENS = 256
ROUTER_CHUNK = LANES
PEER_TOKENS = 1024
PEER_TOKENS_SMALL = 512
PEER_ROWS = 8
PEER_EBLK = PEER_ROWS * PEER_KEYS
PEER_CHUNK = 256

NEG_INF = float("-inf")


def _sort_network(n):
    def merge(lo, hi, r):
        step = r * 2
        if step < hi - lo:
            yield from merge(lo, hi, step)
            yield from merge(lo + r, hi, step)
            yield from [(i, i + r) for i in range(lo + r, hi - r, step)]
        else:
            yield (lo, lo + r)

    def sort(lo, hi):
        if hi - lo >= 1:
            mid = lo + (hi - lo) // 2
            yield from sort(lo, mid)
            yield from sort(mid + 1, hi)
            yield from merge(lo, hi, 1)

    return tuple(sort(0, n - 1))


_SORT16 = _sort_network(PEER_TOPK)


def _merge_top(lists, n):
    lists = list(lists)
    depth = len(lists)
    sub = lax.broadcasted_iota(jnp.int32, lists[0].shape, 0)
    tops = []
    for i in range(n):
        head = lists[0]
        m = jnp.max(head, axis=0, keepdims=True)
        tops.append(m)
        live = min(depth, n - 1 - i)
        if live == 0:
            break
        first = jnp.min(jnp.where(head == m, sub, SUBLANES), axis=0, keepdims=True)
        pop = sub == first
        for r in range(live):
            nxt = lists[r + 1] if r + 1 < depth else NEG_INF
            lists[r] = jnp.where(pop, nxt, lists[r])
    return tops


def _sorted_top(s, n):
    xs = [s[r * SUBLANES:(r + 1) * SUBLANES, :] for r in range(s.shape[0] // SUBLANES)]
    for i, j in _SORT16:
        xs[i], xs[j] = jnp.maximum(xs[i], xs[j]), jnp.minimum(xs[i], xs[j])
    return _merge_top(xs, n)


def _prefix_count(values, pred):
    n = len(values)
    steps = [n >> (i + 1) for i in range(n.bit_length() - 1)]
    taken = []

    def pivot(level, base, step):
        if level == len(taken):
            return values[base + step - 1]
        return jnp.where(taken[level], pivot(level + 1, base + steps[level], step), pivot(level + 1, base, step))

    count = None
    for step in steps:
        ok = pred(pivot(0, 0, step))
        taken.append(ok)
        term = jnp.where(ok, float(step), 0.0)
        count = term if count is None else count + term
    return jnp.where(pred(values[n - 1]), float(n), count)


def _rows_to_sublanes(rows, first_sublane, shape):
    sub = lax.broadcasted_iota(jnp.int32, shape, 0)
    out = jnp.zeros(shape, jnp.float32)
    for k, row in enumerate(rows):
        out = jnp.where(sub == first_sublane + k, row, out)
    return out


_CAND_LEN = (16, 8, 5, 4, 12, 4, 1, 0)


def _router_kernel(h_ref, wq_ref, sk_ref, rank_ref, e2_ref, cnt_ref, e1_ref, q_scr):
    q_scr[...] = jnp.dot(h_ref[...], wq_ref[...], preferred_element_type=jnp.float32).astype(jnp.bfloat16)
    nt = (((1,), (1,)), ((), ()))
    shape8 = (SUBLANES, ROUTER_CHUNK)
    sub = lax.broadcasted_iota(jnp.int32, shape8, 0)
    cand_len = jnp.zeros(shape8, jnp.int32)
    for g, n in enumerate(_CAND_LEN):
        cand_len = jnp.where(sub == g, n, cand_len)
    for h in range(PEER_HEADS):
        q1 = q_scr[:, (2 * h) * PEER_HALF:(2 * h + 1) * PEER_HALF]
        q2 = q_scr[:, (2 * h + 1) * PEER_HALF:(2 * h + 2) * PEER_HALF]
        s1 = lax.dot_general(sk_ref[2 * h], q1, nt, preferred_element_type=jnp.float32)
        s2 = lax.dot_general(sk_ref[2 * h + 1], q2, nt, preferred_element_type=jnp.float32)
        for c in range(ROUTER_TOKENS // ROUTER_CHUNK):
            cs = slice(c * ROUTER_CHUNK, (c + 1) * ROUTER_CHUNK)
            s1c = s1[:, cs]
            s2c = s2[:, cs]
            top1 = _sorted_top(s1c, PEER_TOPK)
            top2 = _sorted_top(s2c, PEER_TOPK)
            a_lo = _rows_to_sublanes(top1[:4], 0, shape8)
            b_lo = _rows_to_sublanes(top2[:3], 4, shape8)
            cands = []
            for r in range(PEER_TOPK):
                by_a = a_lo + top2[r]
                by_b = (top1[4 + r] + b_lo) if 4 + r < PEER_TOPK else by_a
                cands.append(jnp.where(r < cand_len, jnp.where(sub < 4, by_a, by_b), NEG_INF))
            tau = _merge_top(cands, PEER_TOPK)[-1]
            m1, m2 = top1[0], top2[0]
            z = jnp.zeros(shape8, jnp.float32)
            for cand in cands:
                z = z + jnp.where(cand >= tau, jnp.exp(cand - (m1 + m2)), 0.0)
            z = jnp.sum(z, axis=0, keepdims=True)
            rank = _prefix_count(top2, lambda t: t > s2c)
            cnt = _prefix_count(top2, lambda t: s1c + t >= tau)
            rank_ref[h, :, cs] = rank.astype(jnp.bfloat16)
            e2_ref[h, :, cs] = jnp.exp(s2c - m2).astype(jnp.bfloat16)
            cnt_ref[h, :, cs] = cnt
            e1_ref[h, :, cs] = jnp.exp(s1c - m1) * (0.5 / z)


def _peer_router(h2_bf16, wq, subkeys):
    n_tok = h2_bf16.shape[0]
    sk = subkeys.reshape(PEER_HEADS * 2, PEER_KEYS, PEER_HALF)
    shape = (PEER_HEADS, PEER_KEYS, n_tok)
    spec = pl.BlockSpec((PEER_HEADS, PEER_KEYS, ROUTER_TOKENS), lambda i: (0, 0, i))
    return pl.pallas_call(
        _router_kernel,
        grid=(n_tok // ROUTER_TOKENS,),
        in_specs=[
            pl.BlockSpec((ROUTER_TOKENS, D_MODEL), lambda i: (i, 0)),
            pl.BlockSpec((D_MODEL, PEER_HEADS * PEER_QDIM), lambda i: (0, 0)),
            pl.BlockSpec((PEER_HEADS * 2, PEER_KEYS, PEER_HALF), lambda i: (0, 0, 0)),
        ],
        out_specs=[spec, spec, spec, spec],
        out_shape=[jax.ShapeDtypeStruct(shape, jnp.bfloat16), jax.ShapeDtypeStruct(shape, jnp.bfloat16),
                   jax.ShapeDtypeStruct(shape, jnp.float32), jax.ShapeDtypeStruct(shape, jnp.float32)],
        scratch_shapes=[pltpu.VMEM((ROUTER_TOKENS, PEER_HEADS * PEER_QDIM), jnp.bfloat16)],
        compiler_params=pltpu.CompilerParams(dimension_semantics=("arbitrary",),
                                             vmem_limit_bytes=VMEM_LIMIT_BYTES),
        name="peer_router",
    )(h2_bf16, wq.astype(jnp.bfloat16), sk.astype(jnp.bfloat16))


GELU_C0 = math.sqrt(2.0 / math.pi)
GELU_C1 = 0.044715 * GELU_C0


def _peer_dense_kernel(ht_ref, u_ref, vt_ref, rank_ref, e2_ref, cnt_ref, e1_ref, o_ref,
                       acc_scr, pre_scr, p_scr, rank_scr, e2_scr):
    j = pl.program_id(1)
    n_tok = ht_ref.shape[1]
    n_chunks = n_tok // PEER_CHUNK
    nt = (((1,), (1,)), ((), ()))

    @pl.when(j == 0)
    def _():
        acc_scr[...] = jnp.zeros_like(acc_scr)

        def relayout(t, carry):
            lanes = pl.ds(pl.multiple_of(t * LANES, LANES), LANES)
            for h in range(PEER_HEADS):
                rank = rank_ref[h, :, lanes].astype(jnp.float32)
                e2 = e2_ref[h, :, lanes].astype(jnp.float32)
                for b in range(PEER_KEYS // SUBLANES):
                    rank_scr[t, b, h] = rank[b * SUBLANES:(b + 1) * SUBLANES]
                    e2_scr[t, b, h] = e2[b * SUBLANES:(b + 1) * SUBLANES]
            return carry

        lax.fori_loop(0, n_tok // LANES, relayout, 0)

    def pre_matmul(c, parity):
        tok = pl.multiple_of(c * PEER_CHUNK, PEER_CHUNK)
        pre_scr[parity] = jnp.dot(u_ref[...], ht_ref[:, pl.ds(tok, PEER_CHUNK)],
                                  preferred_element_type=jnp.float32)

    def out_matmul(c, parity):
        tok = pl.multiple_of(c * PEER_CHUNK, PEER_CHUNK)
        acc_scr[:, pl.ds(tok, PEER_CHUNK)] += jnp.dot(vt_ref[0], p_scr[parity],
                                                      preferred_element_type=jnp.float32)

    def gates(c, parity):
        for t in range(PEER_CHUNK // LANES):
            lanes = pl.ds(pl.multiple_of(c * PEER_CHUNK + t * LANES, LANES), LANES)
            ts = slice(t * LANES, (t + 1) * LANES)
            tile = c * (PEER_CHUNK // LANES) + t
            for b in range(PEER_KEYS // SUBLANES):
                rank = [rank_scr[tile, b, h] for h in range(PEER_HEADS)]
                e2 = [e2_scr[tile, b, h] for h in range(PEER_HEADS)]
                for a in range(PEER_ROWS):
                    rs = slice(a * PEER_KEYS + b * SUBLANES, a * PEER_KEYS + (b + 1) * SUBLANES)
                    terms = []
                    for h in range(PEER_HEADS):
                        terms.append(jnp.where(rank[h] < cnt_ref[a, h:h + 1, lanes], e2[h], 0.0)
                                     * e1_ref[a, h:h + 1, lanes])
                    while len(terms) > 1:
                        terms = [x + y for x, y in zip(terms[::2], terms[1::2])]
                    x = pre_scr[parity, rs, ts]
                    act = x * (1.0 + jnp.tanh(x * (GELU_C0 + GELU_C1 * (x * x))))
                    pre_scr[parity, rs, ts] = terms[0] * act
        p_scr[parity] = pre_scr[parity].astype(jnp.bfloat16)

    p_scr[1] = jnp.zeros(p_scr.shape[1:], p_scr.dtype)
    pre_matmul(0, 0)

    def body(c, carry):
        parity = c % 2
        out_matmul(jnp.maximum(c - 1, 0), 1 - parity)
        pre_matmul(c + 1, 1 - parity)
        gates(c, parity)
        return carry

    lax.fori_loop(0, n_chunks - 1, body, 0)
    last = n_chunks - 1
    out_matmul(last - 1, 1 - last % 2)
    gates(last, last % 2)
    out_matmul(last, last % 2)

    @pl.when(j == pl.num_programs(1) - 1)
    def _():
        o_ref[...] = acc_scr[...].T


def _peer_dense(h2t_bf16, u_bf16, vt_bf16, rank, e2, cnt, e1, tb):
    n_tok = h2t_bf16.shape[1]
    row_spec = pl.BlockSpec((PEER_ROWS, PEER_HEADS, tb), lambda i, j: (j, 0, i))
    tiled = (tb // LANES, PEER_KEYS // SUBLANES, PEER_HEADS, SUBLANES, LANES)
    key_spec = pl.BlockSpec((PEER_HEADS, PEER_KEYS, tb), lambda i, j: (0, 0, i))
    return pl.pallas_call(
        _peer_dense_kernel,
        grid=(n_tok // tb, PEER_EXPERTS // PEER_EBLK),
        in_specs=[
            pl.BlockSpec((D_MODEL, tb), lambda i, j: (0, i)),
            pl.BlockSpec((PEER_EBLK, D_MODEL), lambda i, j: (j, 0)),
            pl.BlockSpec((1, D_MODEL, PEER_EBLK), lambda i, j: (j, 0, 0)),
            key_spec, key_spec, row_spec, row_spec,
        ],
        out_specs=pl.BlockSpec((tb, D_MODEL), lambda i, j: (i, 0)),
        out_shape=jax.ShapeDtypeStruct((n_tok, D_MODEL), jnp.float32),
        scratch_shapes=[pltpu.VMEM((D_MODEL, tb), jnp.float32),
                        pltpu.VMEM((2, PEER_EBLK, PEER_CHUNK), jnp.float32),
                        pltpu.VMEM((2, PEER_EBLK, PEER_CHUNK), jnp.bfloat16),
                        pltpu.VMEM(tiled, jnp.float32),
                        pltpu.VMEM(tiled, jnp.float32)],
        compiler_params=pltpu.CompilerParams(dimension_semantics=("arbitrary", "arbitrary"),
                                             vmem_limit_bytes=VMEM_LIMIT_BYTES),
        name="peer_dense",
    )(h2t_bf16, u_bf16, vt_bf16, rank, e2, cnt, e1)


def _peer(h2, wq, subkeys, u_bf16, vt_bf16):
    n_tok = h2.shape[0]
    tb = PEER_TOKENS if n_tok % PEER_TOKENS == 0 else PEER_TOKENS_SMALL
    h2_bf16 = h2.astype(jnp.bfloat16)
    rank, e2, cnt, e1 = _peer_router(h2_bf16, wq, subkeys)
    cnt, e1 = cnt.transpose(1, 0, 2), e1.transpose(1, 0, 2)
    return _peer_dense(h2_bf16.T, u_bf16, vt_bf16, rank, e2, cnt, e1, tb)


ATT_Q_TOKENS = 256
ATT_V_COLS = 2 * ATT_HEAD_DIM


def _attention_kernel(q_ref, kt_ref, v_ref, o_ref, *, kv_chunk):
    tq = q_ref.shape[2]
    rows = ATT_GROUP * tq
    q = q_ref[0].reshape(rows, ATT_HEAD_DIM)
    n_chunks = kt_ref.shape[3] // kv_chunk

    def body(c, carry):
        m, acc = carry
        off = pl.multiple_of(c * kv_chunk, kv_chunk)
        s = jnp.dot(q, kt_ref[0, 0, :, pl.ds(off, kv_chunk)], preferred_element_type=jnp.float32)
        m_new = jnp.maximum(m, jnp.max(s, axis=-1, keepdims=True))
        p = jnp.exp(s - m_new).astype(jnp.bfloat16)
        acc = jnp.exp(m - m_new) * acc + jnp.dot(p, v_ref[0, 0, pl.ds(off, kv_chunk), :],
                                                 preferred_element_type=jnp.float32)
        return m_new, acc

    m0 = jnp.full((rows, 1), NEG_INF, jnp.float32)
    acc0 = jnp.zeros((rows, ATT_V_COLS), jnp.float32)
    _, acc = lax.fori_loop(0, n_chunks, body, (m0, acc0), unroll=True)
    out = acc[:, :ATT_HEAD_DIM] / acc[:, ATT_HEAD_DIM:ATT_HEAD_DIM + 1]
    o_ref[0] = out.reshape(ATT_GROUP, tq, ATT_HEAD_DIM)


def _attention(q, k, v, kv_chunk):
    Bsz, S = q.shape[0], q.shape[1]
    SK = k.shape[1]
    tq = min(ATT_Q_TOKENS, S)
    qh = (q * ATT_HEAD_DIM ** -0.5).astype(jnp.bfloat16).transpose(0, 2, 1, 3)
    kt = k.astype(jnp.bfloat16).transpose(0, 2, 3, 1)
    pad = jnp.concatenate([jnp.ones(v.shape[:-1] + (1,), v.dtype),
                           jnp.zeros(v.shape[:-1] + (ATT_V_COLS - ATT_HEAD_DIM - 1,), v.dtype)], axis=-1)
    vx = jnp.concatenate([v, pad], axis=-1).astype(jnp.bfloat16).transpose(0, 2, 1, 3)
    o = pl.pallas_call(
        functools.partial(_attention_kernel, kv_chunk=kv_chunk),
        grid=(Bsz, ATT_KV_HEADS, S // tq),
        in_specs=[
            pl.BlockSpec((1, ATT_GROUP, tq, ATT_HEAD_DIM), lambda b, g, i: (b, g, i, 0)),
            pl.BlockSpec((1, 1, ATT_HEAD_DIM, SK), lambda b, g, i: (b, g, 0, 0)),
            pl.BlockSpec((1, 1, SK, ATT_V_COLS), lambda b, g, i: (b, g, 0, 0)),
        ],
        out_specs=pl.BlockSpec((1, ATT_GROUP, tq, ATT_HEAD_DIM), lambda b, g, i: (b, g, i, 0)),
        out_shape=jax.ShapeDtypeStruct((Bsz, ATT_HEADS, S, ATT_HEAD_DIM), jnp.float32),
        compiler_params=pltpu.CompilerParams(dimension_semantics=("arbitrary", "arbitrary", "arbitrary"),
                                             vmem_limit_bytes=VMEM_LIMIT_BYTES),
        name="attention",
    )(qh, kt, vx)
    return o.transpose(0, 2, 1, 3).reshape(Bsz, S, ATT_HEADS * ATT_HEAD_DIM)


SCAN_STEP_CHUNKS = 1

SSD_GROUP_HEADS = SSD_HEADS // SSD_GROUPS
SSD_GROUP_W = SSD_GROUP_HEADS * SSD_HEAD_DIM
SSD_GATE_COLS = LANES


def _ssd_kernel(*refs, reverse, combine):
    if combine:
        x_ref, bm_ref, cm_ref, grow_ref, gcol_ref, other_ref, z_ref, skip_ref, gain_ref, out_ref, s_scr = refs
    else:
        x_ref, bm_ref, cm_ref, grow_ref, gcol_ref, out_ref, s_scr = refs
    Q, P, N, H = SSD_CHUNK, SSD_HEAD_DIM, SSD_STATE, SSD_HEADS

    @pl.when(pl.program_id(1) == 0)
    def _():
        s_scr[...] = jnp.zeros_like(s_scr)

    row = lax.broadcasted_iota(jnp.int32, (Q, Q), 0)
    col = lax.broadcasted_iota(jnp.int32, (Q, Q), 1)
    not_after = (col >= row) if reverse else (col <= row)
    tri = not_after.astype(jnp.float32)
    last = 0 if reverse else Q - 1
    nt = (((1,), (1,)), ((), ()))
    grow = grow_ref[0]
    gcol = gcol_ref[0]
    a_cols = jnp.dot(tri, gcol, preferred_element_type=jnp.float32, precision=lax.Precision.HIGHEST)
    a_rows = lax.dot_general(grow, tri, nt, preferred_element_type=jnp.float32,
                             precision=lax.Precision.HIGHEST)
    ys = []
    for g in range(SSD_GROUPS):
        ns = slice(g * N, (g + 1) * N)
        bm = bm_ref[0, :, ns].astype(jnp.bfloat16)
        cm = cm_ref[0, :, ns].astype(jnp.bfloat16)
        cb = lax.dot_general(cm, bm, nt, preferred_element_type=jnp.float32)
        state = s_scr[g]
        y_off = jnp.dot(cm, state.astype(jnp.bfloat16), preferred_element_type=jnp.float32)
        xw, decay = [], []
        for r in range(SSD_GROUP_HEADS):
            h = g * SSD_GROUP_HEADS + r
            hs = slice(h * P, (h + 1) * P)
            aq = a_cols[:, H + h:H + h + 1]
            as_ = a_rows[H + h:H + h + 1, :]
            a_tot = as_[:, last:last + 1]
            xdt = x_ref[0, :, hs] * gcol[:, h:h + 1]
            lmat = jnp.exp(jnp.where(not_after, aq - as_, NEG_INF))
            y = jnp.dot((cb * lmat).astype(jnp.bfloat16), xdt.astype(jnp.bfloat16),
                        preferred_element_type=jnp.float32)
            ys.append(y + y_off[:, r * P:(r + 1) * P] * jnp.exp(aq))
            xw.append(jnp.exp(a_tot - aq) * xdt)
            decay.append(jnp.broadcast_to(jnp.exp(a_tot), (1, P)))
        s_loc = lax.dot_general(bm, jnp.concatenate(xw, axis=1).astype(jnp.bfloat16),
                                (((0,), (0,)), ((), ())), preferred_element_type=jnp.float32)
        s_scr[g] = state * jnp.concatenate(decay, axis=1) + s_loc
    y = jnp.concatenate(ys, axis=1)
    if combine:
        x = x_ref[0]
        z = z_ref[0]
        tot = (y + other_ref[0] + x * skip_ref[...]) * (z * jax.nn.sigmoid(z))
        out_ref[0] = tot * lax.rsqrt(jnp.mean(tot * tot, axis=1, keepdims=True) + RMS_EPS) * gain_ref[...]
    else:
        out_ref[0] = y


def _ssd_direction(xbc, grow, gcol, n_ctx_chunks, reverse, extra=None):
    Bsz, TT, _ = xbc.shape
    n_chunks = TT // SSD_CHUNK
    if reverse:
        chunk = lambda c: jnp.where(c < n_ctx_chunks, n_ctx_chunks - 1 - c, n_chunks - 1 + n_ctx_chunks - c)
    else:
        chunk = lambda c: c
    gn = SSD_GROUPS * SSD_STATE
    seq_spec = pl.BlockSpec((1, SSD_CHUNK, SSD_INNER), lambda b, c: (b, chunk(c), 0))
    vec_spec = pl.BlockSpec((1, SSD_INNER), lambda b, c: (0, 0))
    in_specs = [seq_spec,
                pl.BlockSpec((1, SSD_CHUNK, gn), lambda b, c: (b, chunk(c), SSD_INNER // gn)),
                pl.BlockSpec((1, SSD_CHUNK, gn), lambda b, c: (b, chunk(c), SSD_INNER // gn + 1)),
                pl.BlockSpec((1, 2 * SSD_HEADS, SSD_CHUNK), lambda b, c: (b, 0, chunk(c))),
                pl.BlockSpec((1, SSD_CHUNK, SSD_GATE_COLS), lambda b, c: (b, chunk(c), 0))]
    args = [xbc, xbc, xbc, grow, gcol]
    if extra is not None:
        other, z, skip, gain = extra
        z_spec = pl.BlockSpec((1, SSD_CHUNK, SSD_INNER), lambda b, c: (b, chunk(c), z[1]))
        in_specs += [seq_spec, z_spec, vec_spec, vec_spec]
        args += [other, z[0], skip, gain]
    return pl.pallas_call(
        functools.partial(_ssd_kernel, reverse=reverse, combine=extra is not None),
        grid=(Bsz, n_chunks),
        in_specs=in_specs,
        out_specs=seq_spec,
        out_shape=jax.ShapeDtypeStruct((Bsz, TT, SSD_INNER), jnp.float32),
        scratch_shapes=[pltpu.VMEM((SSD_GROUPS, SSD_STATE, SSD_GROUP_W), jnp.float32)],
        compiler_params=pltpu.CompilerParams(dimension_semantics=("arbitrary", "arbitrary"),
                                             vmem_limit_bytes=VMEM_LIMIT_BYTES),
        name="ssd_bwd" if reverse else "ssd_fwd",
    )(*args)


def _ssd_block(z, xbc_act, dt_raw, dt_bias, a_log, d_skip, norm_g, n_ctx):
    Bsz, TT, _ = dt_raw.shape
    dt = jax.nn.softplus(dt_raw.reshape(Bsz, TT, 2, SSD_HEADS) + dt_bias)
    a = dt * -jnp.exp(a_log)
    gate_cols = jnp.concatenate([dt, a], axis=-1)
    skip = jnp.repeat(d_skip, SSD_HEAD_DIM).reshape(1, SSD_INNER)
    outs = None
    for direction in (1, 0):
        gc = gate_cols[:, :, direction]
        grow = gc.transpose(0, 2, 1)
        gcol = jnp.pad(gc, ((0, 0), (0, 0), (0, SSD_GATE_COLS - 2 * SSD_HEADS)))
        extra = None if direction == 1 else (outs, z, skip, norm_g.reshape(1, SSD_INNER))
        outs = _ssd_direction(xbc_act, grow, gcol, n_ctx // SSD_CHUNK, direction == 1, extra)
    return outs


ML_STATE_COLS = 2 * ML_HEAD_DIM
ML_GATE_COLS = LANES


def _mlstm_kernel(*refs, reverse, combine):
    if combine:
        q_ref, k_ref, v_ref, grow_ref, gcol_ref, other_ref, o_ref, gain_ref, out_ref, s_scr, m_scr = refs
    else:
        q_ref, k_ref, v_ref, grow_ref, gcol_ref, out_ref, s_scr, m_scr = refs
    Q, D, H = ML_CHUNK, ML_HEAD_DIM, ML_HEADS

    @pl.when(pl.program_id(1) == 0)
    def _():
        s_scr[...] = jnp.zeros_like(s_scr)
        m_scr[...] = jnp.zeros_like(m_scr)

    row = lax.broadcasted_iota(jnp.int32, (Q, Q), 0)
    col = lax.broadcasted_iota(jnp.int32, (Q, Q), 1)
    not_after = (col >= row) if reverse else (col <= row)
    tri = not_after.astype(jnp.float32)
    last = 0 if reverse else Q - 1
    nt = (((1,), (1,)), ((), ()))
    one_col = (lax.broadcasted_iota(jnp.int32, (Q, D), 1) == 0).astype(jnp.float32)
    states = [s_scr[h] for h in range(H)]
    ms = [m_scr[h, 0:1, 0:1] for h in range(H)]
    chunks = range(SCAN_STEP_CHUNKS)
    for sub in (reversed(chunks) if reverse else chunks):
        rows = slice(sub * Q, (sub + 1) * Q)
        grow = grow_ref[0, :, rows]
        gcol = gcol_ref[0, rows, :]
        b_cols = jnp.dot(tri, gcol, preferred_element_type=jnp.float32, precision=lax.Precision.HIGHEST)
        b_rows = lax.dot_general(grow, tri, nt, preferred_element_type=jnp.float32,
                                 precision=lax.Precision.HIGHEST)
        for h in range(H):
            hs = slice(h * D, (h + 1) * D)
            q = q_ref[0, rows, hs].astype(jnp.bfloat16)
            k = (k_ref[0, rows, hs] * D ** -0.5).astype(jnp.bfloat16)
            v_ext = jnp.concatenate([v_ref[0, rows, hs], one_col], axis=1)
            bq = b_cols[:, H + h:H + h + 1]
            bs = b_rows[H + h:H + h + 1, :]
            i_q = gcol[:, h:h + 1]
            i_s = grow[h:h + 1, :]
            b_last = bs[:, last:last + 1]
            m0, state = ms[h], states[h]
            dm = jnp.where(not_after, bq - bs + i_s, NEG_INF)
            inter = bq + m0
            m_t = jnp.maximum(inter, jnp.max(dm, axis=1, keepdims=True))
            s_mat = lax.dot_general(q, k, nt, preferred_element_type=jnp.float32) * jnp.exp(dm - m_t)
            num = (jnp.dot(s_mat.astype(jnp.bfloat16), v_ext.astype(jnp.bfloat16),
                           preferred_element_type=jnp.float32)
                   + jnp.exp(inter - m_t) * jnp.dot(q, state.astype(jnp.bfloat16),
                                                    preferred_element_type=jnp.float32))
            den = num[:, D:D + 1]
            hout = num[:, :D] / jnp.maximum(jnp.abs(den), jnp.exp(-m_t))
            g_q = b_last - bq + i_q
            m_loc = jnp.max(g_q, axis=0, keepdims=True)
            wv = (jnp.exp(g_q - m_loc) * v_ext).astype(jnp.bfloat16)
            s_loc = lax.dot_general(k, wv, (((0,), (0,)), ((), ())), preferred_element_type=jnp.float32)
            m_new = jnp.maximum(b_last + m0, m_loc)
            states[h] = jnp.exp(b_last + m0 - m_new) * state + jnp.exp(m_loc - m_new) * s_loc
            ms[h] = m_new
            if combine:
                tot = hout + other_ref[0, rows, hs]
                y = tot * lax.rsqrt(jnp.mean(tot * tot, axis=1, keepdims=True) + RMS_EPS) * gain_ref[:, hs]
                out_ref[0, rows, hs] = y * jax.nn.sigmoid(o_ref[0, rows, hs])
            else:
                out_ref[0, rows, hs] = hout
    for h in range(H):
        s_scr[h] = states[h]
        m_scr[h] = jnp.broadcast_to(ms[h], (SUBLANES, LANES))


def _mlstm_direction(q, k, v, grow, gcol, n_ctx_chunks, reverse, extra=None):
    Bsz, TT, _ = q[0].shape
    step = SCAN_STEP_CHUNKS * ML_CHUNK
    n_chunks = TT // step
    if reverse:
        chunk = lambda c: jnp.where(c < n_ctx_chunks, n_ctx_chunks - 1 - c, n_chunks - 1 + n_ctx_chunks - c)
    else:
        chunk = lambda c: c
    seq_spec = pl.BlockSpec((1, step, ML_INNER), lambda b, c: (b, chunk(c), 0))
    col_spec = lambda j: pl.BlockSpec((1, step, ML_INNER), lambda b, c: (b, chunk(c), j))
    in_specs = [col_spec(q[1]), col_spec(k[1]), col_spec(v[1]),
                pl.BlockSpec((1, 2 * ML_HEADS, step), lambda b, c: (b, 0, chunk(c))),
                pl.BlockSpec((1, step, ML_GATE_COLS), lambda b, c: (b, chunk(c), 0))]
    args = [q[0], k[0], v[0], grow, gcol]
    if extra is not None:
        other, o_pre, gain = extra
        in_specs += [seq_spec, col_spec(o_pre[1]), pl.BlockSpec((1, ML_INNER), lambda b, c: (0, 0))]
        args += [other, o_pre[0], gain]
    return pl.pallas_call(
        functools.partial(_mlstm_kernel, reverse=reverse, combine=extra is not None),
        grid=(Bsz, n_chunks),
        in_specs=in_specs,
        out_specs=seq_spec,
        out_shape=jax.ShapeDtypeStruct((Bsz, TT, ML_INNER), jnp.float32),
        scratch_shapes=[pltpu.VMEM((ML_HEADS, ML_HEAD_DIM, ML_STATE_COLS), jnp.float32),
                        pltpu.VMEM((ML_HEADS, SUBLANES, LANES), jnp.float32)],
        compiler_params=pltpu.CompilerParams(dimension_semantics=("arbitrary", "arbitrary"),
                                             vmem_limit_bytes=VMEM_LIMIT_BYTES),
        name="mlstm_bwd" if reverse else "mlstm_fwd",
    )(*args)


def _mlstm_block(q, k, v, o_pre, if_pre, gate_bias, norm_g, n_ctx):
    Bsz, TT, _ = if_pre.shape
    gates = (if_pre.transpose(0, 2, 1) + gate_bias.reshape(1, 4 * ML_HEADS, 1)).reshape(Bsz, 2, 2, ML_HEADS, TT)
    gate_rows = jnp.concatenate([gates[:, :, 0], jax.nn.log_sigmoid(gates[:, :, 1])], axis=2)
    outs = None
    for direction in (1, 0):
        grow = gate_rows[:, direction]
        gcol = jnp.pad(grow.transpose(0, 2, 1), ((0, 0), (0, 0), (0, ML_GATE_COLS - 2 * ML_HEADS)))
        extra = None if direction == 1 else (outs, o_pre, norm_g.reshape(1, ML_INNER))
        outs = _mlstm_direction(q, k, v, grow, gcol, n_ctx // (SCAN_STEP_CHUNKS * ML_CHUNK), direction == 1, extra)
    return outs


_PROJ_ORDER = ('aq', 'sz', 'mq', 'mk', 'mv', 'mo', 'gpre', 'sxbc', 'ak', 'av', 'sdt', 'mif')
_IN_NAMES = ('aq', 'ak', 'av', 'sz', 'sxbc', 'sdt', 'mq', 'mk', 'mv', 'mo', 'mif', 'gpre')
_IN_START = dict(zip(_IN_NAMES, np.cumsum((0,) + IN_SPLITS[:-1]).tolist()))
_IN_WIDTH = dict(zip(_IN_NAMES, IN_SPLITS))
_PROJ_PERM = np.concatenate([np.arange(_IN_START[n], _IN_START[n] + _IN_WIDTH[n]) for n in _PROJ_ORDER])
_PROJ_START = dict(zip(_PROJ_ORDER, np.cumsum([0] + [_IN_WIDTH[n] for n in _PROJ_ORDER[:-1]]).tolist()))
PROJ_COL_BLOCKS = 3
PROJ_COLS = -(-sum(IN_SPLITS) // (PROJ_COL_BLOCKS * LANES)) * (PROJ_COL_BLOCKS * LANES)
ROW_BLOCK = 256


def _proj_cols(proj, name):
    return proj[..., _PROJ_START[name]:_PROJ_START[name] + _IN_WIDTH[name]]


def _proj_block(name, width):
    assert _PROJ_START[name] % width == 0
    return _PROJ_START[name] // width


def _modulated_norm(x, gain, scale, shift):
    y = x * lax.rsqrt(jnp.mean(x * x, axis=-1, keepdims=True) + RMS_EPS)
    return (y * gain) * (1.0 + scale) + shift


def _inproj_kernel(x_ref, gain_ref, mod_ref, w_ref, o_ref):
    h = _modulated_norm(x_ref[0], gain_ref[...], mod_ref[0, 0, 0:1, :], mod_ref[0, 0, 1:2, :])
    o_ref[0] = jnp.dot(h.astype(jnp.bfloat16), w_ref[...], preferred_element_type=jnp.float32)


def _input_projection(x_all, gain, mod, w_perm, n_ctx):
    Bsz, TT, _ = x_all.shape
    cols = PROJ_COLS // PROJ_COL_BLOCKS
    ctx_blocks = n_ctx // ROW_BLOCK
    return pl.pallas_call(
        _inproj_kernel,
        grid=(PROJ_COL_BLOCKS, Bsz, TT // ROW_BLOCK),
        in_specs=[
            pl.BlockSpec((1, ROW_BLOCK, D_MODEL), lambda n, b, r: (b, r, 0)),
            pl.BlockSpec((1, D_MODEL), lambda n, b, r: (0, 0)),
            pl.BlockSpec((1, 1, 2, D_MODEL), lambda n, b, r: (b, jnp.where(r < ctx_blocks, 0, 1), 0, 0)),
            pl.BlockSpec((D_MODEL, cols), lambda n, b, r: (0, n)),
        ],
        out_specs=pl.BlockSpec((1, ROW_BLOCK, cols), lambda n, b, r: (b, r, n)),
        out_shape=jax.ShapeDtypeStruct((Bsz, TT, PROJ_COLS), jnp.float32),
        compiler_params=pltpu.CompilerParams(dimension_semantics=("arbitrary", "arbitrary", "arbitrary"),
                                             vmem_limit_bytes=VMEM_LIMIT_BYTES),
        name="input_projection",
    )(x_all, gain.reshape(1, D_MODEL), mod, w_perm)


def _merge_kernel(att_ref, ssd_ref, ml_ref, g0_ref, g1_ref, g2_ref, wb_ref, wo_ref, x_ref, gain_ref, mod_ref,
                  xo_ref, h2_ref):
    mixed = None
    for r, (branch, gate) in enumerate(((att_ref, g0_ref), (ssd_ref, g1_ref), (ml_ref, g2_ref))):
        p = jnp.dot(branch[0].astype(jnp.bfloat16), wb_ref[r], preferred_element_type=jnp.float32)
        term = jax.nn.sigmoid(gate[0]) * p
        mixed = term if mixed is None else mixed + term
    y = jnp.dot(mixed.astype(jnp.bfloat16), wo_ref[...], preferred_element_type=jnp.float32)
    x_new = x_ref[0] + mod_ref[0, 0, 0:1, :] * y
    xo_ref[0] = x_new
    h2 = _modulated_norm(x_new, gain_ref[...], mod_ref[0, 0, 1:2, :], mod_ref[0, 0, 2:3, :])
    h2_ref[0] = h2.astype(jnp.bfloat16)


def _merge(att, ssd, ml, proj, w_branch, w_out, x_all, gain, mod, n_ctx):
    Bsz, TT, _ = x_all.shape
    ctx_blocks = n_ctx // ROW_BLOCK
    row = lambda w, j=0: pl.BlockSpec((1, ROW_BLOCK, w), lambda b, r: (b, r, j))
    g_first = _proj_block('gpre', D_MODEL)
    return pl.pallas_call(
        _merge_kernel,
        grid=(Bsz, TT // ROW_BLOCK),
        in_specs=[
            row(BRANCH_W), row(BRANCH_W), row(BRANCH_W),
            row(D_MODEL, g_first), row(D_MODEL, g_first + 1), row(D_MODEL, g_first + 2),
            pl.BlockSpec((N_BRANCH, BRANCH_W, D_MODEL), lambda b, r: (0, 0, 0)),
            pl.BlockSpec((D_MODEL, D_MODEL), lambda b, r: (0, 0)),
            row(D_MODEL),
            pl.BlockSpec((1, D_MODEL), lambda b, r: (0, 0)),
            pl.BlockSpec((1, 1, 3, D_MODEL), lambda b, r: (b, jnp.where(r < ctx_blocks, 0, 1), 0, 0)),
        ],
        out_specs=[row(D_MODEL), row(D_MODEL)],
        out_shape=[jax.ShapeDtypeStruct((Bsz, TT, D_MODEL), jnp.float32),
                   jax.ShapeDtypeStruct((Bsz, TT, D_MODEL), jnp.bfloat16)],
        compiler_params=pltpu.CompilerParams(dimension_semantics=("arbitrary", "arbitrary"),
                                             vmem_limit_bytes=VMEM_LIMIT_BYTES),
        name="merge",
    )(att, ssd, ml, proj, proj, proj, w_branch.astype(jnp.bfloat16), w_out.astype(jnp.bfloat16),
      x_all, gain.reshape(1, D_MODEL), mod)


def _rmsnorm(x, g):
    xf = x.astype(jnp.float32)
    y = xf * lax.rsqrt(jnp.mean(xf * xf, axis=-1, keepdims=True) + RMS_EPS)
    return (y * g.astype(jnp.float32)).astype(x.dtype)


def _heads(t, n):
    return t.reshape(t.shape[0], t.shape[1], n, -1)


def _axial_rope(rows):
    n_freq = ATT_HEAD_DIM // 4
    inv = ROPE_THETA ** (-jnp.arange(n_freq, dtype=jnp.float32) / n_freq)
    row = jnp.broadcast_to(jnp.arange(rows, dtype=jnp.float32)[:, None], (rows, GRID_W)).reshape(-1)
    col = jnp.broadcast_to(jnp.arange(GRID_W, dtype=jnp.float32)[None, :], (rows, GRID_W)).reshape(-1)
    ang = jnp.concatenate([row[:, None] * inv, col[:, None] * inv], axis=-1)
    return jnp.cos(ang), jnp.sin(ang)


def _apply_rope(t, cos, sin):
    half = t.shape[-1] // 2
    tf = t.astype(jnp.float32)
    t1, t2 = tf[..., :half], tf[..., half:]
    c, s = cos[:, None, :], sin[:, None, :]
    return jnp.concatenate([t1 * c - t2 * s, t1 * s + t2 * c], axis=-1).astype(t.dtype)


ATT_KV_CHUNK = 768


def _latent_attention(q, k, v, kc, vc):
    kk = jnp.concatenate([k, kc], axis=1)
    vv = jnp.concatenate([v, vc], axis=1)
    assert kk.shape[1] % ATT_KV_CHUNK == 0
    return _attention(q, kk, vv, ATT_KV_CHUNK)


def _context_attention(qc, kc, vc):
    return _attention(qc, kc, vc, kc.shape[1])


def _dwconv(x, w, b):
    out = lax.conv_general_dilated(x, w[:, None, :], window_strides=(1,), padding='SAME',
                                   dimension_numbers=('NWC', 'WIO', 'NWC'),
                                   feature_group_count=x.shape[-1])
    return out + b


def _token_mixer(proj, n_ctx, qn, kn, conv_w, conv_b, dt_bias, a_log, d_skip, ssd_norm_g,
                 ml_bias, ml_norm_g, cos, sin, ctx_out):
    Bsz = proj.shape[0]
    cat = lambda a, b: jnp.concatenate([a, b], axis=1)
    aq, ak = _proj_cols(proj, 'aq'), _proj_cols(proj, 'ak')
    av = lax.optimization_barrier(_proj_cols(proj, 'av'))
    q = _apply_rope(_rmsnorm(_heads(aq[:, n_ctx:], ATT_HEADS), qn), cos, sin)
    k = _apply_rope(_rmsnorm(_heads(ak[:, n_ctx:], ATT_KV_HEADS), kn), cos, sin)
    kc = _rmsnorm(_heads(ak[:, :n_ctx], ATT_KV_HEADS), kn)
    vc = _heads(av[:, :n_ctx], ATT_KV_HEADS)
    att = _latent_attention(q, k, _heads(av[:, n_ctx:], ATT_KV_HEADS), kc, vc)
    if ctx_out:
        att_c = _context_attention(_rmsnorm(_heads(aq[:, :n_ctx], ATT_HEADS), qn), kc, vc)
    else:
        att_c = jnp.zeros((Bsz, n_ctx, att.shape[-1]), att.dtype)
    sxbc = _proj_cols(proj, 'sxbc')
    xbc_act = cat(jax.nn.silu(_dwconv(sxbc[:, :n_ctx], conv_w, conv_b)),
                  jax.nn.silu(_dwconv(sxbc[:, n_ctx:], conv_w, conv_b)))
    ssd = _ssd_block((proj, _proj_block('sz', SSD_INNER)), xbc_act, _proj_cols(proj, 'sdt'),
                     dt_bias, a_log, d_skip, ssd_norm_g, n_ctx)
    blk = lambda name: (proj, _proj_block(name, ML_INNER))
    ml = _mlstm_block(blk('mq'), blk('mk'), blk('mv'), blk('mo'), _proj_cols(proj, 'mif'),
                      ml_bias, ml_norm_g, n_ctx)
    return cat(att_c, att), ssd, ml


def kernel(x, c, ctx, c_ctx, norm1_g, norm2_g, w_mod, b_mod, w_in, att_qnorm, att_knorm, ssd_conv_w, ssd_conv_b, ssd_dt_bias, ssd_a_log, ssd_d, ssd_norm, ml_gate_bias, ml_norm, w_branch, w_out, peer_wq, peer_subkeys, peer_u, peer_v):
    Bsz, S, _ = x.shape
    depth = w_in.shape[0]
    cos, sin = _axial_rope(S // GRID_W)
    silu_c = jax.nn.silu(c)
    silu_cc = jax.nn.silu(c_ctx)
    xc = ctx
    n_ctx = ctx.shape[1]
    for l in range(depth):
        last = l == depth - 1
        m = (silu_c @ w_mod[l] + b_mod[l]).reshape(Bsz, 6, D_MODEL)
        mc = jnp.broadcast_to((silu_cc @ w_mod[l] + b_mod[l]).reshape(1, 6, D_MODEL), (Bsz, 6, D_MODEL))
        both = jnp.stack([mc, m], axis=1)
        x_all = jnp.concatenate([xc, x], axis=1)
        w_perm = jnp.concatenate(
            [w_in[l][:, _IN_START[n]:_IN_START[n] + _IN_WIDTH[n]] for n in _PROJ_ORDER]
            + [jnp.zeros((D_MODEL, PROJ_COLS - sum(IN_SPLITS)), w_in.dtype)], axis=1).astype(jnp.bfloat16)
        proj = _input_projection(x_all, norm1_g[l], both[:, :, np.array([1, 0])], w_perm, n_ctx)
        att, ssd, ml = _token_mixer(proj, n_ctx, att_qnorm[l], att_knorm[l], ssd_conv_w[l], ssd_conv_b[l],
                                    ssd_dt_bias[l], ssd_a_log[l], ssd_d[l], ssd_norm[l], ml_gate_bias[l],
                                    ml_norm[l], cos, sin, not last)
        x_all, h2 = _merge(att, ssd, ml, proj, w_branch[l], w_out[l], x_all, norm2_g[l],
                           both[:, :, np.array([2, 4, 3])], n_ctx)
        u_bf16 = peer_u[l].astype(jnp.bfloat16)
        vt_bf16 = peer_v[l].astype(jnp.bfloat16).reshape(PEER_EXPERTS // PEER_EBLK, PEER_EBLK, D_MODEL)
        vt_bf16 = vt_bf16.transpose(0, 2, 1)
        po = _peer(h2[:, n_ctx:].reshape(-1, D_MODEL), peer_wq[l], peer_subkeys[l], u_bf16, vt_bf16)
        x = x_all[:, n_ctx:] + m[:, None, 5] * po.reshape(x.shape)
        if not last:
            pc = _peer(h2[:, :n_ctx].reshape(-1, D_MODEL), peer_wq[l], peer_subkeys[l], u_bf16, vt_bf16)
            xc = x_all[:, :n_ctx] + mc[:, None, 5] * pc.reshape(xc.shape)
    return x
```

```python
import functools
import math

import jax
import jax.numpy as jnp
import numpy as np
from jax import lax
from jax.experimental import pallas as pl
from jax.experimental.pallas import tpu as pltpu

D_MODEL = 1024
GRID_W = 64
RMS_EPS = 1e-6
N_BRANCH = 3
BRANCH_W = 512
ATT_HEADS = 8
ATT_KV_HEADS = 2
ATT_HEAD_DIM = 64
ATT_GROUP = ATT_HEADS // ATT_KV_HEADS
Q_BLOCK = 128
ROPE_THETA = 10000.0
SSD_HEADS = 8
SSD_HEAD_DIM = 64
SSD_INNER = SSD_HEADS * SSD_HEAD_DIM
SSD_GROUPS = 2
SSD_STATE = 64
SSD_XBC = SSD_INNER + 2 * SSD_GROUPS * SSD_STATE
SSD_CONV = 5
SSD_CHUNK = 128
ML_HEADS = 4
ML_HEAD_DIM = 128
ML_INNER = ML_HEADS * ML_HEAD_DIM
ML_CHUNK = 128
PEER_HEADS = 8
PEER_KEYS = 128
PEER_EXPERTS = PEER_KEYS * PEER_KEYS
PEER_TOPK = 16
PEER_QDIM = 256
PEER_HALF = PEER_QDIM // 2
IN_SPLITS = (ATT_HEADS * ATT_HEAD_DIM, ATT_KV_HEADS * ATT_HEAD_DIM, ATT_KV_HEADS * ATT_HEAD_DIM,
             SSD_INNER, SSD_XBC, 2 * SSD_HEADS,
             ML_INNER, ML_INNER, ML_INNER, ML_INNER, 4 * ML_HEADS,
             N_BRANCH * D_MODEL)

LANES = 128
SUBLANES = 8
VMEM_LIMIT_BYTES = 56 * 1024 * 1024

ROUTER_TOKENS = 256
ROUTER_CHUNK = LANES
PEER_TOKENS = 1024
PEER_TOKENS_SMALL = 512
PEER_ROWS = 8
PEER_EBLK = PEER_ROWS * PEER_KEYS
PEER_CHUNK = 256

NEG_INF = float("-inf")


def _sort_network(n):
    def merge(lo, hi, r):
        step = r * 2
        if step < hi - lo:
            yield from merge(lo, hi, step)
            yield from merge(lo + r, hi, step)
            yield from [(i, i + r) for i in range(lo + r, hi - r, step)]
        else:
            yield (lo, lo + r)

    def sort(lo, hi):
        if hi - lo >= 1:
            mid = lo + (hi - lo) // 2
            yield from sort(lo, mid)
            yield from sort(mid + 1, hi)
            yield from merge(lo, hi, 1)

    return tuple(sort(0, n - 1))


_SORT16 = _sort_network(PEER_TOPK)


def _merge_top(lists, n):
    lists = list(lists)
    depth = len(lists)
    sub = lax.broadcasted_iota(jnp.int32, lists[0].shape, 0)
    tops = []
    for i in range(n):
        head = lists[0]
        m = jnp.max(head, axis=0, keepdims=True)
        tops.append(m)
        live = min(depth, n - 1 - i)
        if live == 0:
            break
        first = jnp.min(jnp.where(head == m, sub, SUBLANES), axis=0, keepdims=True)
        pop = sub == first
        for r in range(live):
            nxt = lists[r + 1] if r + 1 < depth else NEG_INF
            lists[r] = jnp.where(pop, nxt, lists[r])
    return tops


def _sorted_top(s, n):
    xs = [s[r * SUBLANES:(r + 1) * SUBLANES, :] for r in range(s.shape[0] // SUBLANES)]
    for i, j in _SORT16:
        xs[i], xs[j] = jnp.maximum(xs[i], xs[j]), jnp.minimum(xs[i], xs[j])
    return _merge_top(xs, n)


def _prefix_count(values, pred):
    n = len(values)
    steps = [n >> (i + 1) for i in range(n.bit_length() - 1)]
    taken = []

    def pivot(level, base, step):
        if level == len(taken):
            return values[base + step - 1]
        return jnp.where(taken[level], pivot(level + 1, base + steps[level], step), pivot(level + 1, base, step))

    count = None
    for step in steps:
        ok = pred(pivot(0, 0, step))
        taken.append(ok)
        term = jnp.where(ok, float(step), 0.0)
        count = term if count is None else count + term
    return jnp.where(pred(values[n - 1]), float(n), count)


def _rows_to_sublanes(rows, first_sublane, shape):
    sub = lax.broadcasted_iota(jnp.int32, shape, 0)
    out = jnp.zeros(shape, jnp.float32)
    for k, row in enumerate(rows):
        out = jnp.where(sub == first_sublane + k, row, out)
    return out


_CAND_LEN = (16, 8, 5, 4, 12, 4, 1, 0)


def _router_kernel(h_ref, wq_ref, sk_ref, rank_ref, e2_ref, cnt_ref, e1_ref, q_scr):
    q_scr[...] = jnp.dot(h_ref[...], wq_ref[...], preferred_element_type=jnp.float32).astype(jnp.bfloat16)
    nt = (((1,), (1,)), ((), ()))
    shape8 = (SUBLANES, ROUTER_CHUNK)
    sub = lax.broadcasted_iota(jnp.int32, shape8, 0)
    cand_len = jnp.zeros(shape8, jnp.int32)
    for g, n in enumerate(_CAND_LEN):
        cand_len = jnp.where(sub == g, n, cand_len)
    for h in range(PEER_HEADS):
        q1 = q_scr[:, (2 * h) * PEER_HALF:(2 * h + 1) * PEER_HALF]
        q2 = q_scr[:, (2 * h + 1) * PEER_HALF:(2 * h + 2) * PEER_HALF]
        s1 = lax.dot_general(sk_ref[2 * h], q1, nt, preferred_element_type=jnp.float32)
        s2 = lax.dot_general(sk_ref[2 * h + 1], q2, nt, preferred_element_type=jnp.float32)
        for c in range(ROUTER_TOKENS // ROUTER_CHUNK):
            cs = slice(c * ROUTER_CHUNK, (c + 1) * ROUTER_CHUNK)
            s1c = s1[:, cs]
            s2c = s2[:, cs]
            top1 = _sorted_top(s1c, PEER_TOPK)
            top2 = _sorted_top(s2c, PEER_TOPK)
            a_lo = _rows_to_sublanes(top1[:4], 0, shape8)
            b_lo = _rows_to_sublanes(top2[:3], 4, shape8)
            cands = []
            for r in range(PEER_TOPK):
                by_a = a_lo + top2[r]
                by_b = (top1[4 + r] + b_lo) if 4 + r < PEER_TOPK else by_a
                cands.append(jnp.where(r < cand_len, jnp.where(sub < 4, by_a, by_b), NEG_INF))
            tau = _merge_top(cands, PEER_TOPK)[-1]
            m1, m2 = top1[0], top2[0]
            z = jnp.zeros(shape8, jnp.float32)
            for cand in cands:
                z = z + jnp.where(cand >= tau, jnp.exp(cand - (m1 + m2)), 0.0)
            z = jnp.sum(z, axis=0, keepdims=True)
            rank = _prefix_count(top2, lambda t: t > s2c)
            cnt = _prefix_count(top2, lambda t: s1c + t >= tau)
            rank_ref[h, :, cs] = rank.astype(jnp.bfloat16)
            e2_ref[h, :, cs] = jnp.exp(s2c - m2).astype(jnp.bfloat16)
            cnt_ref[h, :, cs] = cnt
            e1_ref[h, :, cs] = jnp.exp(s1c - m1) * (0.5 / z)


def _peer_router(h2_bf16, wq, subkeys):
    n_tok = h2_bf16.shape[0]
    sk = subkeys.reshape(PEER_HEADS * 2, PEER_KEYS, PEER_HALF)
    shape = (PEER_HEADS, PEER_KEYS, n_tok)
    spec = pl.BlockSpec((PEER_HEADS, PEER_KEYS, ROUTER_TOKENS), lambda i: (0, 0, i))
    return pl.pallas_call(
        _router_kernel,
        grid=(n_tok // ROUTER_TOKENS,),
        in_specs=[
            pl.BlockSpec((ROUTER_TOKENS, D_MODEL), lambda i: (i, 0)),
            pl.BlockSpec((D_MODEL, PEER_HEADS * PEER_QDIM), lambda i: (0, 0)),
            pl.BlockSpec((PEER_HEADS * 2, PEER_KEYS, PEER_HALF), lambda i: (0, 0, 0)),
        ],
        out_specs=[spec, spec, spec, spec],
        out_shape=[jax.ShapeDtypeStruct(shape, jnp.bfloat16), jax.ShapeDtypeStruct(shape, jnp.bfloat16),
                   jax.ShapeDtypeStruct(shape, jnp.float32), jax.ShapeDtypeStruct(shape, jnp.float32)],
        scratch_shapes=[pltpu.VMEM((ROUTER_TOKENS, PEER_HEADS * PEER_QDIM), jnp.bfloat16)],
        compiler_params=pltpu.CompilerParams(dimension_semantics=("arbitrary",),
                                             vmem_limit_bytes=VMEM_LIMIT_BYTES),
        name="peer_router",
    )(h2_bf16, wq.astype(jnp.bfloat16), sk.astype(jnp.bfloat16))


GELU_C0 = math.sqrt(2.0 / math.pi)
GELU_C1 = 0.044715 * GELU_C0


def _peer_dense_kernel(ht_ref, u_ref, vt_ref, rank_ref, e2_ref, cnt_ref, e1_ref, o_ref,
                       acc_scr, pre_scr, p_scr, rank_scr, e2_scr):
    j = pl.program_id(1)
    n_tok = ht_ref.shape[1]
    n_chunks = n_tok // PEER_CHUNK
    nt = (((1,), (1,)), ((), ()))

    @pl.when(j == 0)
    def _():
        acc_scr[...] = jnp.zeros_like(acc_scr)

        def relayout(t, carry):
            lanes = pl.ds(pl.multiple_of(t * LANES, LANES), LANES)
            for h in range(PEER_HEADS):
                rank = rank_ref[h, :, lanes].astype(jnp.float32)
                e2 = e2_ref[h, :, lanes].astype(jnp.float32)
                for b in range(PEER_KEYS // SUBLANES):
                    rank_scr[t, b, h] = rank[b * SUBLANES:(b + 1) * SUBLANES]
                    e2_scr[t, b, h] = e2[b * SUBLANES:(b + 1) * SUBLANES]
            return carry

        lax.fori_loop(0, n_tok // LANES, relayout, 0)

    def pre_matmul(c, parity):
        tok = pl.multiple_of(c * PEER_CHUNK, PEER_CHUNK)
        pre_scr[parity] = jnp.dot(u_ref[...], ht_ref[:, pl.ds(tok, PEER_CHUNK)],
                                  preferred_element_type=jnp.float32)

    def out_matmul(c, parity):
        tok = pl.multiple_of(c * PEER_CHUNK, PEER_CHUNK)
        acc_scr[:, pl.ds(tok, PEER_CHUNK)] += jnp.dot(vt_ref[0], p_scr[parity],
                                                      preferred_element_type=jnp.float32)

    def gates(c, parity):
        for t in range(PEER_CHUNK // LANES):
            lanes = pl.ds(pl.multiple_of(c * PEER_CHUNK + t * LANES, LANES), LANES)
            ts = slice(t * LANES, (t + 1) * LANES)
            tile = c * (PEER_CHUNK // LANES) + t
            for b in range(PEER_KEYS // SUBLANES):
                rank = [rank_scr[tile, b, h] for h in range(PEER_HEADS)]
                e2 = [e2_scr[tile, b, h] for h in range(PEER_HEADS)]
                for a in range(PEER_ROWS):
                    rs = slice(a * PEER_KEYS + b * SUBLANES, a * PEER_KEYS + (b + 1) * SUBLANES)
                    terms = []
                    for h in range(PEER_HEADS):
                        terms.append(jnp.where(rank[h] < cnt_ref[a, h:h + 1, lanes], e2[h], 0.0)
                                     * e1_ref[a, h:h + 1, lanes])
                    while len(terms) > 1:
                        terms = [x + y for x, y in zip(terms[::2], terms[1::2])]
                    x = pre_scr[parity, rs, ts]
                    act = x * (1.0 + jnp.tanh(x * (GELU_C0 + GELU_C1 * (x * x))))
                    pre_scr[parity, rs, ts] = terms[0] * act
        p_scr[parity] = pre_scr[parity].astype(jnp.bfloat16)

    p_scr[1] = jnp.zeros(p_scr.shape[1:], p_scr.dtype)
    pre_matmul(0, 0)

    def body(c, carry):
        parity = c % 2
        out_matmul(jnp.maximum(c - 1, 0), 1 - parity)
        pre_matmul(c + 1, 1 - parity)
        gates(c, parity)
        return carry

    lax.fori_loop(0, n_chunks - 1, body, 0)
    last = n_chunks - 1
    out_matmul(last - 1, 1 - last % 2)
    gates(last, last % 2)
    out_matmul(last, last % 2)

    @pl.when(j == pl.num_programs(1) - 1)
    def _():
        o_ref[...] = acc_scr[...].T


def _peer_dense(h2t_bf16, u_bf16, vt_bf16, rank, e2, cnt, e1, tb):
    n_tok = h2t_bf16.shape[1]
    row_spec = pl.BlockSpec((PEER_ROWS, PEER_HEADS, tb), lambda i, j: (j, 0, i))
    tiled = (tb // LANES, PEER_KEYS // SUBLANES, PEER_HEADS, SUBLANES, LANES)
    key_spec = pl.BlockSpec((PEER_HEADS, PEER_KEYS, tb), lambda i, j: (0, 0, i))
    return pl.pallas_call(
        _peer_dense_kernel,
        grid=(n_tok // tb, PEER_EXPERTS // PEER_EBLK),
        in_specs=[
            pl.BlockSpec((D_MODEL, tb), lambda i, j: (0, i)),
            pl.BlockSpec((PEER_EBLK, D_MODEL), lambda i, j: (j, 0)),
            pl.BlockSpec((1, D_MODEL, PEER_EBLK), lambda i, j: (j, 0, 0)),
            key_spec, key_spec, row_spec, row_spec,
        ],
        out_specs=pl.BlockSpec((tb, D_MODEL), lambda i, j: (i, 0)),
        out_shape=jax.ShapeDtypeStruct((n_tok, D_MODEL), jnp.float32),
        scratch_shapes=[pltpu.VMEM((D_MODEL, tb), jnp.float32),
                        pltpu.VMEM((2, PEER_EBLK, PEER_CHUNK), jnp.float32),
                        pltpu.VMEM((2, PEER_EBLK, PEER_CHUNK), jnp.bfloat16),
                        pltpu.VMEM(tiled, jnp.float32),
                        pltpu.VMEM(tiled, jnp.float32)],
        compiler_params=pltpu.CompilerParams(dimension_semantics=("arbitrary", "arbitrary"),
                                             vmem_limit_bytes=VMEM_LIMIT_BYTES),
        name="peer_dense",
    )(h2t_bf16, u_bf16, vt_bf16, rank, e2, cnt, e1)


def _peer(h2, wq, subkeys, u_bf16, vt_bf16):
    n_tok = h2.shape[0]
    tb = PEER_TOKENS if n_tok % PEER_TOKENS == 0 else PEER_TOKENS_SMALL
    h2_bf16 = h2.astype(jnp.bfloat16)
    rank, e2, cnt, e1 = _peer_router(h2_bf16, wq, subkeys)
    cnt, e1 = cnt.transpose(1, 0, 2), e1.transpose(1, 0, 2)
    return _peer_dense(h2_bf16.T, u_bf16, vt_bf16, rank, e2, cnt, e1, tb)


ATT_Q_TOKENS = 256
ATT_V_COLS = 2 * ATT_HEAD_DIM


def _attention_kernel(q_ref, kt_ref, v_ref, o_ref, *, kv_chunk):
    tq = q_ref.shape[2]
    rows = ATT_GROUP * tq
    q = q_ref[0].reshape(rows, ATT_HEAD_DIM)
    n_chunks = kt_ref.shape[3] // kv_chunk

    def body(c, carry):
        m, acc = carry
        off = pl.multiple_of(c * kv_chunk, kv_chunk)
        s = jnp.dot(q, kt_ref[0, 0, :, pl.ds(off, kv_chunk)], preferred_element_type=jnp.float32)
        m_new = jnp.maximum(m, jnp.max(s, axis=-1, keepdims=True))
        p = jnp.exp(s - m_new).astype(jnp.bfloat16)
        acc = jnp.exp(m - m_new) * acc + jnp.dot(p, v_ref[0, 0, pl.ds(off, kv_chunk), :],
                                                 preferred_element_type=jnp.float32)
        return m_new, acc

    m0 = jnp.full((rows, 1), NEG_INF, jnp.float32)
    acc0 = jnp.zeros((rows, ATT_V_COLS), jnp.float32)
    _, acc = lax.fori_loop(0, n_chunks, body, (m0, acc0), unroll=True)
    out = acc[:, :ATT_HEAD_DIM] / acc[:, ATT_HEAD_DIM:ATT_HEAD_DIM + 1]
    o_ref[0] = out.reshape(ATT_GROUP, tq, ATT_HEAD_DIM)


def _attention(q, k, v, kv_chunk):
    Bsz, S = q.shape[0], q.shape[1]
    SK = k.shape[1]
    tq = min(ATT_Q_TOKENS, S)
    qh = (q * ATT_HEAD_DIM ** -0.5).astype(jnp.bfloat16).transpose(0, 2, 1, 3)
    kt = k.astype(jnp.bfloat16).transpose(0, 2, 3, 1)
    pad = jnp.concatenate([jnp.ones(v.shape[:-1] + (1,), v.dtype),
                           jnp.zeros(v.shape[:-1] + (ATT_V_COLS - ATT_HEAD_DIM - 1,), v.dtype)], axis=-1)
    vx = jnp.concatenate([v, pad], axis=-1).astype(jnp.bfloat16).transpose(0, 2, 1, 3)
    o = pl.pallas_call(
        functools.partial(_attention_kernel, kv_chunk=kv_chunk),
        grid=(Bsz, ATT_KV_HEADS, S // tq),
        in_specs=[
            pl.BlockSpec((1, ATT_GROUP, tq, ATT_HEAD_DIM), lambda b, g, i: (b, g, i, 0)),
            pl.BlockSpec((1, 1, ATT_HEAD_DIM, SK), lambda b, g, i: (b, g, 0, 0)),
            pl.BlockSpec((1, 1, SK, ATT_V_COLS), lambda b, g, i: (b, g, 0, 0)),
        ],
        out_specs=pl.BlockSpec((1, ATT_GROUP, tq, ATT_HEAD_DIM), lambda b, g, i: (b, g, i, 0)),
        out_shape=jax.ShapeDtypeStruct((Bsz, ATT_HEADS, S, ATT_HEAD_DIM), jnp.float32),
        compiler_params=pltpu.CompilerParams(dimension_semantics=("arbitrary", "arbitrary", "arbitrary"),
                                             vmem_limit_bytes=VMEM_LIMIT_BYTES),
        name="attention",
    )(qh, kt, vx)
    return o.transpose(0, 2, 1, 3).reshape(Bsz, S, ATT_HEADS * ATT_HEAD_DIM)


SCAN_STEP_CHUNKS = 1

SSD_GROUP_HEADS = SSD_HEADS // SSD_GROUPS
SSD_GROUP_W = SSD_GROUP_HEADS * SSD_HEAD_DIM
SSD_GATE_COLS = LANES


def _ssd_kernel(*refs, reverse, combine):
    if combine:
        x_ref, bm_ref, cm_ref, grow_ref, gcol_ref, other_ref, z_ref, skip_ref, gain_ref, out_ref, s_scr = refs
    else:
        x_ref, bm_ref, cm_ref, grow_ref, gcol_ref, out_ref, s_scr = refs
    Q, P, N, H = SSD_CHUNK, SSD_HEAD_DIM, SSD_STATE, SSD_HEADS

    @pl.when(pl.program_id(1) == 0)
    def _():
        s_scr[...] = jnp.zeros_like(s_scr)

    row = lax.broadcasted_iota(jnp.int32, (Q, Q), 0)
    col = lax.broadcasted_iota(jnp.int32, (Q, Q), 1)
    not_after = (col >= row) if reverse else (col <= row)
    tri = not_after.astype(jnp.float32)
    last = 0 if reverse else Q - 1
    nt = (((1,), (1,)), ((), ()))
    grow = grow_ref[0]
    gcol = gcol_ref[0]
    a_cols = jnp.dot(tri, gcol, preferred_element_type=jnp.float32, precision=lax.Precision.HIGHEST)
    a_rows = lax.dot_general(grow, tri, nt, preferred_element_type=jnp.float32,
                             precision=lax.Precision.HIGHEST)
    ys = []
    for g in range(SSD_GROUPS):
        ns = slice(g * N, (g + 1) * N)
        bm = bm_ref[0, :, ns].astype(jnp.bfloat16)
        cm = cm_ref[0, :, ns].astype(jnp.bfloat16)
        cb = lax.dot_general(cm, bm, nt, preferred_element_type=jnp.float32)
        state = s_scr[g]
        y_off = jnp.dot(cm, state.astype(jnp.bfloat16), preferred_element_type=jnp.float32)
        xw, decay = [], []
        for r in range(SSD_GROUP_HEADS):
            h = g * SSD_GROUP_HEADS + r
            hs = slice(h * P, (h + 1) * P)
            aq = a_cols[:, H + h:H + h + 1]
            as_ = a_rows[H + h:H + h + 1, :]
            a_tot = as_[:, last:last + 1]
            xdt = x_ref[0, :, hs] * gcol[:, h:h + 1]
            lmat = jnp.exp(jnp.where(not_after, aq - as_, NEG_INF))
            y = jnp.dot((cb * lmat).astype(jnp.bfloat16), xdt.astype(jnp.bfloat16),
                        preferred_element_type=jnp.float32)
            ys.append(y + y_off[:, r * P:(r + 1) * P] * jnp.exp(aq))
            xw.append(jnp.exp(a_tot - aq) * xdt)
            decay.append(jnp.broadcast_to(jnp.exp(a_tot), (1, P)))
        s_loc = lax.dot_general(bm, jnp.concatenate(xw, axis=1).astype(jnp.bfloat16),
                                (((0,), (0,)), ((), ())), preferred_element_type=jnp.float32)
        s_scr[g] = state * jnp.concatenate(decay, axis=1) + s_loc
    y = jnp.concatenate(ys, axis=1)
    if combine:
        x = x_ref[0]
        z = z_ref[0]
        tot = (y + other_ref[0] + x * skip_ref[...]) * (z * jax.nn.sigmoid(z))
        out_ref[0] = tot * lax.rsqrt(jnp.mean(tot * tot, axis=1, keepdims=True) + RMS_EPS) * gain_ref[...]
    else:
        out_ref[0] = y


def _ssd_direction(xbc, grow, gcol, n_ctx_chunks, reverse, extra=None):
    Bsz, TT, _ = xbc.shape
    n_chunks = TT // SSD_CHUNK
    if reverse:
        chunk = lambda c: jnp.where(c < n_ctx_chunks, n_ctx_chunks - 1 - c, n_chunks - 1 + n_ctx_chunks - c)
    else:
        chunk = lambda c: c
    gn = SSD_GROUPS * SSD_STATE
    seq_spec = pl.BlockSpec((1, SSD_CHUNK, SSD_INNER), lambda b, c: (b, chunk(c), 0))
    vec_spec = pl.BlockSpec((1, SSD_INNER), lambda b, c: (0, 0))
    in_specs = [seq_spec,
                pl.BlockSpec((1, SSD_CHUNK, gn), lambda b, c: (b, chunk(c), SSD_INNER // gn)),
                pl.BlockSpec((1, SSD_CHUNK, gn), lambda b, c: (b, chunk(c), SSD_INNER // gn + 1)),
                pl.BlockSpec((1, 2 * SSD_HEADS, SSD_CHUNK), lambda b, c: (b, 0, chunk(c))),
                pl.BlockSpec((1, SSD_CHUNK, SSD_GATE_COLS), lambda b, c: (b, chunk(c), 0))]
    args = [xbc, xbc, xbc, grow, gcol]
    if extra is not None:
        other, z, skip, gain = extra
        z_spec = pl.BlockSpec((1, SSD_CHUNK, SSD_INNER), lambda b, c: (b, chunk(c), z[1]))
        in_specs += [seq_spec, z_spec, vec_spec, vec_spec]
        args += [other, z[0], skip, gain]
    return pl.pallas_call(
        functools.partial(_ssd_kernel, reverse=reverse, combine=extra is not None),
        grid=(Bsz, n_chunks),
        in_specs=in_specs,
        out_specs=seq_spec,
        out_shape=jax.ShapeDtypeStruct((Bsz, TT, SSD_INNER), jnp.float32),
        scratch_shapes=[pltpu.VMEM((SSD_GROUPS, SSD_STATE, SSD_GROUP_W), jnp.float32)],
        compiler_params=pltpu.CompilerParams(dimension_semantics=("arbitrary", "arbitrary"),
                                             vmem_limit_bytes=VMEM_LIMIT_BYTES),
        name="ssd_bwd" if reverse else "ssd_fwd",
    )(*args)


def _ssd_block(z, xbc_act, dt_raw, dt_bias, a_log, d_skip, norm_g, n_ctx):
    Bsz, TT, _ = dt_raw.shape
    dt = jax.nn.softplus(dt_raw.reshape(Bsz, TT, 2, SSD_HEADS) + dt_bias)
    a = dt * -jnp.exp(a_log)
    gate_cols = jnp.concatenate([dt, a], axis=-1)
    skip = jnp.repeat(d_skip, SSD_HEAD_DIM).reshape(1, SSD_INNER)
    outs = None
    for direction in (1, 0):
        gc = gate_cols[:, :, direction]
        grow = gc.transpose(0, 2, 1)
        gcol = jnp.pad(gc, ((0, 0), (0, 0), (0, SSD_GATE_COLS - 2 * SSD_HEADS)))
        extra = None if direction == 1 else (outs, z, skip, norm_g.reshape(1, SSD_INNER))
        outs = _ssd_direction(xbc_act, grow, gcol, n_ctx // SSD_CHUNK, direction == 1, extra)
    return outs


ML_STATE_COLS = 2 * ML_HEAD_DIM
ML_GATE_COLS = LANES


def _mlstm_kernel(*refs, reverse, combine):
    if combine:
        q_ref, k_ref, v_ref, grow_ref, gcol_ref, other_ref, o_ref, gain_ref, out_ref, s_scr, m_scr = refs
    else:
        q_ref, k_ref, v_ref, grow_ref, gcol_ref, out_ref, s_scr, m_scr = refs
    Q, D, H = ML_CHUNK, ML_HEAD_DIM, ML_HEADS

    @pl.when(pl.program_id(1) == 0)
    def _():
        s_scr[...] = jnp.zeros_like(s_scr)
        m_scr[...] = jnp.zeros_like(m_scr)

    row = lax.broadcasted_iota(jnp.int32, (Q, Q), 0)
    col = lax.broadcasted_iota(jnp.int32, (Q, Q), 1)
    not_after = (col >= row) if reverse else (col <= row)
    tri = not_after.astype(jnp.float32)
    last = 0 if reverse else Q - 1
    nt = (((1,), (1,)), ((), ()))
    one_col = (lax.broadcasted_iota(jnp.int32, (Q, D), 1) == 0).astype(jnp.float32)
    states = [s_scr[h] for h in range(H)]
    ms = [m_scr[h, 0:1, 0:1] for h in range(H)]
    chunks = range(SCAN_STEP_CHUNKS)
    for sub in (reversed(chunks) if reverse else chunks):
        rows = slice(sub * Q, (sub + 1) * Q)
        grow = grow_ref[0, :, rows]
        gcol = gcol_ref[0, rows, :]
        b_cols = jnp.dot(tri, gcol, preferred_element_type=jnp.float32, precision=lax.Precision.HIGHEST)
        b_rows = lax.dot_general(grow, tri, nt, preferred_element_type=jnp.float32,
                                 precision=lax.Precision.HIGHEST)
        for h in range(H):
            hs = slice(h * D, (h + 1) * D)
            q = q_ref[0, rows, hs].astype(jnp.bfloat16)
            k = (k_ref[0, rows, hs] * D ** -0.5).astype(jnp.bfloat16)
            v_ext = jnp.concatenate([v_ref[0, rows, hs], one_col], axis=1)
            bq = b_cols[:, H + h:H + h + 1]
            bs = b_rows[H + h:H + h + 1, :]
            i_q = gcol[:, h:h + 1]
            i_s = grow[h:h + 1, :]
            b_last = bs[:, last:last + 1]
            m0, state = ms[h], states[h]
            dm = jnp.where(not_after, bq - bs + i_s, NEG_INF)
            inter = bq + m0
            m_t = jnp.maximum(inter, jnp.max(dm, axis=1, keepdims=True))
            s_mat = lax.dot_general(q, k, nt, preferred_element_type=jnp.float32) * jnp.exp(dm - m_t)
            num = (jnp.dot(s_mat.astype(jnp.bfloat16), v_ext.astype(jnp.bfloat16),
                           preferred_element_type=jnp.float32)
                   + jnp.exp(inter - m_t) * jnp.dot(q, state.astype(jnp.bfloat16),
                                                    preferred_element_type=jnp.float32))
            den = num[:, D:D + 1]
            hout = num[:, :D] / jnp.maximum(jnp.abs(den), jnp.exp(-m_t))
            g_q = b_last - bq + i_q
            m_loc = jnp.max(g_q, axis=0, keepdims=True)
            wv = (jnp.exp(g_q - m_loc) * v_ext).astype(jnp.bfloat16)
            s_loc = lax.dot_general(k, wv, (((0,), (0,)), ((), ())), preferred_element_type=jnp.float32)
            m_new = jnp.maximum(b_last + m0, m_loc)
            states[h] = jnp.exp(b_last + m0 - m_new) * state + jnp.exp(m_loc - m_new) * s_loc
            ms[h] = m_new
            if combine:
                tot = hout + other_ref[0, rows, hs]
                y = tot * lax.rsqrt(jnp.mean(tot * tot, axis=1, keepdims=True) + RMS_EPS) * gain_ref[:, hs]
                out_ref[0, rows, hs] = y * jax.nn.sigmoid(o_ref[0, rows, hs])
            else:
                out_ref[0, rows, hs] = hout
    for h in range(H):
        s_scr[h] = states[h]
        m_scr[h] = jnp.broadcast_to(ms[h], (SUBLANES, LANES))


def _mlstm_direction(q, k, v, grow, gcol, n_ctx_chunks, reverse, extra=None):
    Bsz, TT, _ = q[0].shape
    step = SCAN_STEP_CHUNKS * ML_CHUNK
    n_chunks = TT // step
    if reverse:
        chunk = lambda c: jnp.where(c < n_ctx_chunks, n_ctx_chunks - 1 - c, n_chunks - 1 + n_ctx_chunks - c)
    else:
        chunk = lambda c: c
    seq_spec = pl.BlockSpec((1, step, ML_INNER), lambda b, c: (b, chunk(c), 0))
    col_spec = lambda j: pl.BlockSpec((1, step, ML_INNER), lambda b, c: (b, chunk(c), j))
    in_specs = [col_spec(q[1]), col_spec(k[1]), col_spec(v[1]),
                pl.BlockSpec((1, 2 * ML_HEADS, step), lambda b, c: (b, 0, chunk(c))),
                pl.BlockSpec((1, step, ML_GATE_COLS), lambda b, c: (b, chunk(c), 0))]
    args = [q[0], k[0], v[0], grow, gcol]
    if extra is not None:
        other, o_pre, gain = extra
        in_specs += [seq_spec, col_spec(o_pre[1]), pl.BlockSpec((1, ML_INNER), lambda b, c: (0, 0))]
        args += [other, o_pre[0], gain]
    return pl.pallas_call(
        functools.partial(_mlstm_kernel, reverse=reverse, combine=extra is not None),
        grid=(Bsz, n_chunks),
        in_specs=in_specs,
        out_specs=seq_spec,
        out_shape=jax.ShapeDtypeStruct((Bsz, TT, ML_INNER), jnp.float32),
        scratch_shapes=[pltpu.VMEM((ML_HEADS, ML_HEAD_DIM, ML_STATE_COLS), jnp.float32),
                        pltpu.VMEM((ML_HEADS, SUBLANES, LANES), jnp.float32)],
        compiler_params=pltpu.CompilerParams(dimension_semantics=("arbitrary", "arbitrary"),
                                             vmem_limit_bytes=VMEM_LIMIT_BYTES),
        name="mlstm_bwd" if reverse else "mlstm_fwd",
    )(*args)


def _mlstm_block(q, k, v, o_pre, if_pre, gate_bias, norm_g, n_ctx):
    Bsz, TT, _ = if_pre.shape
    gates = (if_pre.transpose(0, 2, 1) + gate_bias.reshape(1, 4 * ML_HEADS, 1)).reshape(Bsz, 2, 2, ML_HEADS, TT)
    gate_rows = jnp.concatenate([gates[:, :, 0], jax.nn.log_sigmoid(gates[:, :, 1])], axis=2)
    outs = None
    for direction in (1, 0):
        grow = gate_rows[:, direction]
        gcol = jnp.pad(grow.transpose(0, 2, 1), ((0, 0), (0, 0), (0, ML_GATE_COLS - 2 * ML_HEADS)))
        extra = None if direction == 1 else (outs, o_pre, norm_g.reshape(1, ML_INNER))
        outs = _mlstm_direction(q, k, v, grow, gcol, n_ctx // (SCAN_STEP_CHUNKS * ML_CHUNK), direction == 1, extra)
    return outs


_PROJ_ORDER = ('aq', 'sz', 'mq', 'mk', 'mv', 'mo', 'gpre', 'sxbc', 'ak', 'av', 'sdt', 'mif')
_IN_NAMES = ('aq', 'ak', 'av', 'sz', 'sxbc', 'sdt', 'mq', 'mk', 'mv', 'mo', 'mif', 'gpre')
_IN_START = dict(zip(_IN_NAMES, np.cumsum((0,) + IN_SPLITS[:-1]).tolist()))
_IN_WIDTH = dict(zip(_IN_NAMES, IN_SPLITS))
_PROJ_PERM = np.concatenate([np.arange(_IN_START[n], _IN_START[n] + _IN_WIDTH[n]) for n in _PROJ_ORDER])
_PROJ_START = dict(zip(_PROJ_ORDER, np.cumsum([0] + [_IN_WIDTH[n] for n in _PROJ_ORDER[:-1]]).tolist()))
PROJ_COL_BLOCKS = 3
PROJ_COLS = -(-sum(IN_SPLITS) // (PROJ_COL_BLOCKS * LANES)) * (PROJ_COL_BLOCKS * LANES)
ROW_BLOCK = 256


def _proj_cols(proj, name):
    return proj[..., _PROJ_START[name]:_PROJ_START[name] + _IN_WIDTH[name]]


def _proj_block(name, width):
    assert _PROJ_START[name] % width == 0
    return _PROJ_START[name] // width


def _modulated_norm(x, gain, scale, shift):
    y = x * lax.rsqrt(jnp.mean(x * x, axis=-1, keepdims=True) + RMS_EPS)
    return (y * gain) * (1.0 + scale) + shift


def _inproj_kernel(x_ref, gain_ref, mod_ref, w_ref, o_ref):
    h = _modulated_norm(x_ref[0], gain_ref[...], mod_ref[0, 0, 0:1, :], mod_ref[0, 0, 1:2, :])
    o_ref[0] = jnp.dot(h.astype(jnp.bfloat16), w_ref[...], preferred_element_type=jnp.float32)


def _input_projection(x_all, gain, mod, w_perm, n_ctx):
    Bsz, TT, _ = x_all.shape
    cols = PROJ_COLS // PROJ_COL_BLOCKS
    ctx_blocks = n_ctx // ROW_BLOCK
    return pl.pallas_call(
        _inproj_kernel,
        grid=(PROJ_COL_BLOCKS, Bsz, TT // ROW_BLOCK),
        in_specs=[
            pl.BlockSpec((1, ROW_BLOCK, D_MODEL), lambda n, b, r: (b, r, 0)),
            pl.BlockSpec((1, D_MODEL), lambda n, b, r: (0, 0)),
            pl.BlockSpec((1, 1, 2, D_MODEL), lambda n, b, r: (b, jnp.where(r < ctx_blocks, 0, 1), 0, 0)),
            pl.BlockSpec((D_MODEL, cols), lambda n, b, r: (0, n)),
        ],
        out_specs=pl.BlockSpec((1, ROW_BLOCK, cols), lambda n, b, r: (b, r, n)),
        out_shape=jax.ShapeDtypeStruct((Bsz, TT, PROJ_COLS), jnp.float32),
        compiler_params=pltpu.CompilerParams(dimension_semantics=("arbitrary", "arbitrary", "arbitrary"),
                                             vmem_limit_bytes=VMEM_LIMIT_BYTES),
        name="input_projection",
    )(x_all, gain.reshape(1, D_MODEL), mod, w_perm)


def _merge_kernel(att_ref, ssd_ref, ml_ref, g0_ref, g1_ref, g2_ref, wb_ref, wo_ref, x_ref, gain_ref, mod_ref,
                  xo_ref, h2_ref):
    mixed = None
    for r, (branch, gate) in enumerate(((att_ref, g0_ref), (ssd_ref, g1_ref), (ml_ref, g2_ref))):
        p = jnp.dot(branch[0].astype(jnp.bfloat16), wb_ref[r], preferred_element_type=jnp.float32)
        term = jax.nn.sigmoid(gate[0]) * p
        mixed = term if mixed is None else mixed + term
    y = jnp.dot(mixed.astype(jnp.bfloat16), wo_ref[...], preferred_element_type=jnp.float32)
    x_new = x_ref[0] + mod_ref[0, 0, 0:1, :] * y
    xo_ref[0] = x_new
    h2 = _modulated_norm(x_new, gain_ref[...], mod_ref[0, 0, 1:2, :], mod_ref[0, 0, 2:3, :])
    h2_ref[0] = h2.astype(jnp.bfloat16)


def _merge(att, ssd, ml, proj, w_branch, w_out, x_all, gain, mod, n_ctx):
    Bsz, TT, _ = x_all.shape
    ctx_blocks = n_ctx // ROW_BLOCK
    row = lambda w, j=0: pl.BlockSpec((1, ROW_BLOCK, w), lambda b, r: (b, r, j))
    g_first = _proj_block('gpre', D_MODEL)
    return pl.pallas_call(
        _merge_kernel,
        grid=(Bsz, TT // ROW_BLOCK),
        in_specs=[
            row(BRANCH_W), row(BRANCH_W), row(BRANCH_W),
            row(D_MODEL, g_first), row(D_MODEL, g_first + 1), row(D_MODEL, g_first + 2),
            pl.BlockSpec((N_BRANCH, BRANCH_W, D_MODEL), lambda b, r: (0, 0, 0)),
            pl.BlockSpec((D_MODEL, D_MODEL), lambda b, r: (0, 0)),
            row(D_MODEL),
            pl.BlockSpec((1, D_MODEL), lambda b, r: (0, 0)),
            pl.BlockSpec((1, 1, 3, D_MODEL), lambda b, r: (b, jnp.where(r < ctx_blocks, 0, 1), 0, 0)),
        ],
        out_specs=[row(D_MODEL), row(D_MODEL)],
        out_shape=[jax.ShapeDtypeStruct((Bsz, TT, D_MODEL), jnp.float32),
                   jax.ShapeDtypeStruct((Bsz, TT, D_MODEL), jnp.bfloat16)],
        compiler_params=pltpu.CompilerParams(dimension_semantics=("arbitrary", "arbitrary"),
                                             vmem_limit_bytes=VMEM_LIMIT_BYTES),
        name="merge",
    )(att, ssd, ml, proj, proj, proj, w_branch.astype(jnp.bfloat16), w_out.astype(jnp.bfloat16),
      x_all, gain.reshape(1, D_MODEL), mod)


def _rmsnorm(x, g):
    xf = x.astype(jnp.float32)
    y = xf * lax.rsqrt(jnp.mean(xf * xf, axis=-1, keepdims=True) + RMS_EPS)
    return (y * g.astype(jnp.float32)).astype(x.dtype)


def _heads(t, n):
    return t.reshape(t.shape[0], t.shape[1], n, -1)


def _axial_rope(rows):
    n_freq = ATT_HEAD_DIM // 4
    inv = ROPE_THETA ** (-jnp.arange(n_freq, dtype=jnp.float32) / n_freq)
    row = jnp.broadcast_to(jnp.arange(rows, dtype=jnp.float32)[:, None], (rows, GRID_W)).reshape(-1)
    col = jnp.broadcast_to(jnp.arange(GRID_W, dtype=jnp.float32)[None, :], (rows, GRID_W)).reshape(-1)
    ang = jnp.concatenate([row[:, None] * inv, col[:, None] * inv], axis=-1)
    return jnp.cos(ang), jnp.sin(ang)


def _apply_rope(t, cos, sin):
    half = t.shape[-1] // 2
    tf = t.astype(jnp.float32)
    t1, t2 = tf[..., :half], tf[..., half:]
    c, s = cos[:, None, :], sin[:, None, :]
    return jnp.concatenate([t1 * c - t2 * s, t1 * s + t2 * c], axis=-1).astype(t.dtype)


ATT_KV_CHUNK = 768


def _latent_attention(q, k, v, kc, vc):
    kk = jnp.concatenate([k, kc], axis=1)
    vv = jnp.concatenate([v, vc], axis=1)
    assert kk.shape[1] % ATT_KV_CHUNK == 0
    return _attention(q, kk, vv, ATT_KV_CHUNK)


def _context_attention(qc, kc, vc):
    return _attention(qc, kc, vc, kc.shape[1])


def _conv_kernel(prev_ref, cur_ref, next_ref, w_ref, b_ref, o_ref, *, ctx_blocks):
    r = pl.program_id(1)
    rows = cur_ref.shape[1]
    half = SSD_CONV // 2
    has_prev = jnp.logical_and(r != 0, r != ctx_blocks).astype(jnp.float32)
    has_next = jnp.logical_and(r != ctx_blocks - 1, r != pl.num_programs(1) - 1).astype(jnp.float32)
    cur = cur_ref[0]
    prev = prev_ref[0] * has_prev
    nxt = next_ref[0] * has_next
    row = lax.broadcasted_iota(jnp.int32, cur.shape, 0)
    acc = cur * w_ref[half:half + 1, :] + b_ref[...]
    for d in range(1, half + 1):
        back = jnp.where(row < d, pltpu.roll(prev, d, 0), pltpu.roll(cur, d, 0))
        fwd = jnp.where(row >= rows - d, pltpu.roll(nxt, rows - d, 0), pltpu.roll(cur, rows - d, 0))
        acc = acc + back * w_ref[half - d:half - d + 1, :] + fwd * w_ref[half + d:half + d + 1, :]
    o_ref[0] = acc * jax.nn.sigmoid(acc)


def _conv_silu(proj, conv_w, conv_b, n_ctx):
    Bsz, TT, _ = proj.shape
    assert n_ctx == ROW_BLOCK
    n_blocks = TT // ROW_BLOCK
    col = _proj_block('sxbc', SSD_XBC)
    spec = lambda f: pl.BlockSpec((1, ROW_BLOCK, SSD_XBC), lambda b, r: (b, f(r), col))
    return pl.pallas_call(
        functools.partial(_conv_kernel, ctx_blocks=n_ctx // ROW_BLOCK),
        grid=(Bsz, n_blocks),
        in_specs=[spec(lambda r: jnp.maximum(r - 1, 0)), spec(lambda r: r),
                  spec(lambda r: jnp.minimum(r + 1, n_blocks - 1)),
                  pl.BlockSpec((SSD_CONV, SSD_XBC), lambda b, r: (0, 0)),
                  pl.BlockSpec((1, SSD_XBC), lambda b, r: (0, 0))],
        out_specs=pl.BlockSpec((1, ROW_BLOCK, SSD_XBC), lambda b, r: (b, r, 0)),
        out_shape=jax.ShapeDtypeStruct((Bsz, TT, SSD_XBC), jnp.float32),
        compiler_params=pltpu.CompilerParams(dimension_semantics=("arbitrary", "arbitrary"),
                                             vmem_limit_bytes=VMEM_LIMIT_BYTES),
        name="conv_silu",
    )(proj, proj, proj, conv_w, conv_b.reshape(1, SSD_XBC))


def _token_mixer(proj, n_ctx, qn, kn, conv_w, conv_b, dt_bias, a_log, d_skip, ssd_norm_g,
                 ml_bias, ml_norm_g, cos, sin, ctx_out):
    Bsz = proj.shape[0]
    cat = lambda a, b: jnp.concatenate([a, b], axis=1)
    aq, ak = _proj_cols(proj, 'aq'), _proj_cols(proj, 'ak')
    av = lax.optimization_barrier(_proj_cols(proj, 'av'))
    q = _apply_rope(_rmsnorm(_heads(aq[:, n_ctx:], ATT_HEADS), qn), cos, sin)
    k = _apply_rope(_rmsnorm(_heads(ak[:, n_ctx:], ATT_KV_HEADS), kn), cos, sin)
    kc = _rmsnorm(_heads(ak[:, :n_ctx], ATT_KV_HEADS), kn)
    vc = _heads(av[:, :n_ctx], ATT_KV_HEADS)
    att = _latent_attention(q, k, _heads(av[:, n_ctx:], ATT_KV_HEADS), kc, vc)
    if ctx_out:
        att_c = _context_attention(_rmsnorm(_heads(aq[:, :n_ctx], ATT_HEADS), qn), kc, vc)
    else:
        att_c = jnp.zeros((Bsz, n_ctx, att.shape[-1]), att.dtype)
    xbc_act = _conv_silu(proj, conv_w, conv_b, n_ctx)
    ssd = _ssd_block((proj, _proj_block('sz', SSD_INNER)), xbc_act, _proj_cols(proj, 'sdt'),
                     dt_bias, a_log, d_skip, ssd_norm_g, n_ctx)
    blk = lambda name: (proj, _proj_block(name, ML_INNER))
    ml = _mlstm_block(blk('mq'), blk('mk'), blk('mv'), blk('mo'), _proj_cols(proj, 'mif'),
                      ml_bias, ml_norm_g, n_ctx)
    return cat(att_c, att), ssd, ml


def kernel(x, c, ctx, c_ctx, norm1_g, norm2_g, w_mod, b_mod, w_in, att_qnorm, att_knorm, ssd_conv_w, ssd_conv_b, ssd_dt_bias, ssd_a_log, ssd_d, ssd_norm, ml_gate_bias, ml_norm, w_branch, w_out, peer_wq, peer_subkeys, peer_u, peer_v):
    Bsz, S, _ = x.shape
    depth = w_in.shape[0]
    cos, sin = _axial_rope(S // GRID_W)
    silu_c = jax.nn.silu(c)
    silu_cc = jax.nn.silu(c_ctx)
    xc = ctx
    n_ctx = ctx.shape[1]
    for l in range(depth):
        last = l == depth - 1
        m = (silu_c @ w_mod[l] + b_mod[l]).reshape(Bsz, 6, D_MODEL)
        mc = jnp.broadcast_to((silu_cc @ w_mod[l] + b_mod[l]).reshape(1, 6, D_MODEL), (Bsz, 6, D_MODEL))
        both = jnp.stack([mc, m], axis=1)
        x_all = jnp.concatenate([xc, x], axis=1)
        w_perm = jnp.concatenate(
            [w_in[l][:, _IN_START[n]:_IN_START[n] + _IN_WIDTH[n]] for n in _PROJ_ORDER]
            + [jnp.zeros((D_MODEL, PROJ_COLS - sum(IN_SPLITS)), w_in.dtype)], axis=1).astype(jnp.bfloat16)
        proj = _input_projection(x_all, norm1_g[l], both[:, :, np.array([1, 0])], w_perm, n_ctx)
        att, ssd, ml = _token_mixer(proj, n_ctx, att_qnorm[l], att_knorm[l], ssd_conv_w[l], ssd_conv_b[l],
                                    ssd_dt_bias[l], ssd_a_log[l], ssd_d[l], ssd_norm[l], ml_gate_bias[l],
                                    ml_norm[l], cos, sin, not last)
        x_all, h2 = _merge(att, ssd, ml, proj, w_branch[l], w_out[l], x_all, norm2_g[l],
                           both[:, :, np.array([2, 4, 3])], n_ctx)
        u_bf16 = peer_u[l].astype(jnp.bfloat16)
        vt_bf16 = peer_v[l].astype(jnp.bfloat16).reshape(PEER_EXPERTS // PEER_EBLK, PEER_EBLK, D_MODEL)
        vt_bf16 = vt_bf16.transpose(0, 2, 1)
        po = _peer(h2[:, n_ctx:].reshape(-1, D_MODEL), peer_wq[l], peer_subkeys[l], u_bf16, vt_bf16)
        x = x_all[:, n_ctx:] + m[:, None, 5] * po.reshape(x.shape)
        if not last:
            pc = _peer(h2[:, :n_ctx].reshape(-1, D_MODEL), peer_wq[l], peer_subkeys[l], u_bf16, vt_bf16)
            xc = x_all[:, :n_ctx] + mc[:, None, 5] * pc.reshape(xc.shape)
    return x
```

```python
import functools
import math

import jax
import jax.numpy as jnp
import numpy as np
from jax import lax
from jax.experimental import pallas as pl
from jax.experimental.pallas import tpu as pltpu

D_MODEL = 1024
GRID_W = 64
RMS_EPS = 1e-6
N_BRANCH = 3
BRANCH_W = 512
ATT_HEADS = 8
ATT_KV_HEADS = 2
ATT_HEAD_DIM = 64
ATT_GROUP = ATT_HEADS // ATT_KV_HEADS
Q_BLOCK = 128
ROPE_THETA = 10000.0
SSD_HEADS = 8
SSD_HEAD_DIM = 64
SSD_INNER = SSD_HEADS * SSD_HEAD_DIM
SSD_GROUPS = 2
SSD_STATE = 64
SSD_XBC = SSD_INNER + 2 * SSD_GROUPS * SSD_STATE
SSD_CONV = 5
SSD_CHUNK = 128
ML_HEADS = 4
ML_HEAD_DIM = 128
ML_INNER = ML_HEADS * ML_HEAD_DIM
ML_CHUNK = 128
PEER_HEADS = 8
PEER_KEYS = 128
PEER_EXPERTS = PEER_KEYS * PEER_KEYS
PEER_TOPK = 16
PEER_QDIM = 256
PEER_HALF = PEER_QDIM // 2
IN_SPLITS = (ATT_HEADS * ATT_HEAD_DIM, ATT_KV_HEADS * ATT_HEAD_DIM, ATT_KV_HEADS * ATT_HEAD_DIM,
             SSD_INNER, SSD_XBC, 2 * SSD_HEADS,
             ML_INNER, ML_INNER, ML_INNER, ML_INNER, 4 * ML_HEADS,
             N_BRANCH * D_MODEL)

LANES = 128
SUBLANES = 8
VMEM_LIMIT_BYTES = 56 * 1024 * 1024

ROUTER_TOKENS = 256
ROUTER_CHUNK = LANES
PEER_TOKENS = 1024
PEER_TOKENS_SMALL = 512
PEER_ROWS = 8
PEER_EBLK = PEER_ROWS * PEER_KEYS
PEER_CHUNK = 256

NEG_INF = float("-inf")


def _sort_network(n):
    def merge(lo, hi, r):
        step = r * 2
        if step < hi - lo:
            yield from merge(lo, hi, step)
            yield from merge(lo + r, hi, step)
            yield from [(i, i + r) for i in range(lo + r, hi - r, step)]
        else:
            yield (lo, lo + r)

    def sort(lo, hi):
        if hi - lo >= 1:
            mid = lo + (hi - lo) // 2
            yield from sort(lo, mid)
            yield from sort(mid + 1, hi)
            yield from merge(lo, hi, 1)

    return tuple(sort(0, n - 1))


_SORT16 = _sort_network(PEER_TOPK)


def _merge_top(lists, n):
    lists = list(lists)
    depth = len(lists)
    sub = lax.broadcasted_iota(jnp.int32, lists[0].shape, 0)
    tops = []
    for i in range(n):
        head = lists[0]
        m = jnp.max(head, axis=0, keepdims=True)
        tops.append(m)
        live = min(depth, n - 1 - i)
        if live == 0:
            break
        first = jnp.min(jnp.where(head == m, sub, SUBLANES), axis=0, keepdims=True)
        pop = sub == first
        for r in range(live):
            nxt = lists[r + 1] if r + 1 < depth else NEG_INF
            lists[r] = jnp.where(pop, nxt, lists[r])
    return tops


def _sorted_top(s, n):
    xs = [s[r * SUBLANES:(r + 1) * SUBLANES, :] for r in range(s.shape[0] // SUBLANES)]
    for i, j in _SORT16:
        xs[i], xs[j] = jnp.maximum(xs[i], xs[j]), jnp.minimum(xs[i], xs[j])
    return _merge_top(xs, n)


def _prefix_count(values, pred):
    n = len(values)
    steps = [n >> (i + 1) for i in range(n.bit_length() - 1)]
    taken = []

    def pivot(level, base, step):
        if level == len(taken):
            return values[base + step - 1]
        return jnp.where(taken[level], pivot(level + 1, base + steps[level], step), pivot(level + 1, base, step))

    count = None
    for step in steps:
        ok = pred(pivot(0, 0, step))
        taken.append(ok)
        term = jnp.where(ok, float(step), 0.0)
        count = term if count is None else count + term
    return jnp.where(pred(values[n - 1]), float(n), count)


def _rows_to_sublanes(rows, first_sublane, shape):
    sub = lax.broadcasted_iota(jnp.int32, shape, 0)
    out = jnp.zeros(shape, jnp.float32)
    for k, row in enumerate(rows):
        out = jnp.where(sub == first_sublane + k, row, out)
    return out


_CAND_LEN = (16, 8, 5, 4, 12, 4, 1, 0)


def _router_kernel(h_ref, wq_ref, sk_ref, rank_ref, e2_ref, cnt_ref, e1_ref, q_scr):
    q_scr[...] = jnp.dot(h_ref[...], wq_ref[...], preferred_element_type=jnp.float32).astype(jnp.bfloat16)
    nt = (((1,), (1,)), ((), ()))
    shape8 = (SUBLANES, ROUTER_CHUNK)
    sub = lax.broadcasted_iota(jnp.int32, shape8, 0)
    cand_len = jnp.zeros(shape8, jnp.int32)
    for g, n in enumerate(_CAND_LEN):
        cand_len = jnp.where(sub == g, n, cand_len)
    for h in range(PEER_HEADS):
        q1 = q_scr[:, (2 * h) * PEER_HALF:(2 * h + 1) * PEER_HALF]
        q2 = q_scr[:, (2 * h + 1) * PEER_HALF:(2 * h + 2) * PEER_HALF]
        s1 = lax.dot_general(sk_ref[2 * h], q1, nt, preferred_element_type=jnp.float32)
        s2 = lax.dot_general(sk_ref[2 * h + 1], q2, nt, preferred_element_type=jnp.float32)
        for c in range(ROUTER_TOKENS // ROUTER_CHUNK):
            cs = slice(c * ROUTER_CHUNK, (c + 1) * ROUTER_CHUNK)
            s1c = s1[:, cs]
            s2c = s2[:, cs]
            top1 = _sorted_top(s1c, PEER_TOPK)
            top2 = _sorted_top(s2c, PEER_TOPK)
            a_lo = _rows_to_sublanes(top1[:4], 0, shape8)
            b_lo = _rows_to_sublanes(top2[:3], 4, shape8)
            cands = []
            for r in range(PEER_TOPK):
                by_a = a_lo + top2[r]
                by_b = (top1[4 + r] + b_lo) if 4 + r < PEER_TOPK else by_a
                cands.append(jnp.where(r < cand_len, jnp.where(sub < 4, by_a, by_b), NEG_INF))
            tau = _merge_top(cands, PEER_TOPK)[-1]
            m1, m2 = top1[0], top2[0]
            z = jnp.zeros(shape8, jnp.float32)
            for cand in cands:
                z = z + jnp.where(cand >= tau, jnp.exp(cand - (m1 + m2)), 0.0)
            z = jnp.sum(z, axis=0, keepdims=True)
            rank = _prefix_count(top2, lambda t: t > s2c)
            cnt = _prefix_count(top2, lambda t: s1c + t >= tau)
            rank_ref[h, :, cs] = rank.astype(jnp.bfloat16)
            e2_ref[h, :, cs] = jnp.exp(s2c - m2).astype(jnp.bfloat16)
            cnt_ref[h, :, cs] = cnt
            e1_ref[h, :, cs] = jnp.exp(s1c - m1) * (0.5 / z)


def _peer_router(h2_bf16, wq, subkeys):
    n_tok = h2_bf16.shape[0]
    sk = subkeys.reshape(PEER_HEADS * 2, PEER_KEYS, PEER_HALF)
    shape = (PEER_HEADS, PEER_KEYS, n_tok)
    spec = pl.BlockSpec((PEER_HEADS, PEER_KEYS, ROUTER_TOKENS), lambda i: (0, 0, i))
    return pl.pallas_call(
        _router_kernel,
        grid=(n_tok // ROUTER_TOKENS,),
        in_specs=[
            pl.BlockSpec((ROUTER_TOKENS, D_MODEL), lambda i: (i, 0)),
            pl.BlockSpec((D_MODEL, PEER_HEADS * PEER_QDIM), lambda i: (0, 0)),
            pl.BlockSpec((PEER_HEADS * 2, PEER_KEYS, PEER_HALF), lambda i: (0, 0, 0)),
        ],
        out_specs=[spec, spec, spec, spec],
        out_shape=[jax.ShapeDtypeStruct(shape, jnp.bfloat16), jax.ShapeDtypeStruct(shape, jnp.bfloat16),
                   jax.ShapeDtypeStruct(shape, jnp.float32), jax.ShapeDtypeStruct(shape, jnp.float32)],
        scratch_shapes=[pltpu.VMEM((ROUTER_TOKENS, PEER_HEADS * PEER_QDIM), jnp.bfloat16)],
        compiler_params=pltpu.CompilerParams(dimension_semantics=("arbitrary",),
                                             vmem_limit_bytes=VMEM_LIMIT_BYTES),
        name="peer_router",
    )(h2_bf16, wq.astype(jnp.bfloat16), sk.astype(jnp.bfloat16))


GELU_C0 = math.sqrt(2.0 / math.pi)
GELU_C1 = 0.044715 * GELU_C0


def _peer_dense_kernel(ht_ref, u_ref, vt_ref, rank_ref, e2_ref, cnt_ref, e1_ref, o_ref,
                       acc_scr, pre_scr, p_scr, rank_scr, e2_scr):
    j = pl.program_id(1)
    n_tok = ht_ref.shape[1]
    n_chunks = n_tok // PEER_CHUNK
    nt = (((1,), (1,)), ((), ()))

    @pl.when(j == 0)
    def _():
        acc_scr[...] = jnp.zeros_like(acc_scr)

        def relayout(t, carry):
            lanes = pl.ds(pl.multiple_of(t * LANES, LANES), LANES)
            for h in range(PEER_HEADS):
                rank = rank_ref[h, :, lanes].astype(jnp.float32)
                e2 = e2_ref[h, :, lanes].astype(jnp.float32)
                for b in range(PEER_KEYS // SUBLANES):
                    rank_scr[t, b, h] = rank[b * SUBLANES:(b + 1) * SUBLANES]
                    e2_scr[t, b, h] = e2[b * SUBLANES:(b + 1) * SUBLANES]
            return carry

        lax.fori_loop(0, n_tok // LANES, relayout, 0)

    def pre_matmul(c, parity):
        tok = pl.multiple_of(c * PEER_CHUNK, PEER_CHUNK)
        pre_scr[parity] = jnp.dot(u_ref[...], ht_ref[:, pl.ds(tok, PEER_CHUNK)],
                                  preferred_element_type=jnp.float32)

    def out_matmul(c, parity):
        tok = pl.multiple_of(c * PEER_CHUNK, PEER_CHUNK)
        acc_scr[:, pl.ds(tok, PEER_CHUNK)] += jnp.dot(vt_ref[0], p_scr[parity],
                                                      preferred_element_type=jnp.float32)

    def gates(c, parity):
        for t in range(PEER_CHUNK // LANES):
            lanes = pl.ds(pl.multiple_of(c * PEER_CHUNK + t * LANES, LANES), LANES)
            ts = slice(t * LANES, (t + 1) * LANES)
            tile = c * (PEER_CHUNK // LANES) + t
            for b in range(PEER_KEYS // SUBLANES):
                rank = [rank_scr[tile, b, h] for h in range(PEER_HEADS)]
                e2 = [e2_scr[tile, b, h] for h in range(PEER_HEADS)]
                for a in range(PEER_ROWS):
                    rs = slice(a * PEER_KEYS + b * SUBLANES, a * PEER_KEYS + (b + 1) * SUBLANES)
                    terms = []
                    for h in range(PEER_HEADS):
                        terms.append(jnp.where(rank[h] < cnt_ref[a, h:h + 1, lanes], e2[h], 0.0)
                                     * e1_ref[a, h:h + 1, lanes])
                    while len(terms) > 1:
                        terms = [x + y for x, y in zip(terms[::2], terms[1::2])]
                    x = pre_scr[parity, rs, ts]
                    act = x * (1.0 + jnp.tanh(x * (GELU_C0 + GELU_C1 * (x * x))))
                    pre_scr[parity, rs, ts] = terms[0] * act
        p_scr[parity] = pre_scr[parity].astype(jnp.bfloat16)

    p_scr[1] = jnp.zeros(p_scr.shape[1:], p_scr.dtype)
    pre_matmul(0, 0)

    def body(c, carry):
        parity = c % 2
        out_matmul(jnp.maximum(c - 1, 0), 1 - parity)
        pre_matmul(c + 1, 1 - parity)
        gates(c, parity)
        return carry

    lax.fori_loop(0, n_chunks - 1, body, 0)
    last = n_chunks - 1
    out_matmul(last - 1, 1 - last % 2)
    gates(last, last % 2)
    out_matmul(last, last % 2)

    @pl.when(j == pl.num_programs(1) - 1)
    def _():
        o_ref[...] = acc_scr[...].T


def _peer_dense(h2t_bf16, u_bf16, vt_bf16, rank, e2, cnt, e1, tb):
    n_tok = h2t_bf16.shape[1]
    row_spec = pl.BlockSpec((PEER_ROWS, PEER_HEADS, tb), lambda i, j: (j, 0, i))
    tiled = (tb // LANES, PEER_KEYS // SUBLANES, PEER_HEADS, SUBLANES, LANES)
    key_spec = pl.BlockSpec((PEER_HEADS, PEER_KEYS, tb), lambda i, j: (0, 0, i))
    return pl.pallas_call(
        _peer_dense_kernel,
        grid=(n_tok // tb, PEER_EXPERTS // PEER_EBLK),
        in_specs=[
            pl.BlockSpec((D_MODEL, tb), lambda i, j: (0, i)),
            pl.BlockSpec((PEER_EBLK, D_MODEL), lambda i, j: (j, 0)),
            pl.BlockSpec((1, D_MODEL, PEER_EBLK), lambda i, j: (j, 0, 0)),
            key_spec, key_spec, row_spec, row_spec,
        ],
        out_specs=pl.BlockSpec((tb, D_MODEL), lambda i, j: (i, 0)),
        out_shape=jax.ShapeDtypeStruct((n_tok, D_MODEL), jnp.float32),
        scratch_shapes=[pltpu.VMEM((D_MODEL, tb), jnp.float32),
                        pltpu.VMEM((2, PEER_EBLK, PEER_CHUNK), jnp.float32),
                        pltpu.VMEM((2, PEER_EBLK, PEER_CHUNK), jnp.bfloat16),
                        pltpu.VMEM(tiled, jnp.float32),
                        pltpu.VMEM(tiled, jnp.float32)],
        compiler_params=pltpu.CompilerParams(dimension_semantics=("arbitrary", "arbitrary"),
                                             vmem_limit_bytes=VMEM_LIMIT_BYTES),
        name="peer_dense",
    )(h2t_bf16, u_bf16, vt_bf16, rank, e2, cnt, e1)


def _peer(h2, wq, subkeys, u_bf16, vt_bf16):
    n_tok = h2.shape[0]
    tb = PEER_TOKENS if n_tok % PEER_TOKENS == 0 else PEER_TOKENS_SMALL
    h2_bf16 = h2.astype(jnp.bfloat16)
    rank, e2, cnt, e1 = _peer_router(h2_bf16, wq, subkeys)
    cnt, e1 = cnt.transpose(1, 0, 2), e1.transpose(1, 0, 2)
    return _peer_dense(h2_bf16.T, u_bf16, vt_bf16, rank, e2, cnt, e1, tb)


ATT_Q_TOKENS = 256
ATT_V_COLS = 2 * ATT_HEAD_DIM


def _attention_kernel(q_ref, kt_ref, v_ref, o_ref, *, kv_chunk):
    tq = q_ref.shape[2]
    rows = ATT_GROUP * tq
    q = q_ref[0].reshape(rows, ATT_HEAD_DIM)
    n_chunks = kt_ref.shape[3] // kv_chunk

    def body(c, carry):
        m, acc = carry
        off = pl.multiple_of(c * kv_chunk, kv_chunk)
        s = jnp.dot(q, kt_ref[0, 0, :, pl.ds(off, kv_chunk)], preferred_element_type=jnp.float32)
        m_new = jnp.maximum(m, jnp.max(s, axis=-1, keepdims=True))
        p = jnp.exp(s - m_new).astype(jnp.bfloat16)
        acc = jnp.exp(m - m_new) * acc + jnp.dot(p, v_ref[0, 0, pl.ds(off, kv_chunk), :],
                                                 preferred_element_type=jnp.float32)
        return m_new, acc

    m0 = jnp.full((rows, 1), NEG_INF, jnp.float32)
    acc0 = jnp.zeros((rows, ATT_V_COLS), jnp.float32)
    _, acc = lax.fori_loop(0, n_chunks, body, (m0, acc0), unroll=True)
    out = acc[:, :ATT_HEAD_DIM] / acc[:, ATT_HEAD_DIM:ATT_HEAD_DIM + 1]
    o_ref[0] = out.reshape(ATT_GROUP, tq, ATT_HEAD_DIM)


def _attention(q, k, v, kv_chunk):
    Bsz, S = q.shape[0], q.shape[1]
    SK = k.shape[1]
    tq = min(ATT_Q_TOKENS, S)
    qh = (q * ATT_HEAD_DIM ** -0.5).astype(jnp.bfloat16).transpose(0, 2, 1, 3)
    kt = k.astype(jnp.bfloat16).transpose(0, 2, 3, 1)
    pad = jnp.concatenate([jnp.ones(v.shape[:-1] + (1,), v.dtype),
                           jnp.zeros(v.shape[:-1] + (ATT_V_COLS - ATT_HEAD_DIM - 1,), v.dtype)], axis=-1)
    vx = jnp.concatenate([v, pad], axis=-1).astype(jnp.bfloat16).transpose(0, 2, 1, 3)
    o = pl.pallas_call(
        functools.partial(_attention_kernel, kv_chunk=kv_chunk),
        grid=(Bsz, ATT_KV_HEADS, S // tq),
        in_specs=[
            pl.BlockSpec((1, ATT_GROUP, tq, ATT_HEAD_DIM), lambda b, g, i: (b, g, i, 0)),
            pl.BlockSpec((1, 1, ATT_HEAD_DIM, SK), lambda b, g, i: (b, g, 0, 0)),
            pl.BlockSpec((1, 1, SK, ATT_V_COLS), lambda b, g, i: (b, g, 0, 0)),
        ],
        out_specs=pl.BlockSpec((1, ATT_GROUP, tq, ATT_HEAD_DIM), lambda b, g, i: (b, g, i, 0)),
        out_shape=jax.ShapeDtypeStruct((Bsz, ATT_HEADS, S, ATT_HEAD_DIM), jnp.float32),
        compiler_params=pltpu.CompilerParams(dimension_semantics=("arbitrary", "arbitrary", "arbitrary"),
                                             vmem_limit_bytes=VMEM_LIMIT_BYTES),
        name="attention",
    )(qh, kt, vx)
    return o.transpose(0, 2, 1, 3).reshape(Bsz, S, ATT_HEADS * ATT_HEAD_DIM)


SCAN_STEP_CHUNKS = 1

SSD_GROUP_HEADS = SSD_HEADS // SSD_GROUPS
SSD_GROUP_W = SSD_GROUP_HEADS * SSD_HEAD_DIM
SSD_GATE_COLS = LANES


def _ssd_kernel(*refs, reverse, combine):
    if combine:
        x_ref, bm_ref, cm_ref, grow_ref, gcol_ref, other_ref, z_ref, skip_ref, gain_ref, out_ref, s_scr = refs
    else:
        x_ref, bm_ref, cm_ref, grow_ref, gcol_ref, out_ref, s_scr = refs
    Q, P, N, H = SSD_CHUNK, SSD_HEAD_DIM, SSD_STATE, SSD_HEADS

    @pl.when(pl.program_id(1) == 0)
    def _():
        s_scr[...] = jnp.zeros_like(s_scr)

    row = lax.broadcasted_iota(jnp.int32, (Q, Q), 0)
    col = lax.broadcasted_iota(jnp.int32, (Q, Q), 1)
    not_after = (col >= row) if reverse else (col <= row)
    tri = not_after.astype(jnp.float32)
    last = 0 if reverse else Q - 1
    nt = (((1,), (1,)), ((), ()))
    grow = grow_ref[0]
    gcol = gcol_ref[0]
    a_cols = jnp.dot(tri, gcol, preferred_element_type=jnp.float32, precision=lax.Precision.HIGHEST)
    a_rows = lax.dot_general(grow, tri, nt, preferred_element_type=jnp.float32,
                             precision=lax.Precision.HIGHEST)
    ys = []
    for g in range(SSD_GROUPS):
        ns = slice(g * N, (g + 1) * N)
        bm = bm_ref[0, :, ns].astype(jnp.bfloat16)
        cm = cm_ref[0, :, ns].astype(jnp.bfloat16)
        cb = lax.dot_general(cm, bm, nt, preferred_element_type=jnp.float32)
        state = s_scr[g]
        y_off = jnp.dot(cm, state.astype(jnp.bfloat16), preferred_element_type=jnp.float32)
        xw, decay = [], []
        for r in range(SSD_GROUP_HEADS):
            h = g * SSD_GROUP_HEADS + r
            hs = slice(h * P, (h + 1) * P)
            aq = a_cols[:, H + h:H + h + 1]
            as_ = a_rows[H + h:H + h + 1, :]
            a_tot = as_[:, last:last + 1]
            xdt = x_ref[0, :, hs] * gcol[:, h:h + 1]
            lmat = jnp.exp(jnp.where(not_after, aq - as_, NEG_INF))
            y = jnp.dot((cb * lmat).astype(jnp.bfloat16), xdt.astype(jnp.bfloat16),
                        preferred_element_type=jnp.float32)
            ys.append(y + y_off[:, r * P:(r + 1) * P] * jnp.exp(aq))
            xw.append(jnp.exp(a_tot - aq) * xdt)
            decay.append(jnp.broadcast_to(jnp.exp(a_tot), (1, P)))
        s_loc = lax.dot_general(bm, jnp.concatenate(xw, axis=1).astype(jnp.bfloat16),
                                (((0,), (0,)), ((), ())), preferred_element_type=jnp.float32)
        s_scr[g] = state * jnp.concatenate(decay, axis=1) + s_loc
    y = jnp.concatenate(ys, axis=1)
    if combine:
        x = x_ref[0]
        z = z_ref[0]
        tot = (y + other_ref[0] + x * skip_ref[...]) * (z * jax.nn.sigmoid(z))
        out_ref[0] = tot * lax.rsqrt(jnp.mean(tot * tot, axis=1, keepdims=True) + RMS_EPS) * gain_ref[...]
    else:
        out_ref[0] = y


def _ssd_direction(xbc, grow, gcol, n_ctx_chunks, reverse, extra=None):
    Bsz, TT, _ = xbc.shape
    n_chunks = TT // SSD_CHUNK
    if reverse:
        chunk = lambda c: jnp.where(c < n_ctx_chunks, n_ctx_chunks - 1 - c, n_chunks - 1 + n_ctx_chunks - c)
    else:
        chunk = lambda c: c
    gn = SSD_GROUPS * SSD_STATE
    seq_spec = pl.BlockSpec((1, SSD_CHUNK, SSD_INNER), lambda b, c: (b, chunk(c), 0))
    vec_spec = pl.BlockSpec((1, SSD_INNER), lambda b, c: (0, 0))
    in_specs = [seq_spec,
                pl.BlockSpec((1, SSD_CHUNK, gn), lambda b, c: (b, chunk(c), SSD_INNER // gn)),
                pl.BlockSpec((1, SSD_CHUNK, gn), lambda b, c: (b, chunk(c), SSD_INNER // gn + 1)),
                pl.BlockSpec((1, 2 * SSD_HEADS, SSD_CHUNK), lambda b, c: (b, 0, chunk(c))),
                pl.BlockSpec((1, SSD_CHUNK, SSD_GATE_COLS), lambda b, c: (b, chunk(c), 0))]
    args = [xbc, xbc, xbc, grow, gcol]
    if extra is not None:
        other, z, skip, gain = extra
        z_spec = pl.BlockSpec((1, SSD_CHUNK, SSD_INNER), lambda b, c: (b, chunk(c), z[1]))
        in_specs += [seq_spec, z_spec, vec_spec, vec_spec]
        args += [other, z[0], skip, gain]
    return pl.pallas_call(
        functools.partial(_ssd_kernel, reverse=reverse, combine=extra is not None),
        grid=(Bsz, n_chunks),
        in_specs=in_specs,
        out_specs=seq_spec,
        out_shape=jax.ShapeDtypeStruct((Bsz, TT, SSD_INNER), jnp.float32),
        scratch_shapes=[pltpu.VMEM((SSD_GROUPS, SSD_STATE, SSD_GROUP_W), jnp.float32)],
        compiler_params=pltpu.CompilerParams(dimension_semantics=("arbitrary", "arbitrary"),
                                             vmem_limit_bytes=VMEM_LIMIT_BYTES),
        name="ssd_bwd" if reverse else "ssd_fwd",
    )(*args)


def _ssd_block(z, xbc_act, dt_raw, dt_bias, a_log, d_skip, norm_g, n_ctx):
    Bsz, TT, _ = dt_raw.shape
    dt = jax.nn.softplus(dt_raw.reshape(Bsz, TT, 2, SSD_HEADS) + dt_bias)
    a = dt * -jnp.exp(a_log)
    gate_cols = jnp.concatenate([dt, a], axis=-1)
    skip = jnp.repeat(d_skip, SSD_HEAD_DIM).reshape(1, SSD_INNER)
    outs = None
    for direction in (1, 0):
        gc = gate_cols[:, :, direction]
        grow = gc.transpose(0, 2, 1)
        gcol = jnp.pad(gc, ((0, 0), (0, 0), (0, SSD_GATE_COLS - 2 * SSD_HEADS)))
        extra = None if direction == 1 else (outs, z, skip, norm_g.reshape(1, SSD_INNER))
        outs = _ssd_direction(xbc_act, grow, gcol, n_ctx // SSD_CHUNK, direction == 1, extra)
    return outs


ML_STATE_COLS = 2 * ML_HEAD_DIM
ML_GATE_COLS = LANES


def _mlstm_kernel(*refs, reverse, combine):
    if combine:
        q_ref, k_ref, v_ref, grow_ref, gcol_ref, other_ref, o_ref, gain_ref, out_ref, s_scr, m_scr = refs
    else:
        q_ref, k_ref, v_ref, grow_ref, gcol_ref, out_ref, s_scr, m_scr = refs
    Q, D, H = ML_CHUNK, ML_HEAD_DIM, ML_HEADS

    @pl.when(pl.program_id(1) == 0)
    def _():
        s_scr[...] = jnp.zeros_like(s_scr)
        m_scr[...] = jnp.zeros_like(m_scr)

    row = lax.broadcasted_iota(jnp.int32, (Q, Q), 0)
    col = lax.broadcasted_iota(jnp.int32, (Q, Q), 1)
    not_after = (col >= row) if reverse else (col <= row)
    tri = not_after.astype(jnp.float32)
    last = 0 if reverse else Q - 1
    nt = (((1,), (1,)), ((), ()))
    one_col = (lax.broadcasted_iota(jnp.int32, (Q, D), 1) == 0).astype(jnp.float32)
    states = [s_scr[h] for h in range(H)]
    ms = [m_scr[h, 0:1, 0:1] for h in range(H)]
    chunks = range(SCAN_STEP_CHUNKS)
    for sub in (reversed(chunks) if reverse else chunks):
        rows = slice(sub * Q, (sub + 1) * Q)
        grow = grow_ref[0, :, rows]
        gcol = gcol_ref[0, rows, :]
        b_cols = jnp.dot(tri, gcol, preferred_element_type=jnp.float32, precision=lax.Precision.HIGHEST)
        b_rows = lax.dot_general(grow, tri, nt, preferred_element_type=jnp.float32,
                                 precision=lax.Precision.HIGHEST)
        for h in range(H):
            hs = slice(h * D, (h + 1) * D)
            q = q_ref[0, rows, hs].astype(jnp.bfloat16)
            k = (k_ref[0, rows, hs] * D ** -0.5).astype(jnp.bfloat16)
            v_ext = jnp.concatenate([v_ref[0, rows, hs], one_col], axis=1)
            bq = b_cols[:, H + h:H + h + 1]
            bs = b_rows[H + h:H + h + 1, :]
            i_q = gcol[:, h:h + 1]
            i_s = grow[h:h + 1, :]
            b_last = bs[:, last:last + 1]
            m0, state = ms[h], states[h]
            dm = jnp.where(not_after, bq - bs + i_s, NEG_INF)
            inter = bq + m0
            m_t = jnp.maximum(inter, jnp.max(dm, axis=1, keepdims=True))
            s_mat = lax.dot_general(q, k, nt, preferred_element_type=jnp.float32) * jnp.exp(dm - m_t)
            num = (jnp.dot(s_mat.astype(jnp.bfloat16), v_ext.astype(jnp.bfloat16),
                           preferred_element_type=jnp.float32)
                   + jnp.exp(inter - m_t) * jnp.dot(q, state.astype(jnp.bfloat16),
                                                    preferred_element_type=jnp.float32))
            den = num[:, D:D + 1]
            hout = num[:, :D] / jnp.maximum(jnp.abs(den), jnp.exp(-m_t))
            g_q = b_last - bq + i_q
            m_loc = jnp.max(g_q, axis=0, keepdims=True)
            wv = (jnp.exp(g_q - m_loc) * v_ext).astype(jnp.bfloat16)
            s_loc = lax.dot_general(k, wv, (((0,), (0,)), ((), ())), preferred_element_type=jnp.float32)
            m_new = jnp.maximum(b_last + m0, m_loc)
            states[h] = jnp.exp(b_last + m0 - m_new) * state + jnp.exp(m_loc - m_new) * s_loc
            ms[h] = m_new
            if combine:
                tot = hout + other_ref[0, rows, hs]
                y = tot * lax.rsqrt(jnp.mean(tot * tot, axis=1, keepdims=True) + RMS_EPS) * gain_ref[:, hs]
                out_ref[0, rows, hs] = y * jax.nn.sigmoid(o_ref[0, rows, hs])
            else:
                out_ref[0, rows, hs] = hout
    for h in range(H):
        s_scr[h] = states[h]
        m_scr[h] = jnp.broadcast_to(ms[h], (SUBLANES, LANES))


def _mlstm_direction(q, k, v, grow, gcol, n_ctx_chunks, reverse, extra=None):
    Bsz, TT, _ = q[0].shape
    step = SCAN_STEP_CHUNKS * ML_CHUNK
    n_chunks = TT // step
    if reverse:
        chunk = lambda c: jnp.where(c < n_ctx_chunks, n_ctx_chunks - 1 - c, n_chunks - 1 + n_ctx_chunks - c)
    else:
        chunk = lambda c: c
    seq_spec = pl.BlockSpec((1, step, ML_INNER), lambda b, c: (b, chunk(c), 0))
    col_spec = lambda j: pl.BlockSpec((1, step, ML_INNER), lambda b, c: (b, chunk(c), j))
    in_specs = [col_spec(q[1]), col_spec(k[1]), col_spec(v[1]),
                pl.BlockSpec((1, 2 * ML_HEADS, step), lambda b, c: (b, 0, chunk(c))),
                pl.BlockSpec((1, step, ML_GATE_COLS), lambda b, c: (b, chunk(c), 0))]
    args = [q[0], k[0], v[0], grow, gcol]
    if extra is not None:
        other, o_pre, gain = extra
        in_specs += [seq_spec, col_spec(o_pre[1]), pl.BlockSpec((1, ML_INNER), lambda b, c: (0, 0))]
        args += [other, o_pre[0], gain]
    return pl.pallas_call(
        functools.partial(_mlstm_kernel, reverse=reverse, combine=extra is not None),
        grid=(Bsz, n_chunks),
        in_specs=in_specs,
        out_specs=seq_spec,
        out_shape=jax.ShapeDtypeStruct((Bsz, TT, ML_INNER), jnp.float32),
        scratch_shapes=[pltpu.VMEM((ML_HEADS, ML_HEAD_DIM, ML_STATE_COLS), jnp.float32),
                        pltpu.VMEM((ML_HEADS, SUBLANES, LANES), jnp.float32)],
        compiler_params=pltpu.CompilerParams(dimension_semantics=("arbitrary", "arbitrary"),
                                             vmem_limit_bytes=VMEM_LIMIT_BYTES),
        name="mlstm_bwd" if reverse else "mlstm_fwd",
    )(*args)


def _mlstm_block(q, k, v, o_pre, if_pre, gate_bias, norm_g, n_ctx):
    Bsz, TT, _ = if_pre.shape
    gates = (if_pre.transpose(0, 2, 1) + gate_bias.reshape(1, 4 * ML_HEADS, 1)).reshape(Bsz, 2, 2, ML_HEADS, TT)
    gate_rows = jnp.concatenate([gates[:, :, 0], jax.nn.log_sigmoid(gates[:, :, 1])], axis=2)
    outs = None
    for direction in (1, 0):
        grow = gate_rows[:, direction]
        gcol = jnp.pad(grow.transpose(0, 2, 1), ((0, 0), (0, 0), (0, ML_GATE_COLS - 2 * ML_HEADS)))
        extra = None if direction == 1 else (outs, o_pre, norm_g.reshape(1, ML_INNER))
        outs = _mlstm_direction(q, k, v, grow, gcol, n_ctx // (SCAN_STEP_CHUNKS * ML_CHUNK), direction == 1, extra)
    return outs


_PROJ_ORDER = ('aq', 'sz', 'mq', 'mk', 'mv', 'mo', 'gpre', 'sxbc', 'ak', 'av', 'sdt', 'mif')
_IN_NAMES = ('aq', 'ak', 'av', 'sz', 'sxbc', 'sdt', 'mq', 'mk', 'mv', 'mo', 'mif', 'gpre')
_IN_START = dict(zip(_IN_NAMES, np.cumsum((0,) + IN_SPLITS[:-1]).tolist()))
_IN_WIDTH = dict(zip(_IN_NAMES, IN_SPLITS))
_PROJ_PERM = np.concatenate([np.arange(_IN_START[n], _IN_START[n] + _IN_WIDTH[n]) for n in _PROJ_ORDER])
_PROJ_START = dict(zip(_PROJ_ORDER, np.cumsum([0] + [_IN_WIDTH[n] for n in _PROJ_ORDER[:-1]]).tolist()))
PROJ_COL_BLOCKS = 3
PROJ_COLS = -(-sum(IN_SPLITS) // (PROJ_COL_BLOCKS * LANES)) * (PROJ_COL_BLOCKS * LANES)
ROW_BLOCK = 256


def _proj_cols(proj, name):
    return proj[..., _PROJ_START[name]:_PROJ_START[name] + _IN_WIDTH[name]]


def _proj_block(name, width):
    assert _PROJ_START[name] % width == 0
    return _PROJ_START[name] // width


def _modulated_norm(x, gain, scale, shift):
    y = x * lax.rsqrt(jnp.mean(x * x, axis=-1, keepdims=True) + RMS_EPS)
    return (y * gain) * (1.0 + scale) + shift


def _inproj_kernel(x_ref, gain_ref, mod_ref, w_ref, o_ref):
    h = _modulated_norm(x_ref[0], gain_ref[...], mod_ref[0, 0, 0:1, :], mod_ref[0, 0, 1:2, :])
    o_ref[0] = jnp.dot(h.astype(jnp.bfloat16), w_ref[...], preferred_element_type=jnp.float32)


def _input_projection(x_all, gain, mod, w, n_ctx, col_blocks):
    Bsz, TT, _ = x_all.shape
    n_cols = w.shape[1]
    cols = n_cols // col_blocks
    ctx_blocks = n_ctx // ROW_BLOCK
    return pl.pallas_call(
        _inproj_kernel,
        grid=(col_blocks, Bsz, TT // ROW_BLOCK),
        in_specs=[
            pl.BlockSpec((1, ROW_BLOCK, D_MODEL), lambda n, b, r: (b, r, 0)),
            pl.BlockSpec((1, D_MODEL), lambda n, b, r: (0, 0)),
            pl.BlockSpec((1, 1, 2, D_MODEL), lambda n, b, r: (b, jnp.where(r < ctx_blocks, 0, 1), 0, 0)),
            pl.BlockSpec((D_MODEL, cols), lambda n, b, r: (0, n)),
        ],
        out_specs=pl.BlockSpec((1, ROW_BLOCK, cols), lambda n, b, r: (b, r, n)),
        out_shape=jax.ShapeDtypeStruct((Bsz, TT, n_cols), jnp.float32),
        compiler_params=pltpu.CompilerParams(dimension_semantics=("arbitrary", "arbitrary", "arbitrary"),
                                             vmem_limit_bytes=VMEM_LIMIT_BYTES),
        name="input_projection",
    )(x_all, gain.reshape(1, D_MODEL), mod, w)


def _merge_kernel(att_ref, ssd_ref, ml_ref, g0_ref, g1_ref, g2_ref, wb_ref, wo_ref, x_ref, gain_ref, mod_ref,
                  xo_ref, h2_ref):
    mixed = None
    for r, (branch, gate) in enumerate(((att_ref, g0_ref), (ssd_ref, g1_ref), (ml_ref, g2_ref))):
        p = jnp.dot(branch[0].astype(jnp.bfloat16), wb_ref[r], preferred_element_type=jnp.float32)
        term = jax.nn.sigmoid(gate[0]) * p
        mixed = term if mixed is None else mixed + term
    y = jnp.dot(mixed.astype(jnp.bfloat16), wo_ref[...], preferred_element_type=jnp.float32)
    x_new = x_ref[0] + mod_ref[0, 0, 0:1, :] * y
    xo_ref[0] = x_new
    h2 = _modulated_norm(x_new, gain_ref[...], mod_ref[0, 0, 1:2, :], mod_ref[0, 0, 2:3, :])
    h2_ref[0] = h2.astype(jnp.bfloat16)


def _merge(att, ssd, ml, proj, w_branch, w_out, x_all, gain, mod, n_ctx):
    Bsz, TT, _ = x_all.shape
    ctx_blocks = n_ctx // ROW_BLOCK
    row = lambda w, j=0: pl.BlockSpec((1, ROW_BLOCK, w), lambda b, r: (b, r, j))
    g_first = _proj_block('gpre', D_MODEL)
    return pl.pallas_call(
        _merge_kernel,
        grid=(Bsz, TT // ROW_BLOCK),
        in_specs=[
            row(BRANCH_W), row(BRANCH_W), row(BRANCH_W),
            row(D_MODEL, g_first), row(D_MODEL, g_first + 1), row(D_MODEL, g_first + 2),
            pl.BlockSpec((N_BRANCH, BRANCH_W, D_MODEL), lambda b, r: (0, 0, 0)),
            pl.BlockSpec((D_MODEL, D_MODEL), lambda b, r: (0, 0)),
            row(D_MODEL),
            pl.BlockSpec((1, D_MODEL), lambda b, r: (0, 0)),
            pl.BlockSpec((1, 1, 3, D_MODEL), lambda b, r: (b, jnp.where(r < ctx_blocks, 0, 1), 0, 0)),
        ],
        out_specs=[row(D_MODEL), row(D_MODEL)],
        out_shape=[jax.ShapeDtypeStruct((Bsz, TT, D_MODEL), jnp.float32),
                   jax.ShapeDtypeStruct((Bsz, TT, D_MODEL), jnp.bfloat16)],
        compiler_params=pltpu.CompilerParams(dimension_semantics=("arbitrary", "arbitrary"),
                                             vmem_limit_bytes=VMEM_LIMIT_BYTES),
        name="merge",
    )(att, ssd, ml, proj, proj, proj, w_branch.astype(jnp.bfloat16), w_out.astype(jnp.bfloat16),
      x_all, gain.reshape(1, D_MODEL), mod)


def _rmsnorm(x, g):
    xf = x.astype(jnp.float32)
    y = xf * lax.rsqrt(jnp.mean(xf * xf, axis=-1, keepdims=True) + RMS_EPS)
    return (y * g.astype(jnp.float32)).astype(x.dtype)


def _heads(t, n):
    return t.reshape(t.shape[0], t.shape[1], n, -1)


def _axial_rope(rows):
    n_freq = ATT_HEAD_DIM // 4
    inv = ROPE_THETA ** (-jnp.arange(n_freq, dtype=jnp.float32) / n_freq)
    row = jnp.broadcast_to(jnp.arange(rows, dtype=jnp.float32)[:, None], (rows, GRID_W)).reshape(-1)
    col = jnp.broadcast_to(jnp.arange(GRID_W, dtype=jnp.float32)[None, :], (rows, GRID_W)).reshape(-1)
    ang = jnp.concatenate([row[:, None] * inv, col[:, None] * inv], axis=-1)
    return jnp.cos(ang), jnp.sin(ang)


def _apply_rope(t, cos, sin):
    half = t.shape[-1] // 2
    tf = t.astype(jnp.float32)
    t1, t2 = tf[..., :half], tf[..., half:]
    c, s = cos[:, None, :], sin[:, None, :]
    return jnp.concatenate([t1 * c - t2 * s, t1 * s + t2 * c], axis=-1).astype(t.dtype)


ATT_KV_CHUNK = 768


def _latent_attention(q, k, v, kc, vc):
    kk = jnp.concatenate([k, kc], axis=1)
    vv = jnp.concatenate([v, vc], axis=1)
    assert kk.shape[1] % ATT_KV_CHUNK == 0
    return _attention(q, kk, vv, ATT_KV_CHUNK)


def _context_attention(qc, kc, vc):
    return _attention(qc, kc, vc, kc.shape[1])


def _conv_kernel(prev_ref, cur_ref, next_ref, w_ref, b_ref, o_ref, *, ctx_blocks):
    r = pl.program_id(1)
    rows = cur_ref.shape[1]
    half = SSD_CONV // 2
    has_prev = jnp.logical_and(r != 0, r != ctx_blocks).astype(jnp.float32)
    has_next = jnp.logical_and(r != ctx_blocks - 1, r != pl.num_programs(1) - 1).astype(jnp.float32)
    cur = cur_ref[0]
    prev = prev_ref[0] * has_prev
    nxt = next_ref[0] * has_next
    row = lax.broadcasted_iota(jnp.int32, cur.shape, 0)
    acc = cur * w_ref[half:half + 1, :] + b_ref[...]
    for d in range(1, half + 1):
        back = jnp.where(row < d, pltpu.roll(prev, d, 0), pltpu.roll(cur, d, 0))
        fwd = jnp.where(row >= rows - d, pltpu.roll(nxt, rows - d, 0), pltpu.roll(cur, rows - d, 0))
        acc = acc + back * w_ref[half - d:half - d + 1, :] + fwd * w_ref[half + d:half + d + 1, :]
    o_ref[0] = acc * jax.nn.sigmoid(acc)


def _conv_silu(proj, conv_w, conv_b, n_ctx):
    Bsz, TT, _ = proj.shape
    assert n_ctx == ROW_BLOCK
    n_blocks = TT // ROW_BLOCK
    col = _proj_block('sxbc', SSD_XBC)
    spec = lambda f: pl.BlockSpec((1, ROW_BLOCK, SSD_XBC), lambda b, r: (b, f(r), col))
    return pl.pallas_call(
        functools.partial(_conv_kernel, ctx_blocks=n_ctx // ROW_BLOCK),
        grid=(Bsz, n_blocks),
        in_specs=[spec(lambda r: jnp.maximum(r - 1, 0)), spec(lambda r: r),
                  spec(lambda r: jnp.minimum(r + 1, n_blocks - 1)),
                  pl.BlockSpec((SSD_CONV, SSD_XBC), lambda b, r: (0, 0)),
                  pl.BlockSpec((1, SSD_XBC), lambda b, r: (0, 0))],
        out_specs=pl.BlockSpec((1, ROW_BLOCK, SSD_XBC), lambda b, r: (b, r, 0)),
        out_shape=jax.ShapeDtypeStruct((Bsz, TT, SSD_XBC), jnp.float32),
        compiler_params=pltpu.CompilerParams(dimension_semantics=("arbitrary", "arbitrary"),
                                             vmem_limit_bytes=VMEM_LIMIT_BYTES),
        name="conv_silu",
    )(proj, proj, proj, conv_w, conv_b.reshape(1, SSD_XBC))


def _token_mixer(proj, gates, n_ctx, qn, kn, conv_w, conv_b, dt_bias, a_log, d_skip, ssd_norm_g,
                 ml_bias, ml_norm_g, cos, sin, ctx_out):
    sdt = gates[..., :_IN_WIDTH['sdt']]
    mif = gates[..., _IN_WIDTH['sdt']:_IN_WIDTH['sdt'] + _IN_WIDTH['mif']]
    Bsz = proj.shape[0]
    cat = lambda a, b: jnp.concatenate([a, b], axis=1)
    aq, ak = _proj_cols(proj, 'aq'), _proj_cols(proj, 'ak')
    av = lax.optimization_barrier(_proj_cols(proj, 'av'))
    q = _apply_rope(_rmsnorm(_heads(aq[:, n_ctx:], ATT_HEADS), qn), cos, sin)
    k = _apply_rope(_rmsnorm(_heads(ak[:, n_ctx:], ATT_KV_HEADS), kn), cos, sin)
    kc = _rmsnorm(_heads(ak[:, :n_ctx], ATT_KV_HEADS), kn)
    vc = _heads(av[:, :n_ctx], ATT_KV_HEADS)
    att = _latent_attention(q, k, _heads(av[:, n_ctx:], ATT_KV_HEADS), kc, vc)
    if ctx_out:
        att_c = _context_attention(_rmsnorm(_heads(aq[:, :n_ctx], ATT_HEADS), qn), kc, vc)
    else:
        att_c = jnp.zeros((Bsz, n_ctx, att.shape[-1]), att.dtype)
    xbc_act = _conv_silu(proj, conv_w, conv_b, n_ctx)
    ssd = _ssd_block((proj, _proj_block('sz', SSD_INNER)), xbc_act, sdt, dt_bias, a_log, d_skip, ssd_norm_g, n_ctx)
    blk = lambda name: (proj, _proj_block(name, ML_INNER))
    ml = _mlstm_block(blk('mq'), blk('mk'), blk('mv'), blk('mo'), mif, ml_bias, ml_norm_g, n_ctx)
    return cat(att_c, att), ssd, ml


def kernel(x, c, ctx, c_ctx, norm1_g, norm2_g, w_mod, b_mod, w_in, att_qnorm, att_knorm, ssd_conv_w, ssd_conv_b, ssd_dt_bias, ssd_a_log, ssd_d, ssd_norm, ml_gate_bias, ml_norm, w_branch, w_out, peer_wq, peer_subkeys, peer_u, peer_v):
    Bsz, S, _ = x.shape
    depth = w_in.shape[0]
    cos, sin = _axial_rope(S // GRID_W)
    silu_c = jax.nn.silu(c)
    silu_cc = jax.nn.silu(c_ctx)
    xc = ctx
    n_ctx = ctx.shape[1]
    for l in range(depth):
        last = l == depth - 1
        m = (silu_c @ w_mod[l] + b_mod[l]).reshape(Bsz, 6, D_MODEL)
        mc = jnp.broadcast_to((silu_cc @ w_mod[l] + b_mod[l]).reshape(1, 6, D_MODEL), (Bsz, 6, D_MODEL))
        both = jnp.stack([mc, m], axis=1)
        x_all = jnp.concatenate([xc, x], axis=1)
        w_perm = jnp.concatenate(
            [w_in[l][:, _IN_START[n]:_IN_START[n] + _IN_WIDTH[n]] for n in _PROJ_ORDER]
            + [jnp.zeros((D_MODEL, PROJ_COLS - sum(IN_SPLITS)), w_in.dtype)], axis=1).astype(jnp.bfloat16)
        mod1 = both[:, :, np.array([1, 0])]
        proj = _input_projection(x_all, norm1_g[l], mod1, w_perm, n_ctx, PROJ_COL_BLOCKS)
        w_gates = jnp.concatenate(
            [w_in[l][:, _IN_START[n]:_IN_START[n] + _IN_WIDTH[n]] for n in ('sdt', 'mif')]
            + [jnp.zeros((D_MODEL, LANES - _IN_WIDTH['sdt'] - _IN_WIDTH['mif']), w_in.dtype)], axis=1)
        gates = _input_projection(x_all, norm1_g[l], mod1, w_gates.astype(jnp.bfloat16), n_ctx, 1)
        att, ssd, ml = _token_mixer(proj, gates, n_ctx, att_qnorm[l], att_knorm[l], ssd_conv_w[l], ssd_conv_b[l],
                                    ssd_dt_bias[l], ssd_a_log[l], ssd_d[l], ssd_norm[l], ml_gate_bias[l],
                                    ml_norm[l], cos, sin, not last)
        x_all, h2 = _merge(att, ssd, ml, proj, w_branch[l], w_out[l], x_all, norm2_g[l],
                           both[:, :, np.array([2, 4, 3])], n_ctx)
        u_bf16 = peer_u[l].astype(jnp.bfloat16)
        vt_bf16 = peer_v[l].astype(jnp.bfloat16).reshape(PEER_EXPERTS // PEER_EBLK, PEER_EBLK, D_MODEL)
        vt_bf16 = vt_bf16.transpose(0, 2, 1)
        po = _peer(h2[:, n_ctx:].reshape(-1, D_MODEL), peer_wq[l], peer_subkeys[l], u_bf16, vt_bf16)
        x = x_all[:, n_ctx:] + m[:, None, 5] * po.reshape(x.shape)
        if not last:
            pc = _peer(h2[:, :n_ctx].reshape(-1, D_MODEL), peer_wq[l], peer_subkeys[l], u_bf16, vt_bf16)
            xc = x_all[:, :n_ctx] + mc[:, None, 5] * pc.reshape(xc.shape)
    return x
```

```python
import functools
import math

import jax
import jax.numpy as jnp
import numpy as np
from jax import lax
from jax.experimental import pallas as pl
from jax.experimental.pallas import tpu as pltpu

D_MODEL = 1024
GRID_W = 64
RMS_EPS = 1e-6
N_BRANCH = 3
BRANCH_W = 512
ATT_HEADS = 8
ATT_KV_HEADS = 2
ATT_HEAD_DIM = 64
ATT_GROUP = ATT_HEADS // ATT_KV_HEADS
Q_BLOCK = 128
ROPE_THETA = 10000.0
SSD_HEADS = 8
SSD_HEAD_DIM = 64
SSD_INNER = SSD_HEADS * SSD_HEAD_DIM
SSD_GROUPS = 2
SSD_STATE = 64
SSD_XBC = SSD_INNER + 2 * SSD_GROUPS * SSD_STATE
SSD_CONV = 5
SSD_CHUNK = 128
ML_HEADS = 4
ML_HEAD_DIM = 128
ML_INNER = ML_HEADS * ML_HEAD_DIM
ML_CHUNK = 128
PEER_HEADS = 8
PEER_KEYS = 128
PEER_EXPERTS = PEER_KEYS * PEER_KEYS
PEER_TOPK = 16
PEER_QDIM = 256
PEER_HALF = PEER_QDIM // 2
IN_SPLITS = (ATT_HEADS * ATT_HEAD_DIM, ATT_KV_HEADS * ATT_HEAD_DIM, ATT_KV_HEADS * ATT_HEAD_DIM,
             SSD_INNER, SSD_XBC, 2 * SSD_HEADS,
             ML_INNER, ML_INNER, ML_INNER, ML_INNER, 4 * ML_HEADS,
             N_BRANCH * D_MODEL)

LANES = 128
SUBLANES = 8
VMEM_LIMIT_BYTES = 56 * 1024 * 1024

ROUTER_TOKENS = 256
ROUTER_CHUNK = LANES
PEER_TOKENS = 1024
PEER_TOKENS_SMALL = 512
PEER_ROWS = 8
PEER_EBLK = PEER_ROWS * PEER_KEYS
PEER_CHUNK = 256

NEG_INF = float("-inf")


def _sort_network(n):
    def merge(lo, hi, r):
        step = r * 2
        if step < hi - lo:
            yield from merge(lo, hi, step)
            yield from merge(lo + r, hi, step)
            yield from [(i, i + r) for i in range(lo + r, hi - r, step)]
        else:
            yield (lo, lo + r)

    def sort(lo, hi):
        if hi - lo >= 1:
            mid = lo + (hi - lo) // 2
            yield from sort(lo, mid)
            yield from sort(mid + 1, hi)
            yield from merge(lo, hi, 1)

    return tuple(sort(0, n - 1))


_SORT16 = _sort_network(PEER_TOPK)


def _merge_top(lists, n):
    lists = list(lists)
    depth = len(lists)
    sub = lax.broadcasted_iota(jnp.int32, lists[0].shape, 0)
    tops = []
    for i in range(n):
        head = lists[0]
        m = jnp.max(head, axis=0, keepdims=True)
        tops.append(m)
        live = min(depth, n - 1 - i)
        if live == 0:
            break
        first = jnp.min(jnp.where(head == m, sub, SUBLANES), axis=0, keepdims=True)
        pop = sub == first
        for r in range(live):
            nxt = lists[r + 1] if r + 1 < depth else NEG_INF
            lists[r] = jnp.where(pop, nxt, lists[r])
    return tops


def _sorted_top(s, n):
    xs = [s[r * SUBLANES:(r + 1) * SUBLANES, :] for r in range(s.shape[0] // SUBLANES)]
    for i, j in _SORT16:
        xs[i], xs[j] = jnp.maximum(xs[i], xs[j]), jnp.minimum(xs[i], xs[j])
    return _merge_top(xs, n)


def _prefix_count(values, pred):
    n = len(values)
    steps = [n >> (i + 1) for i in range(n.bit_length() - 1)]
    taken = []

    def pivot(level, base, step):
        if level == len(taken):
            return values[base + step - 1]
        return jnp.where(taken[level], pivot(level + 1, base + steps[level], step), pivot(level + 1, base, step))

    count = None
    for step in steps:
        ok = pred(pivot(0, 0, step))
        taken.append(ok)
        term = jnp.where(ok, float(step), 0.0)
        count = term if count is None else count + term
    return jnp.where(pred(values[n - 1]), float(n), count)


def _rows_to_sublanes(rows, first_sublane, shape):
    sub = lax.broadcasted_iota(jnp.int32, shape, 0)
    out = jnp.zeros(shape, jnp.float32)
    for k, row in enumerate(rows):
        out = jnp.where(sub == first_sublane + k, row, out)
    return out


_CAND_LEN = (16, 8, 5, 4, 12, 4, 1, 0)


def _router_kernel(h_ref, wq_ref, sk_ref, rank_ref, e2_ref, cnt_ref, e1_ref, q_scr):
    q_scr[...] = jnp.dot(h_ref[...], wq_ref[...], preferred_element_type=jnp.float32).astype(jnp.bfloat16)
    nt = (((1,), (1,)), ((), ()))
    shape8 = (SUBLANES, ROUTER_CHUNK)
    sub = lax.broadcasted_iota(jnp.int32, shape8, 0)
    cand_len = jnp.zeros(shape8, jnp.int32)
    for g, n in enumerate(_CAND_LEN):
        cand_len = jnp.where(sub == g, n, cand_len)
    for h in range(PEER_HEADS):
        q1 = q_scr[:, (2 * h) * PEER_HALF:(2 * h + 1) * PEER_HALF]
        q2 = q_scr[:, (2 * h + 1) * PEER_HALF:(2 * h + 2) * PEER_HALF]
        s1 = lax.dot_general(sk_ref[2 * h], q1, nt, preferred_element_type=jnp.float32)
        s2 = lax.dot_general(sk_ref[2 * h + 1], q2, nt, preferred_element_type=jnp.float32)
        for c in range(ROUTER_TOKENS // ROUTER_CHUNK):
            cs = slice(c * ROUTER_CHUNK, (c + 1) * ROUTER_CHUNK)
            s1c = s1[:, cs]
            s2c = s2[:, cs]
            top1 = _sorted_top(s1c, PEER_TOPK)
            top2 = _sorted_top(s2c, PEER_TOPK)
            a_lo = _rows_to_sublanes(top1[:4], 0, shape8)
            b_lo = _rows_to_sublanes(top2[:3], 4, shape8)
            cands = []
            for r in range(PEER_TOPK):
                by_a = a_lo + top2[r]
                by_b = (top1[4 + r] + b_lo) if 4 + r < PEER_TOPK else by_a
                cands.append(jnp.where(r < cand_len, jnp.where(sub < 4, by_a, by_b), NEG_INF))
            tau = _merge_top(cands, PEER_TOPK)[-1]
            m1, m2 = top1[0], top2[0]
            z = jnp.zeros(shape8, jnp.float32)
            for cand in cands:
                z = z + jnp.where(cand >= tau, jnp.exp(cand - (m1 + m2)), 0.0)
            z = jnp.sum(z, axis=0, keepdims=True)
            rank = _prefix_count(top2, lambda t: t > s2c)
            cnt = _prefix_count(top2, lambda t: s1c + t >= tau)
            rank_ref[h, :, cs] = rank.astype(jnp.bfloat16)
            e2_ref[h, :, cs] = jnp.exp(s2c - m2).astype(jnp.bfloat16)
            cnt_ref[h, :, cs] = cnt
            e1_ref[h, :, cs] = jnp.exp(s1c - m1) * (0.5 / z)


def _peer_router(h2_bf16, wq, subkeys):
    n_tok = h2_bf16.shape[0]
    sk = subkeys.reshape(PEER_HEADS * 2, PEER_KEYS, PEER_HALF)
    shape = (PEER_HEADS, PEER_KEYS, n_tok)
    spec = pl.BlockSpec((PEER_HEADS, PEER_KEYS, ROUTER_TOKENS), lambda i: (0, 0, i))
    return pl.pallas_call(
        _router_kernel,
        grid=(n_tok // ROUTER_TOKENS,),
        in_specs=[
            pl.BlockSpec((ROUTER_TOKENS, D_MODEL), lambda i: (i, 0)),
            pl.BlockSpec((D_MODEL, PEER_HEADS * PEER_QDIM), lambda i: (0, 0)),
            pl.BlockSpec((PEER_HEADS * 2, PEER_KEYS, PEER_HALF), lambda i: (0, 0, 0)),
        ],
        out_specs=[spec, spec, spec, spec],
        out_shape=[jax.ShapeDtypeStruct(shape, jnp.bfloat16), jax.ShapeDtypeStruct(shape, jnp.bfloat16),
                   jax.ShapeDtypeStruct(shape, jnp.float32), jax.ShapeDtypeStruct(shape, jnp.float32)],
        scratch_shapes=[pltpu.VMEM((ROUTER_TOKENS, PEER_HEADS * PEER_QDIM), jnp.bfloat16)],
        compiler_params=pltpu.CompilerParams(dimension_semantics=("arbitrary",),
                                             vmem_limit_bytes=VMEM_LIMIT_BYTES),
        name="peer_router",
    )(h2_bf16, wq.astype(jnp.bfloat16), sk.astype(jnp.bfloat16))


GELU_C0 = math.sqrt(2.0 / math.pi)
GELU_C1 = 0.044715 * GELU_C0


def _peer_dense_kernel(ht_ref, u_ref, vt_ref, rank_ref, e2_ref, cnt_ref, e1_ref, o_ref,
                       acc_scr, pre_scr, p_scr, rank_scr, e2_scr):
    j = pl.program_id(1)
    n_tok = ht_ref.shape[1]
    n_chunks = n_tok // PEER_CHUNK
    nt = (((1,), (1,)), ((), ()))

    @pl.when(j == 0)
    def _():
        acc_scr[...] = jnp.zeros_like(acc_scr)

        def relayout(t, carry):
            lanes = pl.ds(pl.multiple_of(t * LANES, LANES), LANES)
            for h in range(PEER_HEADS):
                rank = rank_ref[h, :, lanes].astype(jnp.float32)
                e2 = e2_ref[h, :, lanes].astype(jnp.float32)
                for b in range(PEER_KEYS // SUBLANES):
                    rank_scr[t, b, h] = rank[b * SUBLANES:(b + 1) * SUBLANES]
                    e2_scr[t, b, h] = e2[b * SUBLANES:(b + 1) * SUBLANES]
            return carry

        lax.fori_loop(0, n_tok // LANES, relayout, 0)

    def pre_matmul(c, parity):
        tok = pl.multiple_of(c * PEER_CHUNK, PEER_CHUNK)
        pre_scr[parity] = jnp.dot(u_ref[...], ht_ref[:, pl.ds(tok, PEER_CHUNK)],
                                  preferred_element_type=jnp.float32)

    def out_matmul(c, parity):
        tok = pl.multiple_of(c * PEER_CHUNK, PEER_CHUNK)
        acc_scr[:, pl.ds(tok, PEER_CHUNK)] += jnp.dot(vt_ref[0], p_scr[parity],
                                                      preferred_element_type=jnp.float32)

    def gates(c, parity):
        for t in range(PEER_CHUNK // LANES):
            lanes = pl.ds(pl.multiple_of(c * PEER_CHUNK + t * LANES, LANES), LANES)
            ts = slice(t * LANES, (t + 1) * LANES)
            tile = c * (PEER_CHUNK // LANES) + t
            for b in range(PEER_KEYS // SUBLANES):
                rank = [rank_scr[tile, b, h] for h in range(PEER_HEADS)]
                e2 = [e2_scr[tile, b, h] for h in range(PEER_HEADS)]
                for a in range(PEER_ROWS):
                    rs = slice(a * PEER_KEYS + b * SUBLANES, a * PEER_KEYS + (b + 1) * SUBLANES)
                    terms = []
                    for h in range(PEER_HEADS):
                        terms.append(jnp.where(rank[h] < cnt_ref[a, h:h + 1, lanes], e2[h], 0.0)
                                     * e1_ref[a, h:h + 1, lanes])
                    while len(terms) > 1:
                        terms = [x + y for x, y in zip(terms[::2], terms[1::2])]
                    x = pre_scr[parity, rs, ts]
                    act = x * (1.0 + jnp.tanh(x * (GELU_C0 + GELU_C1 * (x * x))))
                    pre_scr[parity, rs, ts] = terms[0] * act
        p_scr[parity] = pre_scr[parity].astype(jnp.bfloat16)

    p_scr[1] = jnp.zeros(p_scr.shape[1:], p_scr.dtype)
    pre_matmul(0, 0)

    def body(c, carry):
        parity = c % 2
        out_matmul(jnp.maximum(c - 1, 0), 1 - parity)
        pre_matmul(c + 1, 1 - parity)
        gates(c, parity)
        return carry

    lax.fori_loop(0, n_chunks - 1, body, 0)
    last = n_chunks - 1
    out_matmul(last - 1, 1 - last % 2)
    gates(last, last % 2)
    out_matmul(last, last % 2)

    @pl.when(j == pl.num_programs(1) - 1)
    def _():
        o_ref[...] = acc_scr[...].T


def _peer_dense(h2t_bf16, u_bf16, vt_bf16, rank, e2, cnt, e1, tb):
    n_tok = h2t_bf16.shape[1]
    row_spec = pl.BlockSpec((PEER_ROWS, PEER_HEADS, tb), lambda i, j: (j, 0, i))
    tiled = (tb // LANES, PEER_KEYS // SUBLANES, PEER_HEADS, SUBLANES, LANES)
    key_spec = pl.BlockSpec((PEER_HEADS, PEER_KEYS, tb), lambda i, j: (0, 0, i))
    return pl.pallas_call(
        _peer_dense_kernel,
        grid=(n_tok // tb, PEER_EXPERTS // PEER_EBLK),
        in_specs=[
            pl.BlockSpec((D_MODEL, tb), lambda i, j: (0, i)),
            pl.BlockSpec((PEER_EBLK, D_MODEL), lambda i, j: (j, 0)),
            pl.BlockSpec((1, D_MODEL, PEER_EBLK), lambda i, j: (j, 0, 0)),
            key_spec, key_spec, row_spec, row_spec,
        ],
        out_specs=pl.BlockSpec((tb, D_MODEL), lambda i, j: (i, 0)),
        out_shape=jax.ShapeDtypeStruct((n_tok, D_MODEL), jnp.float32),
        scratch_shapes=[pltpu.VMEM((D_MODEL, tb), jnp.float32),
                        pltpu.VMEM((2, PEER_EBLK, PEER_CHUNK), jnp.float32),
                        pltpu.VMEM((2, PEER_EBLK, PEER_CHUNK), jnp.bfloat16),
                        pltpu.VMEM(tiled, jnp.float32),
                        pltpu.VMEM(tiled, jnp.float32)],
        compiler_params=pltpu.CompilerParams(dimension_semantics=("arbitrary", "arbitrary"),
                                             vmem_limit_bytes=VMEM_LIMIT_BYTES),
        name="peer_dense",
    )(h2t_bf16, u_bf16, vt_bf16, rank, e2, cnt, e1)


def _peer(h2, wq, subkeys, u_bf16, vt_bf16):
    n_tok = h2.shape[0]
    tb = PEER_TOKENS if n_tok % PEER_TOKENS == 0 else PEER_TOKENS_SMALL
    h2_bf16 = h2.astype(jnp.bfloat16)
    rank, e2, cnt, e1 = _peer_router(h2_bf16, wq, subkeys)
    cnt, e1 = cnt.transpose(1, 0, 2), e1.transpose(1, 0, 2)
    return _peer_dense(h2_bf16.T, u_bf16, vt_bf16, rank, e2, cnt, e1, tb)


ATT_Q_TOKENS = 256
ATT_V_COLS = 2 * ATT_HEAD_DIM


def _attention_kernel(q_ref, kt_ref, v_ref, o_ref, *, kv_chunk):
    tq = q_ref.shape[2]
    rows = ATT_GROUP * tq
    q = q_ref[0].reshape(rows, ATT_HEAD_DIM)
    n_chunks = kt_ref.shape[3] // kv_chunk

    def body(c, carry):
        m, acc = carry
        off = pl.multiple_of(c * kv_chunk, kv_chunk)
        s = jnp.dot(q, kt_ref[0, 0, :, pl.ds(off, kv_chunk)], preferred_element_type=jnp.float32)
        m_new = jnp.maximum(m, jnp.max(s, axis=-1, keepdims=True))
        p = jnp.exp(s - m_new).astype(jnp.bfloat16)
        acc = jnp.exp(m - m_new) * acc + jnp.dot(p, v_ref[0, 0, pl.ds(off, kv_chunk), :],
                                                 preferred_element_type=jnp.float32)
        return m_new, acc

    m0 = jnp.full((rows, 1), NEG_INF, jnp.float32)
    acc0 = jnp.zeros((rows, ATT_V_COLS), jnp.float32)
    _, acc = lax.fori_loop(0, n_chunks, body, (m0, acc0), unroll=True)
    out = acc[:, :ATT_HEAD_DIM] / acc[:, ATT_HEAD_DIM:ATT_HEAD_DIM + 1]
    o_ref[0] = out.reshape(ATT_GROUP, tq, ATT_HEAD_DIM)


def _attention(q, k, v, kv_chunk):
    Bsz, S = q.shape[0], q.shape[1]
    SK = k.shape[1]
    tq = min(ATT_Q_TOKENS, S)
    qh = (q * ATT_HEAD_DIM ** -0.5).astype(jnp.bfloat16).transpose(0, 2, 1, 3)
    kt = k.astype(jnp.bfloat16).transpose(0, 2, 3, 1)
    pad = jnp.concatenate([jnp.ones(v.shape[:-1] + (1,), v.dtype),
                           jnp.zeros(v.shape[:-1] + (ATT_V_COLS - ATT_HEAD_DIM - 1,), v.dtype)], axis=-1)
    vx = jnp.concatenate([v, pad], axis=-1).astype(jnp.bfloat16).transpose(0, 2, 1, 3)
    o = pl.pallas_call(
        functools.partial(_attention_kernel, kv_chunk=kv_chunk),
        grid=(Bsz, ATT_KV_HEADS, S // tq),
        in_specs=[
            pl.BlockSpec((1, ATT_GROUP, tq, ATT_HEAD_DIM), lambda b, g, i: (b, g, i, 0)),
            pl.BlockSpec((1, 1, ATT_HEAD_DIM, SK), lambda b, g, i: (b, g, 0, 0)),
            pl.BlockSpec((1, 1, SK, ATT_V_COLS), lambda b, g, i: (b, g, 0, 0)),
        ],
        out_specs=pl.BlockSpec((1, ATT_GROUP, tq, ATT_HEAD_DIM), lambda b, g, i: (b, g, i, 0)),
        out_shape=jax.ShapeDtypeStruct((Bsz, ATT_HEADS, S, ATT_HEAD_DIM), jnp.float32),
        compiler_params=pltpu.CompilerParams(dimension_semantics=("arbitrary", "arbitrary", "arbitrary"),
                                             vmem_limit_bytes=VMEM_LIMIT_BYTES),
        name="attention",
    )(qh, kt, vx)
    return o.transpose(0, 2, 1, 3).reshape(Bsz, S, ATT_HEADS * ATT_HEAD_DIM)


SCAN_STEP_CHUNKS = 1

SSD_GROUP_HEADS = SSD_HEADS // SSD_GROUPS
SSD_GROUP_W = SSD_GROUP_HEADS * SSD_HEAD_DIM
SSD_GATE_COLS = LANES


def _ssd_kernel(*refs, reverse, combine):
    if combine:
        x_ref, bm_ref, cm_ref, grow_ref, gcol_ref, other_ref, z_ref, skip_ref, gain_ref, out_ref, s_scr = refs
    else:
        x_ref, bm_ref, cm_ref, grow_ref, gcol_ref, out_ref, s_scr = refs
    Q, P, N, H = SSD_CHUNK, SSD_HEAD_DIM, SSD_STATE, SSD_HEADS

    @pl.when(pl.program_id(1) == 0)
    def _():
        s_scr[...] = jnp.zeros_like(s_scr)

    row = lax.broadcasted_iota(jnp.int32, (Q, Q), 0)
    col = lax.broadcasted_iota(jnp.int32, (Q, Q), 1)
    not_after = (col >= row) if reverse else (col <= row)
    tri = not_after.astype(jnp.float32)
    last = 0 if reverse else Q - 1
    nt = (((1,), (1,)), ((), ()))
    grow = grow_ref[0]
    gcol = gcol_ref[0]
    a_cols = jnp.dot(tri, gcol, preferred_element_type=jnp.float32, precision=lax.Precision.HIGHEST)
    a_rows = lax.dot_general(grow, tri, nt, preferred_element_type=jnp.float32,
                             precision=lax.Precision.HIGHEST)
    ys = []
    for g in range(SSD_GROUPS):
        ns = slice(g * N, (g + 1) * N)
        bm = bm_ref[0, :, ns].astype(jnp.bfloat16)
        cm = cm_ref[0, :, ns].astype(jnp.bfloat16)
        cb = lax.dot_general(cm, bm, nt, preferred_element_type=jnp.float32)
        state = s_scr[g]
        y_off = jnp.dot(cm, state.astype(jnp.bfloat16), preferred_element_type=jnp.float32)
        xw, decay = [], []
        for pair in range(SSD_GROUP_HEADS // 2):
            even = g * SSD_GROUP_HEADS + 2 * pair
            ps = slice(even * P, (even + 2) * P)
            low = lax.broadcasted_iota(jnp.int32, (Q, 2 * P), 1) < P
            per_head = lambda col0, col1: jnp.where(low, col0, col1)
            aq = [a_cols[:, H + even + i:H + even + i + 1] for i in range(2)]
            as_ = [a_rows[H + even + i:H + even + i + 1, :] for i in range(2)]
            a_tot = [a[:, last:last + 1] for a in as_]
            xdt = x_ref[0, :, ps] * per_head(gcol[:, even:even + 1], gcol[:, even + 1:even + 2])
            xdt_b = xdt.astype(jnp.bfloat16)
            y = [jnp.dot((cb * jnp.exp(jnp.where(not_after, aq[i] - as_[i], NEG_INF))).astype(jnp.bfloat16), xdt_b,
                         preferred_element_type=jnp.float32) for i in range(2)]
            aq_pair = per_head(aq[0], aq[1])
            tot_pair = jnp.where(low[0:1], a_tot[0], a_tot[1])
            ys.append(jnp.where(low, y[0], y[1]) + y_off[:, 2 * pair * P:(2 * pair + 2) * P] * jnp.exp(aq_pair))
            xw.append(jnp.exp(tot_pair - aq_pair) * xdt)
            decay.append(jnp.exp(tot_pair))
        s_loc = lax.dot_general(bm, jnp.concatenate(xw, axis=1).astype(jnp.bfloat16),
                                (((0,), (0,)), ((), ())), preferred_element_type=jnp.float32)
        s_scr[g] = state * jnp.concatenate(decay, axis=1) + s_loc
    y = jnp.concatenate(ys, axis=1)
    if combine:
        x = x_ref[0]
        z = z_ref[0]
        tot = (y + other_ref[0] + x * skip_ref[...]) * (z * jax.nn.sigmoid(z))
        out_ref[0] = tot * lax.rsqrt(jnp.mean(tot * tot, axis=1, keepdims=True) + RMS_EPS) * gain_ref[...]
    else:
        out_ref[0] = y


def _ssd_direction(xbc, grow, gcol, n_ctx_chunks, reverse, extra=None):
    Bsz, TT, _ = xbc.shape
    n_chunks = TT // SSD_CHUNK
    if reverse:
        chunk = lambda c: jnp.where(c < n_ctx_chunks, n_ctx_chunks - 1 - c, n_chunks - 1 + n_ctx_chunks - c)
    else:
        chunk = lambda c: c
    gn = SSD_GROUPS * SSD_STATE
    seq_spec = pl.BlockSpec((1, SSD_CHUNK, SSD_INNER), lambda b, c: (b, chunk(c), 0))
    vec_spec = pl.BlockSpec((1, SSD_INNER), lambda b, c: (0, 0))
    in_specs = [seq_spec,
                pl.BlockSpec((1, SSD_CHUNK, gn), lambda b, c: (b, chunk(c), SSD_INNER // gn)),
                pl.BlockSpec((1, SSD_CHUNK, gn), lambda b, c: (b, chunk(c), SSD_INNER // gn + 1)),
                pl.BlockSpec((1, 2 * SSD_HEADS, SSD_CHUNK), lambda b, c: (b, 0, chunk(c))),
                pl.BlockSpec((1, SSD_CHUNK, SSD_GATE_COLS), lambda b, c: (b, chunk(c), 0))]
    args = [xbc, xbc, xbc, grow, gcol]
    if extra is not None:
        other, z, skip, gain = extra
        z_spec = pl.BlockSpec((1, SSD_CHUNK, SSD_INNER), lambda b, c: (b, chunk(c), z[1]))
        in_specs += [seq_spec, z_spec, vec_spec, vec_spec]
        args += [other, z[0], skip, gain]
    return pl.pallas_call(
        functools.partial(_ssd_kernel, reverse=reverse, combine=extra is not None),
        grid=(Bsz, n_chunks),
        in_specs=in_specs,
        out_specs=seq_spec,
        out_shape=jax.ShapeDtypeStruct((Bsz, TT, SSD_INNER), jnp.float32),
        scratch_shapes=[pltpu.VMEM((SSD_GROUPS, SSD_STATE, SSD_GROUP_W), jnp.float32)],
        compiler_params=pltpu.CompilerParams(dimension_semantics=("arbitrary", "arbitrary"),
                                             vmem_limit_bytes=VMEM_LIMIT_BYTES),
        name="ssd_bwd" if reverse else "ssd_fwd",
    )(*args)


def _ssd_block(z, xbc_act, dt_raw, dt_bias, a_log, d_skip, norm_g, n_ctx):
    Bsz, TT, _ = dt_raw.shape
    dt = jax.nn.softplus(dt_raw.reshape(Bsz, TT, 2, SSD_HEADS) + dt_bias)
    a = dt * -jnp.exp(a_log)
    gate_cols = jnp.concatenate([dt, a], axis=-1)
    skip = jnp.repeat(d_skip, SSD_HEAD_DIM).reshape(1, SSD_INNER)
    outs = None
    for direction in (1, 0):
        gc = gate_cols[:, :, direction]
        grow = gc.transpose(0, 2, 1)
        gcol = jnp.pad(gc, ((0, 0), (0, 0), (0, SSD_GATE_COLS - 2 * SSD_HEADS)))
        extra = None if direction == 1 else (outs, z, skip, norm_g.reshape(1, SSD_INNER))
        outs = _ssd_direction(xbc_act, grow, gcol, n_ctx // SSD_CHUNK, direction == 1, extra)
    return outs


ML_STATE_COLS = 2 * ML_HEAD_DIM
ML_GATE_COLS = LANES


def _mlstm_kernel(*refs, reverse, combine):
    if combine:
        q_ref, k_ref, v_ref, grow_ref, gcol_ref, other_ref, o_ref, gain_ref, out_ref, s_scr, m_scr = refs
    else:
        q_ref, k_ref, v_ref, grow_ref, gcol_ref, out_ref, s_scr, m_scr = refs
    Q, D, H = ML_CHUNK, ML_HEAD_DIM, ML_HEADS

    @pl.when(pl.program_id(1) == 0)
    def _():
        s_scr[...] = jnp.zeros_like(s_scr)
        m_scr[...] = jnp.zeros_like(m_scr)

    row = lax.broadcasted_iota(jnp.int32, (Q, Q), 0)
    col = lax.broadcasted_iota(jnp.int32, (Q, Q), 1)
    not_after = (col >= row) if reverse else (col <= row)
    tri = not_after.astype(jnp.float32)
    last = 0 if reverse else Q - 1
    nt = (((1,), (1,)), ((), ()))
    one_col = (lax.broadcasted_iota(jnp.int32, (Q, D), 1) == 0).astype(jnp.float32)
    states = [s_scr[h] for h in range(H)]
    ms = [m_scr[h, 0:1, 0:1] for h in range(H)]
    chunks = range(SCAN_STEP_CHUNKS)
    for sub in (reversed(chunks) if reverse else chunks):
        rows = slice(sub * Q, (sub + 1) * Q)
        grow = grow_ref[0, :, rows]
        gcol = gcol_ref[0, rows, :]
        b_cols = jnp.dot(tri, gcol, preferred_element_type=jnp.float32, precision=lax.Precision.HIGHEST)
        b_rows = lax.dot_general(grow, tri, nt, preferred_element_type=jnp.float32,
                                 precision=lax.Precision.HIGHEST)
        for h in range(H):
            hs = slice(h * D, (h + 1) * D)
            q = q_ref[0, rows, hs].astype(jnp.bfloat16)
            k = (k_ref[0, rows, hs] * D ** -0.5).astype(jnp.bfloat16)
            v_ext = jnp.concatenate([v_ref[0, rows, hs], one_col], axis=1)
            bq = b_cols[:, H + h:H + h + 1]
            bs = b_rows[H + h:H + h + 1, :]
            i_q = gcol[:, h:h + 1]
            i_s = grow[h:h + 1, :]
            b_last = bs[:, last:last + 1]
            m0, state = ms[h], states[h]
            dm = jnp.where(not_after, bq - bs + i_s, NEG_INF)
            inter = bq + m0
            m_t = jnp.maximum(inter, jnp.max(dm, axis=1, keepdims=True))
            s_mat = lax.dot_general(q, k, nt, preferred_element_type=jnp.float32) * jnp.exp(dm - m_t)
            num = (jnp.dot(s_mat.astype(jnp.bfloat16), v_ext.astype(jnp.bfloat16),
                           preferred_element_type=jnp.float32)
                   + jnp.exp(inter - m_t) * jnp.dot(q, state.astype(jnp.bfloat16),
                                                    preferred_element_type=jnp.float32))
            den = num[:, D:D + 1]
            hout = num[:, :D] / jnp.maximum(jnp.abs(den), jnp.exp(-m_t))
            g_q = b_last - bq + i_q
            m_loc = jnp.max(g_q, axis=0, keepdims=True)
            wv = (jnp.exp(g_q - m_loc) * v_ext).astype(jnp.bfloat16)
            s_loc = lax.dot_general(k, wv, (((0,), (0,)), ((), ())), preferred_element_type=jnp.float32)
            m_new = jnp.maximum(b_last + m0, m_loc)
            states[h] = jnp.exp(b_last + m0 - m_new) * state + jnp.exp(m_loc - m_new) * s_loc
            ms[h] = m_new
            if combine:
                tot = hout + other_ref[0, rows, hs]
                y = tot * lax.rsqrt(jnp.mean(tot * tot, axis=1, keepdims=True) + RMS_EPS) * gain_ref[:, hs]
                out_ref[0, rows, hs] = y * jax.nn.sigmoid(o_ref[0, rows, hs])
            else:
                out_ref[0, rows, hs] = hout
    for h in range(H):
        s_scr[h] = states[h]
        m_scr[h] = jnp.broadcast_to(ms[h], (SUBLANES, LANES))


def _mlstm_direction(q, k, v, grow, gcol, n_ctx_chunks, reverse, extra=None):
    Bsz, TT, _ = q[0].shape
    step = SCAN_STEP_CHUNKS * ML_CHUNK
    n_chunks = TT // step
    if reverse:
        chunk = lambda c: jnp.where(c < n_ctx_chunks, n_ctx_chunks - 1 - c, n_chunks - 1 + n_ctx_chunks - c)
    else:
        chunk = lambda c: c
    seq_spec = pl.BlockSpec((1, step, ML_INNER), lambda b, c: (b, chunk(c), 0))
    col_spec = lambda j: pl.BlockSpec((1, step, ML_INNER), lambda b, c: (b, chunk(c), j))
    in_specs = [col_spec(q[1]), col_spec(k[1]), col_spec(v[1]),
                pl.BlockSpec((1, 2 * ML_HEADS, step), lambda b, c: (b, 0, chunk(c))),
                pl.BlockSpec((1, step, ML_GATE_COLS), lambda b, c: (b, chunk(c), 0))]
    args = [q[0], k[0], v[0], grow, gcol]
    if extra is not None:
        other, o_pre, gain = extra
        in_specs += [seq_spec, col_spec(o_pre[1]), pl.BlockSpec((1, ML_INNER), lambda b, c: (0, 0))]
        args += [other, o_pre[0], gain]
    return pl.pallas_call(
        functools.partial(_mlstm_kernel, reverse=reverse, combine=extra is not None),
        grid=(Bsz, n_chunks),
        in_specs=in_specs,
        out_specs=seq_spec,
        out_shape=jax.ShapeDtypeStruct((Bsz, TT, ML_INNER), jnp.float32),
        scratch_shapes=[pltpu.VMEM((ML_HEADS, ML_HEAD_DIM, ML_STATE_COLS), jnp.float32),
                        pltpu.VMEM((ML_HEADS, SUBLANES, LANES), jnp.float32)],
        compiler_params=pltpu.CompilerParams(dimension_semantics=("arbitrary", "arbitrary"),
                                             vmem_limit_bytes=VMEM_LIMIT_BYTES),
        name="mlstm_bwd" if reverse else "mlstm_fwd",
    )(*args)


def _mlstm_block(q, k, v, o_pre, if_pre, gate_bias, norm_g, n_ctx):
    Bsz, TT, _ = if_pre.shape
    gates = (if_pre.transpose(0, 2, 1) + gate_bias.reshape(1, 4 * ML_HEADS, 1)).reshape(Bsz, 2, 2, ML_HEADS, TT)
    gate_rows = jnp.concatenate([gates[:, :, 0], jax.nn.log_sigmoid(gates[:, :, 1])], axis=2)
    outs = None
    for direction in (1, 0):
        grow = gate_rows[:, direction]
        gcol = jnp.pad(grow.transpose(0, 2, 1), ((0, 0), (0, 0), (0, ML_GATE_COLS - 2 * ML_HEADS)))
        extra = None if direction == 1 else (outs, o_pre, norm_g.reshape(1, ML_INNER))
        outs = _mlstm_direction(q, k, v, grow, gcol, n_ctx // (SCAN_STEP_CHUNKS * ML_CHUNK), direction == 1, extra)
    return outs


_PROJ_ORDER = ('aq', 'sz', 'mq', 'mk', 'mv', 'mo', 'gpre', 'sxbc', 'ak', 'av', 'sdt', 'mif')
_IN_NAMES = ('aq', 'ak', 'av', 'sz', 'sxbc', 'sdt', 'mq', 'mk', 'mv', 'mo', 'mif', 'gpre')
_IN_START = dict(zip(_IN_NAMES, np.cumsum((0,) + IN_SPLITS[:-1]).tolist()))
_IN_WIDTH = dict(zip(_IN_NAMES, IN_SPLITS))
_PROJ_PERM = np.concatenate([np.arange(_IN_START[n], _IN_START[n] + _IN_WIDTH[n]) for n in _PROJ_ORDER])
_PROJ_START = dict(zip(_PROJ_ORDER, np.cumsum([0] + [_IN_WIDTH[n] for n in _PROJ_ORDER[:-1]]).tolist()))
PROJ_COL_BLOCKS = 3
PROJ_COLS = -(-sum(IN_SPLITS) // (PROJ_COL_BLOCKS * LANES)) * (PROJ_COL_BLOCKS * LANES)
ROW_BLOCK = 256


def _proj_cols(proj, name):
    return proj[..., _PROJ_START[name]:_PROJ_START[name] + _IN_WIDTH[name]]


def _proj_block(name, width):
    assert _PROJ_START[name] % width == 0
    return _PROJ_START[name] // width


def _modulated_norm(x, gain, scale, shift):
    y = x * lax.rsqrt(jnp.mean(x * x, axis=-1, keepdims=True) + RMS_EPS)
    return (y * gain) * (1.0 + scale) + shift


def _inproj_kernel(x_ref, gain_ref, mod_ref, w_ref, o_ref):
    h = _modulated_norm(x_ref[0], gain_ref[...], mod_ref[0, 0, 0:1, :], mod_ref[0, 0, 1:2, :])
    o_ref[0] = jnp.dot(h.astype(jnp.bfloat16), w_ref[...], preferred_element_type=jnp.float32)


def _input_projection(x_all, gain, mod, w_perm, n_ctx):
    Bsz, TT, _ = x_all.shape
    cols = PROJ_COLS // PROJ_COL_BLOCKS
    ctx_blocks = n_ctx // ROW_BLOCK
    return pl.pallas_call(
        _inproj_kernel,
        grid=(PROJ_COL_BLOCKS, Bsz, TT // ROW_BLOCK),
        in_specs=[
            pl.BlockSpec((1, ROW_BLOCK, D_MODEL), lambda n, b, r: (b, r, 0)),
            pl.BlockSpec((1, D_MODEL), lambda n, b, r: (0, 0)),
            pl.BlockSpec((1, 1, 2, D_MODEL), lambda n, b, r: (b, jnp.where(r < ctx_blocks, 0, 1), 0, 0)),
            pl.BlockSpec((D_MODEL, cols), lambda n, b, r: (0, n)),
        ],
        out_specs=pl.BlockSpec((1, ROW_BLOCK, cols), lambda n, b, r: (b, r, n)),
        out_shape=jax.ShapeDtypeStruct((Bsz, TT, PROJ_COLS), jnp.float32),
        compiler_params=pltpu.CompilerParams(dimension_semantics=("arbitrary", "arbitrary", "arbitrary"),
                                             vmem_limit_bytes=VMEM_LIMIT_BYTES),
        name="input_projection",
    )(x_all, gain.reshape(1, D_MODEL), mod, w_perm)


def _merge_kernel(att_ref, ssd_ref, ml_ref, g0_ref, g1_ref, g2_ref, wb_ref, wo_ref, x_ref, gain_ref, mod_ref,
                  xo_ref, h2_ref):
    mixed = None
    for r, (branch, gate) in enumerate(((att_ref, g0_ref), (ssd_ref, g1_ref), (ml_ref, g2_ref))):
        p = jnp.dot(branch[0].astype(jnp.bfloat16), wb_ref[r], preferred_element_type=jnp.float32)
        term = jax.nn.sigmoid(gate[0]) * p
        mixed = term if mixed is None else mixed + term
    y = jnp.dot(mixed.astype(jnp.bfloat16), wo_ref[...], preferred_element_type=jnp.float32)
    x_new = x_ref[0] + mod_ref[0, 0, 0:1, :] * y
    xo_ref[0] = x_new
    h2 = _modulated_norm(x_new, gain_ref[...], mod_ref[0, 0, 1:2, :], mod_ref[0, 0, 2:3, :])
    h2_ref[0] = h2.astype(jnp.bfloat16)


def _merge(att, ssd, ml, proj, w_branch, w_out, x_all, gain, mod, n_ctx):
    Bsz, TT, _ = x_all.shape
    ctx_blocks = n_ctx // ROW_BLOCK
    row = lambda w, j=0: pl.BlockSpec((1, ROW_BLOCK, w), lambda b, r: (b, r, j))
    g_first = _proj_block('gpre', D_MODEL)
    return pl.pallas_call(
        _merge_kernel,
        grid=(Bsz, TT // ROW_BLOCK),
        in_specs=[
            row(BRANCH_W), row(BRANCH_W), row(BRANCH_W),
            row(D_MODEL, g_first), row(D_MODEL, g_first + 1), row(D_MODEL, g_first + 2),
            pl.BlockSpec((N_BRANCH, BRANCH_W, D_MODEL), lambda b, r: (0, 0, 0)),
            pl.BlockSpec((D_MODEL, D_MODEL), lambda b, r: (0, 0)),
            row(D_MODEL),
            pl.BlockSpec((1, D_MODEL), lambda b, r: (0, 0)),
            pl.BlockSpec((1, 1, 3, D_MODEL), lambda b, r: (b, jnp.where(r < ctx_blocks, 0, 1), 0, 0)),
        ],
        out_specs=[row(D_MODEL), row(D_MODEL)],
        out_shape=[jax.ShapeDtypeStruct((Bsz, TT, D_MODEL), jnp.float32),
                   jax.ShapeDtypeStruct((Bsz, TT, D_MODEL), jnp.bfloat16)],
        compiler_params=pltpu.CompilerParams(dimension_semantics=("arbitrary", "arbitrary"),
                                             vmem_limit_bytes=VMEM_LIMIT_BYTES),
        name="merge",
    )(att, ssd, ml, proj, proj, proj, w_branch.astype(jnp.bfloat16), w_out.astype(jnp.bfloat16),
      x_all, gain.reshape(1, D_MODEL), mod)


def _rmsnorm(x, g):
    xf = x.astype(jnp.float32)
    y = xf * lax.rsqrt(jnp.mean(xf * xf, axis=-1, keepdims=True) + RMS_EPS)
    return (y * g.astype(jnp.float32)).astype(x.dtype)


def _heads(t, n):
    return t.reshape(t.shape[0], t.shape[1], n, -1)


def _axial_rope(rows):
    n_freq = ATT_HEAD_DIM // 4
    inv = ROPE_THETA ** (-jnp.arange(n_freq, dtype=jnp.float32) / n_freq)
    row = jnp.broadcast_to(jnp.arange(rows, dtype=jnp.float32)[:, None], (rows, GRID_W)).reshape(-1)
    col = jnp.broadcast_to(jnp.arange(GRID_W, dtype=jnp.float32)[None, :], (rows, GRID_W)).reshape(-1)
    ang = jnp.concatenate([row[:, None] * inv, col[:, None] * inv], axis=-1)
    return jnp.cos(ang), jnp.sin(ang)


def _apply_rope(t, cos, sin):
    half = t.shape[-1] // 2
    tf = t.astype(jnp.float32)
    t1, t2 = tf[..., :half], tf[..., half:]
    c, s = cos[:, None, :], sin[:, None, :]
    return jnp.concatenate([t1 * c - t2 * s, t1 * s + t2 * c], axis=-1).astype(t.dtype)


ATT_KV_CHUNK = 768


def _latent_attention(q, k, v, kc, vc):
    kk = jnp.concatenate([k, kc], axis=1)
    vv = jnp.concatenate([v, vc], axis=1)
    assert kk.shape[1] % ATT_KV_CHUNK == 0
    return _attention(q, kk, vv, ATT_KV_CHUNK)


def _context_attention(qc, kc, vc):
    return _attention(qc, kc, vc, kc.shape[1])


def _conv_kernel(prev_ref, cur_ref, next_ref, w_ref, b_ref, o_ref, *, ctx_blocks):
    r = pl.program_id(1)
    rows = cur_ref.shape[1]
    half = SSD_CONV // 2
    has_prev = jnp.logical_and(r != 0, r != ctx_blocks).astype(jnp.float32)
    has_next = jnp.logical_and(r != ctx_blocks - 1, r != pl.num_programs(1) - 1).astype(jnp.float32)
    cur = cur_ref[0]
    prev = prev_ref[0] * has_prev
    nxt = next_ref[0] * has_next
    row = lax.broadcasted_iota(jnp.int32, cur.shape, 0)
    acc = cur * w_ref[half:half + 1, :] + b_ref[...]
    for d in range(1, half + 1):
        back = jnp.where(row < d, pltpu.roll(prev, d, 0), pltpu.roll(cur, d, 0))
        fwd = jnp.where(row >= rows - d, pltpu.roll(nxt, rows - d, 0), pltpu.roll(cur, rows - d, 0))
        acc = acc + back * w_ref[half - d:half - d + 1, :] + fwd * w_ref[half + d:half + d + 1, :]
    o_ref[0] = acc * jax.nn.sigmoid(acc)


def _conv_silu(proj, conv_w, conv_b, n_ctx):
    Bsz, TT, _ = proj.shape
    assert n_ctx == ROW_BLOCK
    n_blocks = TT // ROW_BLOCK
    col = _proj_block('sxbc', SSD_XBC)
    spec = lambda f: pl.BlockSpec((1, ROW_BLOCK, SSD_XBC), lambda b, r: (b, f(r), col))
    return pl.pallas_call(
        functools.partial(_conv_kernel, ctx_blocks=n_ctx // ROW_BLOCK),
        grid=(Bsz, n_blocks),
        in_specs=[spec(lambda r: jnp.maximum(r - 1, 0)), spec(lambda r: r),
                  spec(lambda r: jnp.minimum(r + 1, n_blocks - 1)),
                  pl.BlockSpec((SSD_CONV, SSD_XBC), lambda b, r: (0, 0)),
                  pl.BlockSpec((1, SSD_XBC), lambda b, r: (0, 0))],
        out_specs=pl.BlockSpec((1, ROW_BLOCK, SSD_XBC), lambda b, r: (b, r, 0)),
        out_shape=jax.ShapeDtypeStruct((Bsz, TT, SSD_XBC), jnp.float32),
        compiler_params=pltpu.CompilerParams(dimension_semantics=("arbitrary", "arbitrary"),
                                             vmem_limit_bytes=VMEM_LIMIT_BYTES),
        name="conv_silu",
    )(proj, proj, proj, conv_w, conv_b.reshape(1, SSD_XBC))


def _token_mixer(proj, n_ctx, qn, kn, conv_w, conv_b, dt_bias, a_log, d_skip, ssd_norm_g,
                 ml_bias, ml_norm_g, cos, sin, ctx_out):
    Bsz = proj.shape[0]
    cat = lambda a, b: jnp.concatenate([a, b], axis=1)
    aq, ak = _proj_cols(proj, 'aq'), _proj_cols(proj, 'ak')
    av = lax.optimization_barrier(_proj_cols(proj, 'av'))
    q = _apply_rope(_rmsnorm(_heads(aq[:, n_ctx:], ATT_HEADS), qn), cos, sin)
    k = _apply_rope(_rmsnorm(_heads(ak[:, n_ctx:], ATT_KV_HEADS), kn), cos, sin)
    kc = _rmsnorm(_heads(ak[:, :n_ctx], ATT_KV_HEADS), kn)
    vc = _heads(av[:, :n_ctx], ATT_KV_HEADS)
    att = _latent_attention(q, k, _heads(av[:, n_ctx:], ATT_KV_HEADS), kc, vc)
    if ctx_out:
        att_c = _context_attention(_rmsnorm(_heads(aq[:, :n_ctx], ATT_HEADS), qn), kc, vc)
    else:
        att_c = jnp.zeros((Bsz, n_ctx, att.shape[-1]), att.dtype)
    xbc_act = _conv_silu(proj, conv_w, conv_b, n_ctx)
    ssd = _ssd_block((proj, _proj_block('sz', SSD_INNER)), xbc_act, _proj_cols(proj, 'sdt'),
                     dt_bias, a_log, d_skip, ssd_norm_g, n_ctx)
    blk = lambda name: (proj, _proj_block(name, ML_INNER))
    ml = _mlstm_block(blk('mq'), blk('mk'), blk('mv'), blk('mo'), _proj_cols(proj, 'mif'),
                      ml_bias, ml_norm_g, n_ctx)
    return cat(att_c, att), ssd, ml


def kernel(x, c, ctx, c_ctx, norm1_g, norm2_g, w_mod, b_mod, w_in, att_qnorm, att_knorm, ssd_conv_w, ssd_conv_b, ssd_dt_bias, ssd_a_log, ssd_d, ssd_norm, ml_gate_bias, ml_norm, w_branch, w_out, peer_wq, peer_subkeys, peer_u, peer_v):
    Bsz, S, _ = x.shape
    depth = w_in.shape[0]
    cos, sin = _axial_rope(S // GRID_W)
    silu_c = jax.nn.silu(c)
    silu_cc = jax.nn.silu(c_ctx)
    xc = ctx
    n_ctx = ctx.shape[1]
    for l in range(depth):
        last = l == depth - 1
        m = (silu_c @ w_mod[l] + b_mod[l]).reshape(Bsz, 6, D_MODEL)
        mc = jnp.broadcast_to((silu_cc @ w_mod[l] + b_mod[l]).reshape(1, 6, D_MODEL), (Bsz, 6, D_MODEL))
        both = jnp.stack([mc, m], axis=1)
        x_all = jnp.concatenate([xc, x], axis=1)
        w_perm = jnp.concatenate(
            [w_in[l][:, _IN_START[n]:_IN_START[n] + _IN_WIDTH[n]] for n in _PROJ_ORDER]
            + [jnp.zeros((D_MODEL, PROJ_COLS - sum(IN_SPLITS)), w_in.dtype)], axis=1).astype(jnp.bfloat16)
        proj = _input_projection(x_all, norm1_g[l], both[:, :, np.array([1, 0])], w_perm, n_ctx)
        att, ssd, ml = _token_mixer(proj, n_ctx, att_qnorm[l], att_knorm[l], ssd_conv_w[l], ssd_conv_b[l],
                                    ssd_dt_bias[l], ssd_a_log[l], ssd_d[l], ssd_norm[l], ml_gate_bias[l],
                                    ml_norm[l], cos, sin, not last)
        x_all, h2 = _merge(att, ssd, ml, proj, w_branch[l], w_out[l], x_all, norm2_g[l],
                           both[:, :, np.array([2, 4, 3])], n_ctx)
        u_bf16 = peer_u[l].astype(jnp.bfloat16)
        vt_bf16 = peer_v[l].astype(jnp.bfloat16).reshape(PEER_EXPERTS // PEER_EBLK, PEER_EBLK, D_MODEL)
        vt_bf16 = vt_bf16.transpose(0, 2, 1)
        po = _peer(h2[:, n_ctx:].reshape(-1, D_MODEL), peer_wq[l], peer_subkeys[l], u_bf16, vt_bf16)
        x = x_all[:, n_ctx:] + m[:, None, 5] * po.reshape(x.shape)
        if not last:
            pc = _peer(h2[:, :n_ctx].reshape(-1, D_MODEL), peer_wq[l], peer_subkeys[l], u_bf16, vt_bf16)
            xc = x_all[:, :n_ctx] + mc[:, None, 5] * pc.reshape(xc.shape)
    return x
```

```python
import functools
import math

import jax
import jax.numpy as jnp
import numpy as np
from jax import lax
from jax.experimental import pallas as pl
from jax.experimental.pallas import tpu as pltpu

D_MODEL = 1024
GRID_W = 64
RMS_EPS = 1e-6
N_BRANCH = 3
BRANCH_W = 512
ATT_HEADS = 8
ATT_KV_HEADS = 2
ATT_HEAD_DIM = 64
ATT_GROUP = ATT_HEADS // ATT_KV_HEADS
Q_BLOCK = 128
ROPE_THETA = 10000.0
SSD_HEADS = 8
SSD_HEAD_DIM = 64
SSD_INNER = SSD_HEADS * SSD_HEAD_DIM
SSD_GROUPS = 2
SSD_STATE = 64
SSD_XBC = SSD_INNER + 2 * SSD_GROUPS * SSD_STATE
SSD_CONV = 5
SSD_CHUNK = 128
ML_HEADS = 4
ML_HEAD_DIM = 128
ML_INNER = ML_HEADS * ML_HEAD_DIM
ML_CHUNK = 128
PEER_HEADS = 8
PEER_KEYS = 128
PEER_EXPERTS = PEER_KEYS * PEER_KEYS
PEER_TOPK = 16
PEER_QDIM = 256
PEER_HALF = PEER_QDIM // 2
IN_SPLITS = (ATT_HEADS * ATT_HEAD_DIM, ATT_KV_HEADS * ATT_HEAD_DIM, ATT_KV_HEADS * ATT_HEAD_DIM,
             SSD_INNER, SSD_XBC, 2 * SSD_HEADS,
             ML_INNER, ML_INNER, ML_INNER, ML_INNER, 4 * ML_HEADS,
             N_BRANCH * D_MODEL)

LANES = 128
SUBLANES = 8
VMEM_LIMIT_BYTES = 56 * 1024 * 1024

ROUTER_TOKENS = 256
ROUTER_CHUNK = LANES
PEER_TOKENS = 1024
PEER_TOKENS_SMALL = 512
PEER_ROWS = 8
PEER_EBLK = PEER_ROWS * PEER_KEYS
PEER_CHUNK = 256

NEG_INF = float("-inf")


def _sort_network(n):
    def merge(lo, hi, r):
        step = r * 2
        if step < hi - lo:
            yield from merge(lo, hi, step)
            yield from merge(lo + r, hi, step)
            yield from [(i, i + r) for i in range(lo + r, hi - r, step)]
        else:
            yield (lo, lo + r)

    def sort(lo, hi):
        if hi - lo >= 1:
            mid = lo + (hi - lo) // 2
            yield from sort(lo, mid)
            yield from sort(mid + 1, hi)
            yield from merge(lo, hi, 1)

    return tuple(sort(0, n - 1))


_SORT16 = _sort_network(PEER_TOPK)


def _merge_top(lists, n):
    lists = list(lists)
    depth = len(lists)
    sub = lax.broadcasted_iota(jnp.int32, lists[0].shape, 0)
    tops = []
    for i in range(n):
        head = lists[0]
        m = jnp.max(head, axis=0, keepdims=True)
        tops.append(m)
        live = min(depth, n - 1 - i)
        if live == 0:
            break
        first = jnp.min(jnp.where(head == m, sub, SUBLANES), axis=0, keepdims=True)
        pop = sub == first
        for r in range(live):
            nxt = lists[r + 1] if r + 1 < depth else NEG_INF
            lists[r] = jnp.where(pop, nxt, lists[r])
    return tops


def _sorted_top(s, n):
    xs = [s[r * SUBLANES:(r + 1) * SUBLANES, :] for r in range(s.shape[0] // SUBLANES)]
    for i, j in _SORT16:
        xs[i], xs[j] = jnp.maximum(xs[i], xs[j]), jnp.minimum(xs[i], xs[j])
    return _merge_top(xs, n)


def _prefix_count(values, pred):
    n = len(values)
    steps = [n >> (i + 1) for i in range(n.bit_length() - 1)]
    taken = []

    def pivot(level, base, step):
        if level == len(taken):
            return values[base + step - 1]
        return jnp.where(taken[level], pivot(level + 1, base + steps[level], step), pivot(level + 1, base, step))

    count = None
    for step in steps:
        ok = pred(pivot(0, 0, step))
        taken.append(ok)
        term = jnp.where(ok, float(step), 0.0)
        count = term if count is None else count + term
    return jnp.where(pred(values[n - 1]), float(n), count)


def _rows_to_sublanes(rows, first_sublane, shape):
    sub = lax.broadcasted_iota(jnp.int32, shape, 0)
    out = jnp.zeros(shape, jnp.float32)
    for k, row in enumerate(rows):
        out = jnp.where(sub == first_sublane + k, row, out)
    return out


_CAND_LEN = (16, 8, 5, 4, 12, 4, 1, 0)


def _router_kernel(h_ref, wq_ref, sk_ref, rank_ref, e2_ref, cnt_ref, e1_ref, q_scr):
    q_scr[...] = jnp.dot(h_ref[...], wq_ref[...], preferred_element_type=jnp.float32).astype(jnp.bfloat16)
    nt = (((1,), (1,)), ((), ()))
    shape8 = (SUBLANES, ROUTER_CHUNK)
    sub = lax.broadcasted_iota(jnp.int32, shape8, 0)
    cand_len = jnp.zeros(shape8, jnp.int32)
    for g, n in enumerate(_CAND_LEN):
        cand_len = jnp.where(sub == g, n, cand_len)
    for h in range(PEER_HEADS):
        q1 = q_scr[:, (2 * h) * PEER_HALF:(2 * h + 1) * PEER_HALF]
        q2 = q_scr[:, (2 * h + 1) * PEER_HALF:(2 * h + 2) * PEER_HALF]
        s1 = lax.dot_general(sk_ref[2 * h], q1, nt, preferred_element_type=jnp.float32)
        s2 = lax.dot_general(sk_ref[2 * h + 1], q2, nt, preferred_element_type=jnp.float32)
        for c in range(ROUTER_TOKENS // ROUTER_CHUNK):
            cs = slice(c * ROUTER_CHUNK, (c + 1) * ROUTER_CHUNK)
            s1c = s1[:, cs]
            s2c = s2[:, cs]
            top1 = _sorted_top(s1c, PEER_TOPK)
            top2 = _sorted_top(s2c, PEER_TOPK)
            a_lo = _rows_to_sublanes(top1[:4], 0, shape8)
            b_lo = _rows_to_sublanes(top2[:3], 4, shape8)
            cands = []
            for r in range(PEER_TOPK):
                by_a = a_lo + top2[r]
                by_b = (top1[4 + r] + b_lo) if 4 + r < PEER_TOPK else by_a
                cands.append(jnp.where(r < cand_len, jnp.where(sub < 4, by_a, by_b), NEG_INF))
            tau = _merge_top(cands, PEER_TOPK)[-1]
            m1, m2 = top1[0], top2[0]
            z = jnp.zeros(shape8, jnp.float32)
            for cand in cands:
                z = z + jnp.where(cand >= tau, jnp.exp(cand - (m1 + m2)), 0.0)
            z = jnp.sum(z, axis=0, keepdims=True)
            rank = _prefix_count(top2, lambda t: t > s2c)
            cnt = _prefix_count(top2, lambda t: s1c + t >= tau)
            rank_ref[h, :, cs] = rank.astype(jnp.bfloat16)
            e2_ref[h, :, cs] = jnp.exp(s2c - m2).astype(jnp.bfloat16)
            cnt_ref[h, :, cs] = cnt
            e1_ref[h, :, cs] = jnp.exp(s1c - m1) * (0.5 / z)


def _peer_router(h2_bf16, wq, subkeys):
    n_tok = h2_bf16.shape[0]
    sk = subkeys.reshape(PEER_HEADS * 2, PEER_KEYS, PEER_HALF)
    shape = (PEER_HEADS, PEER_KEYS, n_tok)
    spec = pl.BlockSpec((PEER_HEADS, PEER_KEYS, ROUTER_TOKENS), lambda i: (0, 0, i))
    return pl.pallas_call(
        _router_kernel,
        grid=(n_tok // ROUTER_TOKENS,),
        in_specs=[
            pl.BlockSpec((ROUTER_TOKENS, D_MODEL), lambda i: (i, 0)),
            pl.BlockSpec((D_MODEL, PEER_HEADS * PEER_QDIM), lambda i: (0, 0)),
            pl.BlockSpec((PEER_HEADS * 2, PEER_KEYS, PEER_HALF), lambda i: (0, 0, 0)),
        ],
        out_specs=[spec, spec, spec, spec],
        out_shape=[jax.ShapeDtypeStruct(shape, jnp.bfloat16), jax.ShapeDtypeStruct(shape, jnp.bfloat16),
                   jax.ShapeDtypeStruct(shape, jnp.float32), jax.ShapeDtypeStruct(shape, jnp.float32)],
        scratch_shapes=[pltpu.VMEM((ROUTER_TOKENS, PEER_HEADS * PEER_QDIM), jnp.bfloat16)],
        compiler_params=pltpu.CompilerParams(dimension_semantics=("arbitrary",),
                                             vmem_limit_bytes=VMEM_LIMIT_BYTES),
        name="peer_router",
    )(h2_bf16, wq.astype(jnp.bfloat16), sk.astype(jnp.bfloat16))


GELU_C0 = math.sqrt(2.0 / math.pi)
GELU_C1 = 0.044715 * GELU_C0


def _peer_dense_kernel(ht_ref, u_ref, vt_ref, rank_ref, e2_ref, cnt_ref, e1_ref, o_ref,
                       acc_scr, pre_scr, p_scr, rank_scr, e2_scr):
    j = pl.program_id(1)
    n_tok = ht_ref.shape[1]
    n_chunks = n_tok // PEER_CHUNK
    nt = (((1,), (1,)), ((), ()))

    @pl.when(j == 0)
    def _():
        acc_scr[...] = jnp.zeros_like(acc_scr)

        def relayout(t, carry):
            lanes = pl.ds(pl.multiple_of(t * LANES, LANES), LANES)
            for h in range(PEER_HEADS):
                rank = rank_ref[h, :, lanes].astype(jnp.float32)
                e2 = e2_ref[h, :, lanes].astype(jnp.float32)
                for b in range(PEER_KEYS // SUBLANES):
                    rank_scr[t, b, h] = rank[b * SUBLANES:(b + 1) * SUBLANES]
                    e2_scr[t, b, h] = e2[b * SUBLANES:(b + 1) * SUBLANES]
            return carry

        lax.fori_loop(0, n_tok // LANES, relayout, 0)

    def pre_matmul(c, parity):
        tok = pl.multiple_of(c * PEER_CHUNK, PEER_CHUNK)
        pre_scr[parity] = jnp.dot(u_ref[...], ht_ref[:, pl.ds(tok, PEER_CHUNK)],
                                  preferred_element_type=jnp.float32)

    def out_matmul(c, parity):
        tok = pl.multiple_of(c * PEER_CHUNK, PEER_CHUNK)
        acc_scr[:, pl.ds(tok, PEER_CHUNK)] += jnp.dot(vt_ref[0], p_scr[parity],
                                                      preferred_element_type=jnp.float32)

    def gates(c, parity):
        for t in range(PEER_CHUNK // LANES):
            lanes = pl.ds(pl.multiple_of(c * PEER_CHUNK + t * LANES, LANES), LANES)
            ts = slice(t * LANES, (t + 1) * LANES)
            tile = c * (PEER_CHUNK // LANES) + t
            for b in range(PEER_KEYS // SUBLANES):
                rank = [rank_scr[tile, b, h] for h in range(PEER_HEADS)]
                e2 = [e2_scr[tile, b, h] for h in range(PEER_HEADS)]
                for a in range(PEER_ROWS):
                    rs = slice(a * PEER_KEYS + b * SUBLANES, a * PEER_KEYS + (b + 1) * SUBLANES)
                    terms = []
                    for h in range(PEER_HEADS):
                        terms.append(jnp.where(rank[h] < cnt_ref[a, h:h + 1, lanes], e2[h], 0.0)
                                     * e1_ref[a, h:h + 1, lanes])
                    while len(terms) > 1:
                        terms = [x + y for x, y in zip(terms[::2], terms[1::2])]
                    x = pre_scr[parity, rs, ts]
                    act = x * (1.0 + jnp.tanh(x * (GELU_C0 + GELU_C1 * (x * x))))
                    pre_scr[parity, rs, ts] = terms[0] * act
        p_scr[parity] = pre_scr[parity].astype(jnp.bfloat16)

    p_scr[1] = jnp.zeros(p_scr.shape[1:], p_scr.dtype)
    pre_matmul(0, 0)

    def body(c, carry):
        parity = c % 2
        out_matmul(jnp.maximum(c - 1, 0), 1 - parity)
        pre_matmul(c + 1, 1 - parity)
        gates(c, parity)
        return carry

    lax.fori_loop(0, n_chunks - 1, body, 0)
    last = n_chunks - 1
    out_matmul(last - 1, 1 - last % 2)
    gates(last, last % 2)
    out_matmul(last, last % 2)

    @pl.when(j == pl.num_programs(1) - 1)
    def _():
        o_ref[...] = acc_scr[...].T


def _peer_dense(h2t_bf16, u_bf16, vt_bf16, rank, e2, cnt, e1, tb):
    n_tok = h2t_bf16.shape[1]
    row_spec = pl.BlockSpec((PEER_ROWS, PEER_HEADS, tb), lambda i, j: (j, 0, i))
    tiled = (tb // LANES, PEER_KEYS // SUBLANES, PEER_HEADS, SUBLANES, LANES)
    key_spec = pl.BlockSpec((PEER_HEADS, PEER_KEYS, tb), lambda i, j: (0, 0, i))
    return pl.pallas_call(
        _peer_dense_kernel,
        grid=(n_tok // tb, PEER_EXPERTS // PEER_EBLK),
        in_specs=[
            pl.BlockSpec((D_MODEL, tb), lambda i, j: (0, i)),
            pl.BlockSpec((PEER_EBLK, D_MODEL), lambda i, j: (j, 0)),
            pl.BlockSpec((1, D_MODEL, PEER_EBLK), lambda i, j: (j, 0, 0)),
            key_spec, key_spec, row_spec, row_spec,
        ],
        out_specs=pl.BlockSpec((tb, D_MODEL), lambda i, j: (i, 0)),
        out_shape=jax.ShapeDtypeStruct((n_tok, D_MODEL), jnp.float32),
        scratch_shapes=[pltpu.VMEM((D_MODEL, tb), jnp.float32),
                        pltpu.VMEM((2, PEER_EBLK, PEER_CHUNK), jnp.float32),
                        pltpu.VMEM((2, PEER_EBLK, PEER_CHUNK), jnp.bfloat16),
                        pltpu.VMEM(tiled, jnp.float32),
                        pltpu.VMEM(tiled, jnp.float32)],
        compiler_params=pltpu.CompilerParams(dimension_semantics=("arbitrary", "arbitrary"),
                                             vmem_limit_bytes=VMEM_LIMIT_BYTES),
        name="peer_dense",
    )(h2t_bf16, u_bf16, vt_bf16, rank, e2, cnt, e1)


def _peer(h2, wq, subkeys, u_bf16, vt_bf16):
    n_tok = h2.shape[0]
    tb = PEER_TOKENS if n_tok % PEER_TOKENS == 0 else PEER_TOKENS_SMALL
    h2_bf16 = h2.astype(jnp.bfloat16)
    rank, e2, cnt, e1 = _peer_router(h2_bf16, wq, subkeys)
    cnt, e1 = cnt.transpose(1, 0, 2), e1.transpose(1, 0, 2)
    return _peer_dense(h2_bf16.T, u_bf16, vt_bf16, rank, e2, cnt, e1, tb)


ATT_Q_TOKENS = 256
ATT_V_COLS = 2 * ATT_HEAD_DIM


def _attention_kernel(q_ref, kt_ref, v_ref, o_ref, *, kv_chunk):
    tq = q_ref.shape[2]
    rows = ATT_GROUP * tq
    q = q_ref[0].reshape(rows, ATT_HEAD_DIM)
    n_chunks = kt_ref.shape[3] // kv_chunk

    def body(c, carry):
        m, acc = carry
        off = pl.multiple_of(c * kv_chunk, kv_chunk)
        s = jnp.dot(q, kt_ref[0, 0, :, pl.ds(off, kv_chunk)], preferred_element_type=jnp.float32)
        m_new = jnp.maximum(m, jnp.max(s, axis=-1, keepdims=True))
        p = jnp.exp(s - m_new).astype(jnp.bfloat16)
        acc = jnp.exp(m - m_new) * acc + jnp.dot(p, v_ref[0, 0, pl.ds(off, kv_chunk), :],
                                                 preferred_element_type=jnp.float32)
        return m_new, acc

    m0 = jnp.full((rows, 1), NEG_INF, jnp.float32)
    acc0 = jnp.zeros((rows, ATT_V_COLS), jnp.float32)
    _, acc = lax.fori_loop(0, n_chunks, body, (m0, acc0), unroll=True)
    out = acc[:, :ATT_HEAD_DIM] / acc[:, ATT_HEAD_DIM:ATT_HEAD_DIM + 1]
    o_ref[0] = out.reshape(ATT_GROUP, tq, ATT_HEAD_DIM)


def _attention(q, k, v, kv_chunk):
    Bsz, S = q.shape[0], q.shape[1]
    SK = k.shape[1]
    tq = min(ATT_Q_TOKENS, S)
    qh = (q * ATT_HEAD_DIM ** -0.5).astype(jnp.bfloat16).transpose(0, 2, 1, 3)
    kt = k.astype(jnp.bfloat16).transpose(0, 2, 3, 1)
    pad = jnp.concatenate([jnp.ones(v.shape[:-1] + (1,), v.dtype),
                           jnp.zeros(v.shape[:-1] + (ATT_V_COLS - ATT_HEAD_DIM - 1,), v.dtype)], axis=-1)
    vx = jnp.concatenate([v, pad], axis=-1).astype(jnp.bfloat16).transpose(0, 2, 1, 3)
    o = pl.pallas_call(
        functools.partial(_attention_kernel, kv_chunk=kv_chunk),
        grid=(Bsz, ATT_KV_HEADS, S // tq),
        in_specs=[
            pl.BlockSpec((1, ATT_GROUP, tq, ATT_HEAD_DIM), lambda b, g, i: (b, g, i, 0)),
            pl.BlockSpec((1, 1, ATT_HEAD_DIM, SK), lambda b, g, i: (b, g, 0, 0)),
            pl.BlockSpec((1, 1, SK, ATT_V_COLS), lambda b, g, i: (b, g, 0, 0)),
        ],
        out_specs=pl.BlockSpec((1, ATT_GROUP, tq, ATT_HEAD_DIM), lambda b, g, i: (b, g, i, 0)),
        out_shape=jax.ShapeDtypeStruct((Bsz, ATT_HEADS, S, ATT_HEAD_DIM), jnp.float32),
        compiler_params=pltpu.CompilerParams(dimension_semantics=("arbitrary", "arbitrary", "arbitrary"),
                                             vmem_limit_bytes=VMEM_LIMIT_BYTES),
        name="attention",
    )(qh, kt, vx)
    return o.transpose(0, 2, 1, 3).reshape(Bsz, S, ATT_HEADS * ATT_HEAD_DIM)


SCAN_STEP_CHUNKS = 1

SSD_GROUP_HEADS = SSD_HEADS // SSD_GROUPS
SSD_GROUP_W = SSD_GROUP_HEADS * SSD_HEAD_DIM
SSD_GATE_COLS = LANES


def _ssd_kernel(*refs, reverse, combine):
    if combine:
        x_ref, bm_ref, cm_ref, grow_ref, gcol_ref, other_ref, z_ref, skip_ref, gain_ref, out_ref, s_scr = refs
    else:
        x_ref, bm_ref, cm_ref, grow_ref, gcol_ref, out_ref, s_scr = refs
    Q, P, N, H = SSD_CHUNK, SSD_HEAD_DIM, SSD_STATE, SSD_HEADS

    @pl.when(pl.program_id(1) == 0)
    def _():
        s_scr[...] = jnp.zeros_like(s_scr)

    row = lax.broadcasted_iota(jnp.int32, (Q, Q), 0)
    col = lax.broadcasted_iota(jnp.int32, (Q, Q), 1)
    not_after = (col >= row) if reverse else (col <= row)
    tri = not_after.astype(jnp.float32)
    last = 0 if reverse else Q - 1
    nt = (((1,), (1,)), ((), ()))
    grow = grow_ref[0]
    gcol = gcol_ref[0]
    a_cols = jnp.dot(tri, gcol, preferred_element_type=jnp.float32, precision=lax.Precision.HIGHEST)
    a_rows = lax.dot_general(grow, tri, nt, preferred_element_type=jnp.float32,
                             precision=lax.Precision.HIGHEST)
    ys = []
    for g in range(SSD_GROUPS):
        in_group = lax.broadcasted_iota(jnp.int32, (Q, SSD_GROUPS * N), 1) // N == g
        bm = bm_ref[0].astype(jnp.bfloat16)
        cm = jnp.where(in_group, cm_ref[0], 0.0).astype(jnp.bfloat16)
        cb = lax.dot_general(cm, bm, nt, preferred_element_type=jnp.float32)
        state = s_scr[g]
        y_off = jnp.dot(cm, state.astype(jnp.bfloat16), preferred_element_type=jnp.float32)
        xw, decay = [], []
        for pair in range(SSD_GROUP_HEADS // 2):
            even = g * SSD_GROUP_HEADS + 2 * pair
            ps = slice(even * P, (even + 2) * P)
            low = lax.broadcasted_iota(jnp.int32, (Q, 2 * P), 1) < P
            per_head = lambda col0, col1: jnp.where(low, col0, col1)
            aq = [a_cols[:, H + even + i:H + even + i + 1] for i in range(2)]
            as_ = [a_rows[H + even + i:H + even + i + 1, :] for i in range(2)]
            a_tot = [a[:, last:last + 1] for a in as_]
            xdt = x_ref[0, :, ps] * per_head(gcol[:, even:even + 1], gcol[:, even + 1:even + 2])
            xdt_b = xdt.astype(jnp.bfloat16)
            y = [jnp.dot((cb * jnp.exp(jnp.where(not_after, aq[i] - as_[i], NEG_INF))).astype(jnp.bfloat16), xdt_b,
                         preferred_element_type=jnp.float32) for i in range(2)]
            aq_pair = per_head(aq[0], aq[1])
            tot_pair = jnp.where(low[0:1], a_tot[0], a_tot[1])
            ys.append(jnp.where(low, y[0], y[1]) + y_off[:, 2 * pair * P:(2 * pair + 2) * P] * jnp.exp(aq_pair))
            xw.append(jnp.exp(tot_pair - aq_pair) * xdt)
            decay.append(jnp.exp(tot_pair))
        s_loc = lax.dot_general(bm, jnp.concatenate(xw, axis=1).astype(jnp.bfloat16),
                                (((0,), (0,)), ((), ())), preferred_element_type=jnp.float32)
        s_scr[g] = state * jnp.concatenate(decay, axis=1) + s_loc
    y = jnp.concatenate(ys, axis=1)
    if combine:
        x = x_ref[0]
        z = z_ref[0]
        tot = (y + other_ref[0] + x * skip_ref[...]) * (z * jax.nn.sigmoid(z))
        out_ref[0] = tot * lax.rsqrt(jnp.mean(tot * tot, axis=1, keepdims=True) + RMS_EPS) * gain_ref[...]
    else:
        out_ref[0] = y


def _ssd_direction(xbc, grow, gcol, n_ctx_chunks, reverse, extra=None):
    Bsz, TT, _ = xbc.shape
    n_chunks = TT // SSD_CHUNK
    if reverse:
        chunk = lambda c: jnp.where(c < n_ctx_chunks, n_ctx_chunks - 1 - c, n_chunks - 1 + n_ctx_chunks - c)
    else:
        chunk = lambda c: c
    gn = SSD_GROUPS * SSD_STATE
    seq_spec = pl.BlockSpec((1, SSD_CHUNK, SSD_INNER), lambda b, c: (b, chunk(c), 0))
    vec_spec = pl.BlockSpec((1, SSD_INNER), lambda b, c: (0, 0))
    in_specs = [seq_spec,
                pl.BlockSpec((1, SSD_CHUNK, gn), lambda b, c: (b, chunk(c), SSD_INNER // gn)),
                pl.BlockSpec((1, SSD_CHUNK, gn), lambda b, c: (b, chunk(c), SSD_INNER // gn + 1)),
                pl.BlockSpec((1, 2 * SSD_HEADS, SSD_CHUNK), lambda b, c: (b, 0, chunk(c))),
                pl.BlockSpec((1, SSD_CHUNK, SSD_GATE_COLS), lambda b, c: (b, chunk(c), 0))]
    args = [xbc, xbc, xbc, grow, gcol]
    if extra is not None:
        other, z, skip, gain = extra
        z_spec = pl.BlockSpec((1, SSD_CHUNK, SSD_INNER), lambda b, c: (b, chunk(c), z[1]))
        in_specs += [seq_spec, z_spec, vec_spec, vec_spec]
        args += [other, z[0], skip, gain]
    return pl.pallas_call(
        functools.partial(_ssd_kernel, reverse=reverse, combine=extra is not None),
        grid=(Bsz, n_chunks),
        in_specs=in_specs,
        out_specs=seq_spec,
        out_shape=jax.ShapeDtypeStruct((Bsz, TT, SSD_INNER), jnp.float32),
        scratch_shapes=[pltpu.VMEM((SSD_GROUPS, SSD_GROUPS * SSD_STATE, SSD_GROUP_W), jnp.float32)],
        compiler_params=pltpu.CompilerParams(dimension_semantics=("arbitrary", "arbitrary"),
                                             vmem_limit_bytes=VMEM_LIMIT_BYTES),
        name="ssd_bwd" if reverse else "ssd_fwd",
    )(*args)


def _ssd_block(z, xbc_act, dt_raw, dt_bias, a_log, d_skip, norm_g, n_ctx):
    Bsz, TT, _ = dt_raw.shape
    dt = jax.nn.softplus(dt_raw.reshape(Bsz, TT, 2, SSD_HEADS) + dt_bias)
    a = dt * -jnp.exp(a_log)
    gate_cols = jnp.concatenate([dt, a], axis=-1)
    skip = jnp.repeat(d_skip, SSD_HEAD_DIM).reshape(1, SSD_INNER)
    outs = None
    for direction in (1, 0):
        gc = gate_cols[:, :, direction]
        grow = gc.transpose(0, 2, 1)
        gcol = jnp.pad(gc, ((0, 0), (0, 0), (0, SSD_GATE_COLS - 2 * SSD_HEADS)))
        extra = None if direction == 1 else (outs, z, skip, norm_g.reshape(1, SSD_INNER))
        outs = _ssd_direction(xbc_act, grow, gcol, n_ctx // SSD_CHUNK, direction == 1, extra)
    return outs


ML_STATE_COLS = 2 * ML_HEAD_DIM
ML_GATE_COLS = LANES


def _mlstm_kernel(*refs, reverse, combine):
    if combine:
        q_ref, k_ref, v_ref, grow_ref, gcol_ref, other_ref, o_ref, gain_ref, out_ref, s_scr, m_scr = refs
    else:
        q_ref, k_ref, v_ref, grow_ref, gcol_ref, out_ref, s_scr, m_scr = refs
    Q, D, H = ML_CHUNK, ML_HEAD_DIM, ML_HEADS

    @pl.when(pl.program_id(1) == 0)
    def _():
        s_scr[...] = jnp.zeros_like(s_scr)
        m_scr[...] = jnp.zeros_like(m_scr)

    row = lax.broadcasted_iota(jnp.int32, (Q, Q), 0)
    col = lax.broadcasted_iota(jnp.int32, (Q, Q), 1)
    not_after = (col >= row) if reverse else (col <= row)
    tri = not_after.astype(jnp.float32)
    last = 0 if reverse else Q - 1
    nt = (((1,), (1,)), ((), ()))
    one_col = (lax.broadcasted_iota(jnp.int32, (Q, D), 1) == 0).astype(jnp.float32)
    states = [s_scr[h] for h in range(H)]
    ms = [m_scr[h, 0:1, 0:1] for h in range(H)]
    chunks = range(SCAN_STEP_CHUNKS)
    for sub in (reversed(chunks) if reverse else chunks):
        rows = slice(sub * Q, (sub + 1) * Q)
        grow = grow_ref[0, :, rows]
        gcol = gcol_ref[0, rows, :]
        b_cols = jnp.dot(tri, gcol, preferred_element_type=jnp.float32, precision=lax.Precision.HIGHEST)
        b_rows = lax.dot_general(grow, tri, nt, preferred_element_type=jnp.float32,
                                 precision=lax.Precision.HIGHEST)
        for h in range(H):
            hs = slice(h * D, (h + 1) * D)
            q = q_ref[0, rows, hs].astype(jnp.bfloat16)
            k = (k_ref[0, rows, hs] * D ** -0.5).astype(jnp.bfloat16)
            v_ext = jnp.concatenate([v_ref[0, rows, hs], one_col], axis=1)
            bq = b_cols[:, H + h:H + h + 1]
            bs = b_rows[H + h:H + h + 1, :]
            i_q = gcol[:, h:h + 1]
            i_s = grow[h:h + 1, :]
            b_last = bs[:, last:last + 1]
            m0, state = ms[h], states[h]
            dm = jnp.where(not_after, bq - bs + i_s, NEG_INF)
            inter = bq + m0
            m_t = jnp.maximum(inter, jnp.max(dm, axis=1, keepdims=True))
            s_mat = lax.dot_general(q, k, nt, preferred_element_type=jnp.float32) * jnp.exp(dm - m_t)
            num = (jnp.dot(s_mat.astype(jnp.bfloat16), v_ext.astype(jnp.bfloat16),
                           preferred_element_type=jnp.float32)
                   + jnp.exp(inter - m_t) * jnp.dot(q, state.astype(jnp.bfloat16),
                                                    preferred_element_type=jnp.float32))
            den = num[:, D:D + 1]
            hout = num[:, :D] / jnp.maximum(jnp.abs(den), jnp.exp(-m_t))
            g_q = b_last - bq + i_q
            m_loc = jnp.max(g_q, axis=0, keepdims=True)
            wv = (jnp.exp(g_q - m_loc) * v_ext).astype(jnp.bfloat16)
            s_loc = lax.dot_general(k, wv, (((0,), (0,)), ((), ())), preferred_element_type=jnp.float32)
            m_new = jnp.maximum(b_last + m0, m_loc)
            states[h] = jnp.exp(b_last + m0 - m_new) * state + jnp.exp(m_loc - m_new) * s_loc
            ms[h] = m_new
            if combine:
                tot = hout + other_ref[0, rows, hs]
                y = tot * lax.rsqrt(jnp.mean(tot * tot, axis=1, keepdims=True) + RMS_EPS) * gain_ref[:, hs]
                out_ref[0, rows, hs] = y * jax.nn.sigmoid(o_ref[0, rows, hs])
            else:
                out_ref[0, rows, hs] = hout
    for h in range(H):
        s_scr[h] = states[h]
        m_scr[h] = jnp.broadcast_to(ms[h], (SUBLANES, LANES))


def _mlstm_direction(q, k, v, grow, gcol, n_ctx_chunks, reverse, extra=None):
    Bsz, TT, _ = q[0].shape
    step = SCAN_STEP_CHUNKS * ML_CHUNK
    n_chunks = TT // step
    if reverse:
        chunk = lambda c: jnp.where(c < n_ctx_chunks, n_ctx_chunks - 1 - c, n_chunks - 1 + n_ctx_chunks - c)
    else:
        chunk = lambda c: c
    seq_spec = pl.BlockSpec((1, step, ML_INNER), lambda b, c: (b, chunk(c), 0))
    col_spec = lambda j: pl.BlockSpec((1, step, ML_INNER), lambda b, c: (b, chunk(c), j))
    in_specs = [col_spec(q[1]), col_spec(k[1]), col_spec(v[1]),
                pl.BlockSpec((1, 2 * ML_HEADS, step), lambda b, c: (b, 0, chunk(c))),
                pl.BlockSpec((1, step, ML_GATE_COLS), lambda b, c: (b, chunk(c), 0))]
    args = [q[0], k[0], v[0], grow, gcol]
    if extra is not None:
        other, o_pre, gain = extra
        in_specs += [seq_spec, col_spec(o_pre[1]), pl.BlockSpec((1, ML_INNER), lambda b, c: (0, 0))]
        args += [other, o_pre[0], gain]
    return pl.pallas_call(
        functools.partial(_mlstm_kernel, reverse=reverse, combine=extra is not None),
        grid=(Bsz, n_chunks),
        in_specs=in_specs,
        out_specs=seq_spec,
        out_shape=jax.ShapeDtypeStruct((Bsz, TT, ML_INNER), jnp.float32),
        scratch_shapes=[pltpu.VMEM((ML_HEADS, ML_HEAD_DIM, ML_STATE_COLS), jnp.float32),
                        pltpu.VMEM((ML_HEADS, SUBLANES, LANES), jnp.float32)],
        compiler_params=pltpu.CompilerParams(dimension_semantics=("arbitrary", "arbitrary"),
                                             vmem_limit_bytes=VMEM_LIMIT_BYTES),
        name="mlstm_bwd" if reverse else "mlstm_fwd",
    )(*args)


def _mlstm_block(q, k, v, o_pre, if_pre, gate_bias, norm_g, n_ctx):
    Bsz, TT, _ = if_pre.shape
    gates = (if_pre.transpose(0, 2, 1) + gate_bias.reshape(1, 4 * ML_HEADS, 1)).reshape(Bsz, 2, 2, ML_HEADS, TT)
    gate_rows = jnp.concatenate([gates[:, :, 0], jax.nn.log_sigmoid(gates[:, :, 1])], axis=2)
    outs = None
    for direction in (1, 0):
        grow = gate_rows[:, direction]
        gcol = jnp.pad(grow.transpose(0, 2, 1), ((0, 0), (0, 0), (0, ML_GATE_COLS - 2 * ML_HEADS)))
        extra = None if direction == 1 else (outs, o_pre, norm_g.reshape(1, ML_INNER))
        outs = _mlstm_direction(q, k, v, grow, gcol, n_ctx // (SCAN_STEP_CHUNKS * ML_CHUNK), direction == 1, extra)
    return outs


_PROJ_ORDER = ('aq', 'sz', 'mq', 'mk', 'mv', 'mo', 'gpre', 'sxbc', 'ak', 'av', 'sdt', 'mif')
_IN_NAMES = ('aq', 'ak', 'av', 'sz', 'sxbc', 'sdt', 'mq', 'mk', 'mv', 'mo', 'mif', 'gpre')
_IN_START = dict(zip(_IN_NAMES, np.cumsum((0,) + IN_SPLITS[:-1]).tolist()))
_IN_WIDTH = dict(zip(_IN_NAMES, IN_SPLITS))
_PROJ_PERM = np.concatenate([np.arange(_IN_START[n], _IN_START[n] + _IN_WIDTH[n]) for n in _PROJ_ORDER])
_PROJ_START = dict(zip(_PROJ_ORDER, np.cumsum([0] + [_IN_WIDTH[n] for n in _PROJ_ORDER[:-1]]).tolist()))
PROJ_COL_BLOCKS = 3
PROJ_COLS = -(-sum(IN_SPLITS) // (PROJ_COL_BLOCKS * LANES)) * (PROJ_COL_BLOCKS * LANES)
ROW_BLOCK = 256


def _proj_cols(proj, name):
    return proj[..., _PROJ_START[name]:_PROJ_START[name] + _IN_WIDTH[name]]


def _proj_block(name, width):
    assert _PROJ_START[name] % width == 0
    return _PROJ_START[name] // width


def _modulated_norm(x, gain, scale, shift):
    y = x * lax.rsqrt(jnp.mean(x * x, axis=-1, keepdims=True) + RMS_EPS)
    return (y * gain) * (1.0 + scale) + shift


def _inproj_kernel(x_ref, gain_ref, mod_ref, w_ref, o_ref):
    h = _modulated_norm(x_ref[0], gain_ref[...], mod_ref[0, 0, 0:1, :], mod_ref[0, 0, 1:2, :])
    o_ref[0] = jnp.dot(h.astype(jnp.bfloat16), w_ref[...], preferred_element_type=jnp.float32)


def _input_projection(x_all, gain, mod, w_perm, n_ctx):
    Bsz, TT, _ = x_all.shape
    cols = PROJ_COLS // PROJ_COL_BLOCKS
    ctx_blocks = n_ctx // ROW_BLOCK
    return pl.pallas_call(
        _inproj_kernel,
        grid=(PROJ_COL_BLOCKS, Bsz, TT // ROW_BLOCK),
        in_specs=[
            pl.BlockSpec((1, ROW_BLOCK, D_MODEL), lambda n, b, r: (b, r, 0)),
            pl.BlockSpec((1, D_MODEL), lambda n, b, r: (0, 0)),
            pl.BlockSpec((1, 1, 2, D_MODEL), lambda n, b, r: (b, jnp.where(r < ctx_blocks, 0, 1), 0, 0)),
            pl.BlockSpec((D_MODEL, cols), lambda n, b, r: (0, n)),
        ],
        out_specs=pl.BlockSpec((1, ROW_BLOCK, cols), lambda n, b, r: (b, r, n)),
        out_shape=jax.ShapeDtypeStruct((Bsz, TT, PROJ_COLS), jnp.float32),
        compiler_params=pltpu.CompilerParams(dimension_semantics=("arbitrary", "arbitrary", "arbitrary"),
                                             vmem_limit_bytes=VMEM_LIMIT_BYTES),
        name="input_projection",
    )(x_all, gain.reshape(1, D_MODEL), mod, w_perm)


def _merge_kernel(att_ref, ssd_ref, ml_ref, g0_ref, g1_ref, g2_ref, wb_ref, wo_ref, x_ref, gain_ref, mod_ref,
                  xo_ref, h2_ref):
    mixed = None
    for r, (branch, gate) in enumerate(((att_ref, g0_ref), (ssd_ref, g1_ref), (ml_ref, g2_ref))):
        p = jnp.dot(branch[0].astype(jnp.bfloat16), wb_ref[r], preferred_element_type=jnp.float32)
        term = jax.nn.sigmoid(gate[0]) * p
        mixed = term if mixed is None else mixed + term
    y = jnp.dot(mixed.astype(jnp.bfloat16), wo_ref[...], preferred_element_type=jnp.float32)
    x_new = x_ref[0] + mod_ref[0, 0, 0:1, :] * y
    xo_ref[0] = x_new
    h2 = _modulated_norm(x_new, gain_ref[...], mod_ref[0, 0, 1:2, :], mod_ref[0, 0, 2:3, :])
    h2_ref[0] = h2.astype(jnp.bfloat16)


def _merge(att, ssd, ml, proj, w_branch, w_out, x_all, gain, mod, n_ctx):
    Bsz, TT, _ = x_all.shape
    ctx_blocks = n_ctx // ROW_BLOCK
    row = lambda w, j=0: pl.BlockSpec((1, ROW_BLOCK, w), lambda b, r: (b, r, j))
    g_first = _proj_block('gpre', D_MODEL)
    return pl.pallas_call(
        _merge_kernel,
        grid=(Bsz, TT // ROW_BLOCK),
        in_specs=[
            row(BRANCH_W), row(BRANCH_W), row(BRANCH_W),
            row(D_MODEL, g_first), row(D_MODEL, g_first + 1), row(D_MODEL, g_first + 2),
            pl.BlockSpec((N_BRANCH, BRANCH_W, D_MODEL), lambda b, r: (0, 0, 0)),
            pl.BlockSpec((D_MODEL, D_MODEL), lambda b, r: (0, 0)),
            row(D_MODEL),
            pl.BlockSpec((1, D_MODEL), lambda b, r: (0, 0)),
            pl.BlockSpec((1, 1, 3, D_MODEL), lambda b, r: (b, jnp.where(r < ctx_blocks, 0, 1), 0, 0)),
        ],
        out_specs=[row(D_MODEL), row(D_MODEL)],
        out_shape=[jax.ShapeDtypeStruct((Bsz, TT, D_MODEL), jnp.float32),
                   jax.ShapeDtypeStruct((Bsz, TT, D_MODEL), jnp.bfloat16)],
        compiler_params=pltpu.CompilerParams(dimension_semantics=("arbitrary", "arbitrary"),
                                             vmem_limit_bytes=VMEM_LIMIT_BYTES),
        name="merge",
    )(att, ssd, ml, proj, proj, proj, w_branch.astype(jnp.bfloat16), w_out.astype(jnp.bfloat16),
      x_all, gain.reshape(1, D_MODEL), mod)


def _rmsnorm(x, g):
    xf = x.astype(jnp.float32)
    y = xf * lax.rsqrt(jnp.mean(xf * xf, axis=-1, keepdims=True) + RMS_EPS)
    return (y * g.astype(jnp.float32)).astype(x.dtype)


def _heads(t, n):
    return t.reshape(t.shape[0], t.shape[1], n, -1)


def _axial_rope(rows):
    n_freq = ATT_HEAD_DIM // 4
    inv = ROPE_THETA ** (-jnp.arange(n_freq, dtype=jnp.float32) / n_freq)
    row = jnp.broadcast_to(jnp.arange(rows, dtype=jnp.float32)[:, None], (rows, GRID_W)).reshape(-1)
    col = jnp.broadcast_to(jnp.arange(GRID_W, dtype=jnp.float32)[None, :], (rows, GRID_W)).reshape(-1)
    ang = jnp.concatenate([row[:, None] * inv, col[:, None] * inv], axis=-1)
    return jnp.cos(ang), jnp.sin(ang)


def _apply_rope(t, cos, sin):
    half = t.shape[-1] // 2
    tf = t.astype(jnp.float32)
    t1, t2 = tf[..., :half], tf[..., half:]
    c, s = cos[:, None, :], sin[:, None, :]
    return jnp.concatenate([t1 * c - t2 * s, t1 * s + t2 * c], axis=-1).astype(t.dtype)


ATT_KV_CHUNK = 768


def _latent_attention(q, k, v, kc, vc):
    kk = jnp.concatenate([k, kc], axis=1)
    vv = jnp.concatenate([v, vc], axis=1)
    assert kk.shape[1] % ATT_KV_CHUNK == 0
    return _attention(q, kk, vv, ATT_KV_CHUNK)


def _context_attention(qc, kc, vc):
    return _attention(qc, kc, vc, kc.shape[1])


def _conv_kernel(prev_ref, cur_ref, next_ref, w_ref, b_ref, o_ref, *, ctx_blocks):
    r = pl.program_id(1)
    rows = cur_ref.shape[1]
    half = SSD_CONV // 2
    has_prev = jnp.logical_and(r != 0, r != ctx_blocks).astype(jnp.float32)
    has_next = jnp.logical_and(r != ctx_blocks - 1, r != pl.num_programs(1) - 1).astype(jnp.float32)
    cur = cur_ref[0]
    prev = prev_ref[0] * has_prev
    nxt = next_ref[0] * has_next
    row = lax.broadcasted_iota(jnp.int32, cur.shape, 0)
    acc = cur * w_ref[half:half + 1, :] + b_ref[...]
    for d in range(1, half + 1):
        back = jnp.where(row < d, pltpu.roll(prev, d, 0), pltpu.roll(cur, d, 0))
        fwd = jnp.where(row >= rows - d, pltpu.roll(nxt, rows - d, 0), pltpu.roll(cur, rows - d, 0))
        acc = acc + back * w_ref[half - d:half - d + 1, :] + fwd * w_ref[half + d:half + d + 1, :]
    o_ref[0] = acc * jax.nn.sigmoid(acc)


def _conv_silu(proj, conv_w, conv_b, n_ctx):
    Bsz, TT, _ = proj.shape
    assert n_ctx == ROW_BLOCK
    n_blocks = TT // ROW_BLOCK
    col = _proj_block('sxbc', SSD_XBC)
    spec = lambda f: pl.BlockSpec((1, ROW_BLOCK, SSD_XBC), lambda b, r: (b, f(r), col))
    return pl.pallas_call(
        functools.partial(_conv_kernel, ctx_blocks=n_ctx // ROW_BLOCK),
        grid=(Bsz, n_blocks),
        in_specs=[spec(lambda r: jnp.maximum(r - 1, 0)), spec(lambda r: r),
                  spec(lambda r: jnp.minimum(r + 1, n_blocks - 1)),
                  pl.BlockSpec((SSD_CONV, SSD_XBC), lambda b, r: (0, 0)),
                  pl.BlockSpec((1, SSD_XBC), lambda b, r: (0, 0))],
        out_specs=pl.BlockSpec((1, ROW_BLOCK, SSD_XBC), lambda b, r: (b, r, 0)),
        out_shape=jax.ShapeDtypeStruct((Bsz, TT, SSD_XBC), jnp.float32),
        compiler_params=pltpu.CompilerParams(dimension_semantics=("arbitrary", "arbitrary"),
                                             vmem_limit_bytes=VMEM_LIMIT_BYTES),
        name="conv_silu",
    )(proj, proj, proj, conv_w, conv_b.reshape(1, SSD_XBC))


def _token_mixer(proj, n_ctx, qn, kn, conv_w, conv_b, dt_bias, a_log, d_skip, ssd_norm_g,
                 ml_bias, ml_norm_g, cos, sin, ctx_out):
    Bsz = proj.shape[0]
    cat = lambda a, b: jnp.concatenate([a, b], axis=1)
    aq, ak = _proj_cols(proj, 'aq'), _proj_cols(proj, 'ak')
    av = lax.optimization_barrier(_proj_cols(proj, 'av'))
    q = _apply_rope(_rmsnorm(_heads(aq[:, n_ctx:], ATT_HEADS), qn), cos, sin)
    k = _apply_rope(_rmsnorm(_heads(ak[:, n_ctx:], ATT_KV_HEADS), kn), cos, sin)
    kc = _rmsnorm(_heads(ak[:, :n_ctx], ATT_KV_HEADS), kn)
    vc = _heads(av[:, :n_ctx], ATT_KV_HEADS)
    att = _latent_attention(q, k, _heads(av[:, n_ctx:], ATT_KV_HEADS), kc, vc)
    if ctx_out:
        att_c = _context_attention(_rmsnorm(_heads(aq[:, :n_ctx], ATT_HEADS), qn), kc, vc)
    else:
        att_c = jnp.zeros((Bsz, n_ctx, att.shape[-1]), att.dtype)
    xbc_act = _conv_silu(proj, conv_w, conv_b, n_ctx)
    ssd = _ssd_block((proj, _proj_block('sz', SSD_INNER)), xbc_act, _proj_cols(proj, 'sdt'),
                     dt_bias, a_log, d_skip, ssd_norm_g, n_ctx)
    blk = lambda name: (proj, _proj_block(name, ML_INNER))
    ml = _mlstm_block(blk('mq'), blk('mk'), blk('mv'), blk('mo'), _proj_cols(proj, 'mif'),
                      ml_bias, ml_norm_g, n_ctx)
    return cat(att_c, att), ssd, ml


def kernel(x, c, ctx, c_ctx, norm1_g, norm2_g, w_mod, b_mod, w_in, att_qnorm, att_knorm, ssd_conv_w, ssd_conv_b, ssd_dt_bias, ssd_a_log, ssd_d, ssd_norm, ml_gate_bias, ml_norm, w_branch, w_out, peer_wq, peer_subkeys, peer_u, peer_v):
    Bsz, S, _ = x.shape
    depth = w_in.shape[0]
    cos, sin = _axial_rope(S // GRID_W)
    silu_c = jax.nn.silu(c)
    silu_cc = jax.nn.silu(c_ctx)
    xc = ctx
    n_ctx = ctx.shape[1]
    for l in range(depth):
        last = l == depth - 1
        m = (silu_c @ w_mod[l] + b_mod[l]).reshape(Bsz, 6, D_MODEL)
        mc = jnp.broadcast_to((silu_cc @ w_mod[l] + b_mod[l]).reshape(1, 6, D_MODEL), (Bsz, 6, D_MODEL))
        both = jnp.stack([mc, m], axis=1)
        x_all = jnp.concatenate([xc, x], axis=1)
        w_perm = jnp.concatenate(
            [w_in[l][:, _IN_START[n]:_IN_START[n] + _IN_WIDTH[n]] for n in _PROJ_ORDER]
            + [jnp.zeros((D_MODEL, PROJ_COLS - sum(IN_SPLITS)), w_in.dtype)], axis=1).astype(jnp.bfloat16)
        proj = _input_projection(x_all, norm1_g[l], both[:, :, np.array([1, 0])], w_perm, n_ctx)
        att, ssd, ml = _token_mixer(proj, n_ctx, att_qnorm[l], att_knorm[l], ssd_conv_w[l], ssd_conv_b[l],
                                    ssd_dt_bias[l], ssd_a_log[l], ssd_d[l], ssd_norm[l], ml_gate_bias[l],
                                    ml_norm[l], cos, sin, not last)
        x_all, h2 = _merge(att, ssd, ml, proj, w_branch[l], w_out[l], x_all, norm2_g[l],
                           both[:, :, np.array([2, 4, 3])], n_ctx)
        u_bf16 = peer_u[l].astype(jnp.bfloat16)
        vt_bf16 = peer_v[l].astype(jnp.bfloat16).reshape(PEER_EXPERTS // PEER_EBLK, PEER_EBLK, D_MODEL)
        vt_bf16 = vt_bf16.transpose(0, 2, 1)
        po = _peer(h2[:, n_ctx:].reshape(-1, D_MODEL), peer_wq[l], peer_subkeys[l], u_bf16, vt_bf16)
        x = x_all[:, n_ctx:] + m[:, None, 5] * po.reshape(x.shape)
        if not last:
            pc = _peer(h2[:, :n_ctx].reshape(-1, D_MODEL), peer_wq[l], peer_subkeys[l], u_bf16, vt_bf16)
            xc = x_all[:, :n_ctx] + mc[:, None, 5] * pc.reshape(xc.shape)
    return x
```
